```python
import jax, jax.numpy as jnp
from jax import lax
import numpy as np

D_MODEL = 1024
BATCH = 4
SEQ = 4096
DEPTH = 2
DEC_BATCH = 32
DEC_SEQ = 1
PAST_LEN = 8192
PAGE_SIZE = 128

N_HEADS = 8
N_KV_HEADS = 2
HEAD_DIM = D_MODEL // N_HEADS
GROUP = N_HEADS // N_KV_HEADS
ROT_DIM = HEAD_DIM // 4
ROPE_THETA = 500000.0
MOBA_BLOCK = 256
TOP_K_BLOCKS = 3
Q_CHUNK = 16
CONV_WIDTH = 31
D_CONV = D_MODEL
D_FF = (7 * D_MODEL) // 2
N_EXPERTS = 8
TOP_K_EXPERTS = 2
PLE_DIM = 256
EPS = 1e-6

kernel_name = "hybrid_conformerconv_moba_moe_decoder_step"


def rmsnorm(x, g):
    xf = x.astype(jnp.float32)
    r = lax.rsqrt(jnp.mean(xf * xf, axis=-1, keepdims=True) + EPS)
    return (xf * r * g.astype(jnp.float32)).astype(x.dtype)


def layernorm(x, g, b):
    xf = x.astype(jnp.float32)
    mu = jnp.mean(xf, axis=-1, keepdims=True)
    var = jnp.mean(jnp.square(xf - mu), axis=-1, keepdims=True)
    return ((xf - mu) * lax.rsqrt(var + EPS) * g.astype(jnp.float32) + b.astype(jnp.float32)).astype(x.dtype)


def swiglu(h, w_gate, w_up, w_down):
    return (jax.nn.silu(h @ w_gate) * (h @ w_up)) @ w_down


def apply_partial_rope(x, pos):
    half = ROT_DIM // 2
    inv = jnp.power(ROPE_THETA, -jnp.arange(half, dtype=jnp.float32) / half)
    ang = pos.astype(jnp.float32)[:, None] * inv[None, :]
    cos = jnp.cos(ang)[:, None, :]
    sin = jnp.sin(ang)[:, None, :]
    xr = x[..., :ROT_DIM].astype(jnp.float32)
    x1, x2 = xr[..., :half], xr[..., half:]
    rot = jnp.concatenate([x1 * cos - x2 * sin, x2 * cos + x1 * sin], axis=-1)
    return jnp.concatenate([rot.astype(x.dtype), x[..., ROT_DIM:]], axis=-1)


def conv_module(hn, left, w_pw1, b_pw1, w_dw, b_dw, ln_g, ln_b, w_pw2, b_pw2):
    u = hn @ w_pw1 + b_pw1
    u = u[..., :D_CONV] * jax.nn.sigmoid(u[..., D_CONV:])
    u_pad = jnp.concatenate([left.astype(u.dtype), u], axis=1)
    y = lax.conv_general_dilated(u_pad, w_dw[:, None, :].astype(u.dtype), window_strides=(1,), padding='VALID',
                                 dimension_numbers=('NWC', 'WIO', 'NWC'), feature_group_count=D_CONV) + b_dw
    y = jax.nn.silu(layernorm(y, ln_g, ln_b))
    return y @ w_pw2 + b_pw2, u_pad[:, -(CONV_WIDTH - 1):]


def qkv_project(hn, w_qkv, pos):
    B, T, _ = hn.shape
    qkv = hn @ w_qkv
    nq, nk = N_HEADS * HEAD_DIM, N_KV_HEADS * HEAD_DIM
    q = qkv[..., :nq].reshape(B, T, N_HEADS, HEAD_DIM)
    k = qkv[..., nq:nq + nk].reshape(B, T, N_KV_HEADS, HEAD_DIM)
    v = qkv[..., nq + nk:].reshape(B, T, N_KV_HEADS, HEAD_DIM)
    q = apply_partial_rope(q, pos) * (HEAD_DIM ** -0.5)
    k = apply_partial_rope(k, pos)
    return q, k, v


def to_blocks(k, v):
    B, T = k.shape[0], k.shape[1]
    pad = (-T) % MOBA_BLOCK
    k = jnp.pad(k, ((0, 0), (0, pad), (0, 0), (0, 0)))
    v = jnp.pad(v, ((0, 0), (0, pad), (0, 0), (0, 0)))
    nb = (T + pad) // MOBA_BLOCK
    kb = k.reshape(B, nb, MOBA_BLOCK, N_KV_HEADS, HEAD_DIM).transpose(0, 3, 1, 2, 4)
    vb = v.reshape(B, nb, MOBA_BLOCK, N_KV_HEADS, HEAD_DIM).transpose(0, 3, 1, 2, 4)
    kmean = jnp.mean(kb.astype(jnp.float32), axis=3)
    return kb, vb, kmean


def moba_attend(q, pos, kb, vb, kmean):
    B, Tq = q.shape[0], q.shape[1]
    nb = kb.shape[2]
    k_sel = min(TOP_K_BLOCKS, nb)
    neg = jnp.finfo(jnp.float32).min
    own = pos // MOBA_BLOCK
    qg = q.reshape(B, Tq, N_KV_HEADS, GROUP, HEAD_DIM)
    s_blk = jnp.einsum('bqhgd,bhnd->bqhgn', qg.astype(jnp.float32), kmean)
    past = jnp.arange(nb)[None, :] < own[:, None]
    s_blk = jnp.where(past[None, :, None, None, :], s_blk, neg)
    _, idx = lax.top_k(s_blk, k_sel)
    own_b = jnp.broadcast_to(own[None, :, None, None, None], idx.shape[:-1] + (1,)).astype(jnp.int32)
    idx_all = jnp.concatenate([idx.astype(jnp.int32), own_b], axis=-1)
    slot_ok = jnp.concatenate([idx < own[None, :, None, None, None], jnp.ones(own_b.shape, bool)], axis=-1)
    b_ix = jnp.arange(B)[:, None, None, None, None]
    h_ix = jnp.arange(N_KV_HEADS)[None, None, :, None, None]
    kg = kb[b_ix, h_ix, idx_all]
    vg = vb[b_ix, h_ix, idx_all]
    kpos = idx_all[..., None] * MOBA_BLOCK + jnp.arange(MOBA_BLOCK)
    mask = slot_ok[..., None] & (kpos <= pos[None, :, None, None, None, None])
    s = jnp.einsum('bqhgd,bqhgkjd->bqhgkj', qg, kg, preferred_element_type=jnp.float32)
    s = jnp.where(mask, s, neg).reshape(B, Tq, N_KV_HEADS, GROUP, -1)
    p = jax.nn.softmax(s, axis=-1).reshape(mask.shape).astype(vg.dtype)
    o = jnp.einsum('bqhgkj,bqhgkjd->bqhgd', p, vg)
    return o.reshape(B, Tq, N_HEADS * HEAD_DIM)


def moba_prompt(q, k, v):
    B, S = q.shape[0], q.shape[1]
    kb, vb, kmean = to_blocks(k, v)
    n_c = S // Q_CHUNK
    qc = q.reshape(B, n_c, Q_CHUNK, N_HEADS, HEAD_DIM).transpose(1, 0, 2, 3, 4)
    pc = jnp.arange(S, dtype=jnp.int32).reshape(n_c, Q_CHUNK)
    out = lax.map(lambda a: moba_attend(a[0], a[1], kb, vb, kmean), (qc, pc))
    return out.transpose(1, 0, 2, 3).reshape(B, S, N_HEADS * HEAD_DIM)


def gather_pages(pool, page_table):
    g = pool[page_table]
    return g.reshape(page_table.shape[0], page_table.shape[1] * PAGE_SIZE, N_KV_HEADS, HEAD_DIM)


def moe_swiglu(h, w_router, w_gate, w_up, w_down):
    logits = (h @ w_router).astype(jnp.float32)
    probs = jax.nn.softmax(logits, axis=-1)
    top_p, top_i = lax.top_k(probs, TOP_K_EXPERTS)
    top_p = top_p / jnp.sum(top_p, axis=-1, keepdims=True)
    gates = jnp.sum(jax.nn.one_hot(top_i, N_EXPERTS, dtype=jnp.float32) * top_p[..., None], axis=-2)
    out = jnp.zeros_like(h)
    for e in range(N_EXPERTS):
        out = out + gates[..., e:e + 1].astype(h.dtype) * swiglu(h, w_gate[e], w_up[e], w_down[e])
    return out


def ple_add(h, p, g, w_gate, w_proj):
    gate = jax.nn.sigmoid((rmsnorm(h, g) @ w_gate).astype(jnp.float32)).astype(h.dtype)
    return h + gate * (p @ w_proj)


def setup_inputs(seed: int = 0) -> dict:
    key = jax.random.key(seed)
    ks = iter(jax.random.split(key, 40))
    f32 = jnp.float32
    n_even = (DEPTH + 1) // 2
    n_odd = DEPTH // 2
    n_pages = PAST_LEN // PAGE_SIZE
    n_used = DEC_BATCH * n_pages
    n_pool = n_used + max(1, n_used // 4)
    qkv_w = (N_HEADS + 2 * N_KV_HEADS) * HEAD_DIM

    def nrm(shape, scale=1.0):
        return jax.random.normal(next(ks), shape, f32) * scale

    d = {}
    d["x_prompt"] = nrm((BATCH, SEQ, D_MODEL))
    d["x_sample"] = nrm((DEC_BATCH, DEC_SEQ, D_MODEL))
    d["state_conv"] = nrm((n_even, DEC_BATCH, CONV_WIDTH - 1, D_CONV), 0.5)
    d["cache_k"] = nrm((n_odd, n_pool, PAGE_SIZE, N_KV_HEADS, HEAD_DIM))
    d["cache_v"] = nrm((n_odd, n_pool, PAGE_SIZE, N_KV_HEADS, HEAD_DIM))
    d["page_table"] = jax.random.permutation(next(ks), n_pool)[:n_used].reshape(DEC_BATCH, n_pages).astype(jnp.int32)
    d["p_prompt"] = nrm((DEPTH, BATCH, SEQ, PLE_DIM))
    d["p_sample"] = nrm((DEPTH, DEC_BATCH, DEC_SEQ, PLE_DIM))
    d["norm_mix"] = 1.0 + nrm((DEPTH, D_MODEL), 0.02)
    d["norm_ffn"] = 1.0 + nrm((DEPTH, D_MODEL), 0.02)
    d["norm_ple"] = 1.0 + nrm((DEPTH, D_MODEL), 0.02)
    d["ple_w_gate"] = nrm((DEPTH, D_MODEL, D_MODEL), D_MODEL ** -0.5)
    d["ple_w_proj"] = nrm((DEPTH, PLE_DIM, D_MODEL), PLE_DIM ** -0.5)
    d["conv_w_pw1"] = nrm((n_even, D_MODEL, 2 * D_CONV), D_MODEL ** -0.5)
    d["conv_b_pw1"] = nrm((n_even, 2 * D_CONV), 0.02)
    d["conv_w_dw"] = nrm((n_even, CONV_WIDTH, D_CONV), CONV_WIDTH ** -0.5)
    d["conv_b_dw"] = nrm((n_even, D_CONV), 0.02)
    d["conv_ln_g"] = 1.0 + nrm((n_even, D_CONV), 0.02)
    d["conv_ln_b"] = nrm((n_even, D_CONV), 0.02)
    d["conv_w_pw2"] = nrm((n_even, D_CONV, D_MODEL), D_CONV ** -0.5)
    d["conv_b_pw2"] = nrm((n_even, D_MODEL), 0.02)
    d["ffn_w_gate"] = nrm((n_even, D_MODEL, D_FF), D_MODEL ** -0.5)
    d["ffn_w_up"] = nrm((n_even, D_MODEL, D_FF), D_MODEL ** -0.5)
    d["ffn_w_down"] = nrm((n_even, D_FF, D_MODEL), D_FF ** -0.5)
    d["attn_w_qkv"] = nrm((n_odd, D_MODEL, qkv_w), D_MODEL ** -0.5)
    d["attn_w_o"] = nrm((n_odd, N_HEADS * HEAD_DIM, D_MODEL), (N_HEADS * HEAD_DIM) ** -0.5)
    d["moe_w_router"] = nrm((n_odd, D_MODEL, N_EXPERTS), D_MODEL ** -0.5)
    d["moe_w_gate"] = nrm((n_odd, N_EXPERTS, D_MODEL, D_FF), D_MODEL ** -0.5)
    d["moe_w_up"] = nrm((n_odd, N_EXPERTS, D_MODEL, D_FF), D_MODEL ** -0.5)
    d["moe_w_down"] = nrm((n_odd, N_EXPERTS, D_FF, D_MODEL), D_FF ** -0.5)
    d["norm_final"] = 1.0 + nrm((D_MODEL,), 0.02)
    return d


def reference(x_prompt, x_sample, state_conv, cache_k, cache_v, page_table, p_prompt, p_sample,
              norm_mix, norm_ffn, norm_ple, ple_w_gate, ple_w_proj,
              conv_w_pw1, conv_b_pw1, conv_w_dw, conv_b_dw, conv_ln_g, conv_ln_b, conv_w_pw2, conv_b_pw2,
              ffn_w_gate, ffn_w_up, ffn_w_down,
              attn_w_qkv, attn_w_o,
              moe_w_router, moe_w_gate, moe_w_up, moe_w_down,
              norm_final):
    hp, hs = x_prompt, x_sample
    pos_p = jnp.arange(x_prompt.shape[1], dtype=jnp.int32)
    pos_s = PAST_LEN + jnp.arange(x_sample.shape[1], dtype=jnp.int32)
    conv_p, conv_s, k_p, v_p, k_s, v_s = [], [], [], [], [], []
    for i in range(DEPTH):
        j = i // 2
        np_ = rmsnorm(hp, norm_mix[i])
        ns_ = rmsnorm(hs, norm_mix[i])
        if i % 2 == 0:
            cw = (conv_w_pw1[j], conv_b_pw1[j], conv_w_dw[j], conv_b_dw[j], conv_ln_g[j], conv_ln_b[j],
                  conv_w_pw2[j], conv_b_pw2[j])
            zeros = jnp.zeros((hp.shape[0], CONV_WIDTH - 1, D_CONV), hp.dtype)
            mp, cp = conv_module(np_, zeros, *cw)
            ms, cs = conv_module(ns_, state_conv[j], *cw)
            conv_p.append(cp)
            conv_s.append(cs)
            hp = hp + mp
            hs = hs + ms
            hp = hp + swiglu(rmsnorm(hp, norm_ffn[i]), ffn_w_gate[j], ffn_w_up[j], ffn_w_down[j])
            hs = hs + swiglu(rmsnorm(hs, norm_ffn[i]), ffn_w_gate[j], ffn_w_up[j], ffn_w_down[j])
        else:
            qp, kp, vp = qkv_project(np_, attn_w_qkv[j], pos_p)
            hp = hp + moba_prompt(qp, kp, vp) @ attn_w_o[j]
            qs, kn, vn = qkv_project(ns_, attn_w_qkv[j], pos_s)
            k_all = jnp.concatenate([gather_pages(cache_k[j], page_table).astype(kn.dtype), kn], axis=1)
            v_all = jnp.concatenate([gather_pages(cache_v[j], page_table).astype(vn.dtype), vn], axis=1)
            kb, vb, kmean = to_blocks(k_all, v_all)
            hs = hs + moba_attend(qs, pos_s, kb, vb, kmean) @ attn_w_o[j]
            k_p.append(kp)
            v_p.append(vp)
            k_s.append(kn)
            v_s.append(vn)
            hp = hp + moe_swiglu(rmsnorm(hp, norm_ffn[i]), moe_w_router[j], moe_w_gate[j], moe_w_up[j], moe_w_down[j])
            hs = hs + moe_swiglu(rmsnorm(hs, norm_ffn[i]), moe_w_router[j], moe_w_gate[j], moe_w_up[j], moe_w_down[j])
        hp = ple_add(hp, p_prompt[i], norm_ple[i], ple_w_gate[i], ple_w_proj[i])
        hs = ple_add(hs, p_sample[i], norm_ple[i], ple_w_gate[i], ple_w_proj[i])
    return (rmsnorm(hp, norm_final), rmsnorm(hs, norm_final),
            jnp.stack(conv_p), jnp.stack(conv_s),
            jnp.stack(k_p), jnp.stack(v_p), jnp.stack(k_s), jnp.stack(v_s))
```

```python
import functools

import jax
import jax.numpy as jnp
from jax import lax
from jax.experimental import pallas as pl
from jax.experimental.pallas import tpu as pltpu

F32 = jnp.float32
BF16 = jnp.bfloat16

EPS = 1e-6
HEAD_DIM = 128
ROT_DIM = HEAD_DIM // 4
ROPE_THETA = 500000.0
MOBA_BLOCK = 256
TOP_K_BLOCKS = 3
TOP_K_EXPERTS = 2
LANES = 128
MASK_VALUE = -1e30
VMEM_LIMIT = 56 * 1024 * 1024


def _params(*sem):
    return pltpu.CompilerParams(dimension_semantics=sem, vmem_limit_bytes=VMEM_LIMIT)


def _rms(x, g):
    r = lax.rsqrt(jnp.mean(x * x, axis=-1, keepdims=True) + EPS)
    return x * r * g


def _dot(a, b):
    return jnp.dot(a, b, preferred_element_type=F32)


def _dot_t(a, b, precision=None):
    return lax.dot_general(a, b, (((1,), (1,)), ((), ())), preferred_element_type=F32, precision=precision)


def _silu(x):
    return x * jax.nn.sigmoid(x)


def _ple(h, p, g, w_gate, w_proj):
    gate = jax.nn.sigmoid(_dot(_rms(h, g).astype(BF16), w_gate))
    return h + gate * _dot(p.astype(BF16), w_proj)


def _rope_head(x, cos, sin, lane):
    half = ROT_DIM // 2
    partner = jnp.where(lane < half, pltpu.roll(x, HEAD_DIM - half, 1), pltpu.roll(x, half, 1))
    return x * cos + partner * sin


CONV_HALO = 32
CONV_CHUNK = 16


def _conv_prompt_kernel(x_ref, g_ref, w1_ref, b1_ref, wdw_ref, bdw_ref, lng_ref, lnb_ref, w2_ref, b2_ref,
                        out_ref, state_ref, upad_ref, y_ref, *, tm, width, dc):
    t = pl.program_id(1)

    @pl.when(t == 0)
    def _():
        upad_ref[0:CONV_HALO, :] = jnp.zeros((CONV_HALO, dc), F32)

    x = x_ref[0]
    uu = _dot(_rms(x, g_ref[...]).astype(BF16), w1_ref[...]) + b1_ref[...]
    upad_ref[CONV_HALO:CONV_HALO + tm, :] = uu[:, :dc] * jax.nn.sigmoid(uu[:, dc:])

    first = CONV_HALO - (width - 1)

    def chunk(i, carry):
        base = pl.multiple_of(i * CONV_CHUNK, CONV_CHUNK)
        win = upad_ref[pl.ds(base, CONV_CHUNK + CONV_HALO), :]
        acc = jnp.broadcast_to(bdw_ref[...], (CONV_CHUNK, dc))
        for k in range(width):
            acc = acc + win[first + k:first + k + CONV_CHUNK, :] * wdw_ref[k:k + 1, :]
        mu = jnp.mean(acc, axis=-1, keepdims=True)
        d = acc - mu
        var = jnp.mean(d * d, axis=-1, keepdims=True)
        z = d * lax.rsqrt(var + EPS) * lng_ref[...] + lnb_ref[...]
        y_ref[pl.ds(base, CONV_CHUNK), :] = _silu(z).astype(BF16)
        return carry

    lax.fori_loop(0, tm // CONV_CHUNK, chunk, 0)
    out_ref[0] = x + _dot(y_ref[...], w2_ref[...]) + b2_ref[...]
    state_ref[0] = upad_ref[tm + first:tm + CONV_HALO, :]
    upad_ref[0:CONV_HALO, :] = upad_ref[tm:tm + CONV_HALO, :]


def _conv_prompt(x, g, w1, b1, wdw, bdw, lng, lnb, w2, b2, *, tm):
    bsz, seq, d = x.shape
    width, dc = wdw.shape
    assert seq % tm == 0 and tm % CONV_CHUNK == 0 and width - 1 <= CONV_HALO
    row = lambda a: a.reshape(1, -1)
    full = lambda a: pl.BlockSpec(a.shape, lambda b, t: (0,) * a.ndim)
    args = (row(g), w1, row(b1), wdw, row(bdw), row(lng), row(lnb), w2, row(b2))
    return pl.pallas_call(
        functools.partial(_conv_prompt_kernel, tm=tm, width=width, dc=dc),
        grid=(bsz, seq // tm),
        in_specs=[pl.BlockSpec((1, tm, d), lambda b, t: (b, t, 0))] + [full(a) for a in args],
        out_specs=[pl.BlockSpec((1, tm, d), lambda b, t: (b, t, 0)),
                   pl.BlockSpec((1, width - 1, dc), lambda b, t: (b, 0, 0))],
        out_shape=[jax.ShapeDtypeStruct((bsz, seq, d), F32),
                   jax.ShapeDtypeStruct((bsz, width - 1, dc), F32)],
        scratch_shapes=[pltpu.VMEM((tm + CONV_HALO, dc), F32), pltpu.VMEM((tm, dc), BF16)],
        compiler_params=_params("parallel", "arbitrary"),
        name="conv_prompt",
    )(x, *args)


def _conv_sample_kernel(x_ref, st_ref, g_ref, w1_ref, b1_ref, wdw_ref, bdw_ref, lng_ref, lnb_ref, w2_ref, b2_ref,
                        out_ref, state_ref, *, width, dc):
    nb = x_ref.shape[0]
    x = x_ref[...]
    uu = _dot(_rms(x, g_ref[...]).astype(BF16), w1_ref[...]) + b1_ref[...]
    u = uu[:, :dc] * jax.nn.sigmoid(uu[:, dc:])
    st = st_ref[...]
    acc = jnp.sum(st * wdw_ref[0:width - 1, :][None], axis=1) + u * wdw_ref[width - 1:width, :] + bdw_ref[...]
    mu = jnp.mean(acc, axis=-1, keepdims=True)
    d = acc - mu
    var = jnp.mean(d * d, axis=-1, keepdims=True)
    z = d * lax.rsqrt(var + EPS) * lng_ref[...] + lnb_ref[...]
    out_ref[...] = x + _dot(_silu(z).astype(BF16), w2_ref[...]) + b2_ref[...]
    state_ref[:, 0:width - 2, :] = st_ref[:, 1:width - 1, :]
    for b in range(nb):
        state_ref[b, width - 2:width - 1, :] = u[b:b + 1, :]


def _conv_sample(x, st, g, w1, b1, wdw, bdw, lng, lnb, w2, b2):
    nb, d = x.shape
    width, dc = wdw.shape
    row = lambda a: a.reshape(1, -1)
    return pl.pallas_call(
        functools.partial(_conv_sample_kernel, width=width, dc=dc),
        out_shape=[jax.ShapeDtypeStruct((nb, d), F32), jax.ShapeDtypeStruct((nb, width - 1, dc), F32)],
        compiler_params=pltpu.CompilerParams(vmem_limit_bytes=VMEM_LIMIT),
        name="conv_sample",
    )(x, st, row(g), w1, row(b1), wdw, row(bdw), row(lng), row(lnb), w2, row(b2))


def _ffn_kernel(x_ref, g_ref, wg_ref, wu_ref, wd_ref, p_ref, gp_ref, wpg_ref, wpp_ref, out_ref, xn_ref, acc_ref):
    f = pl.program_id(1)

    @pl.when(f == 0)
    def _():
        xn_ref[...] = _rms(x_ref[...], g_ref[...]).astype(BF16)
        acc_ref[...] = jnp.zeros(acc_ref.shape, F32)

    xn = xn_ref[...]
    a = (_silu(_dot(xn, wg_ref[...])) * _dot(xn, wu_ref[...])).astype(BF16)
    acc_ref[...] += _dot(a, wd_ref[...])

    @pl.when(f == pl.num_programs(1) - 1)
    def _():
        out_ref[...] = _ple(x_ref[...] + acc_ref[...], p_ref[...], gp_ref[...], wpg_ref[...], wpp_ref[...])


def _ffn(x, g, wg, wu, wd, p, gp, wpg, wpp, *, tm, tf):
    n, d = x.shape
    dff = wg.shape[1]
    assert n % tm == 0 and dff % tf == 0
    row = lambda a: a.reshape(1, -1)
    full = lambda a: pl.BlockSpec(a.shape, lambda i, f: (0,) * a.ndim)
    return pl.pallas_call(
        _ffn_kernel,
        grid=(n // tm, dff // tf),
        in_specs=[pl.BlockSpec((tm, d), lambda i, f: (i, 0)), full(row(g)),
                  pl.BlockSpec((d, tf), lambda i, f: (0, f)), pl.BlockSpec((d, tf), lambda i, f: (0, f)),
                  pl.BlockSpec((tf, d), lambda i, f: (f, 0)),
                  pl.BlockSpec((tm, p.shape[1]), lambda i, f: (i, 0)), full(row(gp)), full(wpg), full(wpp)],
        out_specs=pl.BlockSpec((tm, d), lambda i, f: (i, 0)),
        out_shape=jax.ShapeDtypeStruct((n, d), F32),
        scratch_shapes=[pltpu.VMEM((tm, d), BF16), pltpu.VMEM((tm, d), F32)],
        compiler_params=_params("parallel", "arbitrary"),
        name="ffn_ple",
    )(x, row(g), wg, wu, wd, p, row(gp), wpg, wpp)


def _qkv_prompt_kernel(x_ref, g_ref, w_ref, cos_ref, sin_ref,
                       q_ref, k_ref, v_ref, kb_ref, vb_ref, sel_ref, kmt_ref, *, tm, nh, nkv, nblk):
    t = pl.program_id(1)
    group = nh // nkv
    nq = nh * HEAD_DIM
    nk = nkv * HEAD_DIM

    @pl.when(t == 0)
    def _():
        kmt_ref[...] = jnp.zeros(kmt_ref.shape, F32)

    qkv = _dot(_rms(x_ref[0], g_ref[...]).astype(BF16), w_ref[...])
    cos = cos_ref[...]
    sin = sin_ref[...]
    lane = lax.broadcasted_iota(jnp.int32, (tm, HEAD_DIM), 1)
    scale = HEAD_DIM ** -0.5
    q = jnp.concatenate([_rope_head(qkv[:, h * HEAD_DIM:(h + 1) * HEAD_DIM], cos, sin, lane) * scale
                         for h in range(nh)], axis=1)
    k = jnp.concatenate([_rope_head(qkv[:, nq + h * HEAD_DIM:nq + (h + 1) * HEAD_DIM], cos, sin, lane)
                         for h in range(nkv)], axis=1)
    v = qkv[:, nq + nk:]
    q_ref[0] = q.astype(BF16)
    k_ref[0] = k
    v_ref[0] = v
    kb_ref[0] = k.astype(BF16)
    vb_ref[0] = v.astype(BF16)

    kmt = kmt_ref[...]
    kmt_row = lax.broadcasted_iota(jnp.int32, kmt.shape, 0)
    kmt_head = lax.broadcasted_iota(jnp.int32, kmt.shape, 1) // HEAD_DIM
    for i in range(tm // MOBA_BLOCK):
        km = jnp.mean(k[i * MOBA_BLOCK:(i + 1) * MOBA_BLOCK, :], axis=0, keepdims=True)
        km = jnp.concatenate([km[:, (h // group) * HEAD_DIM:(h // group + 1) * HEAD_DIM] for h in range(nh)], axis=1)
        n = t * (tm // MOBA_BLOCK) + i
        kmt = jnp.where(kmt_row == kmt_head * nblk + n, km, kmt)
    kmt_ref[...] = kmt

    s = _dot_t(q, kmt, precision=lax.Precision.HIGHEST)
    width = nh * nblk
    col = lax.broadcasted_iota(jnp.int32, (tm, width), 1)
    n_idx = col % nblk
    own = (t * tm + lax.broadcasted_iota(jnp.int32, (tm, width), 0)) // MOBA_BLOCK
    rank = jnp.zeros((tm, width), jnp.int32)
    for dlt in range(1, nblk):
        wrap = n_idx + dlt >= nblk
        other = jnp.where(wrap, pltpu.roll(s, nblk - dlt, 1), pltpu.roll(s, width - dlt, 1))
        m_idx = jnp.where(wrap, n_idx + dlt - nblk, n_idx + dlt)
        beats = (m_idx < own) & ((other > s) | ((other == s) & (m_idx < n_idx)))
        rank = rank + beats.astype(jnp.int32)
    sel_ref[0] = ((n_idx < own) & (rank < TOP_K_BLOCKS)).astype(F32)


def _qkv_prompt(x, g, w, cos, sin, *, tm, nh, nkv):
    bsz, seq, d = x.shape
    nq, nk = nh * HEAD_DIM, nkv * HEAD_DIM
    nblk = seq // MOBA_BLOCK
    assert seq % tm == 0 and tm % MOBA_BLOCK == 0 and nh * nblk == LANES
    tok = lambda c: pl.BlockSpec((1, tm, c), lambda b, t: (b, t, 0))
    full = lambda a: pl.BlockSpec(a.shape, lambda b, t: (0,) * a.ndim)
    g = g.reshape(1, -1)
    return pl.pallas_call(
        functools.partial(_qkv_prompt_kernel, tm=tm, nh=nh, nkv=nkv, nblk=nblk),
        grid=(bsz, seq // tm),
        in_specs=[tok(d), full(g), full(w),
                  pl.BlockSpec((tm, HEAD_DIM), lambda b, t: (t, 0)), pl.BlockSpec((tm, HEAD_DIM), lambda b, t: (t, 0))],
        out_specs=[tok(nq), tok(nk), tok(nk), tok(nk), tok(nk), tok(nh * nblk)],
        out_shape=[jax.ShapeDtypeStruct((bsz, seq, nq), BF16),
                   jax.ShapeDtypeStruct((bsz, seq, nk), F32), jax.ShapeDtypeStruct((bsz, seq, nk), F32),
                   jax.ShapeDtypeStruct((bsz, seq, nk), BF16), jax.ShapeDtypeStruct((bsz, seq, nk), BF16),
                   jax.ShapeDtypeStruct((bsz, seq, nh * nblk), F32)],
        scratch_shapes=[pltpu.VMEM((nh * nblk, nq), F32)],
        compiler_params=_params("parallel", "arbitrary"),
        name="qkv_prompt",
    )(x, g, w, cos, sin)


def _qkv_sample_kernel(x_ref, g_ref, w_ref, cos_ref, sin_ref, q_ref, k_ref, v_ref, *, nh, nkv):
    nb = x_ref.shape[0]
    nq = nh * HEAD_DIM
    nk = nkv * HEAD_DIM
    qkv = _dot(_rms(x_ref[...], g_ref[...]).astype(BF16), w_ref[...])
    cos = cos_ref[...]
    sin = sin_ref[...]
    lane = lax.broadcasted_iota(jnp.int32, (nb, HEAD_DIM), 1)
    scale = HEAD_DIM ** -0.5
    q_ref[...] = jnp.concatenate([_rope_head(qkv[:, h * HEAD_DIM:(h + 1) * HEAD_DIM], cos, sin, lane) * scale
                                  for h in range(nh)], axis=1)
    k_ref[...] = jnp.concatenate([_rope_head(qkv[:, nq + h * HEAD_DIM:nq + (h + 1) * HEAD_DIM], cos, sin, lane)
                                  for h in range(nkv)], axis=1)
    v_ref[...] = qkv[:, nq + nk:]


def _qkv_sample(x, g, w, cos, sin, *, nh, nkv):
    nb = x.shape[0]
    nq, nk = nh * HEAD_DIM, nkv * HEAD_DIM
    return pl.pallas_call(
        functools.partial(_qkv_sample_kernel, nh=nh, nkv=nkv),
        out_shape=[jax.ShapeDtypeStruct((nb, nq), F32), jax.ShapeDtypeStruct((nb, nk), F32),
                   jax.ShapeDtypeStruct((nb, nk), F32)],
        compiler_params=pltpu.CompilerParams(vmem_limit_bytes=VMEM_LIMIT),
        name="qkv_sample",
    )(x, g.reshape(1, -1), w, cos, sin)


def _attn_prompt_kernel(q_ref, kb_ref, vb_ref, sel_ref, h_ref, wo_ref, out_ref, *, nh, nkv, nblk):
    o = pl.program_id(1)
    hk = pl.program_id(2)
    group = nh // nkv
    blk = MOBA_BLOCK
    q = jnp.concatenate([q_ref[0, :, g * HEAD_DIM:(g + 1) * HEAD_DIM] for g in range(group)], axis=0)
    rows = group * blk

    own0 = pl.multiple_of(o * blk, blk)
    s = _dot_t(q, kb_ref[0, pl.ds(own0, blk), :])
    qpos = lax.broadcasted_iota(jnp.int32, (rows, blk), 0) % blk
    kpos = lax.broadcasted_iota(jnp.int32, (rows, blk), 1)
    s = jnp.where(kpos <= qpos, s, MASK_VALUE)
    m = jnp.max(s, axis=-1, keepdims=True)
    p = jnp.exp(s - m)
    l = jnp.sum(p, axis=-1, keepdims=True)
    acc = _dot(p.astype(BF16), vb_ref[0, pl.ds(own0, blk), :])

    sel = sel_ref[0]
    lane = lax.broadcasted_iota(jnp.int32, sel.shape, 1)

    def past(j, carry):
        m, l, acc = carry
        cols = [jnp.sum(jnp.where(lane == (hk * group + g) * nblk + j, sel, 0.0), axis=-1, keepdims=True)
                for g in range(group)]
        keep = jnp.concatenate(cols, axis=0) > 0.5
        j0 = pl.multiple_of(j * blk, blk)
        s = jnp.where(keep, _dot_t(q, kb_ref[0, pl.ds(j0, blk), :]), MASK_VALUE)
        m_new = jnp.maximum(m, jnp.max(s, axis=-1, keepdims=True))
        alpha = jnp.exp(m - m_new)
        p = jnp.exp(s - m_new)
        l = alpha * l + jnp.sum(p, axis=-1, keepdims=True)
        acc = alpha * acc + _dot(p.astype(BF16), vb_ref[0, pl.ds(j0, blk), :])
        return m_new, l, acc

    m, l, acc = lax.fori_loop(0, o, past, (m, l, acc))
    att = acc / l
    att = jnp.concatenate([att[g * blk:(g + 1) * blk, :] for g in range(group)], axis=1).astype(BF16)
    proj = _dot(att, wo_ref[...])

    @pl.when(hk == 0)
    def _():
        out_ref[0] = h_ref[0] + proj

    @pl.when(hk != 0)
    def _():
        out_ref[0] += proj


def _attn_prompt(q, kb, vb, sel, h, wo, *, nh, nkv):
    bsz, seq, d = h.shape
    nblk = seq // MOBA_BLOCK
    gw = (nh // nkv) * HEAD_DIM
    return pl.pallas_call(
        functools.partial(_attn_prompt_kernel, nh=nh, nkv=nkv, nblk=nblk),
        grid=(bsz, nblk, nkv),
        in_specs=[pl.BlockSpec((1, MOBA_BLOCK, gw), lambda b, o, k: (b, o, k)),
                  pl.BlockSpec((1, seq, HEAD_DIM), lambda b, o, k: (b, 0, k)),
                  pl.BlockSpec((1, seq, HEAD_DIM), lambda b, o, k: (b, 0, k)),
                  pl.BlockSpec((1, MOBA_BLOCK, sel.shape[2]), lambda b, o, k: (b, o, 0)),
                  pl.BlockSpec((1, MOBA_BLOCK, d), lambda b, o, k: (b, o, 0)),
                  pl.BlockSpec((gw, d), lambda b, o, k: (k, 0))],
        out_specs=pl.BlockSpec((1, MOBA_BLOCK, d), lambda b, o, k: (b, o, 0)),
        out_shape=jax.ShapeDtypeStruct((bsz, seq, d), F32),
        compiler_params=_params("parallel", "parallel", "arbitrary"),
        name="attn_prompt",
    )(q, kb, vb, sel, h, wo)


PAGES_PER_STEP = 8


def _sample_select_kernel(pt_ref, *refs, nh, nkv, ppb, nblk):
    pages = refs[:PAGES_PER_STEP]
    q_ref, idx_ref, km_ref = refs[PAGES_PER_STEP:]
    s = pl.program_id(1)
    group = nh // nkv
    bps = PAGES_PER_STEP // ppb
    rows = pages[0].shape[1] * ppb
    @pl.when(s == 0)
    def _():
        km_ref[...] = jnp.zeros(km_ref.shape, F32)

    km = km_ref[...]
    km_row = lax.broadcasted_iota(jnp.int32, km.shape, 0)
    for i in range(bps):
        tot = sum(jnp.sum(pages[i * ppb + j][0], axis=0, keepdims=True) for j in range(ppb))
        km = jnp.where(km_row == s * bps + i, tot / rows, km)
    km_ref[...] = km

    @pl.when(s == pl.num_programs(1) - 1)
    def _():
        km = km_ref[...]
        q = q_ref[0]
        sc = jnp.concatenate(
            [_dot_t(q[kv * group:(kv + 1) * group, :], km[:, kv * HEAD_DIM:(kv + 1) * HEAD_DIM],
                    precision=lax.Precision.HIGHEST) for kv in range(nkv)], axis=0)
        col = lax.broadcasted_iota(jnp.int32, sc.shape, 1)
        lane = lax.broadcasted_iota(jnp.int32, (nh, LANES), 1)
        out = jnp.zeros((nh, LANES), jnp.int32)
        for r in range(TOP_K_BLOCKS):
            best = jnp.max(sc, axis=-1, keepdims=True)
            pick = jnp.min(jnp.where(sc == best, col, nblk), axis=-1, keepdims=True)
            out = jnp.where(lane == r, pick, out)
            sc = jnp.where(col == pick, -jnp.inf, sc)
        idx_ref[0] = out


def _sample_select(cache_k, page_table, q, *, nh, nkv):
    npool, psize, kw = cache_k.shape
    nb, npages = page_table.shape
    ppb = MOBA_BLOCK // psize
    nblk = npages // ppb
    assert npages % PAGES_PER_STEP == 0 and PAGES_PER_STEP % ppb == 0 and nblk >= TOP_K_BLOCKS
    page_spec = lambda j: pl.BlockSpec((1, psize, kw), lambda b, s, pt: (pt[b * npages + s * PAGES_PER_STEP + j], 0, 0))
    grid_spec = pltpu.PrefetchScalarGridSpec(
        num_scalar_prefetch=1,
        grid=(nb, npages // PAGES_PER_STEP),
        in_specs=[page_spec(j) for j in range(PAGES_PER_STEP)]
        + [pl.BlockSpec((1, nh, HEAD_DIM), lambda b, s, pt: (b, 0, 0))],
        out_specs=pl.BlockSpec((1, nh, LANES), lambda b, s, pt: (b, 0, 0)),
        scratch_shapes=[pltpu.VMEM((nblk, kw), F32)],
    )
    return pl.pallas_call(
        functools.partial(_sample_select_kernel, nh=nh, nkv=nkv, ppb=ppb, nblk=nblk),
        grid_spec=grid_spec,
        out_shape=jax.ShapeDtypeStruct((nb, nh, LANES), jnp.int32),
        compiler_params=_params("parallel", "arbitrary"),
        name="sample_select",
    )(page_table.reshape(-1), *([cache_k] * PAGES_PER_STEP), q.reshape(nb, nh, HEAD_DIM))


def _sample_attn_kernel(pt_ref, idx_ref, q_ref, kn_ref, vn_ref, *refs, ppb):
    kpages = refs[:ppb]
    vpages = refs[ppb:2 * ppb]
    out_ref, m_ref, l_ref, acc_ref = refs[2 * ppb:]
    s = pl.program_id(2)
    q = q_ref[0]

    @pl.when(s == 0)
    def _():
        m_ref[...] = jnp.sum(q * kn_ref[0], axis=-1, keepdims=True)
        l_ref[...] = jnp.ones(l_ref.shape, F32)
        acc_ref[...] = vn_ref[0]

    k = jnp.concatenate([r[0] for r in kpages], axis=0)
    v = jnp.concatenate([r[0] for r in vpages], axis=0)
    sc = jnp.sum(k * q, axis=-1, keepdims=True)
    m_old = m_ref[...]
    m_new = jnp.maximum(m_old, jnp.max(sc, axis=0, keepdims=True))
    alpha = jnp.exp(m_old - m_new)
    p = jnp.exp(sc - m_new)
    l_new = alpha * l_ref[...] + jnp.sum(p, axis=0, keepdims=True)
    acc_new = alpha * acc_ref[...] + jnp.sum(p * v, axis=0, keepdims=True)
    m_ref[...] = m_new
    l_ref[...] = l_new
    acc_ref[...] = acc_new

    @pl.when(s == pl.num_programs(2) - 1)
    def _():
        out_ref[0] = acc_new / l_new


def _sample_attn(cache_k, cache_v, page_table, idx, q, kn, vn, *, nh, nkv):
    npool, psize, kw = cache_k.shape
    nb, npages = page_table.shape
    ppb = MOBA_BLOCK // psize
    group = nh // nkv
    nsel = TOP_K_BLOCKS

    def page_spec(j):
        def imap(b, h, s, pt, ix):
            return (pt[b * npages + ix[(b * nh + h) * nsel + s] * ppb + j], 0, h // group)
        return pl.BlockSpec((1, psize, HEAD_DIM), imap)

    qspec = pl.BlockSpec((1, 1, HEAD_DIM), lambda b, h, s, pt, ix: (b * nh + h, 0, 0))
    kvspec = pl.BlockSpec((1, 1, HEAD_DIM), lambda b, h, s, pt, ix: (b * nkv + h // group, 0, 0))
    grid_spec = pltpu.PrefetchScalarGridSpec(
        num_scalar_prefetch=2,
        grid=(nb, nh, nsel),
        in_specs=[qspec, kvspec, kvspec] + [page_spec(j) for j in range(ppb)] * 2,
        out_specs=qspec,
        scratch_shapes=[pltpu.VMEM((1, 1), F32), pltpu.VMEM((1, 1), F32), pltpu.VMEM((1, HEAD_DIM), F32)],
    )
    out = pl.pallas_call(
        functools.partial(_sample_attn_kernel, ppb=ppb),
        grid_spec=grid_spec,
        out_shape=jax.ShapeDtypeStruct((nb * nh, 1, HEAD_DIM), F32),
        compiler_params=_params("parallel", "parallel", "arbitrary"),
        name="sample_attn",
    )(page_table.reshape(-1), idx.reshape(-1),
      q.reshape(nb * nh, 1, HEAD_DIM), kn.reshape(nb * nkv, 1, HEAD_DIM), vn.reshape(nb * nkv, 1, HEAD_DIM),
      *([cache_k] * ppb), *([cache_v] * ppb))
    return out.reshape(nb, nh * HEAD_DIM)


def _proj_residual_kernel(a_ref, h_ref, w_ref, out_ref):
    out_ref[...] = h_ref[...] + _dot(a_ref[...].astype(BF16), w_ref[...])


def _proj_residual(a, h, w):
    return pl.pallas_call(
        _proj_residual_kernel,
        out_shape=jax.ShapeDtypeStruct(h.shape, F32),
        compiler_params=pltpu.CompilerParams(vmem_limit_bytes=VMEM_LIMIT),
        name="proj_residual",
    )(a, h, w)


def _router_kernel(x_ref, g_ref, wrt_ref, tri_ref, xn_ref, gate_ref, rank_ref, cnt_ref):
    xn = _rms(x_ref[...], g_ref[...])
    xn_ref[...] = xn.astype(BF16)
    logits = _dot_t(wrt_ref[...], xn, precision=lax.Precision.HIGHEST)
    ne = logits.shape[0]
    e = jnp.exp(logits - jnp.max(logits, axis=0, keepdims=True))
    probs = e / jnp.sum(e, axis=0, keepdims=True)
    eid = lax.broadcasted_iota(jnp.int32, probs.shape, 0)
    rest = probs
    member = jnp.zeros(probs.shape, jnp.bool_)
    top_sum = jnp.zeros((1, probs.shape[1]), F32)
    for _ in range(TOP_K_EXPERTS):
        best = jnp.max(rest, axis=0, keepdims=True)
        pick = eid == jnp.min(jnp.where(rest == best, eid, ne), axis=0, keepdims=True)
        member = member | pick
        top_sum = top_sum + best
        rest = jnp.where(pick, -1.0, rest)
    gate_ref[...] = jnp.where(member, probs / top_sum, 0.0)
    mem = jnp.where(member, 1.0, 0.0)
    rank = _dot(mem.astype(BF16), tri_ref[...])
    rank_ref[...] = jnp.where(member, rank, -1.0)
    cnt_ref[0] = jnp.broadcast_to(jnp.sum(mem, axis=1, keepdims=True), cnt_ref.shape[1:])


def _router(x, g, wr, *, tm):
    n, d = x.shape
    ne = wr.shape[1]
    nt = n // tm
    assert n % tm == 0
    tri = (lax.broadcasted_iota(jnp.int32, (tm, tm), 0) < lax.broadcasted_iota(jnp.int32, (tm, tm), 1)).astype(BF16)
    full = lambda a: pl.BlockSpec(a.shape, lambda i: (0,) * a.ndim)
    g = g.reshape(1, -1)
    wrt = wr.T
    xn, gate, rank, cnt = pl.pallas_call(
        _router_kernel,
        grid=(nt,),
        in_specs=[pl.BlockSpec((tm, d), lambda i: (i, 0)), full(g), full(wrt), full(tri)],
        out_specs=[pl.BlockSpec((tm, d), lambda i: (i, 0)), pl.BlockSpec((ne, tm), lambda i: (0, i)),
                   pl.BlockSpec((ne, tm), lambda i: (0, i)), pl.BlockSpec((1, ne, LANES), lambda i: (i, 0, 0))],
        out_shape=[jax.ShapeDtypeStruct((n, d), BF16), jax.ShapeDtypeStruct((ne, n), F32),
                   jax.ShapeDtypeStruct((ne, n), F32), jax.ShapeDtypeStruct((nt, ne, LANES), F32)],
        compiler_params=_params("parallel"),
        name="router",
    )(x, g, wrt, tri)
    return xn, gate, rank, cnt[:, :, 0].astype(jnp.int32).reshape(-1)


def _moe_kernel(cnt_ref, xn_ref, gate_ref, rank_ref, wg_ref, wu_ref, wd_ref, h_ref, p_ref, gp_ref, wpg_ref, wpp_ref,
                gf_ref, out_ref, xc_ref, y_ref, acc_ref, *, tm, rc, ne, final_norm):
    i = pl.program_id(0)
    e = pl.program_id(1)
    f = pl.program_id(2)
    nf = pl.num_programs(2)
    nch = (cnt_ref[i * ne + e] + rc - 1) // rc
    rank = rank_ref[pl.ds(e, 1), :]

    def onehot(c):
        slot = lax.broadcasted_iota(jnp.int32, (rc, tm), 0) + c * rc
        return slot.astype(F32) == rank

    @pl.when((e == 0) & (f == 0))
    def _():
        acc_ref[...] = jnp.zeros(acc_ref.shape, F32)

    @pl.when(f == 0)
    def _():
        def compact(c, carry):
            r0 = pl.multiple_of(c * rc, rc)
            sel = jnp.where(onehot(c), 1.0, 0.0).astype(BF16)
            xc_ref[pl.ds(r0, rc), :] = _dot(sel, xn_ref[...]).astype(BF16)
            y_ref[pl.ds(r0, rc), :] = jnp.zeros((rc, y_ref.shape[1]), F32)
            return carry
        lax.fori_loop(0, nch, compact, 0)

    def expert(c, carry):
        r0 = pl.multiple_of(c * rc, rc)
        xc = xc_ref[pl.ds(r0, rc), :]
        a = (_silu(_dot(xc, wg_ref[0])) * _dot(xc, wu_ref[0])).astype(BF16)
        y_ref[pl.ds(r0, rc), :] += _dot(a, wd_ref[0])
        return carry
    lax.fori_loop(0, nch, expert, 0)

    @pl.when(f == nf - 1)
    def _():
        gate = gate_ref[pl.ds(e, 1), :]

        def scatter(c, carry):
            r0 = pl.multiple_of(c * rc, rc)
            w = jnp.where(onehot(c), gate, 0.0).astype(BF16)
            acc_ref[...] += lax.dot_general(w, y_ref[pl.ds(r0, rc), :].astype(BF16), (((0,), (0,)), ((), ())),
                                            preferred_element_type=F32)
            return carry
        lax.fori_loop(0, nch, scatter, 0)

    @pl.when((e == ne - 1) & (f == nf - 1))
    def _():
        hout = _ple(h_ref[...] + acc_ref[...], p_ref[...], gp_ref[...], wpg_ref[...], wpp_ref[...])
        out_ref[...] = _rms(hout, gf_ref[...]) if final_norm else hout


def _moe(h, g, wr, wg, wu, wd, p, gp, wpg, wpp, gf, *, tm, tf, rc, final_norm=True):
    n, d = h.shape
    ne, _, dff = wg.shape
    assert n % tm == 0 and dff % tf == 0 and tm % rc == 0
    xn, gate, rank, cnt = _router(h, g, wr, tm=tm)
    row = lambda a: a.reshape(1, -1)
    full = lambda a: pl.BlockSpec(a.shape, lambda i, e, f, c: (0,) * a.ndim)
    grid_spec = pltpu.PrefetchScalarGridSpec(
        num_scalar_prefetch=1,
        grid=(n // tm, ne, dff // tf),
        in_specs=[pl.BlockSpec((tm, d), lambda i, e, f, c: (i, 0)),
                  pl.BlockSpec((ne, tm), lambda i, e, f, c: (0, i)),
                  pl.BlockSpec((ne, tm), lambda i, e, f, c: (0, i)),
                  pl.BlockSpec((1, d, tf), lambda i, e, f, c: (e, 0, f)),
                  pl.BlockSpec((1, d, tf), lambda i, e, f, c: (e, 0, f)),
                  pl.BlockSpec((1, tf, d), lambda i, e, f, c: (e, f, 0)),
                  pl.BlockSpec((tm, d), lambda i, e, f, c: (i, 0)),
                  pl.BlockSpec((tm, p.shape[1]), lambda i, e, f, c: (i, 0)),
                  full(row(gp)), full(wpg), full(wpp), full(row(gf))],
        out_specs=pl.BlockSpec((tm, d), lambda i, e, f, c: (i, 0)),
        scratch_shapes=[pltpu.VMEM((tm, d), BF16), pltpu.VMEM((tm, d), F32), pltpu.VMEM((tm, d), F32)],
    )
    return pl.pallas_call(
        functools.partial(_moe_kernel, tm=tm, rc=rc, ne=ne, final_norm=final_norm),
        grid_spec=grid_spec,
        out_shape=jax.ShapeDtypeStruct((n, d), F32),
        compiler_params=_params("parallel", "arbitrary", "arbitrary"),
        name="moe_ple",
    )(cnt, xn, gate, rank, wg, wu, wd, h, p, row(gp), wpg, wpp, row(gf))


def _rope_tables(pos):
    half = ROT_DIM // 2
    inv = jnp.power(ROPE_THETA, -jnp.arange(half, dtype=F32) / half)
    ang = pos.astype(F32)[:, None] * inv[None, :]
    cos, sin = jnp.cos(ang), jnp.sin(ang)
    rest = HEAD_DIM - ROT_DIM
    n = pos.shape[0]
    return (jnp.concatenate([cos, cos, jnp.ones((n, rest), F32)], axis=1),
            jnp.concatenate([-sin, sin, jnp.zeros((n, rest), F32)], axis=1))


def _tile(n, pref):
    return pref if n % pref == 0 else n


def kernel(x_prompt, x_sample, state_conv, cache_k, cache_v, page_table, p_prompt, p_sample, norm_mix, norm_ffn, norm_ple, ple_w_gate, ple_w_proj, conv_w_pw1, conv_b_pw1, conv_w_dw, conv_b_dw, conv_ln_g, conv_ln_b, conv_w_pw2, conv_b_pw2, ffn_w_gate, ffn_w_up, ffn_w_down, attn_w_qkv, attn_w_o, moe_w_router, moe_w_gate, moe_w_up, moe_w_down, norm_final):
    bsz, seq, d = x_prompt.shape
    nb, dec_seq, _ = x_sample.shape
    assert dec_seq == 1 and norm_mix.shape[0] == 2
    nh = d // HEAD_DIM
    nkv = (attn_w_qkv.shape[2] // HEAD_DIM - nh) // 2
    n_tok = bsz * seq
    dff = ffn_w_gate.shape[2]
    psize = cache_k.shape[2]
    past_len = page_table.shape[1] * psize
    bf = lambda w: w.astype(BF16)

    cw = (norm_mix[0], bf(conv_w_pw1[0]), conv_b_pw1[0], conv_w_dw[0], conv_b_dw[0], conv_ln_g[0], conv_ln_b[0],
          bf(conv_w_pw2[0]), conv_b_pw2[0])
    hp, conv_p = _conv_prompt(x_prompt, *cw, tm=_tile(seq, 512))
    hs, conv_s = _conv_sample(x_sample.reshape(nb, d), state_conv[0], *cw)
    fw = (norm_ffn[0], bf(ffn_w_gate[0]), bf(ffn_w_up[0]), bf(ffn_w_down[0]))
    pw0 = (norm_ple[0], bf(ple_w_gate[0]), bf(ple_w_proj[0]))
    tf = _tile(dff, 512)
    hp = _ffn(hp.reshape(n_tok, d), *fw, p_prompt[0].reshape(n_tok, -1), *pw0, tm=_tile(n_tok, 1024), tf=tf)
    hs = _ffn(hs, *fw, p_sample[0].reshape(nb, -1), *pw0, tm=nb, tf=tf)

    wqkv, wo = bf(attn_w_qkv[0]), bf(attn_w_o[0])
    cos_p, sin_p = _rope_tables(jnp.arange(seq, dtype=jnp.int32))
    q, k_p, v_p, kb, vb, sel = _qkv_prompt(hp.reshape(bsz, seq, d), norm_mix[1], wqkv, cos_p, sin_p,
                                           tm=_tile(seq, 512), nh=nh, nkv=nkv)
    hp = _attn_prompt(q, kb, vb, sel, hp.reshape(bsz, seq, d), wo, nh=nh, nkv=nkv).reshape(n_tok, d)

    cos_s, sin_s = _rope_tables(jnp.full((1,), past_len, jnp.int32))
    qs, k_s, v_s = _qkv_sample(hs, norm_mix[1], wqkv, cos_s, sin_s, nh=nh, nkv=nkv)
    ck = cache_k[0].reshape(cache_k.shape[1], psize, nkv * HEAD_DIM)
    cv = cache_v[0].reshape(cache_v.shape[1], psize, nkv * HEAD_DIM)
    idx = _sample_select(ck, page_table, qs, nh=nh, nkv=nkv)[:, :, :TOP_K_BLOCKS]
    att_s = _sample_attn(ck, cv, page_table, idx, qs, k_s, v_s, nh=nh, nkv=nkv)
    hs = _proj_residual(att_s, hs, wo)

    mw = (norm_ffn[1], moe_w_router[0], bf(moe_w_gate[0]), bf(moe_w_up[0]), bf(moe_w_down[0]))
    pw1 = (norm_ple[1], bf(ple_w_gate[1]), bf(ple_w_proj[1]), norm_final)
    tm_moe = _tile(n_tok, 1024)
    yp = _moe(hp, *mw, p_prompt[1].reshape(n_tok, -1), *pw1, tm=tm_moe, tf=tf, rc=128)
    ys = _moe(hs, *mw, p_sample[1].reshape(nb, -1), *pw1, tm=nb, tf=tf, rc=nb)

    return (yp.reshape(bsz, seq, d), ys.reshape(nb, 1, d),
            conv_p[None], conv_s[None],
            k_p.reshape(1, bsz, seq, nkv, HEAD_DIM), v_p.reshape(1, bsz, seq, nkv, HEAD_DIM),
            k_s.reshape(1, nb, 1, nkv, HEAD_DIM), v_s.reshape(1, nb, 1, nkv, HEAD_DIM))
```

```python
import functools

import jax
import jax.numpy as jnp
from jax import lax
from jax.experimental import pallas as pl
from jax.experimental.pallas import tpu as pltpu

F32 = jnp.float32
BF16 = jnp.bfloat16

EPS = 1e-6
HEAD_DIM = 128
ROT_DIM = HEAD_DIM // 4
ROPE_THETA = 500000.0
MOBA_BLOCK = 256
TOP_K_BLOCKS = 3
TOP_K_EXPERTS = 2
LANES = 128
SUBLANES = 8
MASK_VALUE = -1e30
VMEM_LIMIT = 56 * 1024 * 1024


def _params(*sem):
    return pltpu.CompilerParams(dimension_semantics=sem, vmem_limit_bytes=VMEM_LIMIT)


def _rms(x, g):
    r = lax.rsqrt(jnp.mean(x * x, axis=-1, keepdims=True) + EPS)
    return x * r * g


def _dot(a, b):
    return jnp.dot(a, b, preferred_element_type=F32)


def _dot_t(a, b, precision=None):
    return lax.dot_general(a, b, (((1,), (1,)), ((), ())), preferred_element_type=F32, precision=precision)


def _silu(x):
    return x * jax.nn.sigmoid(x)


def _ple(h, p, g, w_gate, w_proj):
    gate = jax.nn.sigmoid(_dot(_rms(h, g).astype(BF16), w_gate))
    return h + gate * _dot(p.astype(BF16), w_proj)


def _rope_head(x, cos, sin, lane):
    half = ROT_DIM // 2
    partner = jnp.where(lane < half, pltpu.roll(x, HEAD_DIM - half, 1), pltpu.roll(x, half, 1))
    return x * cos + partner * sin


CONV_HALO = 32
CONV_CHUNK = 16


def _conv_prompt_kernel(x_ref, g_ref, w1_ref, b1_ref, wrep_ref, bdw_ref, lng_ref, lnb_ref, w2_ref, b2_ref,
                        out_ref, state_ref, upad_ref, y_ref, *, tm, width, dc):
    t = pl.program_id(1)

    @pl.when(t == 0)
    def _():
        upad_ref[...] = jnp.zeros(upad_ref.shape, F32)

    x = x_ref[0]
    uu = _dot(_rms(x, g_ref[...]).astype(BF16), w1_ref[...]) + b1_ref[...]
    upad_ref[CONV_HALO:CONV_HALO + tm, :] = uu[:, :dc] * jax.nn.sigmoid(uu[:, dc:])

    first = CONV_HALO - (width - 1)
    span = CONV_CHUNK + SUBLANES

    def chunk(i, carry):
        base = pl.multiple_of(i * CONV_CHUNK, CONV_CHUNK)
        acc = jnp.broadcast_to(bdw_ref[...], (CONV_CHUNK, dc))
        for r in range(SUBLANES):
            part = None
            for k in range(width):
                a, kr = divmod(first + k, SUBLANES)
                if kr != r:
                    continue
                w = wrep_ref[k * SUBLANES:(k + 1) * SUBLANES, :]
                term = upad_ref[pl.ds(base + a * SUBLANES, span), :] * jnp.concatenate([w] * (span // SUBLANES), axis=0)
                part = term if part is None else part + term
            if part is not None:
                acc = acc + part[r:r + CONV_CHUNK, :]
        mu = jnp.mean(acc, axis=-1, keepdims=True)
        d = acc - mu
        var = jnp.mean(d * d, axis=-1, keepdims=True)
        z = d * lax.rsqrt(var + EPS) * lng_ref[...] + lnb_ref[...]
        y_ref[pl.ds(base, CONV_CHUNK), :] = _silu(z).astype(BF16)
        return carry

    lax.fori_loop(0, tm // CONV_CHUNK, chunk, 0)
    out_ref[0] = x + _dot(y_ref[...], w2_ref[...]) + b2_ref[...]
    state_ref[0] = upad_ref[tm + first:tm + CONV_HALO, :]
    upad_ref[0:CONV_HALO, :] = upad_ref[tm:tm + CONV_HALO, :]


def _conv_prompt(x, g, w1, b1, wdw, bdw, lng, lnb, w2, b2, *, tm):
    bsz, seq, d = x.shape
    width, dc = wdw.shape
    assert seq % tm == 0 and tm % CONV_CHUNK == 0 and width - 1 <= CONV_HALO
    row = lambda a: a.reshape(1, -1)
    full = lambda a: pl.BlockSpec(a.shape, lambda b, t: (0,) * a.ndim)
    wrep = jnp.repeat(wdw, SUBLANES, axis=0)
    args = (row(g), w1, row(b1), wrep, row(bdw), row(lng), row(lnb), w2, row(b2))
    return pl.pallas_call(
        functools.partial(_conv_prompt_kernel, tm=tm, width=width, dc=dc),
        grid=(bsz, seq // tm),
        in_specs=[pl.BlockSpec((1, tm, d), lambda b, t: (b, t, 0))] + [full(a) for a in args],
        out_specs=[pl.BlockSpec((1, tm, d), lambda b, t: (b, t, 0)),
                   pl.BlockSpec((1, width - 1, dc), lambda b, t: (b, 0, 0))],
        out_shape=[jax.ShapeDtypeStruct((bsz, seq, d), F32),
                   jax.ShapeDtypeStruct((bsz, width - 1, dc), F32)],
        scratch_shapes=[pltpu.VMEM((tm + CONV_HALO + SUBLANES, dc), F32), pltpu.VMEM((tm, dc), BF16)],
        compiler_params=_params("parallel", "arbitrary"),
        name="conv_prompt",
    )(x, *args)


def _conv_sample_kernel(x_ref, st_ref, g_ref, w1_ref, b1_ref, wdw_ref, bdw_ref, lng_ref, lnb_ref, w2_ref, b2_ref,
                        out_ref, state_ref, *, width, dc):
    nb = x_ref.shape[0]
    x = x_ref[...]
    uu = _dot(_rms(x, g_ref[...]).astype(BF16), w1_ref[...]) + b1_ref[...]
    u = uu[:, :dc] * jax.nn.sigmoid(uu[:, dc:])
    st = st_ref[...]
    acc = jnp.sum(st * wdw_ref[0:width - 1, :][None], axis=1) + u * wdw_ref[width - 1:width, :] + bdw_ref[...]
    mu = jnp.mean(acc, axis=-1, keepdims=True)
    d = acc - mu
    var = jnp.mean(d * d, axis=-1, keepdims=True)
    z = d * lax.rsqrt(var + EPS) * lng_ref[...] + lnb_ref[...]
    out_ref[...] = x + _dot(_silu(z).astype(BF16), w2_ref[...]) + b2_ref[...]
    state_ref[:, 0:width - 2, :] = st_ref[:, 1:width - 1, :]
    for b in range(nb):
        state_ref[b, width - 2:width - 1, :] = u[b:b + 1, :]


def _conv_sample(x, st, g, w1, b1, wdw, bdw, lng, lnb, w2, b2):
    nb, d = x.shape
    width, dc = wdw.shape
    row = lambda a: a.reshape(1, -1)
    return pl.pallas_call(
        functools.partial(_conv_sample_kernel, width=width, dc=dc),
        out_shape=[jax.ShapeDtypeStruct((nb, d), F32), jax.ShapeDtypeStruct((nb, width - 1, dc), F32)],
        compiler_params=pltpu.CompilerParams(vmem_limit_bytes=VMEM_LIMIT),
        name="conv_sample",
    )(x, st, row(g), w1, row(b1), wdw, row(bdw), row(lng), row(lnb), w2, row(b2))


def _ffn_kernel(x_ref, g_ref, wg_ref, wu_ref, wd_ref, p_ref, gp_ref, wpg_ref, wpp_ref, out_ref, xn_ref, acc_ref):
    f = pl.program_id(1)

    @pl.when(f == 0)
    def _():
        xn_ref[...] = _rms(x_ref[...], g_ref[...]).astype(BF16)
        acc_ref[...] = jnp.zeros(acc_ref.shape, F32)

    xn = xn_ref[...]
    a = (_silu(_dot(xn, wg_ref[...])) * _dot(xn, wu_ref[...])).astype(BF16)
    acc_ref[...] += _dot(a, wd_ref[...])

    @pl.when(f == pl.num_programs(1) - 1)
    def _():
        out_ref[...] = _ple(x_ref[...] + acc_ref[...], p_ref[...], gp_ref[...], wpg_ref[...], wpp_ref[...])


def _ffn(x, g, wg, wu, wd, p, gp, wpg, wpp, *, tm, tf):
    n, d = x.shape
    dff = wg.shape[1]
    assert n % tm == 0 and dff % tf == 0
    row = lambda a: a.reshape(1, -1)
    full = lambda a: pl.BlockSpec(a.shape, lambda i, f: (0,) * a.ndim)
    return pl.pallas_call(
        _ffn_kernel,
        grid=(n // tm, dff // tf),
        in_specs=[pl.BlockSpec((tm, d), lambda i, f: (i, 0)), full(row(g)),
                  pl.BlockSpec((d, tf), lambda i, f: (0, f)), pl.BlockSpec((d, tf), lambda i, f: (0, f)),
                  pl.BlockSpec((tf, d), lambda i, f: (f, 0)),
                  pl.BlockSpec((tm, p.shape[1]), lambda i, f: (i, 0)), full(row(gp)), full(wpg), full(wpp)],
        out_specs=pl.BlockSpec((tm, d), lambda i, f: (i, 0)),
        out_shape=jax.ShapeDtypeStruct((n, d), F32),
        scratch_shapes=[pltpu.VMEM((tm, d), BF16), pltpu.VMEM((tm, d), F32)],
        compiler_params=_params("parallel", "arbitrary"),
        name="ffn_ple",
    )(x, row(g), wg, wu, wd, p, row(gp), wpg, wpp)


def _qkv_prompt_kernel(x_ref, g_ref, w_ref, cos_ref, sin_ref,
                       q_ref, k_ref, v_ref, kb_ref, vb_ref, sel_ref, kmt_ref, *, tm, nh, nkv, nblk):
    t = pl.program_id(1)
    group = nh // nkv
    nq = nh * HEAD_DIM
    nk = nkv * HEAD_DIM

    @pl.when(t == 0)
    def _():
        kmt_ref[...] = jnp.zeros(kmt_ref.shape, F32)

    qkv = _dot(_rms(x_ref[0], g_ref[...]).astype(BF16), w_ref[...])
    cos = cos_ref[...]
    sin = sin_ref[...]
    lane = lax.broadcasted_iota(jnp.int32, (tm, HEAD_DIM), 1)
    scale = HEAD_DIM ** -0.5
    q = jnp.concatenate([_rope_head(qkv[:, h * HEAD_DIM:(h + 1) * HEAD_DIM], cos, sin, lane) * scale
                         for h in range(nh)], axis=1)
    k = jnp.concatenate([_rope_head(qkv[:, nq + h * HEAD_DIM:nq + (h + 1) * HEAD_DIM], cos, sin, lane)
                         for h in range(nkv)], axis=1)
    v = qkv[:, nq + nk:]
    q_ref[0] = q.astype(BF16)
    k_ref[0] = k
    v_ref[0] = v
    kb_ref[0] = k.astype(BF16)
    vb_ref[0] = v.astype(BF16)

    kmt = kmt_ref[...]
    kmt_row = lax.broadcasted_iota(jnp.int32, kmt.shape, 0)
    kmt_head = lax.broadcasted_iota(jnp.int32, kmt.shape, 1) // HEAD_DIM
    for i in range(tm // MOBA_BLOCK):
        km = jnp.mean(k[i * MOBA_BLOCK:(i + 1) * MOBA_BLOCK, :], axis=0, keepdims=True)
        km = jnp.concatenate([km[:, (h // group) * HEAD_DIM:(h // group + 1) * HEAD_DIM] for h in range(nh)], axis=1)
        n = t * (tm // MOBA_BLOCK) + i
        kmt = jnp.where(kmt_row == kmt_head * nblk + n, km, kmt)
    kmt_ref[...] = kmt

    s = _dot_t(q, kmt, precision=lax.Precision.HIGHEST)
    width = nh * nblk
    col = lax.broadcasted_iota(jnp.int32, (tm, width), 1)
    n_idx = col % nblk
    own = (t * tm + lax.broadcasted_iota(jnp.int32, (tm, width), 0)) // MOBA_BLOCK
    rank = jnp.zeros((tm, width), jnp.int32)
    for dlt in range(1, nblk):
        wrap = n_idx + dlt >= nblk
        other = jnp.where(wrap, pltpu.roll(s, nblk - dlt, 1), pltpu.roll(s, width - dlt, 1))
        m_idx = jnp.where(wrap, n_idx + dlt - nblk, n_idx + dlt)
        beats = (m_idx < own) & ((other > s) | ((other == s) & (m_idx < n_idx)))
        rank = rank + beats.astype(jnp.int32)
    sel_ref[0] = ((n_idx < own) & (rank < TOP_K_BLOCKS)).astype(F32)


def _qkv_prompt(x, g, w, cos, sin, *, tm, nh, nkv):
    bsz, seq, d = x.shape
    nq, nk = nh * HEAD_DIM, nkv * HEAD_DIM
    nblk = seq // MOBA_BLOCK
    assert seq % tm == 0 and tm % MOBA_BLOCK == 0 and nh * nblk == LANES
    tok = lambda c: pl.BlockSpec((1, tm, c), lambda b, t: (b, t, 0))
    full = lambda a: pl.BlockSpec(a.shape, lambda b, t: (0,) * a.ndim)
    g = g.reshape(1, -1)
    return pl.pallas_call(
        functools.partial(_qkv_prompt_kernel, tm=tm, nh=nh, nkv=nkv, nblk=nblk),
        grid=(bsz, seq // tm),
        in_specs=[tok(d), full(g), full(w),
                  pl.BlockSpec((tm, HEAD_DIM), lambda b, t: (t, 0)), pl.BlockSpec((tm, HEAD_DIM), lambda b, t: (t, 0))],
        out_specs=[tok(nq), tok(nk), tok(nk), tok(nk), tok(nk), tok(nh * nblk)],
        out_shape=[jax.ShapeDtypeStruct((bsz, seq, nq), BF16),
                   jax.ShapeDtypeStruct((bsz, seq, nk), F32), jax.ShapeDtypeStruct((bsz, seq, nk), F32),
                   jax.ShapeDtypeStruct((bsz, seq, nk), BF16), jax.ShapeDtypeStruct((bsz, seq, nk), BF16),
                   jax.ShapeDtypeStruct((bsz, seq, nh * nblk), F32)],
        scratch_shapes=[pltpu.VMEM((nh * nblk, nq), F32)],
        compiler_params=_params("parallel", "arbitrary"),
        name="qkv_prompt",
    )(x, g, w, cos, sin)


def _qkv_sample_kernel(x_ref, g_ref, w_ref, cos_ref, sin_ref, q_ref, k_ref, v_ref, *, nh, nkv):
    nb = x_ref.shape[0]
    nq = nh * HEAD_DIM
    nk = nkv * HEAD_DIM
    qkv = _dot(_rms(x_ref[...], g_ref[...]).astype(BF16), w_ref[...])
    cos = cos_ref[...]
    sin = sin_ref[...]
    lane = lax.broadcasted_iota(jnp.int32, (nb, HEAD_DIM), 1)
    scale = HEAD_DIM ** -0.5
    q_ref[...] = jnp.concatenate([_rope_head(qkv[:, h * HEAD_DIM:(h + 1) * HEAD_DIM], cos, sin, lane) * scale
                                  for h in range(nh)], axis=1)
    k_ref[...] = jnp.concatenate([_rope_head(qkv[:, nq + h * HEAD_DIM:nq + (h + 1) * HEAD_DIM], cos, sin, lane)
                                  for h in range(nkv)], axis=1)
    v_ref[...] = qkv[:, nq + nk:]


def _qkv_sample(x, g, w, cos, sin, *, nh, nkv):
    nb = x.shape[0]
    nq, nk = nh * HEAD_DIM, nkv * HEAD_DIM
    return pl.pallas_call(
        functools.partial(_qkv_sample_kernel, nh=nh, nkv=nkv),
        out_shape=[jax.ShapeDtypeStruct((nb, nq), F32), jax.ShapeDtypeStruct((nb, nk), F32),
                   jax.ShapeDtypeStruct((nb, nk), F32)],
        compiler_params=pltpu.CompilerParams(vmem_limit_bytes=VMEM_LIMIT),
        name="qkv_sample",
    )(x, g.reshape(1, -1), w, cos, sin)


def _attn_prompt_kernel(q_ref, kb_ref, vb_ref, sel_ref, h_ref, wo_ref, out_ref, *, nh, nkv, nblk):
    o = pl.program_id(1)
    hk = pl.program_id(2)
    group = nh // nkv
    blk = MOBA_BLOCK
    q = jnp.concatenate([q_ref[0, :, g * HEAD_DIM:(g + 1) * HEAD_DIM] for g in range(group)], axis=0)
    rows = group * blk

    own0 = pl.multiple_of(o * blk, blk)
    s = _dot_t(q, kb_ref[0, pl.ds(own0, blk), :])
    qpos = lax.broadcasted_iota(jnp.int32, (rows, blk), 0) % blk
    kpos = lax.broadcasted_iota(jnp.int32, (rows, blk), 1)
    s = jnp.where(kpos <= qpos, s, MASK_VALUE)
    m = jnp.max(s, axis=-1, keepdims=True)
    p = jnp.exp(s - m)
    l = jnp.sum(p, axis=-1, keepdims=True)
    acc = _dot(p.astype(BF16), vb_ref[0, pl.ds(own0, blk), :])

    sel = sel_ref[0]
    lane = lax.broadcasted_iota(jnp.int32, sel.shape, 1)
    bias = (sel - 1.0) * (-MASK_VALUE)
    q_aug = jnp.concatenate(
        [q, jnp.concatenate([jnp.where(lane // nblk == hk * group + g, bias, 0.0) for g in range(group)],
                            axis=0).astype(BF16)], axis=1)

    def past(j, carry):
        m, l, acc = carry
        j0 = pl.multiple_of(j * blk, blk)
        pick = jnp.where(lane % nblk == j, 1.0, 0.0).astype(BF16)
        s = _dot_t(q_aug, jnp.concatenate([kb_ref[0, pl.ds(j0, blk), :], pick], axis=1))
        m_new = jnp.maximum(m, jnp.max(s, axis=-1, keepdims=True))
        alpha = jnp.exp(m - m_new)
        p = jnp.exp(s - m_new)
        l = alpha * l + jnp.sum(p, axis=-1, keepdims=True)
        acc = alpha * acc + _dot(p.astype(BF16), vb_ref[0, pl.ds(j0, blk), :])
        return m_new, l, acc

    m, l, acc = lax.fori_loop(0, o, past, (m, l, acc))
    att = acc / l
    att = jnp.concatenate([att[g * blk:(g + 1) * blk, :] for g in range(group)], axis=1).astype(BF16)
    proj = _dot(att, wo_ref[...])

    @pl.when(hk == 0)
    def _():
        out_ref[0] = h_ref[0] + proj

    @pl.when(hk != 0)
    def _():
        out_ref[0] += proj


def _attn_prompt(q, kb, vb, sel, h, wo, *, nh, nkv):
    bsz, seq, d = h.shape
    nblk = seq // MOBA_BLOCK
    gw = (nh // nkv) * HEAD_DIM
    return pl.pallas_call(
        functools.partial(_attn_prompt_kernel, nh=nh, nkv=nkv, nblk=nblk),
        grid=(bsz, nblk, nkv),
        in_specs=[pl.BlockSpec((1, MOBA_BLOCK, gw), lambda b, o, k: (b, o, k)),
                  pl.BlockSpec((1, seq, HEAD_DIM), lambda b, o, k: (b, 0, k)),
                  pl.BlockSpec((1, seq, HEAD_DIM), lambda b, o, k: (b, 0, k)),
                  pl.BlockSpec((1, MOBA_BLOCK, sel.shape[2]), lambda b, o, k: (b, o, 0)),
                  pl.BlockSpec((1, MOBA_BLOCK, d), lambda b, o, k: (b, o, 0)),
                  pl.BlockSpec((gw, d), lambda b, o, k: (k, 0))],
        out_specs=pl.BlockSpec((1, MOBA_BLOCK, d), lambda b, o, k: (b, o, 0)),
        out_shape=jax.ShapeDtypeStruct((bsz, seq, d), F32),
        compiler_params=_params("parallel", "parallel", "arbitrary"),
        name="attn_prompt",
    )(q, kb, vb, sel, h, wo)


PAGES_PER_STEP = 32


def _sample_select_kernel(pt_ref, *refs, nh, nkv, ppb, nblk):
    pages = refs[:PAGES_PER_STEP]
    q_ref, idx_ref, km_ref = refs[PAGES_PER_STEP:]
    s = pl.program_id(1)
    group = nh // nkv
    bps = PAGES_PER_STEP // ppb
    psize = pages[0].shape[1] // nkv

    @pl.when(s == 0)
    def _():
        km_ref[...] = jnp.zeros(km_ref.shape, F32)

    km = km_ref[...]
    km_row = lax.broadcasted_iota(jnp.int32, km.shape, 0)
    for i in range(bps):
        tot = jnp.concatenate(
            [sum(jnp.sum(pages[i * ppb + j][0, pl.ds(kv, psize, stride=nkv), :], axis=0, keepdims=True)
                 for j in range(ppb)) for kv in range(nkv)], axis=1)
        km = jnp.where(km_row == s * bps + i, tot / (psize * ppb), km)
    km_ref[...] = km

    @pl.when(s == pl.num_programs(1) - 1)
    def _():
        km = km_ref[...]
        q = q_ref[0]
        sc = jnp.concatenate(
            [_dot_t(q[kv * group:(kv + 1) * group, :], km[:, kv * HEAD_DIM:(kv + 1) * HEAD_DIM],
                    precision=lax.Precision.HIGHEST) for kv in range(nkv)], axis=0)
        col = lax.broadcasted_iota(jnp.int32, sc.shape, 1)
        lane = lax.broadcasted_iota(jnp.int32, (nh, LANES), 1)
        out = jnp.zeros((nh, LANES), jnp.int32)
        for r in range(TOP_K_BLOCKS):
            best = jnp.max(sc, axis=-1, keepdims=True)
            pick = jnp.min(jnp.where(sc == best, col, nblk), axis=-1, keepdims=True)
            out = jnp.where(lane == r, pick, out)
            sc = jnp.where(col == pick, -jnp.inf, sc)
        idx_ref[0] = out


def _sample_select(cache_k, page_table, q, *, nh, nkv):
    npool, prow, hd = cache_k.shape
    nb, npages = page_table.shape
    ppb = MOBA_BLOCK // (prow // nkv)
    nblk = npages // ppb
    assert npages % PAGES_PER_STEP == 0 and PAGES_PER_STEP % ppb == 0 and nblk >= TOP_K_BLOCKS
    page_spec = lambda j: pl.BlockSpec((1, prow, hd), lambda b, s, pt: (pt[b * npages + s * PAGES_PER_STEP + j], 0, 0))
    grid_spec = pltpu.PrefetchScalarGridSpec(
        num_scalar_prefetch=1,
        grid=(nb, npages // PAGES_PER_STEP),
        in_specs=[page_spec(j) for j in range(PAGES_PER_STEP)]
        + [pl.BlockSpec((1, nh, HEAD_DIM), lambda b, s, pt: (b, 0, 0))],
        out_specs=pl.BlockSpec((1, nh, LANES), lambda b, s, pt: (b, 0, 0)),
        scratch_shapes=[pltpu.VMEM((nblk, nkv * hd), F32)],
    )
    return pl.pallas_call(
        functools.partial(_sample_select_kernel, nh=nh, nkv=nkv, ppb=ppb, nblk=nblk),
        grid_spec=grid_spec,
        out_shape=jax.ShapeDtypeStruct((nb, nh, LANES), jnp.int32),
        compiler_params=_params("parallel", "arbitrary"),
        name="sample_select",
    )(page_table.reshape(-1), *([cache_k] * PAGES_PER_STEP), q.reshape(nb, nh, HEAD_DIM))


def _sample_attn_kernel(pt_ref, idx_ref, q_ref, kn_ref, vn_ref, *refs, group, nkv, npg):
    kpages = refs[:group * npg]
    vpages = refs[group * npg:2 * group * npg]
    out_ref = refs[2 * group * npg]
    hkv = pl.program_id(1)
    prow = kpages[0].shape[1]
    mine = lax.broadcasted_iota(jnp.int32, (prow, 1), 0) % nkv == hkv
    outs = []
    for g in range(group):
        q = q_ref[0, g:g + 1, :]
        m = jnp.sum(q * kn_ref[0], axis=-1, keepdims=True)
        l = jnp.ones((1, 1), F32)
        acc = vn_ref[0]
        for i in range(npg):
            k = kpages[g * npg + i][0]
            sc = jnp.where(mine, jnp.sum(k * q, axis=-1, keepdims=True), MASK_VALUE)
            m_new = jnp.maximum(m, jnp.max(sc, axis=0, keepdims=True))
            alpha = jnp.exp(m - m_new)
            p = jnp.exp(sc - m_new)
            l = alpha * l + jnp.sum(p, axis=0, keepdims=True)
            acc = alpha * acc + jnp.sum(p * vpages[g * npg + i][0], axis=0, keepdims=True)
            m = m_new
        outs.append(acc / l)
    out_ref[0] = jnp.concatenate(outs, axis=0)


def _sample_attn(cache_k, cache_v, page_table, idx, q, kn, vn, *, nh, nkv):
    npool, prow, hd = cache_k.shape
    nb, npages = page_table.shape
    ppb = MOBA_BLOCK // (prow // nkv)
    group = nh // nkv
    nsel = TOP_K_BLOCKS
    npg = nsel * ppb

    def page_spec(g, i):
        def imap(b, kv, pt, ix):
            blk = ix[(b * nh + kv * group + g) * nsel + i // ppb]
            return (pt[b * npages + blk * ppb + i % ppb], 0, 0)
        return pl.BlockSpec((1, prow, hd), imap)

    qspec = pl.BlockSpec((1, group, HEAD_DIM), lambda b, kv, pt, ix: (b * nkv + kv, 0, 0))
    kvspec = pl.BlockSpec((1, 1, HEAD_DIM), lambda b, kv, pt, ix: (b * nkv + kv, 0, 0))
    pages = [page_spec(g, i) for g in range(group) for i in range(npg)]
    grid_spec = pltpu.PrefetchScalarGridSpec(
        num_scalar_prefetch=2,
        grid=(nb, nkv),
        in_specs=[qspec, kvspec, kvspec] + pages + pages,
        out_specs=qspec,
    )
    out = pl.pallas_call(
        functools.partial(_sample_attn_kernel, group=group, nkv=nkv, npg=npg),
        grid_spec=grid_spec,
        out_shape=jax.ShapeDtypeStruct((nb * nkv, group, HEAD_DIM), F32),
        compiler_params=_params("parallel", "parallel"),
        name="sample_attn",
    )(page_table.reshape(-1), idx.reshape(-1),
      q.reshape(nb * nkv, group, HEAD_DIM), kn.reshape(nb * nkv, 1, HEAD_DIM), vn.reshape(nb * nkv, 1, HEAD_DIM),
      *([cache_k] * (group * npg)), *([cache_v] * (group * npg)))
    return out.reshape(nb, nh * HEAD_DIM)


def _proj_residual_kernel(a_ref, h_ref, w_ref, out_ref):
    out_ref[...] = h_ref[...] + _dot(a_ref[...].astype(BF16), w_ref[...])


def _proj_residual(a, h, w):
    return pl.pallas_call(
        _proj_residual_kernel,
        out_shape=jax.ShapeDtypeStruct(h.shape, F32),
        compiler_params=pltpu.CompilerParams(vmem_limit_bytes=VMEM_LIMIT),
        name="proj_residual",
    )(a, h, w)


def _router_kernel(x_ref, g_ref, wrt_ref, tri_ref, xn_ref, gate_ref, rank_ref, cnt_ref):
    xn = _rms(x_ref[...], g_ref[...])
    xn_ref[...] = xn.astype(BF16)
    logits = _dot_t(wrt_ref[...], xn, precision=lax.Precision.HIGHEST)
    ne = logits.shape[0]
    e = jnp.exp(logits - jnp.max(logits, axis=0, keepdims=True))
    probs = e / jnp.sum(e, axis=0, keepdims=True)
    eid = lax.broadcasted_iota(jnp.int32, probs.shape, 0)
    rest = probs
    member = jnp.zeros(probs.shape, jnp.bool_)
    top_sum = jnp.zeros((1, probs.shape[1]), F32)
    for _ in range(TOP_K_EXPERTS):
        best = jnp.max(rest, axis=0, keepdims=True)
        pick = eid == jnp.min(jnp.where(rest == best, eid, ne), axis=0, keepdims=True)
        member = member | pick
        top_sum = top_sum + best
        rest = jnp.where(pick, -1.0, rest)
    gate_ref[...] = jnp.where(member, probs / top_sum, 0.0)
    mem = jnp.where(member, 1.0, 0.0)
    rank = _dot(mem.astype(BF16), tri_ref[...])
    rank_ref[...] = jnp.where(member, rank, -1.0)
    cnt_ref[0] = jnp.broadcast_to(jnp.sum(mem, axis=1, keepdims=True), cnt_ref.shape[1:])


def _router(x, g, wr, *, tm):
    n, d = x.shape
    ne = wr.shape[1]
    nt = n // tm
    assert n % tm == 0
    tri = (lax.broadcasted_iota(jnp.int32, (tm, tm), 0) < lax.broadcasted_iota(jnp.int32, (tm, tm), 1)).astype(BF16)
    full = lambda a: pl.BlockSpec(a.shape, lambda i: (0,) * a.ndim)
    g = g.reshape(1, -1)
    wrt = wr.T
    xn, gate, rank, cnt = pl.pallas_call(
        _router_kernel,
        grid=(nt,),
        in_specs=[pl.BlockSpec((tm, d), lambda i: (i, 0)), full(g), full(wrt), full(tri)],
        out_specs=[pl.BlockSpec((tm, d), lambda i: (i, 0)), pl.BlockSpec((ne, tm), lambda i: (0, i)),
                   pl.BlockSpec((ne, tm), lambda i: (0, i)), pl.BlockSpec((1, ne, LANES), lambda i: (i, 0, 0))],
        out_shape=[jax.ShapeDtypeStruct((n, d), BF16), jax.ShapeDtypeStruct((ne, n), F32),
                   jax.ShapeDtypeStruct((ne, n), F32), jax.ShapeDtypeStruct((nt, ne, LANES), F32)],
        compiler_params=_params("parallel"),
        name="router",
    )(x, g, wrt, tri)
    return xn, gate, rank, cnt[:, :, 0].astype(jnp.int32).reshape(-1)


def _moe_kernel(cnt_ref, xn_ref, gate_ref, rank_ref, wg_ref, wu_ref, wd_ref, h_ref, p_ref, gp_ref, wpg_ref, wpp_ref,
                gf_ref, out_ref, xc_ref, y_ref, acc_ref, *, tm, rc, ne, final_norm):
    i = pl.program_id(0)
    e = pl.program_id(1)
    f = pl.program_id(2)
    nf = pl.num_programs(2)
    nch = (cnt_ref[i * ne + e] + rc - 1) // rc
    rank = rank_ref[pl.ds(e, 1), :]

    def onehot(c):
        slot = lax.broadcasted_iota(jnp.int32, (rc, tm), 0) + c * rc
        return slot.astype(F32) == rank

    @pl.when((e == 0) & (f == 0))
    def _():
        acc_ref[...] = jnp.zeros(acc_ref.shape, F32)

    @pl.when(f == 0)
    def _():
        def compact(c, carry):
            r0 = pl.multiple_of(c * rc, rc)
            sel = jnp.where(onehot(c), 1.0, 0.0).astype(BF16)
            xc_ref[pl.ds(r0, rc), :] = _dot(sel, xn_ref[...]).astype(BF16)
            y_ref[pl.ds(r0, rc), :] = jnp.zeros((rc, y_ref.shape[1]), F32)
            return carry
        lax.fori_loop(0, nch, compact, 0)

    def expert(c, carry):
        r0 = pl.multiple_of(c * rc, rc)
        xc = xc_ref[pl.ds(r0, rc), :]
        a = (_silu(_dot(xc, wg_ref[0])) * _dot(xc, wu_ref[0])).astype(BF16)
        y_ref[pl.ds(r0, rc), :] += _dot(a, wd_ref[0])
        return carry
    lax.fori_loop(0, nch, expert, 0)

    @pl.when(f == nf - 1)
    def _():
        gate = gate_ref[pl.ds(e, 1), :]

        def scatter(c, carry):
            r0 = pl.multiple_of(c * rc, rc)
            w = jnp.where(onehot(c), gate, 0.0).astype(BF16)
            acc_ref[...] += lax.dot_general(w, y_ref[pl.ds(r0, rc), :].astype(BF16), (((0,), (0,)), ((), ())),
                                            preferred_element_type=F32)
            return carry
        lax.fori_loop(0, nch, scatter, 0)

    @pl.when((e == ne - 1) & (f == nf - 1))
    def _():
        hout = _ple(h_ref[...] + acc_ref[...], p_ref[...], gp_ref[...], wpg_ref[...], wpp_ref[...])
        out_ref[...] = _rms(hout, gf_ref[...]) if final_norm else hout


def _moe(h, g, wr, wg, wu, wd, p, gp, wpg, wpp, gf, *, tm, tf, rc, final_norm=True):
    n, d = h.shape
    ne, _, dff = wg.shape
    assert n % tm == 0 and dff % tf == 0 and tm % rc == 0
    xn, gate, rank, cnt = _router(h, g, wr, tm=tm)
    row = lambda a: a.reshape(1, -1)
    full = lambda a: pl.BlockSpec(a.shape, lambda i, e, f, c: (0,) * a.ndim)
    grid_spec = pltpu.PrefetchScalarGridSpec(
        num_scalar_prefetch=1,
        grid=(n // tm, ne, dff // tf),
        in_specs=[pl.BlockSpec((tm, d), lambda i, e, f, c: (i, 0)),
                  pl.BlockSpec((ne, tm), lambda i, e, f, c: (0, i)),
                  pl.BlockSpec((ne, tm), lambda i, e, f, c: (0, i)),
                  pl.BlockSpec((1, d, tf), lambda i, e, f, c: (e, 0, f)),
                  pl.BlockSpec((1, d, tf), lambda i, e, f, c: (e, 0, f)),
                  pl.BlockSpec((1, tf, d), lambda i, e, f, c: (e, f, 0)),
                  pl.BlockSpec((tm, d), lambda i, e, f, c: (i, 0)),
                  pl.BlockSpec((tm, p.shape[1]), lambda i, e, f, c: (i, 0)),
                  full(row(gp)), full(wpg), full(wpp), full(row(gf))],
        out_specs=pl.BlockSpec((tm, d), lambda i, e, f, c: (i, 0)),
        scratch_shapes=[pltpu.VMEM((tm, d), BF16), pltpu.VMEM((tm, d), F32), pltpu.VMEM((tm, d), F32)],
    )
    return pl.pallas_call(
        functools.partial(_moe_kernel, tm=tm, rc=rc, ne=ne, final_norm=final_norm),
        grid_spec=grid_spec,
        out_shape=jax.ShapeDtypeStruct((n, d), F32),
        compiler_params=_params("parallel", "arbitrary", "arbitrary"),
        name="moe_ple",
    )(cnt, xn, gate, rank, wg, wu, wd, h, p, row(gp), wpg, wpp, row(gf))


def _rope_tables(pos):
    half = ROT_DIM // 2
    inv = jnp.power(ROPE_THETA, -jnp.arange(half, dtype=F32) / half)
    ang = pos.astype(F32)[:, None] * inv[None, :]
    cos, sin = jnp.cos(ang), jnp.sin(ang)
    rest = HEAD_DIM - ROT_DIM
    n = pos.shape[0]
    return (jnp.concatenate([cos, cos, jnp.ones((n, rest), F32)], axis=1),
            jnp.concatenate([-sin, sin, jnp.zeros((n, rest), F32)], axis=1))


def _tile(n, pref):
    return pref if n % pref == 0 else n


def kernel(x_prompt, x_sample, state_conv, cache_k, cache_v, page_table, p_prompt, p_sample, norm_mix, norm_ffn, norm_ple, ple_w_gate, ple_w_proj, conv_w_pw1, conv_b_pw1, conv_w_dw, conv_b_dw, conv_ln_g, conv_ln_b, conv_w_pw2, conv_b_pw2, ffn_w_gate, ffn_w_up, ffn_w_down, attn_w_qkv, attn_w_o, moe_w_router, moe_w_gate, moe_w_up, moe_w_down, norm_final):
    bsz, seq, d = x_prompt.shape
    nb, dec_seq, _ = x_sample.shape
    assert dec_seq == 1 and norm_mix.shape[0] == 2
    nh = d // HEAD_DIM
    nkv = (attn_w_qkv.shape[2] // HEAD_DIM - nh) // 2
    n_tok = bsz * seq
    dff = ffn_w_gate.shape[2]
    psize = cache_k.shape[2]
    past_len = page_table.shape[1] * psize
    bf = lambda w: w.astype(BF16)

    cw = (norm_mix[0], bf(conv_w_pw1[0]), conv_b_pw1[0], conv_w_dw[0], conv_b_dw[0], conv_ln_g[0], conv_ln_b[0],
          bf(conv_w_pw2[0]), conv_b_pw2[0])
    hp, conv_p = _conv_prompt(x_prompt, *cw, tm=_tile(seq, 512))
    hs, conv_s = _conv_sample(x_sample.reshape(nb, d), state_conv[0], *cw)
    fw = (norm_ffn[0], bf(ffn_w_gate[0]), bf(ffn_w_up[0]), bf(ffn_w_down[0]))
    pw0 = (norm_ple[0], bf(ple_w_gate[0]), bf(ple_w_proj[0]))
    tf = _tile(dff, 512)
    hp = _ffn(hp.reshape(n_tok, d), *fw, p_prompt[0].reshape(n_tok, -1), *pw0, tm=_tile(n_tok, 1024), tf=tf)
    hs = _ffn(hs, *fw, p_sample[0].reshape(nb, -1), *pw0, tm=nb, tf=tf)

    wqkv, wo = bf(attn_w_qkv[0]), bf(attn_w_o[0])
    cos_p, sin_p = _rope_tables(jnp.arange(seq, dtype=jnp.int32))
    q, k_p, v_p, kb, vb, sel = _qkv_prompt(hp.reshape(bsz, seq, d), norm_mix[1], wqkv, cos_p, sin_p,
                                           tm=_tile(seq, 512), nh=nh, nkv=nkv)
    hp = _attn_prompt(q, kb, vb, sel, hp.reshape(bsz, seq, d), wo, nh=nh, nkv=nkv).reshape(n_tok, d)

    cos_s, sin_s = _rope_tables(jnp.full((1,), past_len, jnp.int32))
    qs, k_s, v_s = _qkv_sample(hs, norm_mix[1], wqkv, cos_s, sin_s, nh=nh, nkv=nkv)
    ck = cache_k[0].reshape(cache_k.shape[1], psize * nkv, HEAD_DIM)
    cv = cache_v[0].reshape(cache_v.shape[1], psize * nkv, HEAD_DIM)
    idx = _sample_select(ck, page_table, qs, nh=nh, nkv=nkv)[:, :, :TOP_K_BLOCKS]
    att_s = _sample_attn(ck, cv, page_table, idx, qs, k_s, v_s, nh=nh, nkv=nkv)
    hs = _proj_residual(att_s, hs, wo)

    mw = (norm_ffn[1], moe_w_router[0], bf(moe_w_gate[0]), bf(moe_w_up[0]), bf(moe_w_down[0]))
    pw1 = (norm_ple[1], bf(ple_w_gate[1]), bf(ple_w_proj[1]), norm_final)
    tm_moe = _tile(n_tok, 1024)
    yp = _moe(hp, *mw, p_prompt[1].reshape(n_tok, -1), *pw1, tm=tm_moe, tf=tf, rc=128)
    ys = _moe(hs, *mw, p_sample[1].reshape(nb, -1), *pw1, tm=nb, tf=tf, rc=nb)

    return (yp.reshape(bsz, seq, d), ys.reshape(nb, 1, d),
            conv_p[None], conv_s[None],
            k_p.reshape(1, bsz, seq, nkv, HEAD_DIM), v_p.reshape(1, bsz, seq, nkv, HEAD_DIM),
            k_s.reshape(1, nb, 1, nkv, HEAD_DIM), v_s.reshape(1, nb, 1, nkv, HEAD_DIM))
```

```python
import functools

import jax
import jax.numpy as jnp
from jax import lax
from jax.experimental import pallas as pl
from jax.experimental.pallas import tpu as pltpu

F32 = jnp.float32
BF16 = jnp.bfloat16

EPS = 1e-6
HEAD_DIM = 128
ROT_DIM = HEAD_DIM // 4
ROPE_THETA = 500000.0
MOBA_BLOCK = 256
TOP_K_BLOCKS = 3
TOP_K_EXPERTS = 2
LANES = 128
SUBLANES = 8
MASK_VALUE = -1e30
VMEM_LIMIT = 56 * 1024 * 1024


def _params(*sem):
    return pltpu.CompilerParams(dimension_semantics=sem, vmem_limit_bytes=VMEM_LIMIT)


def _rms(x, g):
    r = lax.rsqrt(jnp.mean(x * x, axis=-1, keepdims=True) + EPS)
    return x * r * g


def _dot(a, b):
    return jnp.dot(a, b, preferred_element_type=F32)


def _dot_t(a, b, precision=None):
    return lax.dot_general(a, b, (((1,), (1,)), ((), ())), preferred_element_type=F32, precision=precision)


def _silu(x):
    return x * jax.nn.sigmoid(x)


def _ple(h, p, g, w_gate, w_proj):
    gate = jax.nn.sigmoid(_dot(_rms(h, g).astype(BF16), w_gate))
    return h + gate * _dot(p.astype(BF16), w_proj)


def _rope_head(x, cos, sin, lane):
    half = ROT_DIM // 2
    partner = jnp.where(lane < half, pltpu.roll(x, HEAD_DIM - half, 1), pltpu.roll(x, half, 1))
    return x * cos + partner * sin


CONV_HALO = 32
CONV_CHUNK = 16


def _conv_prompt_kernel(x_ref, g_ref, w1_ref, b1_ref, wrep_ref, bdw_ref, lng_ref, lnb_ref, w2_ref, b2_ref,
                        out_ref, state_ref, upad_ref, y_ref, *, tm, width, dc):
    t = pl.program_id(1)

    @pl.when(t == 0)
    def _():
        upad_ref[...] = jnp.zeros(upad_ref.shape, F32)

    x = x_ref[0]
    uu = _dot(_rms(x, g_ref[...]).astype(BF16), w1_ref[...]) + b1_ref[...]
    upad_ref[CONV_HALO:CONV_HALO + tm, :] = uu[:, :dc] * jax.nn.sigmoid(uu[:, dc:])

    first = CONV_HALO - (width - 1)
    span = CONV_CHUNK + SUBLANES

    def chunk(i, carry):
        base = pl.multiple_of(i * CONV_CHUNK, CONV_CHUNK)
        acc = jnp.broadcast_to(bdw_ref[...], (CONV_CHUNK, dc))
        for r in range(SUBLANES):
            part = None
            for k in range(width):
                a, kr = divmod(first + k, SUBLANES)
                if kr != r:
                    continue
                w = wrep_ref[k * SUBLANES:(k + 1) * SUBLANES, :]
                term = upad_ref[pl.ds(base + a * SUBLANES, span), :] * jnp.concatenate([w] * (span // SUBLANES), axis=0)
                part = term if part is None else part + term
            if part is not None:
                acc = acc + part[r:r + CONV_CHUNK, :]
        mu = jnp.mean(acc, axis=-1, keepdims=True)
        d = acc - mu
        var = jnp.mean(d * d, axis=-1, keepdims=True)
        z = d * lax.rsqrt(var + EPS) * lng_ref[...] + lnb_ref[...]
        y_ref[pl.ds(base, CONV_CHUNK), :] = _silu(z).astype(BF16)
        return carry

    lax.fori_loop(0, tm // CONV_CHUNK, chunk, 0)
    out_ref[0] = x + _dot(y_ref[...], w2_ref[...]) + b2_ref[...]
    state_ref[0] = upad_ref[tm + first:tm + CONV_HALO, :]
    upad_ref[0:CONV_HALO, :] = upad_ref[tm:tm + CONV_HALO, :]


def _conv_prompt(x, g, w1, b1, wdw, bdw, lng, lnb, w2, b2, *, tm):
    bsz, seq, d = x.shape
    width, dc = wdw.shape
    assert seq % tm == 0 and tm % CONV_CHUNK == 0 and width - 1 <= CONV_HALO
    row = lambda a: a.reshape(1, -1)
    full = lambda a: pl.BlockSpec(a.shape, lambda b, t: (0,) * a.ndim)
    wrep = jnp.repeat(wdw, SUBLANES, axis=0)
    args = (row(g), w1, row(b1), wrep, row(bdw), row(lng), row(lnb), w2, row(b2))
    return pl.pallas_call(
        functools.partial(_conv_prompt_kernel, tm=tm, width=width, dc=dc),
        grid=(bsz, seq // tm),
        in_specs=[pl.BlockSpec((1, tm, d), lambda b, t: (b, t, 0))] + [full(a) for a in args],
        out_specs=[pl.BlockSpec((1, tm, d), lambda b, t: (b, t, 0)),
                   pl.BlockSpec((1, width - 1, dc), lambda b, t: (b, 0, 0))],
        out_shape=[jax.ShapeDtypeStruct((bsz, seq, d), F32),
                   jax.ShapeDtypeStruct((bsz, width - 1, dc), F32)],
        scratch_shapes=[pltpu.VMEM((tm + CONV_HALO + SUBLANES, dc), F32), pltpu.VMEM((tm, dc), BF16)],
        compiler_params=_params("parallel", "arbitrary"),
        name="conv_prompt",
    )(x, *args)


def _conv_sample_kernel(x_ref, st_ref, g_ref, w1_ref, b1_ref, wdw_ref, bdw_ref, lng_ref, lnb_ref, w2_ref, b2_ref,
                        out_ref, state_ref, *, width, dc):
    nb = x_ref.shape[0]
    x = x_ref[...]
    uu = _dot(_rms(x, g_ref[...]).astype(BF16), w1_ref[...]) + b1_ref[...]
    u = uu[:, :dc] * jax.nn.sigmoid(uu[:, dc:])
    st = st_ref[...]
    acc = jnp.sum(st * wdw_ref[0:width - 1, :][None], axis=1) + u * wdw_ref[width - 1:width, :] + bdw_ref[...]
    mu = jnp.mean(acc, axis=-1, keepdims=True)
    d = acc - mu
    var = jnp.mean(d * d, axis=-1, keepdims=True)
    z = d * lax.rsqrt(var + EPS) * lng_ref[...] + lnb_ref[...]
    out_ref[...] = x + _dot(_silu(z).astype(BF16), w2_ref[...]) + b2_ref[...]
    state_ref[:, 0:width - 2, :] = st_ref[:, 1:width - 1, :]
    for b in range(nb):
        state_ref[b, width - 2:width - 1, :] = u[b:b + 1, :]


def _conv_sample(x, st, g, w1, b1, wdw, bdw, lng, lnb, w2, b2):
    nb, d = x.shape
    width, dc = wdw.shape
    row = lambda a: a.reshape(1, -1)
    return pl.pallas_call(
        functools.partial(_conv_sample_kernel, width=width, dc=dc),
        out_shape=[jax.ShapeDtypeStruct((nb, d), F32), jax.ShapeDtypeStruct((nb, width - 1, dc), F32)],
        compiler_params=pltpu.CompilerParams(vmem_limit_bytes=VMEM_LIMIT),
        name="conv_sample",
    )(x, st, row(g), w1, row(b1), wdw, row(bdw), row(lng), row(lnb), w2, row(b2))


def _ffn_kernel(x_ref, g_ref, wg_ref, wu_ref, wd_ref, p_ref, gp_ref, wpg_ref, wpp_ref, out_ref, xn_ref, acc_ref):
    f = pl.program_id(1)

    @pl.when(f == 0)
    def _():
        xn_ref[...] = _rms(x_ref[...], g_ref[...]).astype(BF16)
        acc_ref[...] = jnp.zeros(acc_ref.shape, F32)

    xn = xn_ref[...]
    a = (_silu(_dot(xn, wg_ref[...])) * _dot(xn, wu_ref[...])).astype(BF16)
    acc_ref[...] += _dot(a, wd_ref[...])

    @pl.when(f == pl.num_programs(1) - 1)
    def _():
        out_ref[...] = _ple(x_ref[...] + acc_ref[...], p_ref[...], gp_ref[...], wpg_ref[...], wpp_ref[...])


def _ffn(x, g, wg, wu, wd, p, gp, wpg, wpp, *, tm, tf):
    n, d = x.shape
    dff = wg.shape[1]
    assert n % tm == 0 and dff % tf == 0
    row = lambda a: a.reshape(1, -1)
    full = lambda a: pl.BlockSpec(a.shape, lambda i, f: (0,) * a.ndim)
    return pl.pallas_call(
        _ffn_kernel,
        grid=(n // tm, dff // tf),
        in_specs=[pl.BlockSpec((tm, d), lambda i, f: (i, 0)), full(row(g)),
                  pl.BlockSpec((d, tf), lambda i, f: (0, f)), pl.BlockSpec((d, tf), lambda i, f: (0, f)),
                  pl.BlockSpec((tf, d), lambda i, f: (f, 0)),
                  pl.BlockSpec((tm, p.shape[1]), lambda i, f: (i, 0)), full(row(gp)), full(wpg), full(wpp)],
        out_specs=pl.BlockSpec((tm, d), lambda i, f: (i, 0)),
        out_shape=jax.ShapeDtypeStruct((n, d), F32),
        scratch_shapes=[pltpu.VMEM((tm, d), BF16), pltpu.VMEM((tm, d), F32)],
        compiler_params=_params("parallel", "arbitrary"),
        name="ffn_ple",
    )(x, row(g), wg, wu, wd, p, row(gp), wpg, wpp)


def _qkv_prompt_kernel(x_ref, g_ref, w_ref, cos_ref, sin_ref,
                       q_ref, k_ref, v_ref, kb_ref, vb_ref, sel_ref, kmt_ref, *, tm, nh, nkv, nblk):
    t = pl.program_id(1)
    group = nh // nkv
    nq = nh * HEAD_DIM
    nk = nkv * HEAD_DIM

    @pl.when(t == 0)
    def _():
        kmt_ref[...] = jnp.zeros(kmt_ref.shape, F32)

    qkv = _dot(_rms(x_ref[0], g_ref[...]).astype(BF16), w_ref[...])
    cos = cos_ref[...]
    sin = sin_ref[...]
    lane = lax.broadcasted_iota(jnp.int32, (tm, HEAD_DIM), 1)
    scale = HEAD_DIM ** -0.5
    q = jnp.concatenate([_rope_head(qkv[:, h * HEAD_DIM:(h + 1) * HEAD_DIM], cos, sin, lane) * scale
                         for h in range(nh)], axis=1)
    k = jnp.concatenate([_rope_head(qkv[:, nq + h * HEAD_DIM:nq + (h + 1) * HEAD_DIM], cos, sin, lane)
                         for h in range(nkv)], axis=1)
    v = qkv[:, nq + nk:]
    q_ref[0] = q.astype(BF16)
    k_ref[0] = k
    v_ref[0] = v
    kb_ref[0] = k.astype(BF16)
    vb_ref[0] = v.astype(BF16)

    kmt = kmt_ref[...]
    kmt_row = lax.broadcasted_iota(jnp.int32, kmt.shape, 0)
    kmt_head = lax.broadcasted_iota(jnp.int32, kmt.shape, 1) // HEAD_DIM
    for i in range(tm // MOBA_BLOCK):
        km = jnp.mean(k[i * MOBA_BLOCK:(i + 1) * MOBA_BLOCK, :], axis=0, keepdims=True)
        km = jnp.concatenate([km[:, (h // group) * HEAD_DIM:(h // group + 1) * HEAD_DIM] for h in range(nh)], axis=1)
        n = t * (tm // MOBA_BLOCK) + i
        kmt = jnp.where(kmt_row == kmt_head * nblk + n, km, kmt)
    kmt_ref[...] = kmt

    s = _dot_t(q, kmt, precision=lax.Precision.HIGHEST)
    width = nh * nblk
    col = lax.broadcasted_iota(jnp.int32, (tm, width), 1)
    n_idx = col % nblk
    own = (t * tm + lax.broadcasted_iota(jnp.int32, (tm, width), 0)) // MOBA_BLOCK
    rank = jnp.zeros((tm, width), jnp.int32)
    for dlt in range(1, nblk):
        wrap = n_idx + dlt >= nblk
        other = jnp.where(wrap, pltpu.roll(s, nblk - dlt, 1), pltpu.roll(s, width - dlt, 1))
        m_idx = jnp.where(wrap, n_idx + dlt - nblk, n_idx + dlt)
        beats = (m_idx < own) & ((other > s) | ((other == s) & (m_idx < n_idx)))
        rank = rank + beats.astype(jnp.int32)
    sel_ref[0] = ((n_idx < own) & (rank < TOP_K_BLOCKS)).astype(F32)


def _qkv_prompt(x, g, w, cos, sin, *, tm, nh, nkv):
    bsz, seq, d = x.shape
    nq, nk = nh * HEAD_DIM, nkv * HEAD_DIM
    nblk = seq // MOBA_BLOCK
    assert seq % tm == 0 and tm % MOBA_BLOCK == 0 and nh * nblk == LANES
    tok = lambda c: pl.BlockSpec((1, tm, c), lambda b, t: (b, t, 0))
    full = lambda a: pl.BlockSpec(a.shape, lambda b, t: (0,) * a.ndim)
    g = g.reshape(1, -1)
    return pl.pallas_call(
        functools.partial(_qkv_prompt_kernel, tm=tm, nh=nh, nkv=nkv, nblk=nblk),
        grid=(bsz, seq // tm),
        in_specs=[tok(d), full(g), full(w),
                  pl.BlockSpec((tm, HEAD_DIM), lambda b, t: (t, 0)), pl.BlockSpec((tm, HEAD_DIM), lambda b, t: (t, 0))],
        out_specs=[tok(nq), tok(nk), tok(nk), tok(nk), tok(nk), tok(nh * nblk)],
        out_shape=[jax.ShapeDtypeStruct((bsz, seq, nq), BF16),
                   jax.ShapeDtypeStruct((bsz, seq, nk), F32), jax.ShapeDtypeStruct((bsz, seq, nk), F32),
                   jax.ShapeDtypeStruct((bsz, seq, nk), BF16), jax.ShapeDtypeStruct((bsz, seq, nk), BF16),
                   jax.ShapeDtypeStruct((bsz, seq, nh * nblk), F32)],
        scratch_shapes=[pltpu.VMEM((nh * nblk, nq), F32)],
        compiler_params=_params("parallel", "arbitrary"),
        name="qkv_prompt",
    )(x, g, w, cos, sin)


def _qkv_sample_kernel(x_ref, g_ref, w_ref, cos_ref, sin_ref, q_ref, k_ref, v_ref, *, nh, nkv):
    nb = x_ref.shape[0]
    nq = nh * HEAD_DIM
    nk = nkv * HEAD_DIM
    qkv = _dot(_rms(x_ref[...], g_ref[...]).astype(BF16), w_ref[...])
    cos = cos_ref[...]
    sin = sin_ref[...]
    lane = lax.broadcasted_iota(jnp.int32, (nb, HEAD_DIM), 1)
    scale = HEAD_DIM ** -0.5
    q_ref[...] = jnp.concatenate([_rope_head(qkv[:, h * HEAD_DIM:(h + 1) * HEAD_DIM], cos, sin, lane) * scale
                                  for h in range(nh)], axis=1)
    k_ref[...] = jnp.concatenate([_rope_head(qkv[:, nq + h * HEAD_DIM:nq + (h + 1) * HEAD_DIM], cos, sin, lane)
                                  for h in range(nkv)], axis=1)
    v_ref[...] = qkv[:, nq + nk:]


def _qkv_sample(x, g, w, cos, sin, *, nh, nkv):
    nb = x.shape[0]
    nq, nk = nh * HEAD_DIM, nkv * HEAD_DIM
    return pl.pallas_call(
        functools.partial(_qkv_sample_kernel, nh=nh, nkv=nkv),
        out_shape=[jax.ShapeDtypeStruct((nb, nq), F32), jax.ShapeDtypeStruct((nb, nk), F32),
                   jax.ShapeDtypeStruct((nb, nk), F32)],
        compiler_params=pltpu.CompilerParams(vmem_limit_bytes=VMEM_LIMIT),
        name="qkv_sample",
    )(x, g.reshape(1, -1), w, cos, sin)


def _attn_prompt_kernel(q_ref, kb_ref, vb_ref, sel_ref, h_ref, wo_ref, out_ref, *, nh, nkv, nblk):
    o = pl.program_id(1)
    hk = pl.program_id(2)
    group = nh // nkv
    blk = MOBA_BLOCK
    q = jnp.concatenate([q_ref[0, :, g * HEAD_DIM:(g + 1) * HEAD_DIM] for g in range(group)], axis=0)
    rows = group * blk

    own0 = pl.multiple_of(o * blk, blk)
    s = _dot_t(q, kb_ref[0, pl.ds(own0, blk), :])
    qpos = lax.broadcasted_iota(jnp.int32, (rows, blk), 0) % blk
    kpos = lax.broadcasted_iota(jnp.int32, (rows, blk), 1)
    s = jnp.where(kpos <= qpos, s, MASK_VALUE)
    m = jnp.max(s, axis=-1, keepdims=True)
    p = jnp.exp(s - m)
    l = jnp.sum(p, axis=-1, keepdims=True)
    acc = _dot(p.astype(BF16), vb_ref[0, pl.ds(own0, blk), :])

    sel = sel_ref[0]
    lane = lax.broadcasted_iota(jnp.int32, sel.shape, 1)
    bias = (sel - 1.0) * (-MASK_VALUE)
    q_aug = jnp.concatenate(
        [q, jnp.concatenate([jnp.where(lane // nblk == hk * group + g, bias, 0.0) for g in range(group)],
                            axis=0).astype(BF16)], axis=1)

    def past(j, carry):
        m, l, acc = carry
        j0 = pl.multiple_of(j * blk, blk)
        pick = jnp.where(lane % nblk == j, 1.0, 0.0).astype(BF16)
        s = _dot_t(q_aug, jnp.concatenate([kb_ref[0, pl.ds(j0, blk), :], pick], axis=1))
        m_new = jnp.maximum(m, jnp.max(s, axis=-1, keepdims=True))
        alpha = jnp.exp(m - m_new)
        p = jnp.exp(s - m_new)
        l = alpha * l + jnp.sum(p, axis=-1, keepdims=True)
        acc = alpha * acc + _dot(p.astype(BF16), vb_ref[0, pl.ds(j0, blk), :])
        return m_new, l, acc

    m, l, acc = lax.fori_loop(0, o, past, (m, l, acc))
    att = acc / l
    att = jnp.concatenate([att[g * blk:(g + 1) * blk, :] for g in range(group)], axis=1).astype(BF16)
    proj = _dot(att, wo_ref[...])

    @pl.when(hk == 0)
    def _():
        out_ref[0] = h_ref[0] + proj

    @pl.when(hk != 0)
    def _():
        out_ref[0] += proj


def _attn_prompt(q, kb, vb, sel, h, wo, *, nh, nkv):
    bsz, seq, d = h.shape
    nblk = seq // MOBA_BLOCK
    gw = (nh // nkv) * HEAD_DIM
    return pl.pallas_call(
        functools.partial(_attn_prompt_kernel, nh=nh, nkv=nkv, nblk=nblk),
        grid=(bsz, nblk, nkv),
        in_specs=[pl.BlockSpec((1, MOBA_BLOCK, gw), lambda b, o, k: (b, o, k)),
                  pl.BlockSpec((1, seq, HEAD_DIM), lambda b, o, k: (b, 0, k)),
                  pl.BlockSpec((1, seq, HEAD_DIM), lambda b, o, k: (b, 0, k)),
                  pl.BlockSpec((1, MOBA_BLOCK, sel.shape[2]), lambda b, o, k: (b, o, 0)),
                  pl.BlockSpec((1, MOBA_BLOCK, d), lambda b, o, k: (b, o, 0)),
                  pl.BlockSpec((gw, d), lambda b, o, k: (k, 0))],
        out_specs=pl.BlockSpec((1, MOBA_BLOCK, d), lambda b, o, k: (b, o, 0)),
        out_shape=jax.ShapeDtypeStruct((bsz, seq, d), F32),
        compiler_params=_params("parallel", "parallel", "arbitrary"),
        name="attn_prompt",
    )(q, kb, vb, sel, h, wo)


PAGES_PER_STEP = 32


def _sample_select_kernel(pt_ref, *refs, nh, nkv, ppb, nblk):
    pages = refs[:PAGES_PER_STEP]
    q_ref, idx_ref, km_ref = refs[PAGES_PER_STEP:]
    s = pl.program_id(1)
    group = nh // nkv
    bps = PAGES_PER_STEP // ppb
    psize = pages[0].shape[1] // nkv

    @pl.when(s == 0)
    def _():
        km_ref[...] = jnp.zeros(km_ref.shape, F32)

    km = km_ref[...]
    km_row = lax.broadcasted_iota(jnp.int32, km.shape, 0)
    for i in range(bps):
        tot = jnp.concatenate(
            [sum(jnp.sum(pages[i * ppb + j][0, pl.ds(kv, psize, stride=nkv), :], axis=0, keepdims=True)
                 for j in range(ppb)) for kv in range(nkv)], axis=1)
        km = jnp.where(km_row == s * bps + i, tot / (psize * ppb), km)
    km_ref[...] = km

    @pl.when(s == pl.num_programs(1) - 1)
    def _():
        km = km_ref[...]
        q = q_ref[0]
        sc = jnp.concatenate(
            [_dot_t(q[kv * group:(kv + 1) * group, :], km[:, kv * HEAD_DIM:(kv + 1) * HEAD_DIM],
                    precision=lax.Precision.HIGHEST) for kv in range(nkv)], axis=0)
        col = lax.broadcasted_iota(jnp.int32, sc.shape, 1)
        lane = lax.broadcasted_iota(jnp.int32, (nh, LANES), 1)
        out = jnp.zeros((nh, LANES), jnp.int32)
        for r in range(TOP_K_BLOCKS):
            best = jnp.max(sc, axis=-1, keepdims=True)
            pick = jnp.min(jnp.where(sc == best, col, nblk), axis=-1, keepdims=True)
            out = jnp.where(lane == r, pick, out)
            sc = jnp.where(col == pick, -jnp.inf, sc)
        idx_ref[0] = out


def _sample_select(cache_k, page_table, q, *, nh, nkv):
    npool, prow, hd = cache_k.shape
    nb, npages = page_table.shape
    ppb = MOBA_BLOCK // (prow // nkv)
    nblk = npages // ppb
    assert npages % PAGES_PER_STEP == 0 and PAGES_PER_STEP % ppb == 0 and nblk >= TOP_K_BLOCKS
    page_spec = lambda j: pl.BlockSpec((1, prow, hd), lambda b, s, pt: (pt[b * npages + s * PAGES_PER_STEP + j], 0, 0))
    grid_spec = pltpu.PrefetchScalarGridSpec(
        num_scalar_prefetch=1,
        grid=(nb, npages // PAGES_PER_STEP),
        in_specs=[page_spec(j) for j in range(PAGES_PER_STEP)]
        + [pl.BlockSpec((1, nh, HEAD_DIM), lambda b, s, pt: (b, 0, 0))],
        out_specs=pl.BlockSpec((1, nh, LANES), lambda b, s, pt: (b, 0, 0)),
        scratch_shapes=[pltpu.VMEM((nblk, nkv * hd), F32)],
    )
    return pl.pallas_call(
        functools.partial(_sample_select_kernel, nh=nh, nkv=nkv, ppb=ppb, nblk=nblk),
        grid_spec=grid_spec,
        out_shape=jax.ShapeDtypeStruct((nb, nh, LANES), jnp.int32),
        compiler_params=_params("parallel", "arbitrary"),
        name="sample_select",
    )(page_table.reshape(-1), *([cache_k] * PAGES_PER_STEP), q.reshape(nb, nh, HEAD_DIM))


def _sample_attn_kernel(pt_ref, idx_ref, q_ref, kn_ref, vn_ref, *refs, group, nkv, npg):
    kpages = refs[:group * npg]
    vpages = refs[group * npg:2 * group * npg]
    out_ref = refs[2 * group * npg]
    hkv = pl.program_id(1)
    prow = kpages[0].shape[1]
    mine = lax.broadcasted_iota(jnp.int32, (prow, 1), 0) % nkv == hkv
    outs = []
    for g in range(group):
        q = q_ref[0, g:g + 1, :]
        m = jnp.sum(q * kn_ref[0], axis=-1, keepdims=True)
        l = jnp.ones((1, 1), F32)
        acc = vn_ref[0]
        for i in range(npg):
            k = kpages[g * npg + i][0]
            sc = jnp.where(mine, jnp.sum(k * q, axis=-1, keepdims=True), MASK_VALUE)
            m_new = jnp.maximum(m, jnp.max(sc, axis=0, keepdims=True))
            alpha = jnp.exp(m - m_new)
            p = jnp.exp(sc - m_new)
            l = alpha * l + jnp.sum(p, axis=0, keepdims=True)
            acc = alpha * acc + jnp.sum(p * vpages[g * npg + i][0], axis=0, keepdims=True)
            m = m_new
        outs.append(acc / l)
    out_ref[0] = jnp.concatenate(outs, axis=0)


def _sample_attn(cache_k, cache_v, page_table, idx, q, kn, vn, *, nh, nkv):
    npool, prow, hd = cache_k.shape
    nb, npages = page_table.shape
    ppb = MOBA_BLOCK // (prow // nkv)
    group = nh // nkv
    nsel = TOP_K_BLOCKS
    npg = nsel * ppb

    def page_spec(g, i):
        def imap(b, kv, pt, ix):
            blk = ix[(b * nh + kv * group + g) * nsel + i // ppb]
            return (pt[b * npages + blk * ppb + i % ppb], 0, 0)
        return pl.BlockSpec((1, prow, hd), imap)

    qspec = pl.BlockSpec((1, group, HEAD_DIM), lambda b, kv, pt, ix: (b * nkv + kv, 0, 0))
    kvspec = pl.BlockSpec((1, 1, HEAD_DIM), lambda b, kv, pt, ix: (b * nkv + kv, 0, 0))
    pages = [page_spec(g, i) for g in range(group) for i in range(npg)]
    grid_spec = pltpu.PrefetchScalarGridSpec(
        num_scalar_prefetch=2,
        grid=(nb, nkv),
        in_specs=[qspec, kvspec, kvspec] + pages + pages,
        out_specs=qspec,
    )
    out = pl.pallas_call(
        functools.partial(_sample_attn_kernel, group=group, nkv=nkv, npg=npg),
        grid_spec=grid_spec,
        out_shape=jax.ShapeDtypeStruct((nb * nkv, group, HEAD_DIM), F32),
        compiler_params=_params("parallel", "parallel"),
        name="sample_attn",
    )(page_table.reshape(-1), idx.reshape(-1),
      q.reshape(nb * nkv, group, HEAD_DIM), kn.reshape(nb * nkv, 1, HEAD_DIM), vn.reshape(nb * nkv, 1, HEAD_DIM),
      *([cache_k] * (group * npg)), *([cache_v] * (group * npg)))
    return out.reshape(nb, nh * HEAD_DIM)


def _proj_residual_kernel(a_ref, h_ref, w_ref, out_ref):
    out_ref[...] = h_ref[...] + _dot(a_ref[...].astype(BF16), w_ref[...])


def _proj_residual(a, h, w):
    return pl.pallas_call(
        _proj_residual_kernel,
        out_shape=jax.ShapeDtypeStruct(h.shape, F32),
        compiler_params=pltpu.CompilerParams(vmem_limit_bytes=VMEM_LIMIT),
        name="proj_residual",
    )(a, h, w)


MXU_DEPTH = 256
MOE_ROW_GRAN = 64
MOE_VARIANTS = 8


def _router_kernel(x_ref, g_ref, wrt_ref, tri_ref, xn_ref, gate_ref, rank_ref, cnt_ref):
    xn = _rms(x_ref[...], g_ref[...])
    xn_ref[...] = xn.astype(BF16)
    logits = _dot_t(wrt_ref[...], xn, precision=lax.Precision.HIGHEST)
    ne = logits.shape[0]
    e = jnp.exp(logits - jnp.max(logits, axis=0, keepdims=True))
    probs = e / jnp.sum(e, axis=0, keepdims=True)
    eid = lax.broadcasted_iota(jnp.int32, probs.shape, 0)
    rest = probs
    member = jnp.zeros(probs.shape, jnp.bool_)
    top_sum = jnp.zeros((1, probs.shape[1]), F32)
    for _ in range(TOP_K_EXPERTS):
        best = jnp.max(rest, axis=0, keepdims=True)
        pick = eid == jnp.min(jnp.where(rest == best, eid, ne), axis=0, keepdims=True)
        member = member | pick
        top_sum = top_sum + best
        rest = jnp.where(pick, -1.0, rest)
    gate_ref[...] = jnp.where(member, probs / top_sum, 0.0)
    mem = jnp.where(member, 1.0, 0.0)
    rank = _dot(mem.astype(BF16), tri_ref[...])
    rank_ref[...] = jnp.where(member, rank, -1.0)
    cnt_ref[0] = jnp.broadcast_to(jnp.sum(mem, axis=1, keepdims=True), cnt_ref.shape[1:])


def _router(x, g, wr, *, tm):
    n, d = x.shape
    ne = wr.shape[1]
    nt = n // tm
    assert n % tm == 0
    tri = (lax.broadcasted_iota(jnp.int32, (tm, tm), 0) < lax.broadcasted_iota(jnp.int32, (tm, tm), 1)).astype(BF16)
    full = lambda a: pl.BlockSpec(a.shape, lambda i: (0,) * a.ndim)
    g = g.reshape(1, -1)
    wrt = wr.T
    xn, gate, rank, cnt = pl.pallas_call(
        _router_kernel,
        grid=(nt,),
        in_specs=[pl.BlockSpec((tm, d), lambda i: (i, 0)), full(g), full(wrt), full(tri)],
        out_specs=[pl.BlockSpec((tm, d), lambda i: (i, 0)), pl.BlockSpec((ne, tm), lambda i: (0, i)),
                   pl.BlockSpec((ne, tm), lambda i: (0, i)), pl.BlockSpec((1, ne, LANES), lambda i: (i, 0, 0))],
        out_shape=[jax.ShapeDtypeStruct((n, d), BF16), jax.ShapeDtypeStruct((ne, n), F32),
                   jax.ShapeDtypeStruct((ne, n), F32), jax.ShapeDtypeStruct((nt, ne, LANES), F32)],
        compiler_params=_params("parallel"),
        name="router",
    )(x, g, wrt, tri)
    return xn, gate, rank, cnt[:, :, 0].astype(jnp.int32).reshape(-1)


def _moe_kernel(cnt_ref, xn_ref, gate_ref, rank_ref, wg_ref, wu_ref, wd_ref, h_ref, p_ref, gp_ref, wpg_ref, wpp_ref,
                gf_ref, out_ref, xc_ref, y_ref, acc_ref, *, tm, gran, nvar, ne, final_norm):
    i = pl.program_id(0)
    e = pl.program_id(1)
    f = pl.program_id(2)
    nf = pl.num_programs(2)
    cnt = cnt_ref[i * ne + e]
    ngrp = (cnt + gran - 1) // gran
    rank = rank_ref[pl.ds(e, 1), :]
    kq = min(MXU_DEPTH, tm)

    def onehot(r0, rows):
        slot = lax.broadcasted_iota(jnp.int32, (rows, tm), 0) + r0
        return slot.astype(F32) == rank

    def process(r0, rows):
        krows = -(-rows // kq) * kq

        @pl.when(f == 0)
        def _():
            sel = jnp.where(onehot(r0, rows), 1.0, 0.0).astype(BF16)
            xc_ref[pl.ds(r0, rows), :] = _dot(sel, xn_ref[...]).astype(BF16)
            y_ref[pl.ds(r0, krows), :] = jnp.zeros((krows, y_ref.shape[1]), F32)

        xc = xc_ref[pl.ds(r0, rows), :]
        a = (_silu(_dot(xc, wg_ref[0])) * _dot(xc, wu_ref[0])).astype(BF16)
        y_ref[pl.ds(r0, rows), :] += _dot(a, wd_ref[0])

        @pl.when(f == nf - 1)
        def _():
            w = jnp.where(onehot(r0, krows), gate_ref[pl.ds(e, 1), :], 0.0).astype(BF16)
            acc_ref[...] += lax.dot_general(w, y_ref[pl.ds(r0, krows), :].astype(BF16), (((0,), (0,)), ((), ())),
                                            preferred_element_type=F32)

    @pl.when((e == 0) & (f == 0))
    def _():
        acc_ref[...] = jnp.zeros(acc_ref.shape, F32)

    for n in range(1, nvar + 1):
        @pl.when(ngrp == n)
        def _(n=n):
            process(0, n * gran)

    @pl.when(ngrp > nvar)
    def _():
        full = nvar * gran

        def body(c, carry):
            process(pl.multiple_of(c * full, full), full)
            return carry
        lax.fori_loop(0, (cnt + full - 1) // full, body, 0)

    @pl.when((e == ne - 1) & (f == nf - 1))
    def _():
        hout = _ple(h_ref[...] + acc_ref[...], p_ref[...], gp_ref[...], wpg_ref[...], wpp_ref[...])
        out_ref[...] = _rms(hout, gf_ref[...]) if final_norm else hout


def _moe(h, g, wr, wg, wu, wd, p, gp, wpg, wpp, gf, *, tm, tf, final_norm=True):
    n, d = h.shape
    ne, _, dff = wg.shape
    gran = min(MOE_ROW_GRAN, tm)
    nvar = min(MOE_VARIANTS, tm // gran)
    assert n % tm == 0 and dff % tf == 0 and tm % (nvar * gran) == 0 and (nvar * gran) % min(MXU_DEPTH, tm) == 0
    xn, gate, rank, cnt = _router(h, g, wr, tm=tm)
    row = lambda a: a.reshape(1, -1)
    full = lambda a: pl.BlockSpec(a.shape, lambda i, e, f, c: (0,) * a.ndim)
    grid_spec = pltpu.PrefetchScalarGridSpec(
        num_scalar_prefetch=1,
        grid=(n // tm, ne, dff // tf),
        in_specs=[pl.BlockSpec((tm, d), lambda i, e, f, c: (i, 0)),
                  pl.BlockSpec((ne, tm), lambda i, e, f, c: (0, i)),
                  pl.BlockSpec((ne, tm), lambda i, e, f, c: (0, i)),
                  pl.BlockSpec((1, d, tf), lambda i, e, f, c: (e, 0, f)),
                  pl.BlockSpec((1, d, tf), lambda i, e, f, c: (e, 0, f)),
                  pl.BlockSpec((1, tf, d), lambda i, e, f, c: (e, f, 0)),
                  pl.BlockSpec((tm, d), lambda i, e, f, c: (i, 0)),
                  pl.BlockSpec((tm, p.shape[1]), lambda i, e, f, c: (i, 0)),
                  full(row(gp)), full(wpg), full(wpp), full(row(gf))],
        out_specs=pl.BlockSpec((tm, d), lambda i, e, f, c: (i, 0)),
        scratch_shapes=[pltpu.VMEM((tm, d), BF16), pltpu.VMEM((tm, d), F32), pltpu.VMEM((tm, d), F32)],
    )
    return pl.pallas_call(
        functools.partial(_moe_kernel, tm=tm, gran=gran, nvar=nvar, ne=ne, final_norm=final_norm),
        grid_spec=grid_spec,
        out_shape=jax.ShapeDtypeStruct((n, d), F32),
        compiler_params=_params("parallel", "arbitrary", "arbitrary"),
        name="moe_ple",
    )(cnt, xn, gate, rank, wg, wu, wd, h, p, row(gp), wpg, wpp, row(gf))


def _rope_tables(pos):
    half = ROT_DIM // 2
    inv = jnp.power(ROPE_THETA, -jnp.arange(half, dtype=F32) / half)
    ang = pos.astype(F32)[:, None] * inv[None, :]
    cos, sin = jnp.cos(ang), jnp.sin(ang)
    rest = HEAD_DIM - ROT_DIM
    n = pos.shape[0]
    return (jnp.concatenate([cos, cos, jnp.ones((n, rest), F32)], axis=1),
            jnp.concatenate([-sin, sin, jnp.zeros((n, rest), F32)], axis=1))


def _tile(n, pref):
    return pref if n % pref == 0 else n


def kernel(x_prompt, x_sample, state_conv, cache_k, cache_v, page_table, p_prompt, p_sample, norm_mix, norm_ffn, norm_ple, ple_w_gate, ple_w_proj, conv_w_pw1, conv_b_pw1, conv_w_dw, conv_b_dw, conv_ln_g, conv_ln_b, conv_w_pw2, conv_b_pw2, ffn_w_gate, ffn_w_up, ffn_w_down, attn_w_qkv, attn_w_o, moe_w_router, moe_w_gate, moe_w_up, moe_w_down, norm_final):
    bsz, seq, d = x_prompt.shape
    nb, dec_seq, _ = x_sample.shape
    assert dec_seq == 1 and norm_mix.shape[0] == 2
    nh = d // HEAD_DIM
    nkv = (attn_w_qkv.shape[2] // HEAD_DIM - nh) // 2
    n_tok = bsz * seq
    dff = ffn_w_gate.shape[2]
    psize = cache_k.shape[2]
    past_len = page_table.shape[1] * psize
    bf = lambda w: w.astype(BF16)

    cw = (norm_mix[0], bf(conv_w_pw1[0]), conv_b_pw1[0], conv_w_dw[0], conv_b_dw[0], conv_ln_g[0], conv_ln_b[0],
          bf(conv_w_pw2[0]), conv_b_pw2[0])
    hp, conv_p = _conv_prompt(x_prompt, *cw, tm=_tile(seq, 512))
    hs, conv_s = _conv_sample(x_sample.reshape(nb, d), state_conv[0], *cw)
    fw = (norm_ffn[0], bf(ffn_w_gate[0]), bf(ffn_w_up[0]), bf(ffn_w_down[0]))
    pw0 = (norm_ple[0], bf(ple_w_gate[0]), bf(ple_w_proj[0]))
    tf = _tile(dff, 512)
    hp = _ffn(hp.reshape(n_tok, d), *fw, p_prompt[0].reshape(n_tok, -1), *pw0, tm=_tile(n_tok, 1024), tf=tf)
    hs = _ffn(hs, *fw, p_sample[0].reshape(nb, -1), *pw0, tm=nb, tf=tf)

    wqkv, wo = bf(attn_w_qkv[0]), bf(attn_w_o[0])
    cos_p, sin_p = _rope_tables(jnp.arange(seq, dtype=jnp.int32))
    q, k_p, v_p, kb, vb, sel = _qkv_prompt(hp.reshape(bsz, seq, d), norm_mix[1], wqkv, cos_p, sin_p,
                                           tm=_tile(seq, 512), nh=nh, nkv=nkv)
    hp = _attn_prompt(q, kb, vb, sel, hp.reshape(bsz, seq, d), wo, nh=nh, nkv=nkv).reshape(n_tok, d)

    cos_s, sin_s = _rope_tables(jnp.full((1,), past_len, jnp.int32))
    qs, k_s, v_s = _qkv_sample(hs, norm_mix[1], wqkv, cos_s, sin_s, nh=nh, nkv=nkv)
    ck = cache_k[0].reshape(cache_k.shape[1], psize * nkv, HEAD_DIM)
    cv = cache_v[0].reshape(cache_v.shape[1], psize * nkv, HEAD_DIM)
    idx = _sample_select(ck, page_table, qs, nh=nh, nkv=nkv)[:, :, :TOP_K_BLOCKS]
    att_s = _sample_attn(ck, cv, page_table, idx, qs, k_s, v_s, nh=nh, nkv=nkv)
    hs = _proj_residual(att_s, hs, wo)

    mw = (norm_ffn[1], moe_w_router[0], bf(moe_w_gate[0]), bf(moe_w_up[0]), bf(moe_w_down[0]))
    pw1 = (norm_ple[1], bf(ple_w_gate[1]), bf(ple_w_proj[1]), norm_final)
    tm_moe = _tile(n_tok, 1024)
    yp = _moe(hp, *mw, p_prompt[1].reshape(n_tok, -1), *pw1, tm=tm_moe, tf=tf)
    ys = _moe(hs, *mw, p_sample[1].reshape(nb, -1), *pw1, tm=nb, tf=tf)

    return (yp.reshape(bsz, seq, d), ys.reshape(nb, 1, d),
            conv_p[None], conv_s[None],
            k_p.reshape(1, bsz, seq, nkv, HEAD_DIM), v_p.reshape(1, bsz, seq, nkv, HEAD_DIM),
            k_s.reshape(1, nb, 1, nkv, HEAD_DIM), v_s.reshape(1, nb, 1, nkv, HEAD_DIM))
```

```python
import functools

import jax
import jax.numpy as jnp
from jax import lax
from jax.experimental import pallas as pl
from jax.experimental.pallas import tpu as pltpu

F32 = jnp.float32
BF16 = jnp.bfloat16

EPS = 1e-6
HEAD_DIM = 128
ROT_DIM = HEAD_DIM // 4
ROPE_THETA = 500000.0
MOBA_BLOCK = 256
TOP_K_BLOCKS = 3
TOP_K_EXPERTS = 2
LANES = 128
SUBLANES = 8
MASK_VALUE = -1e30
VMEM_LIMIT = 56 * 1024 * 1024


def _params(*sem):
    return pltpu.CompilerParams(dimension_semantics=sem, vmem_limit_bytes=VMEM_LIMIT)


def _rms(x, g):
    r = lax.rsqrt(jnp.mean(x * x, axis=-1, keepdims=True) + EPS)
    return x * r * g


def _dot(a, b):
    return jnp.dot(a, b, preferred_element_type=F32)


def _dot_t(a, b, precision=None):
    return lax.dot_general(a, b, (((1,), (1,)), ((), ())), preferred_element_type=F32, precision=precision)


def _silu(x):
    return x * jax.nn.sigmoid(x)


def _ple(h, p, g, w_gate, w_proj):
    gate = jax.nn.sigmoid(_dot(_rms(h, g).astype(BF16), w_gate))
    return h + gate * _dot(p.astype(BF16), w_proj)


def _rope_head(x, cos, sin, lane):
    half = ROT_DIM // 2
    partner = jnp.where(lane < half, pltpu.roll(x, HEAD_DIM - half, 1), pltpu.roll(x, half, 1))
    return x * cos + partner * sin


CONV_HALO = 32
CONV_CHUNK = 16


def _conv_prompt_kernel(x_ref, g_ref, w1_ref, b1_ref, wrep_ref, bdw_ref, lng_ref, lnb_ref, w2_ref, b2_ref,
                        out_ref, state_ref, upad_ref, y_ref, *, tm, width, dc):
    t = pl.program_id(1)

    @pl.when(t == 0)
    def _():
        upad_ref[...] = jnp.zeros(upad_ref.shape, F32)

    x = x_ref[0]
    uu = _dot(_rms(x, g_ref[...]).astype(BF16), w1_ref[...]) + b1_ref[...]
    upad_ref[CONV_HALO:CONV_HALO + tm, :] = uu[:, :dc] * jax.nn.sigmoid(uu[:, dc:])

    first = CONV_HALO - (width - 1)
    span = CONV_CHUNK + SUBLANES

    def chunk(i, carry):
        base = pl.multiple_of(i * CONV_CHUNK, CONV_CHUNK)
        acc = jnp.broadcast_to(bdw_ref[...], (CONV_CHUNK, dc))
        for r in range(SUBLANES):
            part = None
            for k in range(width):
                a, kr = divmod(first + k, SUBLANES)
                if kr != r:
                    continue
                w = wrep_ref[k * SUBLANES:(k + 1) * SUBLANES, :]
                term = upad_ref[pl.ds(base + a * SUBLANES, span), :] * jnp.concatenate([w] * (span // SUBLANES), axis=0)
                part = term if part is None else part + term
            if part is not None:
                acc = acc + part[r:r + CONV_CHUNK, :]
        mu = jnp.mean(acc, axis=-1, keepdims=True)
        d = acc - mu
        var = jnp.mean(d * d, axis=-1, keepdims=True)
        z = d * lax.rsqrt(var + EPS) * lng_ref[...] + lnb_ref[...]
        y_ref[pl.ds(base, CONV_CHUNK), :] = _silu(z).astype(BF16)
        return carry

    lax.fori_loop(0, tm // CONV_CHUNK, chunk, 0)
    out_ref[0] = x + _dot(y_ref[...], w2_ref[...]) + b2_ref[...]
    state_ref[0] = upad_ref[tm + first:tm + CONV_HALO, :]
    upad_ref[0:CONV_HALO, :] = upad_ref[tm:tm + CONV_HALO, :]


def _conv_prompt(x, g, w1, b1, wdw, bdw, lng, lnb, w2, b2, *, tm):
    bsz, seq, d = x.shape
    width, dc = wdw.shape
    assert seq % tm == 0 and tm % CONV_CHUNK == 0 and width - 1 <= CONV_HALO
    row = lambda a: a.reshape(1, -1)
    full = lambda a: pl.BlockSpec(a.shape, lambda b, t: (0,) * a.ndim)
    wrep = jnp.repeat(wdw, SUBLANES, axis=0)
    args = (row(g), w1, row(b1), wrep, row(bdw), row(lng), row(lnb), w2, row(b2))
    return pl.pallas_call(
        functools.partial(_conv_prompt_kernel, tm=tm, width=width, dc=dc),
        grid=(bsz, seq // tm),
        in_specs=[pl.BlockSpec((1, tm, d), lambda b, t: (b, t, 0))] + [full(a) for a in args],
        out_specs=[pl.BlockSpec((1, tm, d), lambda b, t: (b, t, 0)),
                   pl.BlockSpec((1, width - 1, dc), lambda b, t: (b, 0, 0))],
        out_shape=[jax.ShapeDtypeStruct((bsz, seq, d), F32),
                   jax.ShapeDtypeStruct((bsz, width - 1, dc), F32)],
        scratch_shapes=[pltpu.VMEM((tm + CONV_HALO + SUBLANES, dc), F32), pltpu.VMEM((tm, dc), BF16)],
        compiler_params=_params("parallel", "arbitrary"),
        name="conv_prompt",
    )(x, *args)


def _conv_sample_kernel(x_ref, st_ref, g_ref, w1_ref, b1_ref, wdw_ref, bdw_ref, lng_ref, lnb_ref, w2_ref, b2_ref,
                        out_ref, state_ref, *, width, dc):
    nb = x_ref.shape[0]
    x = x_ref[...]
    uu = _dot(_rms(x, g_ref[...]).astype(BF16), w1_ref[...]) + b1_ref[...]
    u = uu[:, :dc] * jax.nn.sigmoid(uu[:, dc:])
    st = st_ref[...]
    acc = jnp.sum(st * wdw_ref[0:width - 1, :][None], axis=1) + u * wdw_ref[width - 1:width, :] + bdw_ref[...]
    mu = jnp.mean(acc, axis=-1, keepdims=True)
    d = acc - mu
    var = jnp.mean(d * d, axis=-1, keepdims=True)
    z = d * lax.rsqrt(var + EPS) * lng_ref[...] + lnb_ref[...]
    out_ref[...] = x + _dot(_silu(z).astype(BF16), w2_ref[...]) + b2_ref[...]
    state_ref[:, 0:width - 2, :] = st_ref[:, 1:width - 1, :]
    for b in range(nb):
        state_ref[b, width - 2:width - 1, :] = u[b:b + 1, :]


def _conv_sample(x, st, g, w1, b1, wdw, bdw, lng, lnb, w2, b2):
    nb, d = x.shape
    width, dc = wdw.shape
    row = lambda a: a.reshape(1, -1)
    return pl.pallas_call(
        functools.partial(_conv_sample_kernel, width=width, dc=dc),
        out_shape=[jax.ShapeDtypeStruct((nb, d), F32), jax.ShapeDtypeStruct((nb, width - 1, dc), F32)],
        compiler_params=pltpu.CompilerParams(vmem_limit_bytes=VMEM_LIMIT),
        name="conv_sample",
    )(x, st, row(g), w1, row(b1), wdw, row(bdw), row(lng), row(lnb), w2, row(b2))


def _ffn_kernel(x_ref, g_ref, wg_ref, wu_ref, wd_ref, p_ref, gp_ref, wpg_ref, wpp_ref, out_ref, xn_ref, acc_ref):
    f = pl.program_id(1)

    @pl.when(f == 0)
    def _():
        xn_ref[...] = _rms(x_ref[...], g_ref[...]).astype(BF16)
        acc_ref[...] = jnp.zeros(acc_ref.shape, F32)

    xn = xn_ref[...]
    a = (_silu(_dot(xn, wg_ref[...])) * _dot(xn, wu_ref[...])).astype(BF16)
    acc_ref[...] += _dot(a, wd_ref[...])

    @pl.when(f == pl.num_programs(1) - 1)
    def _():
        out_ref[...] = _ple(x_ref[...] + acc_ref[...], p_ref[...], gp_ref[...], wpg_ref[...], wpp_ref[...])


def _ffn(x, g, wg, wu, wd, p, gp, wpg, wpp, *, tm, tf):
    n, d = x.shape
    dff = wg.shape[1]
    assert n % tm == 0 and dff % tf == 0
    row = lambda a: a.reshape(1, -1)
    full = lambda a: pl.BlockSpec(a.shape, lambda i, f: (0,) * a.ndim)
    return pl.pallas_call(
        _ffn_kernel,
        grid=(n // tm, dff // tf),
        in_specs=[pl.BlockSpec((tm, d), lambda i, f: (i, 0)), full(row(g)),
                  pl.BlockSpec((d, tf), lambda i, f: (0, f)), pl.BlockSpec((d, tf), lambda i, f: (0, f)),
                  pl.BlockSpec((tf, d), lambda i, f: (f, 0)),
                  pl.BlockSpec((tm, p.shape[1]), lambda i, f: (i, 0)), full(row(gp)), full(wpg), full(wpp)],
        out_specs=pl.BlockSpec((tm, d), lambda i, f: (i, 0)),
        out_shape=jax.ShapeDtypeStruct((n, d), F32),
        scratch_shapes=[pltpu.VMEM((tm, d), BF16), pltpu.VMEM((tm, d), F32)],
        compiler_params=_params("parallel", "arbitrary"),
        name="ffn_ple",
    )(x, row(g), wg, wu, wd, p, row(gp), wpg, wpp)


def _qkv_prompt_kernel(x_ref, g_ref, w_ref, cos_ref, sin_ref,
                       q_ref, k_ref, v_ref, kb_ref, vb_ref, sel_ref, kmt_ref, *, tm, nh, nkv, nblk):
    t = pl.program_id(1)
    group = nh // nkv
    nq = nh * HEAD_DIM
    nk = nkv * HEAD_DIM

    @pl.when(t == 0)
    def _():
        kmt_ref[...] = jnp.zeros(kmt_ref.shape, F32)

    qkv = _dot(_rms(x_ref[0], g_ref[...]).astype(BF16), w_ref[...])
    cos = cos_ref[...]
    sin = sin_ref[...]
    lane = lax.broadcasted_iota(jnp.int32, (tm, HEAD_DIM), 1)
    scale = HEAD_DIM ** -0.5
    q = jnp.concatenate([_rope_head(qkv[:, h * HEAD_DIM:(h + 1) * HEAD_DIM], cos, sin, lane) * scale
                         for h in range(nh)], axis=1)
    k = jnp.concatenate([_rope_head(qkv[:, nq + h * HEAD_DIM:nq + (h + 1) * HEAD_DIM], cos, sin, lane)
                         for h in range(nkv)], axis=1)
    v = qkv[:, nq + nk:]
    q_ref[0] = q.astype(BF16)
    k_ref[0] = k
    v_ref[0] = v
    kb_ref[0] = k.astype(BF16)
    vb_ref[0] = v.astype(BF16)

    kmt = kmt_ref[...]
    kmt_row = lax.broadcasted_iota(jnp.int32, kmt.shape, 0)
    kmt_head = lax.broadcasted_iota(jnp.int32, kmt.shape, 1) // HEAD_DIM
    for i in range(tm // MOBA_BLOCK):
        km = jnp.mean(k[i * MOBA_BLOCK:(i + 1) * MOBA_BLOCK, :], axis=0, keepdims=True)
        km = jnp.concatenate([km[:, (h // group) * HEAD_DIM:(h // group + 1) * HEAD_DIM] for h in range(nh)], axis=1)
        n = t * (tm // MOBA_BLOCK) + i
        kmt = jnp.where(kmt_row == kmt_head * nblk + n, km, kmt)
    kmt_ref[...] = kmt

    s = _dot_t(q, kmt, precision=lax.Precision.HIGHEST)
    width = nh * nblk
    col = lax.broadcasted_iota(jnp.int32, (tm, width), 1)
    n_idx = col % nblk
    own = (t * tm + lax.broadcasted_iota(jnp.int32, (tm, width), 0)) // MOBA_BLOCK
    rank = jnp.zeros((tm, width), jnp.int32)
    for dlt in range(1, nblk):
        wrap = n_idx + dlt >= nblk
        other = jnp.where(wrap, pltpu.roll(s, nblk - dlt, 1), pltpu.roll(s, width - dlt, 1))
        m_idx = jnp.where(wrap, n_idx + dlt - nblk, n_idx + dlt)
        beats = (m_idx < own) & ((other > s) | ((other == s) & (m_idx < n_idx)))
        rank = rank + beats.astype(jnp.int32)
    sel_ref[0] = ((n_idx < own) & (rank < TOP_K_BLOCKS)).astype(F32)


def _qkv_prompt(x, g, w, cos, sin, *, tm, nh, nkv):
    bsz, seq, d = x.shape
    nq, nk = nh * HEAD_DIM, nkv * HEAD_DIM
    nblk = seq // MOBA_BLOCK
    assert seq % tm == 0 and tm % MOBA_BLOCK == 0 and nh * nblk == LANES
    tok = lambda c: pl.BlockSpec((1, tm, c), lambda b, t: (b, t, 0))
    full = lambda a: pl.BlockSpec(a.shape, lambda b, t: (0,) * a.ndim)
    g = g.reshape(1, -1)
    return pl.pallas_call(
        functools.partial(_qkv_prompt_kernel, tm=tm, nh=nh, nkv=nkv, nblk=nblk),
        grid=(bsz, seq // tm),
        in_specs=[tok(d), full(g), full(w),
                  pl.BlockSpec((tm, HEAD_DIM), lambda b, t: (t, 0)), pl.BlockSpec((tm, HEAD_DIM), lambda b, t: (t, 0))],
        out_specs=[tok(nq), tok(nk), tok(nk), tok(nk), tok(nk), tok(nh * nblk)],
        out_shape=[jax.ShapeDtypeStruct((bsz, seq, nq), BF16),
                   jax.ShapeDtypeStruct((bsz, seq, nk), F32), jax.ShapeDtypeStruct((bsz, seq, nk), F32),
                   jax.ShapeDtypeStruct((bsz, seq, nk), BF16), jax.ShapeDtypeStruct((bsz, seq, nk), BF16),
                   jax.ShapeDtypeStruct((bsz, seq, nh * nblk), F32)],
        scratch_shapes=[pltpu.VMEM((nh * nblk, nq), F32)],
        compiler_params=_params("parallel", "arbitrary"),
        name="qkv_prompt",
    )(x, g, w, cos, sin)


def _qkv_sample_kernel(x_ref, g_ref, w_ref, cos_ref, sin_ref, q_ref, k_ref, v_ref, *, nh, nkv):
    nb = x_ref.shape[0]
    nq = nh * HEAD_DIM
    nk = nkv * HEAD_DIM
    qkv = _dot(_rms(x_ref[...], g_ref[...]).astype(BF16), w_ref[...])
    cos = cos_ref[...]
    sin = sin_ref[...]
    lane = lax.broadcasted_iota(jnp.int32, (nb, HEAD_DIM), 1)
    scale = HEAD_DIM ** -0.5
    q_ref[...] = jnp.concatenate([_rope_head(qkv[:, h * HEAD_DIM:(h + 1) * HEAD_DIM], cos, sin, lane) * scale
                                  for h in range(nh)], axis=1)
    k_ref[...] = jnp.concatenate([_rope_head(qkv[:, nq + h * HEAD_DIM:nq + (h + 1) * HEAD_DIM], cos, sin, lane)
                                  for h in range(nkv)], axis=1)
    v_ref[...] = qkv[:, nq + nk:]


def _qkv_sample(x, g, w, cos, sin, *, nh, nkv):
    nb = x.shape[0]
    nq, nk = nh * HEAD_DIM, nkv * HEAD_DIM
    return pl.pallas_call(
        functools.partial(_qkv_sample_kernel, nh=nh, nkv=nkv),
        out_shape=[jax.ShapeDtypeStruct((nb, nq), F32), jax.ShapeDtypeStruct((nb, nk), F32),
                   jax.ShapeDtypeStruct((nb, nk), F32)],
        compiler_params=pltpu.CompilerParams(vmem_limit_bytes=VMEM_LIMIT),
        name="qkv_sample",
    )(x, g.reshape(1, -1), w, cos, sin)


def _attn_prompt_kernel(q_ref, kb_ref, vb_ref, sel_ref, h_ref, wo_ref, out_ref, *, nh, nkv, nblk):
    o = pl.program_id(1)
    hk = pl.program_id(2)
    group = nh // nkv
    blk = MOBA_BLOCK
    q = jnp.concatenate([q_ref[0, :, g * HEAD_DIM:(g + 1) * HEAD_DIM] for g in range(group)], axis=0)
    rows = group * blk

    own0 = pl.multiple_of(o * blk, blk)
    s = _dot_t(q, kb_ref[0, pl.ds(own0, blk), :])
    qpos = lax.broadcasted_iota(jnp.int32, (rows, blk), 0) % blk
    kpos = lax.broadcasted_iota(jnp.int32, (rows, blk), 1)
    s = jnp.where(kpos <= qpos, s, MASK_VALUE)
    m = jnp.max(s, axis=-1, keepdims=True)
    p = jnp.exp(s - m)
    l = jnp.sum(p, axis=-1, keepdims=True)
    acc = _dot(p.astype(BF16), vb_ref[0, pl.ds(own0, blk), :])

    sel = sel_ref[0]
    lane = lax.broadcasted_iota(jnp.int32, sel.shape, 1)
    bias = (sel - 1.0) * (-MASK_VALUE)
    q_aug = jnp.concatenate(
        [q, jnp.concatenate([jnp.where(lane // nblk == hk * group + g, bias, 0.0) for g in range(group)],
                            axis=0).astype(BF16)], axis=1)

    def past(j, carry):
        m, l, acc = carry
        j0 = pl.multiple_of(j * blk, blk)
        pick = jnp.where(lane % nblk == j, 1.0, 0.0).astype(BF16)
        s = _dot_t(q_aug, jnp.concatenate([kb_ref[0, pl.ds(j0, blk), :], pick], axis=1))
        m_new = jnp.maximum(m, jnp.max(s, axis=-1, keepdims=True))
        alpha = jnp.exp(m - m_new)
        p = jnp.exp(s - m_new)
        l = alpha * l + jnp.sum(p, axis=-1, keepdims=True)
        acc = alpha * acc + _dot(p.astype(BF16), vb_ref[0, pl.ds(j0, blk), :])
        return m_new, l, acc

    m, l, acc = lax.fori_loop(0, o, past, (m, l, acc))
    att = acc / l
    att = jnp.concatenate([att[g * blk:(g + 1) * blk, :] for g in range(group)], axis=1).astype(BF16)
    proj = _dot(att, wo_ref[...])

    @pl.when(hk == 0)
    def _():
        out_ref[0] = h_ref[0] + proj

    @pl.when(hk != 0)
    def _():
        out_ref[0] += proj


def _attn_prompt(q, kb, vb, sel, h, wo, *, nh, nkv):
    bsz, seq, d = h.shape
    nblk = seq // MOBA_BLOCK
    gw = (nh // nkv) * HEAD_DIM
    return pl.pallas_call(
        functools.partial(_attn_prompt_kernel, nh=nh, nkv=nkv, nblk=nblk),
        grid=(bsz, nblk, nkv),
        in_specs=[pl.BlockSpec((1, MOBA_BLOCK, gw), lambda b, o, k: (b, o, k)),
                  pl.BlockSpec((1, seq, HEAD_DIM), lambda b, o, k: (b, 0, k)),
                  pl.BlockSpec((1, seq, HEAD_DIM), lambda b, o, k: (b, 0, k)),
                  pl.BlockSpec((1, MOBA_BLOCK, sel.shape[2]), lambda b, o, k: (b, o, 0)),
                  pl.BlockSpec((1, MOBA_BLOCK, d), lambda b, o, k: (b, o, 0)),
                  pl.BlockSpec((gw, d), lambda b, o, k: (k, 0))],
        out_specs=pl.BlockSpec((1, MOBA_BLOCK, d), lambda b, o, k: (b, o, 0)),
        out_shape=jax.ShapeDtypeStruct((bsz, seq, d), F32),
        compiler_params=_params("parallel", "parallel", "arbitrary"),
        name="attn_prompt",
    )(q, kb, vb, sel, h, wo)


PAGES_PER_STEP = 32


def _sample_select_kernel(pt_ref, *refs, nh, nkv, ppb, nblk):
    pages = refs[:PAGES_PER_STEP]
    q_ref, idx_ref, km_ref = refs[PAGES_PER_STEP:]
    s = pl.program_id(1)
    group = nh // nkv
    bps = PAGES_PER_STEP // ppb
    psize = pages[0].shape[1] // nkv

    @pl.when(s == 0)
    def _():
        km_ref[...] = jnp.zeros(km_ref.shape, F32)

    km = km_ref[...]
    km_row = lax.broadcasted_iota(jnp.int32, km.shape, 0)
    for i in range(bps):
        tot = jnp.concatenate(
            [sum(jnp.sum(pages[i * ppb + j][0, pl.ds(kv, psize, stride=nkv), :], axis=0, keepdims=True)
                 for j in range(ppb)) for kv in range(nkv)], axis=1)
        km = jnp.where(km_row == s * bps + i, tot / (psize * ppb), km)
    km_ref[...] = km

    @pl.when(s == pl.num_programs(1) - 1)
    def _():
        km = km_ref[...]
        q = q_ref[0]
        sc = jnp.concatenate(
            [_dot_t(q[kv * group:(kv + 1) * group, :], km[:, kv * HEAD_DIM:(kv + 1) * HEAD_DIM],
                    precision=lax.Precision.HIGHEST) for kv in range(nkv)], axis=0)
        col = lax.broadcasted_iota(jnp.int32, sc.shape, 1)
        lane = lax.broadcasted_iota(jnp.int32, (nh, LANES), 1)
        out = jnp.zeros((nh, LANES), jnp.int32)
        for r in range(TOP_K_BLOCKS):
            best = jnp.max(sc, axis=-1, keepdims=True)
            pick = jnp.min(jnp.where(sc == best, col, nblk), axis=-1, keepdims=True)
            out = jnp.where(lane == r, pick, out)
            sc = jnp.where(col == pick, -jnp.inf, sc)
        idx_ref[0] = out


def _sample_select(cache_k, page_table, q, *, nh, nkv):
    npool, prow, hd = cache_k.shape
    nb, npages = page_table.shape
    ppb = MOBA_BLOCK // (prow // nkv)
    nblk = npages // ppb
    assert npages % PAGES_PER_STEP == 0 and PAGES_PER_STEP % ppb == 0 and nblk >= TOP_K_BLOCKS
    page_spec = lambda j: pl.BlockSpec((1, prow, hd), lambda b, s, pt: (pt[b * npages + s * PAGES_PER_STEP + j], 0, 0))
    grid_spec = pltpu.PrefetchScalarGridSpec(
        num_scalar_prefetch=1,
        grid=(nb, npages // PAGES_PER_STEP),
        in_specs=[page_spec(j) for j in range(PAGES_PER_STEP)]
        + [pl.BlockSpec((1, nh, HEAD_DIM), lambda b, s, pt: (b, 0, 0))],
        out_specs=pl.BlockSpec((1, nh, LANES), lambda b, s, pt: (b, 0, 0)),
        scratch_shapes=[pltpu.VMEM((nblk, nkv * hd), F32)],
    )
    return pl.pallas_call(
        functools.partial(_sample_select_kernel, nh=nh, nkv=nkv, ppb=ppb, nblk=nblk),
        grid_spec=grid_spec,
        out_shape=jax.ShapeDtypeStruct((nb, nh, LANES), jnp.int32),
        compiler_params=_params("parallel", "arbitrary"),
        name="sample_select",
    )(page_table.reshape(-1), *([cache_k] * PAGES_PER_STEP), q.reshape(nb, nh, HEAD_DIM))


def _sample_attn_kernel(pt_ref, idx_ref, q_ref, kn_ref, vn_ref, *refs, group, nkv, npg):
    kpages = refs[:group * npg]
    vpages = refs[group * npg:2 * group * npg]
    out_ref = refs[2 * group * npg]
    hkv = pl.program_id(1)
    prow = kpages[0].shape[1]
    mine = lax.broadcasted_iota(jnp.int32, (prow, 1), 0) % nkv == hkv
    outs = []
    for g in range(group):
        q = q_ref[0, g:g + 1, :]
        m = jnp.sum(q * kn_ref[0], axis=-1, keepdims=True)
        l = jnp.ones((1, 1), F32)
        acc = vn_ref[0]
        for i in range(npg):
            k = kpages[g * npg + i][0]
            sc = jnp.where(mine, jnp.sum(k * q, axis=-1, keepdims=True), MASK_VALUE)
            m_new = jnp.maximum(m, jnp.max(sc, axis=0, keepdims=True))
            alpha = jnp.exp(m - m_new)
            p = jnp.exp(sc - m_new)
            l = alpha * l + jnp.sum(p, axis=0, keepdims=True)
            acc = alpha * acc + jnp.sum(p * vpages[g * npg + i][0], axis=0, keepdims=True)
            m = m_new
        outs.append(acc / l)
    out_ref[0] = jnp.concatenate(outs, axis=0)


def _sample_attn(cache_k, cache_v, page_table, idx, q, kn, vn, *, nh, nkv):
    npool, prow, hd = cache_k.shape
    nb, npages = page_table.shape
    ppb = MOBA_BLOCK // (prow // nkv)
    group = nh // nkv
    nsel = TOP_K_BLOCKS
    npg = nsel * ppb

    def page_spec(g, i):
        def imap(b, kv, pt, ix):
            blk = ix[(b * nh + kv * group + g) * nsel + i // ppb]
            return (pt[b * npages + blk * ppb + i % ppb], 0, 0)
        return pl.BlockSpec((1, prow, hd), imap)

    qspec = pl.BlockSpec((1, group, HEAD_DIM), lambda b, kv, pt, ix: (b * nkv + kv, 0, 0))
    kvspec = pl.BlockSpec((1, 1, HEAD_DIM), lambda b, kv, pt, ix: (b * nkv + kv, 0, 0))
    pages = [page_spec(g, i) for g in range(group) for i in range(npg)]
    grid_spec = pltpu.PrefetchScalarGridSpec(
        num_scalar_prefetch=2,
        grid=(nb, nkv),
        in_specs=[qspec, kvspec, kvspec] + pages + pages,
        out_specs=qspec,
    )
    out = pl.pallas_call(
        functools.partial(_sample_attn_kernel, group=group, nkv=nkv, npg=npg),
        grid_spec=grid_spec,
        out_shape=jax.ShapeDtypeStruct((nb * nkv, group, HEAD_DIM), F32),
        compiler_params=_params("parallel", "parallel"),
        name="sample_attn",
    )(page_table.reshape(-1), idx.reshape(-1),
      q.reshape(nb * nkv, group, HEAD_DIM), kn.reshape(nb * nkv, 1, HEAD_DIM), vn.reshape(nb * nkv, 1, HEAD_DIM),
      *([cache_k] * (group * npg)), *([cache_v] * (group * npg)))
    return out.reshape(nb, nh * HEAD_DIM)


def _proj_residual_kernel(a_ref, h_ref, w_ref, out_ref):
    out_ref[...] = h_ref[...] + _dot(a_ref[...].astype(BF16), w_ref[...])


def _proj_residual(a, h, w):
    return pl.pallas_call(
        _proj_residual_kernel,
        out_shape=jax.ShapeDtypeStruct(h.shape, F32),
        compiler_params=pltpu.CompilerParams(vmem_limit_bytes=VMEM_LIMIT),
        name="proj_residual",
    )(a, h, w)


MXU_DEPTH = 256
MOE_ROW_GRAN = 64
MOE_VARIANTS = 8


def _router_kernel(x_ref, g_ref, wrt_ref, tri_ref, xn_ref, gate_ref, rank_ref, cnt_ref):
    xn = _rms(x_ref[...], g_ref[...])
    xn_ref[...] = xn.astype(BF16)
    logits = _dot_t(wrt_ref[...], xn, precision=lax.Precision.HIGHEST)
    ne = logits.shape[0]
    e = jnp.exp(logits - jnp.max(logits, axis=0, keepdims=True))
    probs = e / jnp.sum(e, axis=0, keepdims=True)
    eid = lax.broadcasted_iota(jnp.int32, probs.shape, 0)
    rest = probs
    member = jnp.zeros(probs.shape, jnp.bool_)
    top_sum = jnp.zeros((1, probs.shape[1]), F32)
    for _ in range(TOP_K_EXPERTS):
        best = jnp.max(rest, axis=0, keepdims=True)
        pick = eid == jnp.min(jnp.where(rest == best, eid, ne), axis=0, keepdims=True)
        member = member | pick
        top_sum = top_sum + best
        rest = jnp.where(pick, -1.0, rest)
    gate_ref[...] = jnp.where(member, probs / top_sum, 0.0)
    mem = jnp.where(member, 1.0, 0.0)
    rank = _dot(mem.astype(BF16), tri_ref[...])
    rank_ref[...] = jnp.where(member, rank, -1.0)
    cnt_ref[0] = jnp.broadcast_to(jnp.sum(mem, axis=1, keepdims=True), cnt_ref.shape[1:])


def _router(x, g, wr, *, tm):
    n, d = x.shape
    ne = wr.shape[1]
    nt = n // tm
    assert n % tm == 0
    tri = (lax.broadcasted_iota(jnp.int32, (tm, tm), 0) < lax.broadcasted_iota(jnp.int32, (tm, tm), 1)).astype(BF16)
    full = lambda a: pl.BlockSpec(a.shape, lambda i: (0,) * a.ndim)
    g = g.reshape(1, -1)
    wrt = wr.T
    xn, gate, rank, cnt = pl.pallas_call(
        _router_kernel,
        grid=(nt,),
        in_specs=[pl.BlockSpec((tm, d), lambda i: (i, 0)), full(g), full(wrt), full(tri)],
        out_specs=[pl.BlockSpec((tm, d), lambda i: (i, 0)), pl.BlockSpec((ne, tm), lambda i: (0, i)),
                   pl.BlockSpec((ne, tm), lambda i: (0, i)), pl.BlockSpec((1, ne, LANES), lambda i: (i, 0, 0))],
        out_shape=[jax.ShapeDtypeStruct((n, d), BF16), jax.ShapeDtypeStruct((ne, n), F32),
                   jax.ShapeDtypeStruct((ne, n), F32), jax.ShapeDtypeStruct((nt, ne, LANES), F32)],
        compiler_params=_params("parallel"),
        name="router",
    )(x, g, wrt, tri)
    return xn, gate, rank, cnt[:, :, 0].astype(jnp.int32).reshape(-1)


def _moe_kernel(eid_ref, cnt_ref, xn_ref, gate_ref, rank_ref, wg_ref, wu_ref, wd_ref, hin_ref, out_ref, xc_ref, y_ref,
                *, tm, gran, nvar, ne):
    e = eid_ref[0]
    i = pl.program_id(0)
    nf = wg_ref.shape[1]
    cnt = cnt_ref[i * ne + e]
    ngrp = (cnt + gran - 1) // gran
    rank = rank_ref[pl.ds(e, 1), :]
    kq = min(MXU_DEPTH, tm)

    def onehot(r0, rows):
        slot = lax.broadcasted_iota(jnp.int32, (rows, tm), 0) + r0
        return slot.astype(F32) == rank

    def process(r0, rows):
        krows = -(-rows // kq) * kq
        sel = jnp.where(onehot(r0, rows), 1.0, 0.0).astype(BF16)
        xc_ref[0:rows, :] = _dot(sel, xn_ref[...]).astype(BF16)
        y_ref[0:krows, :] = jnp.zeros((krows, y_ref.shape[1]), F32)

        def ffn(fc, carry):
            xc = xc_ref[0:rows, :]
            a = (_silu(_dot(xc, wg_ref[0, fc])) * _dot(xc, wu_ref[0, fc])).astype(BF16)
            y_ref[0:rows, :] += _dot(a, wd_ref[0, fc])
            return carry
        lax.fori_loop(0, nf, ffn, 0)
        w = jnp.where(onehot(r0, krows), gate_ref[pl.ds(e, 1), :], 0.0).astype(BF16)
        out_ref[...] += lax.dot_general(w, y_ref[0:krows, :].astype(BF16), (((0,), (0,)), ((), ())),
                                        preferred_element_type=F32)

    out_ref[...] = hin_ref[...]

    for n in range(1, nvar + 1):
        @pl.when(ngrp == n)
        def _(n=n):
            process(0, n * gran)

    @pl.when(ngrp > nvar)
    def _():
        full = nvar * gran

        def body(c, carry):
            process(c * full, full)
            return carry
        lax.fori_loop(0, (cnt + full - 1) // full, body, 0)


def _moe_finish_kernel(h_ref, p_ref, gp_ref, wpg_ref, wpp_ref, gf_ref, out_ref):
    out_ref[...] = _rms(_ple(h_ref[...], p_ref[...], gp_ref[...], wpg_ref[...], wpp_ref[...]), gf_ref[...])


def _moe(h, g, wr, wg, wu, wd, p, gp, wpg, wpp, gf, *, tm):
    n, d = h.shape
    ne, nf, _, tf = wg.shape
    nt = n // tm
    gran = min(MOE_ROW_GRAN, tm)
    nvar = min(MOE_VARIANTS, tm // gran)
    full = nvar * gran
    assert n % tm == 0 and tm % full == 0 and full % min(MXU_DEPTH, tm) == 0
    xn, gate, rank, cnt = _router(h, g, wr, tm=tm)
    resident = lambda a: pl.BlockSpec((1,) + a.shape[1:], lambda i, eid, c: (eid[0], 0, 0, 0),
                                      pipeline_mode=pl.Buffered(1))
    grid_spec = pltpu.PrefetchScalarGridSpec(
        num_scalar_prefetch=2,
        grid=(nt,),
        in_specs=[pl.BlockSpec((tm, d), lambda i, eid, c: (i, 0)),
                  pl.BlockSpec((ne, tm), lambda i, eid, c: (0, i)),
                  pl.BlockSpec((ne, tm), lambda i, eid, c: (0, i)),
                  resident(wg), resident(wu), resident(wd),
                  pl.BlockSpec((tm, d), lambda i, eid, c: (i, 0))],
        out_specs=pl.BlockSpec((tm, d), lambda i, eid, c: (i, 0)),
        scratch_shapes=[pltpu.VMEM((full, d), BF16), pltpu.VMEM((full, d), F32)],
    )
    one_expert = pl.pallas_call(
        functools.partial(_moe_kernel, tm=tm, gran=gran, nvar=nvar, ne=ne),
        grid_spec=grid_spec,
        out_shape=jax.ShapeDtypeStruct((n, d), F32),
        input_output_aliases={8: 0},
        compiler_params=_params("arbitrary"),
        name="moe_expert",
    )
    hmoe = h
    for e in range(ne):
        hmoe = one_expert(jnp.full((1,), e, jnp.int32), cnt, xn, gate, rank, wg, wu, wd, hmoe)

    row = lambda a: a.reshape(1, -1)
    whole = lambda a: pl.BlockSpec(a.shape, lambda i: (0,) * a.ndim)
    return pl.pallas_call(
        _moe_finish_kernel,
        grid=(nt,),
        in_specs=[pl.BlockSpec((tm, d), lambda i: (i, 0)), pl.BlockSpec((tm, p.shape[1]), lambda i: (i, 0)),
                  whole(row(gp)), whole(wpg), whole(wpp), whole(row(gf))],
        out_specs=pl.BlockSpec((tm, d), lambda i: (i, 0)),
        out_shape=jax.ShapeDtypeStruct((n, d), F32),
        compiler_params=_params("parallel"),
        name="moe_finish",
    )(hmoe, p, row(gp), wpg, wpp, row(gf))


def _rope_tables(pos):
    half = ROT_DIM // 2
    inv = jnp.power(ROPE_THETA, -jnp.arange(half, dtype=F32) / half)
    ang = pos.astype(F32)[:, None] * inv[None, :]
    cos, sin = jnp.cos(ang), jnp.sin(ang)
    rest = HEAD_DIM - ROT_DIM
    n = pos.shape[0]
    return (jnp.concatenate([cos, cos, jnp.ones((n, rest), F32)], axis=1),
            jnp.concatenate([-sin, sin, jnp.zeros((n, rest), F32)], axis=1))


def _tile(n, pref):
    return pref if n % pref == 0 else n


def kernel(x_prompt, x_sample, state_conv, cache_k, cache_v, page_table, p_prompt, p_sample, norm_mix, norm_ffn, norm_ple, ple_w_gate, ple_w_proj, conv_w_pw1, conv_b_pw1, conv_w_dw, conv_b_dw, conv_ln_g, conv_ln_b, conv_w_pw2, conv_b_pw2, ffn_w_gate, ffn_w_up, ffn_w_down, attn_w_qkv, attn_w_o, moe_w_router, moe_w_gate, moe_w_up, moe_w_down, norm_final):
    bsz, seq, d = x_prompt.shape
    nb, dec_seq, _ = x_sample.shape
    assert dec_seq == 1 and norm_mix.shape[0] == 2
    nh = d // HEAD_DIM
    nkv = (attn_w_qkv.shape[2] // HEAD_DIM - nh) // 2
    n_tok = bsz * seq
    dff = ffn_w_gate.shape[2]
    psize = cache_k.shape[2]
    past_len = page_table.shape[1] * psize
    bf = lambda w: w.astype(BF16)

    cw = (norm_mix[0], bf(conv_w_pw1[0]), conv_b_pw1[0], conv_w_dw[0], conv_b_dw[0], conv_ln_g[0], conv_ln_b[0],
          bf(conv_w_pw2[0]), conv_b_pw2[0])
    hp, conv_p = _conv_prompt(x_prompt, *cw, tm=_tile(seq, 512))
    hs, conv_s = _conv_sample(x_sample.reshape(nb, d), state_conv[0], *cw)
    fw = (norm_ffn[0], bf(ffn_w_gate[0]), bf(ffn_w_up[0]), bf(ffn_w_down[0]))
    pw0 = (norm_ple[0], bf(ple_w_gate[0]), bf(ple_w_proj[0]))
    tf = _tile(dff, 512)
    hp = _ffn(hp.reshape(n_tok, d), *fw, p_prompt[0].reshape(n_tok, -1), *pw0, tm=_tile(n_tok, 1024), tf=tf)
    hs = _ffn(hs, *fw, p_sample[0].reshape(nb, -1), *pw0, tm=nb, tf=tf)

    wqkv, wo = bf(attn_w_qkv[0]), bf(attn_w_o[0])
    cos_p, sin_p = _rope_tables(jnp.arange(seq, dtype=jnp.int32))
    q, k_p, v_p, kb, vb, sel = _qkv_prompt(hp.reshape(bsz, seq, d), norm_mix[1], wqkv, cos_p, sin_p,
                                           tm=_tile(seq, 512), nh=nh, nkv=nkv)
    hp = _attn_prompt(q, kb, vb, sel, hp.reshape(bsz, seq, d), wo, nh=nh, nkv=nkv).reshape(n_tok, d)

    cos_s, sin_s = _rope_tables(jnp.full((1,), past_len, jnp.int32))
    qs, k_s, v_s = _qkv_sample(hs, norm_mix[1], wqkv, cos_s, sin_s, nh=nh, nkv=nkv)
    ck = cache_k[0].reshape(cache_k.shape[1], psize * nkv, HEAD_DIM)
    cv = cache_v[0].reshape(cache_v.shape[1], psize * nkv, HEAD_DIM)
    idx = _sample_select(ck, page_table, qs, nh=nh, nkv=nkv)[:, :, :TOP_K_BLOCKS]
    att_s = _sample_attn(ck, cv, page_table, idx, qs, k_s, v_s, nh=nh, nkv=nkv)
    hs = _proj_residual(att_s, hs, wo)

    ne = moe_w_gate.shape[1]
    nf = dff // tf
    slices_in = lambda w: bf(w).reshape(ne, d, nf, tf).transpose(0, 2, 1, 3)
    mw = (norm_ffn[1], moe_w_router[0], slices_in(moe_w_gate[0]), slices_in(moe_w_up[0]),
          bf(moe_w_down[0]).reshape(ne, nf, tf, d))
    pw1 = (norm_ple[1], bf(ple_w_gate[1]), bf(ple_w_proj[1]), norm_final)
    yp = _moe(hp, *mw, p_prompt[1].reshape(n_tok, -1), *pw1, tm=_tile(n_tok, 1024))
    ys = _moe(hs, *mw, p_sample[1].reshape(nb, -1), *pw1, tm=nb)

    return (yp.reshape(bsz, seq, d), ys.reshape(nb, 1, d),
            conv_p[None], conv_s[None],
            k_p.reshape(1, bsz, seq, nkv, HEAD_DIM), v_p.reshape(1, bsz, seq, nkv, HEAD_DIM),
            k_s.reshape(1, nb, 1, nkv, HEAD_DIM), v_s.reshape(1, nb, 1, nkv, HEAD_DIM))
```

```python
import functools

import jax
import jax.numpy as jnp
from jax import lax
from jax.experimental import pallas as pl
from jax.experimental.pallas import tpu as pltpu

F32 = jnp.float32
BF16 = jnp.bfloat16

EPS = 1e-6
HEAD_DIM = 128
ROT_DIM = HEAD_DIM // 4
ROPE_THETA = 500000.0
MOBA_BLOCK = 256
TOP_K_BLOCKS = 3
TOP_K_EXPERTS = 2
LANES = 128
SUBLANES = 8
MASK_VALUE = -1e30
VMEM_LIMIT = 56 * 1024 * 1024


def _params(*sem):
    return pltpu.CompilerParams(dimension_semantics=sem, vmem_limit_bytes=VMEM_LIMIT)


def _rms(x, g):
    r = lax.rsqrt(jnp.mean(x * x, axis=-1, keepdims=True) + EPS)
    return x * r * g


def _dot(a, b):
    return jnp.dot(a, b, preferred_element_type=F32)


def _dot_t(a, b, precision=None):
    return lax.dot_general(a, b, (((1,), (1,)), ((), ())), preferred_element_type=F32, precision=precision)


def _dot_t_split(a, b):
    ah, bh = a.astype(BF16), b.astype(BF16)
    al, bl = (a - ah.astype(F32)).astype(BF16), (b - bh.astype(F32)).astype(BF16)
    return _dot_t(ah, bh) + _dot_t(ah, bl) + _dot_t(al, bh)


def _silu(x):
    return x * jax.nn.sigmoid(x)


def _ple(h, p, g, w_gate, w_proj):
    gate = jax.nn.sigmoid(_dot(_rms(h, g).astype(BF16), w_gate))
    return h + gate * _dot(p.astype(BF16), w_proj)


def _rope_head(x, cos, sin, lane):
    half = ROT_DIM // 2
    partner = jnp.where(lane < half, pltpu.roll(x, HEAD_DIM - half, 1), pltpu.roll(x, half, 1))
    return x * cos + partner * sin


CONV_HALO = 32
CONV_CHUNK = 16


def _conv_prompt_kernel(x_ref, g_ref, w1_ref, b1_ref, wrep_ref, bdw_ref, lng_ref, lnb_ref, w2_ref, b2_ref,
                        out_ref, state_ref, upad_ref, y_ref, *, tm, width, dc):
    t = pl.program_id(1)

    @pl.when(t == 0)
    def _():
        upad_ref[...] = jnp.zeros(upad_ref.shape, F32)

    x = x_ref[0]
    uu = _dot(_rms(x, g_ref[...]).astype(BF16), w1_ref[...]) + b1_ref[...]
    upad_ref[CONV_HALO:CONV_HALO + tm, :] = uu[:, :dc] * jax.nn.sigmoid(uu[:, dc:])

    first = CONV_HALO - (width - 1)
    span = CONV_CHUNK + SUBLANES

    def chunk(i, carry):
        base = pl.multiple_of(i * CONV_CHUNK, CONV_CHUNK)
        acc = jnp.broadcast_to(bdw_ref[...], (CONV_CHUNK, dc))
        for r in range(SUBLANES):
            part = None
            for k in range(width):
                a, kr = divmod(first + k, SUBLANES)
                if kr != r:
                    continue
                w = wrep_ref[k * SUBLANES:(k + 1) * SUBLANES, :]
                term = upad_ref[pl.ds(base + a * SUBLANES, span), :] * jnp.concatenate([w] * (span // SUBLANES), axis=0)
                part = term if part is None else part + term
            if part is not None:
                acc = acc + part[r:r + CONV_CHUNK, :]
        mu = jnp.mean(acc, axis=-1, keepdims=True)
        d = acc - mu
        var = jnp.mean(d * d, axis=-1, keepdims=True)
        z = d * lax.rsqrt(var + EPS) * lng_ref[...] + lnb_ref[...]
        y_ref[pl.ds(base, CONV_CHUNK), :] = _silu(z).astype(BF16)
        return carry

    lax.fori_loop(0, tm // CONV_CHUNK, chunk, 0)
    out_ref[0] = x + _dot(y_ref[...], w2_ref[...]) + b2_ref[...]
    state_ref[0] = upad_ref[tm + first:tm + CONV_HALO, :]
    upad_ref[0:CONV_HALO, :] = upad_ref[tm:tm + CONV_HALO, :]


def _conv_prompt(x, g, w1, b1, wdw, bdw, lng, lnb, w2, b2, *, tm):
    bsz, seq, d = x.shape
    width, dc = wdw.shape
    assert seq % tm == 0 and tm % CONV_CHUNK == 0 and width - 1 <= CONV_HALO
    row = lambda a: a.reshape(1, -1)
    full = lambda a: pl.BlockSpec(a.shape, lambda b, t: (0,) * a.ndim)
    wrep = jnp.repeat(wdw, SUBLANES, axis=0)
    args = (row(g), w1, row(b1), wrep, row(bdw), row(lng), row(lnb), w2, row(b2))
    return pl.pallas_call(
        functools.partial(_conv_prompt_kernel, tm=tm, width=width, dc=dc),
        grid=(bsz, seq // tm),
        in_specs=[pl.BlockSpec((1, tm, d), lambda b, t: (b, t, 0))] + [full(a) for a in args],
        out_specs=[pl.BlockSpec((1, tm, d), lambda b, t: (b, t, 0)),
                   pl.BlockSpec((1, width - 1, dc), lambda b, t: (b, 0, 0))],
        out_shape=[jax.ShapeDtypeStruct((bsz, seq, d), F32),
                   jax.ShapeDtypeStruct((bsz, width - 1, dc), F32)],
        scratch_shapes=[pltpu.VMEM((tm + CONV_HALO + SUBLANES, dc), F32), pltpu.VMEM((tm, dc), BF16)],
        compiler_params=_params("parallel", "arbitrary"),
        name="conv_prompt",
    )(x, *args)


def _conv_sample_kernel(x_ref, st_ref, g_ref, w1_ref, b1_ref, wdw_ref, bdw_ref, lng_ref, lnb_ref, w2_ref, b2_ref,
                        out_ref, state_ref, *, width, dc):
    nb = x_ref.shape[0]
    x = x_ref[...]
    uu = _dot(_rms(x, g_ref[...]).astype(BF16), w1_ref[...]) + b1_ref[...]
    u = uu[:, :dc] * jax.nn.sigmoid(uu[:, dc:])
    st = st_ref[...]
    acc = jnp.sum(st * wdw_ref[0:width - 1, :][None], axis=1) + u * wdw_ref[width - 1:width, :] + bdw_ref[...]
    mu = jnp.mean(acc, axis=-1, keepdims=True)
    d = acc - mu
    var = jnp.mean(d * d, axis=-1, keepdims=True)
    z = d * lax.rsqrt(var + EPS) * lng_ref[...] + lnb_ref[...]
    out_ref[...] = x + _dot(_silu(z).astype(BF16), w2_ref[...]) + b2_ref[...]
    state_ref[:, 0:width - 2, :] = st_ref[:, 1:width - 1, :]
    for b in range(nb):
        state_ref[b, width - 2:width - 1, :] = u[b:b + 1, :]


def _conv_sample(x, st, g, w1, b1, wdw, bdw, lng, lnb, w2, b2):
    nb, d = x.shape
    width, dc = wdw.shape
    row = lambda a: a.reshape(1, -1)
    return pl.pallas_call(
        functools.partial(_conv_sample_kernel, width=width, dc=dc),
        out_shape=[jax.ShapeDtypeStruct((nb, d), F32), jax.ShapeDtypeStruct((nb, width - 1, dc), F32)],
        compiler_params=pltpu.CompilerParams(vmem_limit_bytes=VMEM_LIMIT),
        name="conv_sample",
    )(x, st, row(g), w1, row(b1), wdw, row(bdw), row(lng), row(lnb), w2, row(b2))


def _ffn_kernel(x_ref, g_ref, wg_ref, wu_ref, wd_ref, p_ref, gp_ref, wpg_ref, wpp_ref, out_ref, xn_ref, acc_ref):
    f = pl.program_id(1)

    @pl.when(f == 0)
    def _():
        xn_ref[...] = _rms(x_ref[...], g_ref[...]).astype(BF16)
        acc_ref[...] = jnp.zeros(acc_ref.shape, F32)

    xn = xn_ref[...]
    a = (_silu(_dot(xn, wg_ref[...])) * _dot(xn, wu_ref[...])).astype(BF16)
    acc_ref[...] += _dot(a, wd_ref[...])

    @pl.when(f == pl.num_programs(1) - 1)
    def _():
        out_ref[...] = _ple(x_ref[...] + acc_ref[...], p_ref[...], gp_ref[...], wpg_ref[...], wpp_ref[...])


def _ffn(x, g, wg, wu, wd, p, gp, wpg, wpp, *, tm, tf):
    n, d = x.shape
    dff = wg.shape[1]
    assert n % tm == 0 and dff % tf == 0
    row = lambda a: a.reshape(1, -1)
    full = lambda a: pl.BlockSpec(a.shape, lambda i, f: (0,) * a.ndim)
    return pl.pallas_call(
        _ffn_kernel,
        grid=(n // tm, dff // tf),
        in_specs=[pl.BlockSpec((tm, d), lambda i, f: (i, 0)), full(row(g)),
                  pl.BlockSpec((d, tf), lambda i, f: (0, f)), pl.BlockSpec((d, tf), lambda i, f: (0, f)),
                  pl.BlockSpec((tf, d), lambda i, f: (f, 0)),
                  pl.BlockSpec((tm, p.shape[1]), lambda i, f: (i, 0)), full(row(gp)), full(wpg), full(wpp)],
        out_specs=pl.BlockSpec((tm, d), lambda i, f: (i, 0)),
        out_shape=jax.ShapeDtypeStruct((n, d), F32),
        scratch_shapes=[pltpu.VMEM((tm, d), BF16), pltpu.VMEM((tm, d), F32)],
        compiler_params=_params("parallel", "arbitrary"),
        name="ffn_ple",
    )(x, row(g), wg, wu, wd, p, row(gp), wpg, wpp)


def _qkv_prompt_kernel(x_ref, g_ref, w_ref, cos_ref, sin_ref,
                       q_ref, k_ref, v_ref, kb_ref, vb_ref, sel_ref, kmt_ref, *, tm, nh, nkv, nblk):
    t = pl.program_id(1)
    group = nh // nkv
    nq = nh * HEAD_DIM
    nk = nkv * HEAD_DIM

    @pl.when(t == 0)
    def _():
        kmt_ref[...] = jnp.zeros(kmt_ref.shape, F32)

    qkv = _dot(_rms(x_ref[0], g_ref[...]).astype(BF16), w_ref[...])
    cos = cos_ref[...]
    sin = sin_ref[...]
    lane = lax.broadcasted_iota(jnp.int32, (tm, HEAD_DIM), 1)
    scale = HEAD_DIM ** -0.5
    q = jnp.concatenate([_rope_head(qkv[:, h * HEAD_DIM:(h + 1) * HEAD_DIM], cos, sin, lane) * scale
                         for h in range(nh)], axis=1)
    k = jnp.concatenate([_rope_head(qkv[:, nq + h * HEAD_DIM:nq + (h + 1) * HEAD_DIM], cos, sin, lane)
                         for h in range(nkv)], axis=1)
    v = qkv[:, nq + nk:]
    q_ref[0] = q.astype(BF16)
    for kv in range(nkv):
        k_ref[0, pl.ds(kv, tm, stride=nkv), :] = k[:, kv * HEAD_DIM:(kv + 1) * HEAD_DIM]
        v_ref[0, pl.ds(kv, tm, stride=nkv), :] = v[:, kv * HEAD_DIM:(kv + 1) * HEAD_DIM]
    kb_ref[0] = k.astype(BF16)
    vb_ref[0] = v.astype(BF16)

    kmt = kmt_ref[...]
    kmt_row = lax.broadcasted_iota(jnp.int32, kmt.shape, 0)
    kmt_head = lax.broadcasted_iota(jnp.int32, kmt.shape, 1) // HEAD_DIM
    for i in range(tm // MOBA_BLOCK):
        km = jnp.mean(k[i * MOBA_BLOCK:(i + 1) * MOBA_BLOCK, :], axis=0, keepdims=True)
        km = jnp.concatenate([km[:, (h // group) * HEAD_DIM:(h // group + 1) * HEAD_DIM] for h in range(nh)], axis=1)
        n = t * (tm // MOBA_BLOCK) + i
        kmt = jnp.where(kmt_row == kmt_head * nblk + n, km, kmt)
    kmt_ref[...] = kmt

    width = nh * nblk
    col = lax.broadcasted_iota(jnp.int32, (tm, width), 1)
    n_idx = col % nblk
    own = (t * tm + lax.broadcasted_iota(jnp.int32, (tm, width), 0)) // MOBA_BLOCK
    past = n_idx < own
    s = jnp.where(past, _dot_t_split(q, kmt), -jnp.inf)
    rank = jnp.zeros((tm, width), F32)
    for dlt in range(1, nblk):
        wrap = n_idx + dlt >= nblk
        other = jnp.where(wrap, pltpu.roll(s, nblk - dlt, 1), pltpu.roll(s, width - dlt, 1))
        rank = rank + jnp.where(other > s, 1.0, 0.0) + jnp.where(wrap, jnp.where(other == s, 1.0, 0.0), 0.0)
    sel_ref[0] = jnp.where(past, jnp.where(rank < TOP_K_BLOCKS, 1.0, 0.0), 0.0)


def _qkv_prompt(x, g, w, cos, sin, *, tm, nh, nkv):
    bsz, seq, d = x.shape
    nq, nk = nh * HEAD_DIM, nkv * HEAD_DIM
    nblk = seq // MOBA_BLOCK
    assert seq % tm == 0 and tm % MOBA_BLOCK == 0 and nh * nblk == LANES
    tok = lambda c: pl.BlockSpec((1, tm, c), lambda b, t: (b, t, 0))
    kv_rows = pl.BlockSpec((1, tm * nkv, HEAD_DIM), lambda b, t: (b, t, 0))
    full = lambda a: pl.BlockSpec(a.shape, lambda b, t: (0,) * a.ndim)
    g = g.reshape(1, -1)
    return pl.pallas_call(
        functools.partial(_qkv_prompt_kernel, tm=tm, nh=nh, nkv=nkv, nblk=nblk),
        grid=(bsz, seq // tm),
        in_specs=[tok(d), full(g), full(w),
                  pl.BlockSpec((tm, HEAD_DIM), lambda b, t: (t, 0)), pl.BlockSpec((tm, HEAD_DIM), lambda b, t: (t, 0))],
        out_specs=[tok(nq), kv_rows, kv_rows, tok(nk), tok(nk), tok(nh * nblk)],
        out_shape=[jax.ShapeDtypeStruct((bsz, seq, nq), BF16),
                   jax.ShapeDtypeStruct((bsz, seq * nkv, HEAD_DIM), F32),
                   jax.ShapeDtypeStruct((bsz, seq * nkv, HEAD_DIM), F32),
                   jax.ShapeDtypeStruct((bsz, seq, nk), BF16), jax.ShapeDtypeStruct((bsz, seq, nk), BF16),
                   jax.ShapeDtypeStruct((bsz, seq, nh * nblk), F32)],
        scratch_shapes=[pltpu.VMEM((nh * nblk, nq), F32)],
        compiler_params=_params("parallel", "arbitrary"),
        name="qkv_prompt",
    )(x, g, w, cos, sin)


def _qkv_sample_kernel(x_ref, g_ref, w_ref, cos_ref, sin_ref, q_ref, k_ref, v_ref, *, nh, nkv):
    nb = x_ref.shape[0]
    nq = nh * HEAD_DIM
    nk = nkv * HEAD_DIM
    qkv = _dot(_rms(x_ref[...], g_ref[...]).astype(BF16), w_ref[...])
    cos = cos_ref[...]
    sin = sin_ref[...]
    lane = lax.broadcasted_iota(jnp.int32, (nb, HEAD_DIM), 1)
    scale = HEAD_DIM ** -0.5
    q_ref[...] = jnp.concatenate([_rope_head(qkv[:, h * HEAD_DIM:(h + 1) * HEAD_DIM], cos, sin, lane) * scale
                                  for h in range(nh)], axis=1)
    k_ref[...] = jnp.concatenate([_rope_head(qkv[:, nq + h * HEAD_DIM:nq + (h + 1) * HEAD_DIM], cos, sin, lane)
                                  for h in range(nkv)], axis=1)
    v_ref[...] = qkv[:, nq + nk:]


def _qkv_sample(x, g, w, cos, sin, *, nh, nkv):
    nb = x.shape[0]
    nq, nk = nh * HEAD_DIM, nkv * HEAD_DIM
    return pl.pallas_call(
        functools.partial(_qkv_sample_kernel, nh=nh, nkv=nkv),
        out_shape=[jax.ShapeDtypeStruct((nb, nq), F32), jax.ShapeDtypeStruct((nb, nk), F32),
                   jax.ShapeDtypeStruct((nb, nk), F32)],
        compiler_params=pltpu.CompilerParams(vmem_limit_bytes=VMEM_LIMIT),
        name="qkv_sample",
    )(x, g.reshape(1, -1), w, cos, sin)


def _attn_prompt_kernel(q_ref, kb_ref, vb_ref, sel_ref, h_ref, wo_ref, out_ref, *, nh, nkv, nblk):
    o = pl.program_id(1)
    hk = pl.program_id(2)
    group = nh // nkv
    blk = MOBA_BLOCK
    q = jnp.concatenate([q_ref[0, :, g * HEAD_DIM:(g + 1) * HEAD_DIM] for g in range(group)], axis=0)
    rows = group * blk

    own0 = pl.multiple_of(o * blk, blk)
    s = _dot_t(q, kb_ref[0, pl.ds(own0, blk), :])
    qpos = lax.broadcasted_iota(jnp.int32, (rows, blk), 0) % blk
    kpos = lax.broadcasted_iota(jnp.int32, (rows, blk), 1)
    s = jnp.where(kpos <= qpos, s, MASK_VALUE)
    m = jnp.max(s, axis=-1, keepdims=True)
    p = jnp.exp(s - m)
    l = jnp.sum(p, axis=-1, keepdims=True)
    acc = _dot(p.astype(BF16), vb_ref[0, pl.ds(own0, blk), :])

    sel = sel_ref[0]
    lane = lax.broadcasted_iota(jnp.int32, sel.shape, 1)
    bias = (sel - 1.0) * (-MASK_VALUE)
    q_aug = jnp.concatenate(
        [q, jnp.concatenate([jnp.where(lane // nblk == hk * group + g, bias, 0.0) for g in range(group)],
                            axis=0).astype(BF16)], axis=1)

    def past(j, carry):
        m, l, acc = carry
        j0 = pl.multiple_of(j * blk, blk)
        pick = jnp.where(lane % nblk == j, 1.0, 0.0).astype(BF16)
        s = _dot_t(q_aug, jnp.concatenate([kb_ref[0, pl.ds(j0, blk), :], pick], axis=1))
        m_new = jnp.maximum(m, jnp.max(s, axis=-1, keepdims=True))
        alpha = jnp.exp(m - m_new)
        p = jnp.exp(s - m_new)
        l = alpha * l + jnp.sum(p, axis=-1, keepdims=True)
        acc = alpha * acc + _dot(p.astype(BF16), vb_ref[0, pl.ds(j0, blk), :])
        return m_new, l, acc

    m, l, acc = lax.fori_loop(0, o, past, (m, l, acc))
    att = acc / l
    att = jnp.concatenate([att[g * blk:(g + 1) * blk, :] for g in range(group)], axis=1).astype(BF16)
    proj = _dot(att, wo_ref[...])

    @pl.when(hk == 0)
    def _():
        out_ref[0] = h_ref[0] + proj

    @pl.when(hk != 0)
    def _():
        out_ref[0] += proj


def _attn_prompt(q, kb, vb, sel, h, wo, *, nh, nkv):
    bsz, seq, d = h.shape
    nblk = seq // MOBA_BLOCK
    gw = (nh // nkv) * HEAD_DIM
    return pl.pallas_call(
        functools.partial(_attn_prompt_kernel, nh=nh, nkv=nkv, nblk=nblk),
        grid=(bsz, nblk, nkv),
        in_specs=[pl.BlockSpec((1, MOBA_BLOCK, gw), lambda b, o, k: (b, o, k)),
                  pl.BlockSpec((1, seq, HEAD_DIM), lambda b, o, k: (b, 0, k)),
                  pl.BlockSpec((1, seq, HEAD_DIM), lambda b, o, k: (b, 0, k)),
                  pl.BlockSpec((1, MOBA_BLOCK, sel.shape[2]), lambda b, o, k: (b, o, 0)),
                  pl.BlockSpec((1, MOBA_BLOCK, d), lambda b, o, k: (b, o, 0)),
                  pl.BlockSpec((gw, d), lambda b, o, k: (k, 0))],
        out_specs=pl.BlockSpec((1, MOBA_BLOCK, d), lambda b, o, k: (b, o, 0)),
        out_shape=jax.ShapeDtypeStruct((bsz, seq, d), F32),
        compiler_params=_params("parallel", "parallel", "arbitrary"),
        name="attn_prompt",
    )(q, kb, vb, sel, h, wo)


PAGES_PER_STEP = 32


def _sample_select_kernel(pt_ref, *refs, nh, nkv, ppb, nblk):
    pages = refs[:PAGES_PER_STEP]
    q_ref, idx_ref, km_ref = refs[PAGES_PER_STEP:]
    s = pl.program_id(1)
    group = nh // nkv
    bps = PAGES_PER_STEP // ppb
    psize = pages[0].shape[1] // nkv

    @pl.when(s == 0)
    def _():
        km_ref[...] = jnp.zeros(km_ref.shape, F32)

    km = km_ref[...]
    km_row = lax.broadcasted_iota(jnp.int32, km.shape, 0)
    for i in range(bps):
        tot = jnp.concatenate(
            [sum(jnp.sum(pages[i * ppb + j][0, pl.ds(kv, psize, stride=nkv), :], axis=0, keepdims=True)
                 for j in range(ppb)) for kv in range(nkv)], axis=1)
        km = jnp.where(km_row == s * bps + i, tot / (psize * ppb), km)
    km_ref[...] = km

    @pl.when(s == pl.num_programs(1) - 1)
    def _():
        km = km_ref[...]
        q = q_ref[0]
        sc = jnp.concatenate(
            [_dot_t(q[kv * group:(kv + 1) * group, :], km[:, kv * HEAD_DIM:(kv + 1) * HEAD_DIM],
                    precision=lax.Precision.HIGHEST) for kv in range(nkv)], axis=0)
        col = lax.broadcasted_iota(jnp.int32, sc.shape, 1)
        lane = lax.broadcasted_iota(jnp.int32, (nh, LANES), 1)
        out = jnp.zeros((nh, LANES), jnp.int32)
        for r in range(TOP_K_BLOCKS):
            best = jnp.max(sc, axis=-1, keepdims=True)
            pick = jnp.min(jnp.where(sc == best, col, nblk), axis=-1, keepdims=True)
            out = jnp.where(lane == r, pick, out)
            sc = jnp.where(col == pick, -jnp.inf, sc)
        idx_ref[0] = out


def _sample_select(cache_k, page_table, q, *, nh, nkv):
    npool, prow, hd = cache_k.shape
    nb, npages = page_table.shape
    ppb = MOBA_BLOCK // (prow // nkv)
    nblk = npages // ppb
    assert npages % PAGES_PER_STEP == 0 and PAGES_PER_STEP % ppb == 0 and nblk >= TOP_K_BLOCKS
    page_spec = lambda j: pl.BlockSpec((1, prow, hd), lambda b, s, pt: (pt[b * npages + s * PAGES_PER_STEP + j], 0, 0))
    grid_spec = pltpu.PrefetchScalarGridSpec(
        num_scalar_prefetch=1,
        grid=(nb, npages // PAGES_PER_STEP),
        in_specs=[page_spec(j) for j in range(PAGES_PER_STEP)]
        + [pl.BlockSpec((1, nh, HEAD_DIM), lambda b, s, pt: (b, 0, 0))],
        out_specs=pl.BlockSpec((1, nh, LANES), lambda b, s, pt: (b, 0, 0)),
        scratch_shapes=[pltpu.VMEM((nblk, nkv * hd), F32)],
    )
    return pl.pallas_call(
        functools.partial(_sample_select_kernel, nh=nh, nkv=nkv, ppb=ppb, nblk=nblk),
        grid_spec=grid_spec,
        out_shape=jax.ShapeDtypeStruct((nb, nh, LANES), jnp.int32),
        compiler_params=_params("parallel", "arbitrary"),
        name="sample_select",
    )(page_table.reshape(-1), *([cache_k] * PAGES_PER_STEP), q.reshape(nb, nh, HEAD_DIM))


def _sample_attn_kernel(pt_ref, idx_ref, q_ref, kn_ref, vn_ref, *refs, group, nkv, npg):
    kpages = refs[:group * npg]
    vpages = refs[group * npg:2 * group * npg]
    out_ref = refs[2 * group * npg]
    hkv = pl.program_id(1)
    prow = kpages[0].shape[1]
    mine = lax.broadcasted_iota(jnp.int32, (prow, 1), 0) % nkv == hkv
    outs = []
    for g in range(group):
        q = q_ref[0, g:g + 1, :]
        m = jnp.sum(q * kn_ref[0], axis=-1, keepdims=True)
        l = jnp.ones((1, 1), F32)
        acc = vn_ref[0]
        for i in range(npg):
            k = kpages[g * npg + i][0]
            sc = jnp.where(mine, jnp.sum(k * q, axis=-1, keepdims=True), MASK_VALUE)
            m_new = jnp.maximum(m, jnp.max(sc, axis=0, keepdims=True))
            alpha = jnp.exp(m - m_new)
            p = jnp.exp(sc - m_new)
            l = alpha * l + jnp.sum(p, axis=0, keepdims=True)
            acc = alpha * acc + jnp.sum(p * vpages[g * npg + i][0], axis=0, keepdims=True)
            m = m_new
        outs.append(acc / l)
    out_ref[0] = jnp.concatenate(outs, axis=0)


def _sample_attn(cache_k, cache_v, page_table, idx, q, kn, vn, *, nh, nkv):
    npool, prow, hd = cache_k.shape
    nb, npages = page_table.shape
    ppb = MOBA_BLOCK // (prow // nkv)
    group = nh // nkv
    nsel = TOP_K_BLOCKS
    npg = nsel * ppb

    def page_spec(g, i):
        def imap(b, kv, pt, ix):
            blk = ix[(b * nh + kv * group + g) * nsel + i // ppb]
            return (pt[b * npages + blk * ppb + i % ppb], 0, 0)
        return pl.BlockSpec((1, prow, hd), imap)

    qspec = pl.BlockSpec((1, group, HEAD_DIM), lambda b, kv, pt, ix: (b * nkv + kv, 0, 0))
    kvspec = pl.BlockSpec((1, 1, HEAD_DIM), lambda b, kv, pt, ix: (b * nkv + kv, 0, 0))
    pages = [page_spec(g, i) for g in range(group) for i in range(npg)]
    grid_spec = pltpu.PrefetchScalarGridSpec(
        num_scalar_prefetch=2,
        grid=(nb, nkv),
        in_specs=[qspec, kvspec, kvspec] + pages + pages,
        out_specs=qspec,
    )
    out = pl.pallas_call(
        functools.partial(_sample_attn_kernel, group=group, nkv=nkv, npg=npg),
        grid_spec=grid_spec,
        out_shape=jax.ShapeDtypeStruct((nb * nkv, group, HEAD_DIM), F32),
        compiler_params=_params("parallel", "parallel"),
        name="sample_attn",
    )(page_table.reshape(-1), idx.reshape(-1),
      q.reshape(nb * nkv, group, HEAD_DIM), kn.reshape(nb * nkv, 1, HEAD_DIM), vn.reshape(nb * nkv, 1, HEAD_DIM),
      *([cache_k] * (group * npg)), *([cache_v] * (group * npg)))
    return out.reshape(nb, nh * HEAD_DIM)


def _proj_residual_kernel(a_ref, h_ref, w_ref, out_ref):
    out_ref[...] = h_ref[...] + _dot(a_ref[...].astype(BF16), w_ref[...])


def _proj_residual(a, h, w):
    return pl.pallas_call(
        _proj_residual_kernel,
        out_shape=jax.ShapeDtypeStruct(h.shape, F32),
        compiler_params=pltpu.CompilerParams(vmem_limit_bytes=VMEM_LIMIT),
        name="proj_residual",
    )(a, h, w)


MXU_DEPTH = 256
MOE_ROW_GRAN = 64
MOE_VARIANTS = 8


def _router_kernel(x_ref, g_ref, wrt_ref, tri_ref, xn_ref, gate_ref, rank_ref, cnt_ref):
    xn = _rms(x_ref[...], g_ref[...])
    xn_ref[...] = xn.astype(BF16)
    logits = _dot_t_split(wrt_ref[...], xn)
    ne = logits.shape[0]
    e = jnp.exp(logits - jnp.max(logits, axis=0, keepdims=True))
    probs = e / jnp.sum(e, axis=0, keepdims=True)
    eid = lax.broadcasted_iota(jnp.int32, probs.shape, 0)
    rest = probs
    member = jnp.zeros(probs.shape, jnp.bool_)
    top_sum = jnp.zeros((1, probs.shape[1]), F32)
    for _ in range(TOP_K_EXPERTS):
        best = jnp.max(rest, axis=0, keepdims=True)
        pick = eid == jnp.min(jnp.where(rest == best, eid, ne), axis=0, keepdims=True)
        member = member | pick
        top_sum = top_sum + best
        rest = jnp.where(pick, -1.0, rest)
    gate_ref[...] = jnp.where(member, probs / top_sum, 0.0)
    mem = jnp.where(member, 1.0, 0.0)
    rank = _dot(mem.astype(BF16), tri_ref[...])
    rank_ref[...] = jnp.where(member, rank, -1.0)
    cnt_ref[0] = jnp.broadcast_to(jnp.sum(mem, axis=1, keepdims=True), cnt_ref.shape[1:])


def _router(x, g, wr, *, tm):
    n, d = x.shape
    ne = wr.shape[1]
    nt = n // tm
    assert n % tm == 0
    tri = (lax.broadcasted_iota(jnp.int32, (tm, tm), 0) < lax.broadcasted_iota(jnp.int32, (tm, tm), 1)).astype(BF16)
    full = lambda a: pl.BlockSpec(a.shape, lambda i: (0,) * a.ndim)
    g = g.reshape(1, -1)
    wrt = wr.T
    xn, gate, rank, cnt = pl.pallas_call(
        _router_kernel,
        grid=(nt,),
        in_specs=[pl.BlockSpec((tm, d), lambda i: (i, 0)), full(g), full(wrt), full(tri)],
        out_specs=[pl.BlockSpec((tm, d), lambda i: (i, 0)), pl.BlockSpec((ne, tm), lambda i: (0, i)),
                   pl.BlockSpec((ne, tm), lambda i: (0, i)), pl.BlockSpec((1, ne, LANES), lambda i: (i, 0, 0))],
        out_shape=[jax.ShapeDtypeStruct((n, d), BF16), jax.ShapeDtypeStruct((ne, n), F32),
                   jax.ShapeDtypeStruct((ne, n), F32), jax.ShapeDtypeStruct((nt, ne, LANES), F32)],
        compiler_params=_params("parallel"),
        name="router",
    )(x, g, wrt, tri)
    return xn, gate, rank, cnt[:, :, 0].astype(jnp.int32).reshape(-1)


def _moe_kernel(eid_ref, cnt_ref, xn_ref, gate_ref, rank_ref, wg_ref, wu_ref, wd_ref, hin_ref, out_ref, xc_ref, y_ref,
                *, tm, tf, gran, nvar, ne):
    e = eid_ref[0]
    i = pl.program_id(0)
    cnt = cnt_ref[i * ne + e]
    ngrp = (cnt + gran - 1) // gran
    rank = rank_ref[pl.ds(e, 1), :]
    kq = min(MXU_DEPTH, tm)

    def onehot(r0, rows):
        slot = lax.broadcasted_iota(jnp.int32, (rows, tm), 0) + r0
        return slot.astype(F32) == rank

    def process(r0, rows):
        krows = -(-rows // kq) * kq
        sel = jnp.where(onehot(r0, rows), 1.0, 0.0).astype(BF16)
        xc_ref[0:rows, :] = _dot(sel, xn_ref[...]).astype(BF16)
        y_ref[0:krows, :] = jnp.zeros((krows, y_ref.shape[1]), F32)

        def ffn(fc, carry):
            xc = xc_ref[0:rows, :]
            f0 = pl.multiple_of(fc * tf, tf)
            a = (_silu(_dot(xc, wg_ref[0, :, pl.ds(f0, tf)])) * _dot(xc, wu_ref[0, :, pl.ds(f0, tf)])).astype(BF16)
            y_ref[0:rows, :] += _dot(a, wd_ref[0, pl.ds(f0, tf), :])
            return carry
        lax.fori_loop(0, wg_ref.shape[2] // tf, ffn, 0)
        w = jnp.where(onehot(r0, krows), gate_ref[pl.ds(e, 1), :], 0.0).astype(BF16)
        out_ref[...] += lax.dot_general(w, y_ref[0:krows, :].astype(BF16), (((0,), (0,)), ((), ())),
                                        preferred_element_type=F32)

    out_ref[...] = hin_ref[...]

    for n in range(1, nvar + 1):
        @pl.when(ngrp == n)
        def _(n=n):
            process(0, n * gran)

    @pl.when(ngrp > nvar)
    def _():
        full = nvar * gran

        def body(c, carry):
            process(c * full, full)
            return carry
        lax.fori_loop(0, (cnt + full - 1) // full, body, 0)


def _moe_finish_kernel(h_ref, p_ref, gp_ref, wpg_ref, wpp_ref, gf_ref, out_ref):
    out_ref[...] = _rms(_ple(h_ref[...], p_ref[...], gp_ref[...], wpg_ref[...], wpp_ref[...]), gf_ref[...])


def _moe(h, g, wr, wg, wu, wd, p, gp, wpg, wpp, gf, *, tm, tf):
    n, d = h.shape
    ne, _, dff = wg.shape
    nt = n // tm
    gran = min(MOE_ROW_GRAN, tm)
    nvar = min(MOE_VARIANTS, tm // gran)
    full = nvar * gran
    assert n % tm == 0 and dff % tf == 0 and tm % full == 0 and full % min(MXU_DEPTH, tm) == 0
    xn, gate, rank, cnt = _router(h, g, wr, tm=tm)
    resident = lambda a: pl.BlockSpec((1,) + a.shape[1:], lambda i, eid, c: (eid[0], 0, 0),
                                      pipeline_mode=pl.Buffered(1))
    grid_spec = pltpu.PrefetchScalarGridSpec(
        num_scalar_prefetch=2,
        grid=(nt,),
        in_specs=[pl.BlockSpec((tm, d), lambda i, eid, c: (i, 0)),
                  pl.BlockSpec((ne, tm), lambda i, eid, c: (0, i)),
                  pl.BlockSpec((ne, tm), lambda i, eid, c: (0, i)),
                  resident(wg), resident(wu), resident(wd),
                  pl.BlockSpec((tm, d), lambda i, eid, c: (i, 0))],
        out_specs=pl.BlockSpec((tm, d), lambda i, eid, c: (i, 0)),
        scratch_shapes=[pltpu.VMEM((full, d), BF16), pltpu.VMEM((full, d), F32)],
    )
    one_expert = pl.pallas_call(
        functools.partial(_moe_kernel, tm=tm, tf=tf, gran=gran, nvar=nvar, ne=ne),
        grid_spec=grid_spec,
        out_shape=jax.ShapeDtypeStruct((n, d), F32),
        input_output_aliases={8: 0},
        compiler_params=_params("arbitrary"),
        name="moe_expert",
    )
    hmoe = h
    for e in range(ne):
        hmoe = one_expert(jnp.full((1,), e, jnp.int32), cnt, xn, gate, rank, wg, wu, wd, hmoe)

    row = lambda a: a.reshape(1, -1)
    whole = lambda a: pl.BlockSpec(a.shape, lambda i: (0,) * a.ndim)
    return pl.pallas_call(
        _moe_finish_kernel,
        grid=(nt,),
        in_specs=[pl.BlockSpec((tm, d), lambda i: (i, 0)), pl.BlockSpec((tm, p.shape[1]), lambda i: (i, 0)),
                  whole(row(gp)), whole(wpg), whole(wpp), whole(row(gf))],
        out_specs=pl.BlockSpec((tm, d), lambda i: (i, 0)),
        out_shape=jax.ShapeDtypeStruct((n, d), F32),
        compiler_params=_params("parallel"),
        name="moe_finish",
    )(hmoe, p, row(gp), wpg, wpp, row(gf))


def _rope_tables(pos):
    half = ROT_DIM // 2
    inv = jnp.power(ROPE_THETA, -jnp.arange(half, dtype=F32) / half)
    ang = pos.astype(F32)[:, None] * inv[None, :]
    cos, sin = jnp.cos(ang), jnp.sin(ang)
    rest = HEAD_DIM - ROT_DIM
    n = pos.shape[0]
    return (jnp.concatenate([cos, cos, jnp.ones((n, rest), F32)], axis=1),
            jnp.concatenate([-sin, sin, jnp.zeros((n, rest), F32)], axis=1))


def _tile(n, pref):
    return pref if n % pref == 0 else n


def kernel(x_prompt, x_sample, state_conv, cache_k, cache_v, page_table, p_prompt, p_sample, norm_mix, norm_ffn, norm_ple, ple_w_gate, ple_w_proj, conv_w_pw1, conv_b_pw1, conv_w_dw, conv_b_dw, conv_ln_g, conv_ln_b, conv_w_pw2, conv_b_pw2, ffn_w_gate, ffn_w_up, ffn_w_down, attn_w_qkv, attn_w_o, moe_w_router, moe_w_gate, moe_w_up, moe_w_down, norm_final):
    bsz, seq, d = x_prompt.shape
    nb, dec_seq, _ = x_sample.shape
    assert dec_seq == 1 and norm_mix.shape[0] == 2
    nh = d // HEAD_DIM
    nkv = (attn_w_qkv.shape[2] // HEAD_DIM - nh) // 2
    n_tok = bsz * seq
    dff = ffn_w_gate.shape[2]
    psize = cache_k.shape[2]
    past_len = page_table.shape[1] * psize
    bf = lambda w: w.astype(BF16)

    cw = (norm_mix[0], bf(conv_w_pw1[0]), conv_b_pw1[0], conv_w_dw[0], conv_b_dw[0], conv_ln_g[0], conv_ln_b[0],
          bf(conv_w_pw2[0]), conv_b_pw2[0])
    hp, conv_p = _conv_prompt(x_prompt, *cw, tm=_tile(seq, 512))
    hs, conv_s = _conv_sample(x_sample.reshape(nb, d), state_conv[0], *cw)
    fw = (norm_ffn[0], bf(ffn_w_gate[0]), bf(ffn_w_up[0]), bf(ffn_w_down[0]))
    pw0 = (norm_ple[0], bf(ple_w_gate[0]), bf(ple_w_proj[0]))
    tf = _tile(dff, 512)
    hp = _ffn(hp.reshape(n_tok, d), *fw, p_prompt[0].reshape(n_tok, -1), *pw0, tm=_tile(n_tok, 1024), tf=tf)
    hs = _ffn(hs, *fw, p_sample[0].reshape(nb, -1), *pw0, tm=nb, tf=tf)

    wqkv, wo = bf(attn_w_qkv[0]), bf(attn_w_o[0])
    cos_p, sin_p = _rope_tables(jnp.arange(seq, dtype=jnp.int32))
    q, k_p, v_p, kb, vb, sel = _qkv_prompt(hp.reshape(bsz, seq, d), norm_mix[1], wqkv, cos_p, sin_p,
                                           tm=_tile(seq, 512), nh=nh, nkv=nkv)
    hp = _attn_prompt(q, kb, vb, sel, hp.reshape(bsz, seq, d), wo, nh=nh, nkv=nkv).reshape(n_tok, d)

    cos_s, sin_s = _rope_tables(jnp.full((1,), past_len, jnp.int32))
    qs, k_s, v_s = _qkv_sample(hs, norm_mix[1], wqkv, cos_s, sin_s, nh=nh, nkv=nkv)
    ck = cache_k[0].reshape(cache_k.shape[1], psize * nkv, HEAD_DIM)
    cv = cache_v[0].reshape(cache_v.shape[1], psize * nkv, HEAD_DIM)
    idx = _sample_select(ck, page_table, qs, nh=nh, nkv=nkv)[:, :, :TOP_K_BLOCKS]
    att_s = _sample_attn(ck, cv, page_table, idx, qs, k_s, v_s, nh=nh, nkv=nkv)
    hs = _proj_residual(att_s, hs, wo)

    mw = (norm_ffn[1], moe_w_router[0], bf(moe_w_gate[0]), bf(moe_w_up[0]), bf(moe_w_down[0]))
    pw1 = (norm_ple[1], bf(ple_w_gate[1]), bf(ple_w_proj[1]), norm_final)
    yp = _moe(hp, *mw, p_prompt[1].reshape(n_tok, -1), *pw1, tm=_tile(n_tok, 1024), tf=tf)
    ys = _moe(hs, *mw, p_sample[1].reshape(nb, -1), *pw1, tm=nb, tf=tf)

    return (yp.reshape(bsz, seq, d), ys.reshape(nb, 1, d),
            conv_p[None], conv_s[None],
            k_p.reshape(1, bsz, seq, nkv, HEAD_DIM), v_p.reshape(1, bsz, seq, nkv, HEAD_DIM),
            k_s.reshape(1, nb, 1, nkv, HEAD_DIM), v_s.reshape(1, nb, 1, nkv, HEAD_DIM))
```

```python
import functools

import jax
import jax.numpy as jnp
from jax import lax
from jax.experimental import pallas as pl
from jax.experimental.pallas import tpu as pltpu

F32 = jnp.float32
BF16 = jnp.bfloat16

EPS = 1e-6
HEAD_DIM = 128
ROT_DIM = HEAD_DIM // 4
ROPE_THETA = 500000.0
MOBA_BLOCK = 256
TOP_K_BLOCKS = 3
TOP_K_EXPERTS = 2
LANES = 128
SUBLANES = 8
MASK_VALUE = -1e30
VMEM_LIMIT = 56 * 1024 * 1024


def _params(*sem):
    return pltpu.CompilerParams(dimension_semantics=sem, vmem_limit_bytes=VMEM_LIMIT)


def _rms(x, g):
    r = lax.rsqrt(jnp.mean(x * x, axis=-1, keepdims=True) + EPS)
    return x * r * g


def _dot(a, b):
    return jnp.dot(a, b, preferred_element_type=F32)


def _dot_t(a, b, precision=None):
    return lax.dot_general(a, b, (((1,), (1,)), ((), ())), preferred_element_type=F32, precision=precision)


def _dot_t_split(a, b):
    ah, bh = a.astype(BF16), b.astype(BF16)
    al, bl = (a - ah.astype(F32)).astype(BF16), (b - bh.astype(F32)).astype(BF16)
    return _dot_t(ah, bh) + _dot_t(ah, bl) + _dot_t(al, bh)


def _silu(x):
    return x * jax.nn.sigmoid(x)


def _ple(h, p, g, w_gate, w_proj):
    gate = jax.nn.sigmoid(_dot(_rms(h, g).astype(BF16), w_gate))
    return h + gate * _dot(p.astype(BF16), w_proj)


def _rope_head(x, cos, sin, lane):
    half = ROT_DIM // 2
    partner = jnp.where(lane < half, pltpu.roll(x, HEAD_DIM - half, 1), pltpu.roll(x, half, 1))
    return x * cos + partner * sin


CONV_HALO = 32
CONV_CHUNK = 16


def _conv_prompt_kernel(x_ref, g_ref, w1_ref, b1_ref, wrep_ref, bdw_ref, lng_ref, lnb_ref, w2_ref, b2_ref,
                        out_ref, state_ref, upad_ref, y_ref, *, tm, width, dc):
    t = pl.program_id(1)

    @pl.when(t == 0)
    def _():
        upad_ref[...] = jnp.zeros(upad_ref.shape, F32)

    x = x_ref[0]
    uu = _dot(_rms(x, g_ref[...]).astype(BF16), w1_ref[...]) + b1_ref[...]
    upad_ref[CONV_HALO:CONV_HALO + tm, :] = uu[:, :dc] * jax.nn.sigmoid(uu[:, dc:])

    first = CONV_HALO - (width - 1)
    span = CONV_CHUNK + SUBLANES

    def chunk(i, carry):
        base = pl.multiple_of(i * CONV_CHUNK, CONV_CHUNK)
        acc = jnp.broadcast_to(bdw_ref[...], (CONV_CHUNK, dc))
        for r in range(SUBLANES):
            part = None
            for k in range(width):
                a, kr = divmod(first + k, SUBLANES)
                if kr != r:
                    continue
                w = wrep_ref[k * SUBLANES:(k + 1) * SUBLANES, :]
                term = upad_ref[pl.ds(base + a * SUBLANES, span), :] * jnp.concatenate([w] * (span // SUBLANES), axis=0)
                part = term if part is None else part + term
            if part is not None:
                acc = acc + part[r:r + CONV_CHUNK, :]
        mu = jnp.mean(acc, axis=-1, keepdims=True)
        d = acc - mu
        var = jnp.mean(d * d, axis=-1, keepdims=True)
        z = d * lax.rsqrt(var + EPS) * lng_ref[...] + lnb_ref[...]
        y_ref[pl.ds(base, CONV_CHUNK), :] = _silu(z).astype(BF16)
        return carry

    lax.fori_loop(0, tm // CONV_CHUNK, chunk, 0)
    out_ref[0] = x + _dot(y_ref[...], w2_ref[...]) + b2_ref[...]
    state_ref[0] = upad_ref[tm + first:tm + CONV_HALO, :]
    upad_ref[0:CONV_HALO, :] = upad_ref[tm:tm + CONV_HALO, :]


def _conv_prompt(x, g, w1, b1, wdw, bdw, lng, lnb, w2, b2, *, tm):
    bsz, seq, d = x.shape
    width, dc = wdw.shape
    assert seq % tm == 0 and tm % CONV_CHUNK == 0 and width - 1 <= CONV_HALO
    row = lambda a: a.reshape(1, -1)
    full = lambda a: pl.BlockSpec(a.shape, lambda b, t: (0,) * a.ndim)
    wrep = jnp.repeat(wdw, SUBLANES, axis=0)
    args = (row(g), w1, row(b1), wrep, row(bdw), row(lng), row(lnb), w2, row(b2))
    return pl.pallas_call(
        functools.partial(_conv_prompt_kernel, tm=tm, width=width, dc=dc),
        grid=(bsz, seq // tm),
        in_specs=[pl.BlockSpec((1, tm, d), lambda b, t: (b, t, 0))] + [full(a) for a in args],
        out_specs=[pl.BlockSpec((1, tm, d), lambda b, t: (b, t, 0)),
                   pl.BlockSpec((1, width - 1, dc), lambda b, t: (b, 0, 0))],
        out_shape=[jax.ShapeDtypeStruct((bsz, seq, d), F32),
                   jax.ShapeDtypeStruct((bsz, width - 1, dc), F32)],
        scratch_shapes=[pltpu.VMEM((tm + CONV_HALO + SUBLANES, dc), F32), pltpu.VMEM((tm, dc), BF16)],
        compiler_params=_params("parallel", "arbitrary"),
        name="conv_prompt",
    )(x, *args)


def _conv_sample_kernel(x_ref, st_ref, g_ref, w1_ref, b1_ref, wdw_ref, bdw_ref, lng_ref, lnb_ref, w2_ref, b2_ref,
                        out_ref, state_ref, *, width, dc):
    nb = x_ref.shape[0]
    x = x_ref[...]
    uu = _dot(_rms(x, g_ref[...]).astype(BF16), w1_ref[...]) + b1_ref[...]
    u = uu[:, :dc] * jax.nn.sigmoid(uu[:, dc:])
    st = st_ref[...]
    acc = jnp.sum(st * wdw_ref[0:width - 1, :][None], axis=1) + u * wdw_ref[width - 1:width, :] + bdw_ref[...]
    mu = jnp.mean(acc, axis=-1, keepdims=True)
    d = acc - mu
    var = jnp.mean(d * d, axis=-1, keepdims=True)
    z = d * lax.rsqrt(var + EPS) * lng_ref[...] + lnb_ref[...]
    out_ref[...] = x + _dot(_silu(z).astype(BF16), w2_ref[...]) + b2_ref[...]
    state_ref[:, 0:width - 2, :] = st_ref[:, 1:width - 1, :]
    for b in range(nb):
        state_ref[b, width - 2:width - 1, :] = u[b:b + 1, :]


def _conv_sample(x, st, g, w1, b1, wdw, bdw, lng, lnb, w2, b2):
    nb, d = x.shape
    width, dc = wdw.shape
    row = lambda a: a.reshape(1, -1)
    return pl.pallas_call(
        functools.partial(_conv_sample_kernel, width=width, dc=dc),
        out_shape=[jax.ShapeDtypeStruct((nb, d), F32), jax.ShapeDtypeStruct((nb, width - 1, dc), F32)],
        compiler_params=pltpu.CompilerParams(vmem_limit_bytes=VMEM_LIMIT),
        name="conv_sample",
    )(x, st, row(g), w1, row(b1), wdw, row(bdw), row(lng), row(lnb), w2, row(b2))


def _ffn_kernel(x_ref, g_ref, wg_ref, wu_ref, wd_ref, p_ref, gp_ref, wpg_ref, wpp_ref, out_ref, xn_ref, acc_ref):
    f = pl.program_id(1)

    @pl.when(f == 0)
    def _():
        xn_ref[...] = _rms(x_ref[...], g_ref[...]).astype(BF16)
        acc_ref[...] = jnp.zeros(acc_ref.shape, F32)

    xn = xn_ref[...]
    a = (_silu(_dot(xn, wg_ref[...])) * _dot(xn, wu_ref[...])).astype(BF16)
    acc_ref[...] += _dot(a, wd_ref[...])

    @pl.when(f == pl.num_programs(1) - 1)
    def _():
        out_ref[...] = _ple(x_ref[...] + acc_ref[...], p_ref[...], gp_ref[...], wpg_ref[...], wpp_ref[...])


def _ffn(x, g, wg, wu, wd, p, gp, wpg, wpp, *, tm, tf):
    n, d = x.shape
    dff = wg.shape[1]
    assert n % tm == 0 and dff % tf == 0
    row = lambda a: a.reshape(1, -1)
    full = lambda a: pl.BlockSpec(a.shape, lambda i, f: (0,) * a.ndim)
    return pl.pallas_call(
        _ffn_kernel,
        grid=(n // tm, dff // tf),
        in_specs=[pl.BlockSpec((tm, d), lambda i, f: (i, 0)), full(row(g)),
                  pl.BlockSpec((d, tf), lambda i, f: (0, f)), pl.BlockSpec((d, tf), lambda i, f: (0, f)),
                  pl.BlockSpec((tf, d), lambda i, f: (f, 0)),
                  pl.BlockSpec((tm, p.shape[1]), lambda i, f: (i, 0)), full(row(gp)), full(wpg), full(wpp)],
        out_specs=pl.BlockSpec((tm, d), lambda i, f: (i, 0)),
        out_shape=jax.ShapeDtypeStruct((n, d), F32),
        scratch_shapes=[pltpu.VMEM((tm, d), BF16), pltpu.VMEM((tm, d), F32)],
        compiler_params=_params("parallel", "arbitrary"),
        name="ffn_ple",
    )(x, row(g), wg, wu, wd, p, row(gp), wpg, wpp)


def _qkv_prompt_kernel(x_ref, g_ref, w_ref, cos_ref, sin_ref,
                       qt_ref, k_ref, v_ref, kb_ref, vt_ref, biast_ref, kmt_ref, *, tm, nh, nkv, nblk):
    t = pl.program_id(1)
    group = nh // nkv
    nq = nh * HEAD_DIM
    nk = nkv * HEAD_DIM

    @pl.when(t == 0)
    def _():
        kmt_ref[...] = jnp.zeros(kmt_ref.shape, F32)

    qkv = _dot(_rms(x_ref[0], g_ref[...]).astype(BF16), w_ref[...])
    cos = cos_ref[...]
    sin = sin_ref[...]
    lane = lax.broadcasted_iota(jnp.int32, (tm, HEAD_DIM), 1)
    scale = HEAD_DIM ** -0.5
    q_heads = [_rope_head(qkv[:, h * HEAD_DIM:(h + 1) * HEAD_DIM], cos, sin, lane) * scale for h in range(nh)]
    q = jnp.concatenate(q_heads, axis=1)
    k = jnp.concatenate([_rope_head(qkv[:, nq + h * HEAD_DIM:nq + (h + 1) * HEAD_DIM], cos, sin, lane)
                         for h in range(nkv)], axis=1)
    v = qkv[:, nq + nk:]
    for h in range(nh):
        qt_ref[0, h] = q_heads[h].T.astype(BF16)
    for kv in range(nkv):
        k_ref[0, pl.ds(kv, tm, stride=nkv), :] = k[:, kv * HEAD_DIM:(kv + 1) * HEAD_DIM]
        v_ref[0, pl.ds(kv, tm, stride=nkv), :] = v[:, kv * HEAD_DIM:(kv + 1) * HEAD_DIM]
        vt_ref[0, kv] = v[:, kv * HEAD_DIM:(kv + 1) * HEAD_DIM].T.astype(BF16)
    kb_ref[0] = k.astype(BF16)

    kmt = kmt_ref[...]
    kmt_row = lax.broadcasted_iota(jnp.int32, kmt.shape, 0)
    kmt_head = lax.broadcasted_iota(jnp.int32, kmt.shape, 1) // HEAD_DIM
    for i in range(tm // MOBA_BLOCK):
        km = jnp.mean(k[i * MOBA_BLOCK:(i + 1) * MOBA_BLOCK, :], axis=0, keepdims=True)
        km = jnp.concatenate([km[:, (h // group) * HEAD_DIM:(h // group + 1) * HEAD_DIM] for h in range(nh)], axis=1)
        n = t * (tm // MOBA_BLOCK) + i
        kmt = jnp.where(kmt_row == kmt_head * nblk + n, km, kmt)
    kmt_ref[...] = kmt

    width = nh * nblk
    col = lax.broadcasted_iota(jnp.int32, (tm, width), 1)
    n_idx = col % nblk
    own = (t * tm + lax.broadcasted_iota(jnp.int32, (tm, width), 0)) // MOBA_BLOCK
    past = n_idx < own
    s = jnp.where(past, _dot_t_split(q, kmt), -jnp.inf)
    rank = jnp.zeros((tm, width), F32)
    for dlt in range(1, nblk):
        wrap = n_idx + dlt >= nblk
        other = jnp.where(wrap, pltpu.roll(s, nblk - dlt, 1), pltpu.roll(s, width - dlt, 1))
        rank = rank + jnp.where(other > s, 1.0, 0.0) + jnp.where(wrap, jnp.where(other == s, 1.0, 0.0), 0.0)
    bias = jnp.where(past, jnp.where(rank < TOP_K_BLOCKS, 0.0, MASK_VALUE), MASK_VALUE)
    biast_ref[0] = bias.T.astype(BF16)


def _qkv_prompt(x, g, w, cos, sin, *, tm, nh, nkv):
    bsz, seq, d = x.shape
    nq, nk = nh * HEAD_DIM, nkv * HEAD_DIM
    nblk = seq // MOBA_BLOCK
    assert seq % tm == 0 and tm % MOBA_BLOCK == 0 and nh * nblk == LANES
    tok = lambda c: pl.BlockSpec((1, tm, c), lambda b, t: (b, t, 0))
    kv_rows = pl.BlockSpec((1, tm * nkv, HEAD_DIM), lambda b, t: (b, t, 0))
    full = lambda a: pl.BlockSpec(a.shape, lambda b, t: (0,) * a.ndim)
    g = g.reshape(1, -1)
    return pl.pallas_call(
        functools.partial(_qkv_prompt_kernel, tm=tm, nh=nh, nkv=nkv, nblk=nblk),
        grid=(bsz, seq // tm),
        in_specs=[tok(d), full(g), full(w),
                  pl.BlockSpec((tm, HEAD_DIM), lambda b, t: (t, 0)), pl.BlockSpec((tm, HEAD_DIM), lambda b, t: (t, 0))],
        out_specs=[pl.BlockSpec((1, nh, HEAD_DIM, tm), lambda b, t: (b, 0, 0, t)), kv_rows, kv_rows, tok(nk),
                   pl.BlockSpec((1, nkv, HEAD_DIM, tm), lambda b, t: (b, 0, 0, t)),
                   pl.BlockSpec((1, nh * nblk, tm), lambda b, t: (b, 0, t))],
        out_shape=[jax.ShapeDtypeStruct((bsz, nh, HEAD_DIM, seq), BF16),
                   jax.ShapeDtypeStruct((bsz, seq * nkv, HEAD_DIM), F32),
                   jax.ShapeDtypeStruct((bsz, seq * nkv, HEAD_DIM), F32),
                   jax.ShapeDtypeStruct((bsz, seq, nk), BF16),
                   jax.ShapeDtypeStruct((bsz, nkv, HEAD_DIM, seq), BF16),
                   jax.ShapeDtypeStruct((bsz, nh * nblk, seq), BF16)],
        scratch_shapes=[pltpu.VMEM((nh * nblk, nq), F32)],
        compiler_params=_params("parallel", "arbitrary"),
        name="qkv_prompt",
    )(x, g, w, cos, sin)


def _qkv_sample_kernel(x_ref, g_ref, w_ref, cos_ref, sin_ref, q_ref, k_ref, v_ref, *, nh, nkv):
    nb = x_ref.shape[0]
    nq = nh * HEAD_DIM
    nk = nkv * HEAD_DIM
    qkv = _dot(_rms(x_ref[...], g_ref[...]).astype(BF16), w_ref[...])
    cos = cos_ref[...]
    sin = sin_ref[...]
    lane = lax.broadcasted_iota(jnp.int32, (nb, HEAD_DIM), 1)
    scale = HEAD_DIM ** -0.5
    q_ref[...] = jnp.concatenate([_rope_head(qkv[:, h * HEAD_DIM:(h + 1) * HEAD_DIM], cos, sin, lane) * scale
                                  for h in range(nh)], axis=1)
    k_ref[...] = jnp.concatenate([_rope_head(qkv[:, nq + h * HEAD_DIM:nq + (h + 1) * HEAD_DIM], cos, sin, lane)
                                  for h in range(nkv)], axis=1)
    v_ref[...] = qkv[:, nq + nk:]


def _qkv_sample(x, g, w, cos, sin, *, nh, nkv):
    nb = x.shape[0]
    nq, nk = nh * HEAD_DIM, nkv * HEAD_DIM
    return pl.pallas_call(
        functools.partial(_qkv_sample_kernel, nh=nh, nkv=nkv),
        out_shape=[jax.ShapeDtypeStruct((nb, nq), F32), jax.ShapeDtypeStruct((nb, nk), F32),
                   jax.ShapeDtypeStruct((nb, nk), F32)],
        compiler_params=pltpu.CompilerParams(vmem_limit_bytes=VMEM_LIMIT),
        name="qkv_sample",
    )(x, g.reshape(1, -1), w, cos, sin)


def _attn_prompt_kernel(qt_ref, kb_ref, vt_ref, biast_ref, h_ref, wo_ref, out_ref, *, nh, nkv, nblk):
    o = pl.program_id(1)
    hk = pl.program_id(2)
    group = nh // nkv
    blk = MOBA_BLOCK
    bias = biast_ref[0]
    feat_head = lax.broadcasted_iota(jnp.int32, bias.shape, 0) // nblk
    q_t = jnp.concatenate([qt_ref[0, g] for g in range(group)], axis=1)
    q_aug = jnp.concatenate(
        [q_t, jnp.concatenate([jnp.where(feat_head == hk * group + g, bias, jnp.zeros_like(bias))
                               for g in range(group)], axis=1)], axis=0)
    cols = group * blk

    own0 = pl.multiple_of(o * blk, blk)
    s = _dot(kb_ref[0, pl.ds(own0, blk), :], q_t)
    kpos = lax.broadcasted_iota(jnp.int32, (blk, cols), 0)
    qpos = lax.broadcasted_iota(jnp.int32, (blk, cols), 1) % blk
    s = jnp.where(kpos <= qpos, s, MASK_VALUE)
    m = jnp.max(s, axis=0, keepdims=True)
    p = jnp.exp(s - m)
    l = jnp.sum(p, axis=0, keepdims=True)
    acc = _dot(vt_ref[0, 0, :, pl.ds(own0, blk)], p.astype(BF16))
    lane = lax.broadcasted_iota(jnp.int32, (blk, nh * nblk), 1)

    def past(j, carry):
        m, l, acc = carry
        j0 = pl.multiple_of(j * blk, blk)
        pick = jnp.where(lane % nblk == j, 1.0, 0.0).astype(BF16)
        s = _dot(jnp.concatenate([kb_ref[0, pl.ds(j0, blk), :], pick], axis=1), q_aug)
        m_new = jnp.maximum(m, jnp.max(s, axis=0, keepdims=True))
        alpha = jnp.exp(m - m_new)
        p = jnp.exp(s - m_new)
        l = alpha * l + jnp.sum(p, axis=0, keepdims=True)
        acc = alpha * acc + _dot(vt_ref[0, 0, :, pl.ds(j0, blk)], p.astype(BF16))
        return m_new, l, acc

    m, l, acc = lax.fori_loop(0, o, past, (m, l, acc))
    att = (acc / l).astype(BF16)
    proj = sum(lax.dot_general(att[:, g * blk:(g + 1) * blk], wo_ref[g * HEAD_DIM:(g + 1) * HEAD_DIM, :],
                               (((0,), (0,)), ((), ())), preferred_element_type=F32) for g in range(group))

    @pl.when(hk == 0)
    def _():
        out_ref[0] = h_ref[0] + proj

    @pl.when(hk != 0)
    def _():
        out_ref[0] += proj


def _attn_prompt(qt, kb, vt, biast, h, wo, *, nh, nkv):
    bsz, seq, d = h.shape
    nblk = seq // MOBA_BLOCK
    group = nh // nkv
    gw = group * HEAD_DIM
    return pl.pallas_call(
        functools.partial(_attn_prompt_kernel, nh=nh, nkv=nkv, nblk=nblk),
        grid=(bsz, nblk, nkv),
        in_specs=[pl.BlockSpec((1, group, HEAD_DIM, MOBA_BLOCK), lambda b, o, k: (b, k, 0, o)),
                  pl.BlockSpec((1, seq, HEAD_DIM), lambda b, o, k: (b, 0, k)),
                  pl.BlockSpec((1, 1, HEAD_DIM, seq), lambda b, o, k: (b, k, 0, 0)),
                  pl.BlockSpec((1, biast.shape[1], MOBA_BLOCK), lambda b, o, k: (b, 0, o)),
                  pl.BlockSpec((1, MOBA_BLOCK, d), lambda b, o, k: (b, o, 0)),
                  pl.BlockSpec((gw, d), lambda b, o, k: (k, 0))],
        out_specs=pl.BlockSpec((1, MOBA_BLOCK, d), lambda b, o, k: (b, o, 0)),
        out_shape=jax.ShapeDtypeStruct((bsz, seq, d), F32),
        compiler_params=_params("parallel", "parallel", "arbitrary"),
        name="attn_prompt",
    )(qt, kb, vt, biast, h, wo)


PAGES_PER_STEP = 32


def _sample_select_kernel(pt_ref, *refs, nh, nkv, ppb, nblk):
    pages = refs[:PAGES_PER_STEP]
    q_ref, idx_ref, km_ref = refs[PAGES_PER_STEP:]
    s = pl.program_id(1)
    group = nh // nkv
    bps = PAGES_PER_STEP // ppb
    psize = pages[0].shape[1] // nkv

    @pl.when(s == 0)
    def _():
        km_ref[...] = jnp.zeros(km_ref.shape, F32)

    km = km_ref[...]
    km_row = lax.broadcasted_iota(jnp.int32, km.shape, 0)
    for i in range(bps):
        tot = jnp.concatenate(
            [sum(jnp.sum(pages[i * ppb + j][0, pl.ds(kv, psize, stride=nkv), :], axis=0, keepdims=True)
                 for j in range(ppb)) for kv in range(nkv)], axis=1)
        km = jnp.where(km_row == s * bps + i, tot / (psize * ppb), km)
    km_ref[...] = km

    @pl.when(s == pl.num_programs(1) - 1)
    def _():
        km = km_ref[...]
        q = q_ref[0]
        sc = jnp.concatenate(
            [_dot_t(q[kv * group:(kv + 1) * group, :], km[:, kv * HEAD_DIM:(kv + 1) * HEAD_DIM],
                    precision=lax.Precision.HIGHEST) for kv in range(nkv)], axis=0)
        col = lax.broadcasted_iota(jnp.int32, sc.shape, 1)
        lane = lax.broadcasted_iota(jnp.int32, (nh, LANES), 1)
        out = jnp.zeros((nh, LANES), jnp.int32)
        for r in range(TOP_K_BLOCKS):
            best = jnp.max(sc, axis=-1, keepdims=True)
            pick = jnp.min(jnp.where(sc == best, col, nblk), axis=-1, keepdims=True)
            out = jnp.where(lane == r, pick, out)
            sc = jnp.where(col == pick, -jnp.inf, sc)
        idx_ref[0] = out


def _sample_select(cache_k, page_table, q, *, nh, nkv):
    npool, prow, hd = cache_k.shape
    nb, npages = page_table.shape
    ppb = MOBA_BLOCK // (prow // nkv)
    nblk = npages // ppb
    assert npages % PAGES_PER_STEP == 0 and PAGES_PER_STEP % ppb == 0 and nblk >= TOP_K_BLOCKS
    page_spec = lambda j: pl.BlockSpec((1, prow, hd), lambda b, s, pt: (pt[b * npages + s * PAGES_PER_STEP + j], 0, 0))
    grid_spec = pltpu.PrefetchScalarGridSpec(
        num_scalar_prefetch=1,
        grid=(nb, npages // PAGES_PER_STEP),
        in_specs=[page_spec(j) for j in range(PAGES_PER_STEP)]
        + [pl.BlockSpec((1, nh, HEAD_DIM), lambda b, s, pt: (b, 0, 0))],
        out_specs=pl.BlockSpec((1, nh, LANES), lambda b, s, pt: (b, 0, 0)),
        scratch_shapes=[pltpu.VMEM((nblk, nkv * hd), F32)],
    )
    return pl.pallas_call(
        functools.partial(_sample_select_kernel, nh=nh, nkv=nkv, ppb=ppb, nblk=nblk),
        grid_spec=grid_spec,
        out_shape=jax.ShapeDtypeStruct((nb, nh, LANES), jnp.int32),
        compiler_params=_params("parallel", "arbitrary"),
        name="sample_select",
    )(page_table.reshape(-1), *([cache_k] * PAGES_PER_STEP), q.reshape(nb, nh, HEAD_DIM))


def _sample_attn_kernel(pt_ref, idx_ref, q_ref, kn_ref, vn_ref, *refs, group, nkv, npg):
    kpages = refs[:group * npg]
    vpages = refs[group * npg:2 * group * npg]
    out_ref = refs[2 * group * npg]
    hkv = pl.program_id(1)
    prow = kpages[0].shape[1]
    mine = lax.broadcasted_iota(jnp.int32, (prow, 1), 0) % nkv == hkv
    outs = []
    for g in range(group):
        q = q_ref[0, g:g + 1, :]
        m = jnp.sum(q * kn_ref[0], axis=-1, keepdims=True)
        l = jnp.ones((1, 1), F32)
        acc = vn_ref[0]
        for i in range(npg):
            k = kpages[g * npg + i][0]
            sc = jnp.where(mine, jnp.sum(k * q, axis=-1, keepdims=True), MASK_VALUE)
            m_new = jnp.maximum(m, jnp.max(sc, axis=0, keepdims=True))
            alpha = jnp.exp(m - m_new)
            p = jnp.exp(sc - m_new)
            l = alpha * l + jnp.sum(p, axis=0, keepdims=True)
            acc = alpha * acc + jnp.sum(p * vpages[g * npg + i][0], axis=0, keepdims=True)
            m = m_new
        outs.append(acc / l)
    out_ref[0] = jnp.concatenate(outs, axis=0)


def _sample_attn(cache_k, cache_v, page_table, idx, q, kn, vn, *, nh, nkv):
    npool, prow, hd = cache_k.shape
    nb, npages = page_table.shape
    ppb = MOBA_BLOCK // (prow // nkv)
    group = nh // nkv
    nsel = TOP_K_BLOCKS
    npg = nsel * ppb

    def page_spec(g, i):
        def imap(b, kv, pt, ix):
            blk = ix[(b * nh + kv * group + g) * nsel + i // ppb]
            return (pt[b * npages + blk * ppb + i % ppb], 0, 0)
        return pl.BlockSpec((1, prow, hd), imap)

    qspec = pl.BlockSpec((1, group, HEAD_DIM), lambda b, kv, pt, ix: (b * nkv + kv, 0, 0))
    kvspec = pl.BlockSpec((1, 1, HEAD_DIM), lambda b, kv, pt, ix: (b * nkv + kv, 0, 0))
    pages = [page_spec(g, i) for g in range(group) for i in range(npg)]
    grid_spec = pltpu.PrefetchScalarGridSpec(
        num_scalar_prefetch=2,
        grid=(nb, nkv),
        in_specs=[qspec, kvspec, kvspec] + pages + pages,
        out_specs=qspec,
    )
    out = pl.pallas_call(
        functools.partial(_sample_attn_kernel, group=group, nkv=nkv, npg=npg),
        grid_spec=grid_spec,
        out_shape=jax.ShapeDtypeStruct((nb * nkv, group, HEAD_DIM), F32),
        compiler_params=_params("parallel", "parallel"),
        name="sample_attn",
    )(page_table.reshape(-1), idx.reshape(-1),
      q.reshape(nb * nkv, group, HEAD_DIM), kn.reshape(nb * nkv, 1, HEAD_DIM), vn.reshape(nb * nkv, 1, HEAD_DIM),
      *([cache_k] * (group * npg)), *([cache_v] * (group * npg)))
    return out.reshape(nb, nh * HEAD_DIM)


def _proj_residual_kernel(a_ref, h_ref, w_ref, out_ref):
    out_ref[...] = h_ref[...] + _dot(a_ref[...].astype(BF16), w_ref[...])


def _proj_residual(a, h, w):
    return pl.pallas_call(
        _proj_residual_kernel,
        out_shape=jax.ShapeDtypeStruct(h.shape, F32),
        compiler_params=pltpu.CompilerParams(vmem_limit_bytes=VMEM_LIMIT),
        name="proj_residual",
    )(a, h, w)


MXU_DEPTH = 256
MOE_ROW_GRAN = 64
MOE_VARIANTS = 8


def _router_kernel(x_ref, g_ref, wrt_ref, tri_ref, xn_ref, gate_ref, rank_ref, cnt_ref):
    xn = _rms(x_ref[...], g_ref[...])
    xn_ref[...] = xn.astype(BF16)
    logits = _dot_t_split(wrt_ref[...], xn)
    ne = logits.shape[0]
    e = jnp.exp(logits - jnp.max(logits, axis=0, keepdims=True))
    probs = e / jnp.sum(e, axis=0, keepdims=True)
    eid = lax.broadcasted_iota(jnp.int32, probs.shape, 0)
    rest = probs
    member = jnp.zeros(probs.shape, jnp.bool_)
    top_sum = jnp.zeros((1, probs.shape[1]), F32)
    for _ in range(TOP_K_EXPERTS):
        best = jnp.max(rest, axis=0, keepdims=True)
        pick = eid == jnp.min(jnp.where(rest == best, eid, ne), axis=0, keepdims=True)
        member = member | pick
        top_sum = top_sum + best
        rest = jnp.where(pick, -1.0, rest)
    gate_ref[...] = jnp.where(member, probs / top_sum, 0.0)
    mem = jnp.where(member, 1.0, 0.0)
    rank = _dot(mem.astype(BF16), tri_ref[...])
    rank_ref[...] = jnp.where(member, rank, -1.0)
    cnt_ref[0] = jnp.broadcast_to(jnp.sum(mem, axis=1, keepdims=True), cnt_ref.shape[1:])


def _router(x, g, wr, *, tm):
    n, d = x.shape
    ne = wr.shape[1]
    nt = n // tm
    assert n % tm == 0
    tri = (lax.broadcasted_iota(jnp.int32, (tm, tm), 0) < lax.broadcasted_iota(jnp.int32, (tm, tm), 1)).astype(BF16)
    full = lambda a: pl.BlockSpec(a.shape, lambda i: (0,) * a.ndim)
    g = g.reshape(1, -1)
    wrt = wr.T
    xn, gate, rank, cnt = pl.pallas_call(
        _router_kernel,
        grid=(nt,),
        in_specs=[pl.BlockSpec((tm, d), lambda i: (i, 0)), full(g), full(wrt), full(tri)],
        out_specs=[pl.BlockSpec((tm, d), lambda i: (i, 0)), pl.BlockSpec((ne, tm), lambda i: (0, i)),
                   pl.BlockSpec((ne, tm), lambda i: (0, i)), pl.BlockSpec((1, ne, LANES), lambda i: (i, 0, 0))],
        out_shape=[jax.ShapeDtypeStruct((n, d), BF16), jax.ShapeDtypeStruct((ne, n), F32),
                   jax.ShapeDtypeStruct((ne, n), F32), jax.ShapeDtypeStruct((nt, ne, LANES), F32)],
        compiler_params=_params("parallel"),
        name="router",
    )(x, g, wrt, tri)
    return xn, gate, rank, cnt[:, :, 0].astype(jnp.int32).reshape(-1)


def _moe_kernel(eid_ref, cnt_ref, xn_ref, gate_ref, rank_ref, wg_ref, wu_ref, wd_ref, hin_ref, out_ref, xc_ref, y_ref,
                *, tm, tf, gran, nvar, ne):
    e = eid_ref[0]
    i = pl.program_id(0)
    cnt = cnt_ref[i * ne + e]
    ngrp = (cnt + gran - 1) // gran
    rank = rank_ref[pl.ds(e, 1), :]
    kq = min(MXU_DEPTH, tm)

    def onehot(r0, rows):
        slot = lax.broadcasted_iota(jnp.int32, (rows, tm), 0) + r0
        return slot.astype(F32) == rank

    def process(r0, rows):
        krows = -(-rows // kq) * kq
        sel = jnp.where(onehot(r0, rows), 1.0, 0.0).astype(BF16)
        xc_ref[0:rows, :] = _dot(sel, xn_ref[...]).astype(BF16)
        y_ref[0:krows, :] = jnp.zeros((krows, y_ref.shape[1]), F32)

        def ffn(fc, carry):
            xc = xc_ref[0:rows, :]
            f0 = pl.multiple_of(fc * tf, tf)
            a = (_silu(_dot(xc, wg_ref[0, :, pl.ds(f0, tf)])) * _dot(xc, wu_ref[0, :, pl.ds(f0, tf)])).astype(BF16)
            y_ref[0:rows, :] += _dot(a, wd_ref[0, pl.ds(f0, tf), :])
            return carry
        lax.fori_loop(0, wg_ref.shape[2] // tf, ffn, 0)
        w = jnp.where(onehot(r0, krows), gate_ref[pl.ds(e, 1), :], 0.0).astype(BF16)
        out_ref[...] += lax.dot_general(w, y_ref[0:krows, :].astype(BF16), (((0,), (0,)), ((), ())),
                                        preferred_element_type=F32)

    out_ref[...] = hin_ref[...]

    for n in range(1, nvar + 1):
        @pl.when(ngrp == n)
        def _(n=n):
            process(0, n * gran)

    @pl.when(ngrp > nvar)
    def _():
        full = nvar * gran

        def body(c, carry):
            process(c * full, full)
            return carry
        lax.fori_loop(0, (cnt + full - 1) // full, body, 0)


def _moe_finish_kernel(h_ref, p_ref, gp_ref, wpg_ref, wpp_ref, gf_ref, out_ref):
    out_ref[...] = _rms(_ple(h_ref[...], p_ref[...], gp_ref[...], wpg_ref[...], wpp_ref[...]), gf_ref[...])


def _moe(h, g, wr, wg, wu, wd, p, gp, wpg, wpp, gf, *, tm, tf):
    n, d = h.shape
    ne, _, dff = wg.shape
    nt = n // tm
    gran = min(MOE_ROW_GRAN, tm)
    nvar = min(MOE_VARIANTS, tm // gran)
    full = nvar * gran
    assert n % tm == 0 and dff % tf == 0 and tm % full == 0 and full % min(MXU_DEPTH, tm) == 0
    xn, gate, rank, cnt = _router(h, g, wr, tm=tm)
    resident = lambda a: pl.BlockSpec((1,) + a.shape[1:], lambda i, eid, c: (eid[0], 0, 0),
                                      pipeline_mode=pl.Buffered(1))
    grid_spec = pltpu.PrefetchScalarGridSpec(
        num_scalar_prefetch=2,
        grid=(nt,),
        in_specs=[pl.BlockSpec((tm, d), lambda i, eid, c: (i, 0)),
                  pl.BlockSpec((ne, tm), lambda i, eid, c: (0, i)),
                  pl.BlockSpec((ne, tm), lambda i, eid, c: (0, i)),
                  resident(wg), resident(wu), resident(wd),
                  pl.BlockSpec((tm, d), lambda i, eid, c: (i, 0))],
        out_specs=pl.BlockSpec((tm, d), lambda i, eid, c: (i, 0)),
        scratch_shapes=[pltpu.VMEM((full, d), BF16), pltpu.VMEM((full, d), F32)],
    )
    one_expert = pl.pallas_call(
        functools.partial(_moe_kernel, tm=tm, tf=tf, gran=gran, nvar=nvar, ne=ne),
        grid_spec=grid_spec,
        out_shape=jax.ShapeDtypeStruct((n, d), F32),
        input_output_aliases={8: 0},
        compiler_params=_params("arbitrary"),
        name="moe_expert",
    )
    hmoe = h
    for e in range(ne):
        hmoe = one_expert(jnp.full((1,), e, jnp.int32), cnt, xn, gate, rank, wg, wu, wd, hmoe)

    row = lambda a: a.reshape(1, -1)
    whole = lambda a: pl.BlockSpec(a.shape, lambda i: (0,) * a.ndim)
    return pl.pallas_call(
        _moe_finish_kernel,
        grid=(nt,),
        in_specs=[pl.BlockSpec((tm, d), lambda i: (i, 0)), pl.BlockSpec((tm, p.shape[1]), lambda i: (i, 0)),
                  whole(row(gp)), whole(wpg), whole(wpp), whole(row(gf))],
        out_specs=pl.BlockSpec((tm, d), lambda i: (i, 0)),
        out_shape=jax.ShapeDtypeStruct((n, d), F32),
        compiler_params=_params("parallel"),
        name="moe_finish",
    )(hmoe, p, row(gp), wpg, wpp, row(gf))


def _rope_tables(pos):
    half = ROT_DIM // 2
    inv = jnp.power(ROPE_THETA, -jnp.arange(half, dtype=F32) / half)
    ang = pos.astype(F32)[:, None] * inv[None, :]
    cos, sin = jnp.cos(ang), jnp.sin(ang)
    rest = HEAD_DIM - ROT_DIM
    n = pos.shape[0]
    return (jnp.concatenate([cos, cos, jnp.ones((n, rest), F32)], axis=1),
            jnp.concatenate([-sin, sin, jnp.zeros((n, rest), F32)], axis=1))


def _tile(n, pref):
    return pref if n % pref == 0 else n


def kernel(x_prompt, x_sample, state_conv, cache_k, cache_v, page_table, p_prompt, p_sample, norm_mix, norm_ffn, norm_ple, ple_w_gate, ple_w_proj, conv_w_pw1, conv_b_pw1, conv_w_dw, conv_b_dw, conv_ln_g, conv_ln_b, conv_w_pw2, conv_b_pw2, ffn_w_gate, ffn_w_up, ffn_w_down, attn_w_qkv, attn_w_o, moe_w_router, moe_w_gate, moe_w_up, moe_w_down, norm_final):
    bsz, seq, d = x_prompt.shape
    nb, dec_seq, _ = x_sample.shape
    assert dec_seq == 1 and norm_mix.shape[0] == 2
    nh = d // HEAD_DIM
    nkv = (attn_w_qkv.shape[2] // HEAD_DIM - nh) // 2
    n_tok = bsz * seq
    dff = ffn_w_gate.shape[2]
    psize = cache_k.shape[2]
    past_len = page_table.shape[1] * psize
    bf = lambda w: w.astype(BF16)

    cw = (norm_mix[0], bf(conv_w_pw1[0]), conv_b_pw1[0], conv_w_dw[0], conv_b_dw[0], conv_ln_g[0], conv_ln_b[0],
          bf(conv_w_pw2[0]), conv_b_pw2[0])
    hp, conv_p = _conv_prompt(x_prompt, *cw, tm=_tile(seq, 512))
    hs, conv_s = _conv_sample(x_sample.reshape(nb, d), state_conv[0], *cw)
    fw = (norm_ffn[0], bf(ffn_w_gate[0]), bf(ffn_w_up[0]), bf(ffn_w_down[0]))
    pw0 = (norm_ple[0], bf(ple_w_gate[0]), bf(ple_w_proj[0]))
    tf = _tile(dff, 512)
    hp = _ffn(hp.reshape(n_tok, d), *fw, p_prompt[0].reshape(n_tok, -1), *pw0, tm=_tile(n_tok, 1024), tf=tf)
    hs = _ffn(hs, *fw, p_sample[0].reshape(nb, -1), *pw0, tm=nb, tf=tf)

    wqkv, wo = bf(attn_w_qkv[0]), bf(attn_w_o[0])
    cos_p, sin_p = _rope_tables(jnp.arange(seq, dtype=jnp.int32))
    qt, k_p, v_p, kb, vt, biast = _qkv_prompt(hp.reshape(bsz, seq, d), norm_mix[1], wqkv, cos_p, sin_p,
                                              tm=_tile(seq, 512), nh=nh, nkv=nkv)
    hp = _attn_prompt(qt, kb, vt, biast, hp.reshape(bsz, seq, d), wo, nh=nh, nkv=nkv).reshape(n_tok, d)

    cos_s, sin_s = _rope_tables(jnp.full((1,), past_len, jnp.int32))
    qs, k_s, v_s = _qkv_sample(hs, norm_mix[1], wqkv, cos_s, sin_s, nh=nh, nkv=nkv)
    ck = cache_k[0].reshape(cache_k.shape[1], psize * nkv, HEAD_DIM)
    cv = cache_v[0].reshape(cache_v.shape[1], psize * nkv, HEAD_DIM)
    idx = _sample_select(ck, page_table, qs, nh=nh, nkv=nkv)[:, :, :TOP_K_BLOCKS]
    att_s = _sample_attn(ck, cv, page_table, idx, qs, k_s, v_s, nh=nh, nkv=nkv)
    hs = _proj_residual(att_s, hs, wo)

    mw = (norm_ffn[1], moe_w_router[0], bf(moe_w_gate[0]), bf(moe_w_up[0]), bf(moe_w_down[0]))
    pw1 = (norm_ple[1], bf(ple_w_gate[1]), bf(ple_w_proj[1]), norm_final)
    tf_moe = _tile(dff, 1792)
    yp = _moe(hp, *mw, p_prompt[1].reshape(n_tok, -1), *pw1, tm=_tile(n_tok, 1024), tf=tf_moe)
    ys = _moe(hs, *mw, p_sample[1].reshape(nb, -1), *pw1, tm=nb, tf=tf_moe)

    return (yp.reshape(bsz, seq, d), ys.reshape(nb, 1, d),
            conv_p[None], conv_s[None],
            k_p.reshape(1, bsz, seq, nkv, HEAD_DIM), v_p.reshape(1, bsz, seq, nkv, HEAD_DIM),
            k_s.reshape(1, nb, 1, nkv, HEAD_DIM), v_s.reshape(1, nb, 1, nkv, HEAD_DIM))
```

```python
import functools

import jax
import jax.numpy as jnp
from jax import lax
from jax.experimental import pallas as pl
from jax.experimental.pallas import tpu as pltpu

F32 = jnp.float32
BF16 = jnp.bfloat16

EPS = 1e-6
HEAD_DIM = 128
ROT_DIM = HEAD_DIM // 4
ROPE_THETA = 500000.0
MOBA_BLOCK = 256
TOP_K_BLOCKS = 3
TOP_K_EXPERTS = 2
LANES = 128
SUBLANES = 8
MASK_VALUE = -1e30
VMEM_LIMIT = 56 * 1024 * 1024


def _params(*sem):
    return pltpu.CompilerParams(dimension_semantics=sem, vmem_limit_bytes=VMEM_LIMIT)


def _rms(x, g):
    r = lax.rsqrt(jnp.mean(x * x, axis=-1, keepdims=True) + EPS)
    return x * r * g


def _dot(a, b):
    return jnp.dot(a, b, preferred_element_type=F32)


def _dot_t(a, b, precision=None):
    return lax.dot_general(a, b, (((1,), (1,)), ((), ())), preferred_element_type=F32, precision=precision)


def _dot_t_split(a, b):
    ah, bh = a.astype(BF16), b.astype(BF16)
    al, bl = (a - ah.astype(F32)).astype(BF16), (b - bh.astype(F32)).astype(BF16)
    return _dot_t(ah, bh) + _dot_t(ah, bl) + _dot_t(al, bh)


def _silu(x):
    return x * jax.nn.sigmoid(x)


def _ple(h, p, g, w_gate, w_proj):
    gate = jax.nn.sigmoid(_dot(_rms(h, g).astype(BF16), w_gate))
    return h + gate * _dot(p.astype(BF16), w_proj)


def _rope_head(x, cos, sin, lane):
    half = ROT_DIM // 2
    partner = jnp.where(lane < half, pltpu.roll(x, HEAD_DIM - half, 1), pltpu.roll(x, half, 1))
    return x * cos + partner * sin


CONV_HALO = 32
CONV_CHUNK = 16


def _conv_prompt_kernel(x_ref, g_ref, w1_ref, b1_ref, wrep_ref, bdw_ref, lng_ref, lnb_ref, w2_ref, b2_ref,
                        out_ref, state_ref, upad_ref, y_ref, *, tm, width, dc):
    t = pl.program_id(1)

    @pl.when(t == 0)
    def _():
        upad_ref[...] = jnp.zeros(upad_ref.shape, F32)

    x = x_ref[0]
    uu = _dot(_rms(x, g_ref[...]).astype(BF16), w1_ref[...]) + b1_ref[...]
    upad_ref[CONV_HALO:CONV_HALO + tm, :] = uu[:, :dc] * jax.nn.sigmoid(uu[:, dc:])

    first = CONV_HALO - (width - 1)
    span = CONV_CHUNK + SUBLANES

    def chunk(i, carry):
        base = pl.multiple_of(i * CONV_CHUNK, CONV_CHUNK)
        acc = jnp.broadcast_to(bdw_ref[...], (CONV_CHUNK, dc))
        for r in range(SUBLANES):
            part = None
            for k in range(width):
                a, kr = divmod(first + k, SUBLANES)
                if kr != r:
                    continue
                w = wrep_ref[k * SUBLANES:(k + 1) * SUBLANES, :]
                term = upad_ref[pl.ds(base + a * SUBLANES, span), :] * jnp.concatenate([w] * (span // SUBLANES), axis=0)
                part = term if part is None else part + term
            if part is not None:
                acc = acc + part[r:r + CONV_CHUNK, :]
        mu = jnp.mean(acc, axis=-1, keepdims=True)
        d = acc - mu
        var = jnp.mean(d * d, axis=-1, keepdims=True)
        z = d * lax.rsqrt(var + EPS) * lng_ref[...] + lnb_ref[...]
        y_ref[pl.ds(base, CONV_CHUNK), :] = _silu(z).astype(BF16)
        return carry

    lax.fori_loop(0, tm // CONV_CHUNK, chunk, 0)
    out_ref[0] = x + _dot(y_ref[...], w2_ref[...]) + b2_ref[...]
    state_ref[0] = upad_ref[tm + first:tm + CONV_HALO, :]
    upad_ref[0:CONV_HALO, :] = upad_ref[tm:tm + CONV_HALO, :]


def _conv_prompt(x, g, w1, b1, wdw, bdw, lng, lnb, w2, b2, *, tm):
    bsz, seq, d = x.shape
    width, dc = wdw.shape
    assert seq % tm == 0 and tm % CONV_CHUNK == 0 and width - 1 <= CONV_HALO
    row = lambda a: a.reshape(1, -1)
    full = lambda a: pl.BlockSpec(a.shape, lambda b, t: (0,) * a.ndim)
    wrep = jnp.repeat(wdw, SUBLANES, axis=0)
    args = (row(g), w1, row(b1), wrep, row(bdw), row(lng), row(lnb), w2, row(b2))
    return pl.pallas_call(
        functools.partial(_conv_prompt_kernel, tm=tm, width=width, dc=dc),
        grid=(bsz, seq // tm),
        in_specs=[pl.BlockSpec((1, tm, d), lambda b, t: (b, t, 0))] + [full(a) for a in args],
        out_specs=[pl.BlockSpec((1, tm, d), lambda b, t: (b, t, 0)),
                   pl.BlockSpec((1, width - 1, dc), lambda b, t: (b, 0, 0))],
        out_shape=[jax.ShapeDtypeStruct((bsz, seq, d), F32),
                   jax.ShapeDtypeStruct((bsz, width - 1, dc), F32)],
        scratch_shapes=[pltpu.VMEM((tm + CONV_HALO + SUBLANES, dc), F32), pltpu.VMEM((tm, dc), BF16)],
        compiler_params=_params("parallel", "arbitrary"),
        name="conv_prompt",
    )(x, *args)


def _conv_sample_kernel(x_ref, st_ref, g_ref, w1_ref, b1_ref, wdw_ref, bdw_ref, lng_ref, lnb_ref, w2_ref, b2_ref,
                        out_ref, state_ref, *, width, dc):
    nb = x_ref.shape[0]
    x = x_ref[...]
    uu = _dot(_rms(x, g_ref[...]).astype(BF16), w1_ref[...]) + b1_ref[...]
    u = uu[:, :dc] * jax.nn.sigmoid(uu[:, dc:])
    st = st_ref[...]
    acc = jnp.sum(st * wdw_ref[0:width - 1, :][None], axis=1) + u * wdw_ref[width - 1:width, :] + bdw_ref[...]
    mu = jnp.mean(acc, axis=-1, keepdims=True)
    d = acc - mu
    var = jnp.mean(d * d, axis=-1, keepdims=True)
    z = d * lax.rsqrt(var + EPS) * lng_ref[...] + lnb_ref[...]
    out_ref[...] = x + _dot(_silu(z).astype(BF16), w2_ref[...]) + b2_ref[...]
    state_ref[:, 0:width - 2, :] = st_ref[:, 1:width - 1, :]
    for b in range(nb):
        state_ref[b, width - 2:width - 1, :] = u[b:b + 1, :]


def _conv_sample(x, st, g, w1, b1, wdw, bdw, lng, lnb, w2, b2):
    nb, d = x.shape
    width, dc = wdw.shape
    row = lambda a: a.reshape(1, -1)
    return pl.pallas_call(
        functools.partial(_conv_sample_kernel, width=width, dc=dc),
        out_shape=[jax.ShapeDtypeStruct((nb, d), F32), jax.ShapeDtypeStruct((nb, width - 1, dc), F32)],
        compiler_params=pltpu.CompilerParams(vmem_limit_bytes=VMEM_LIMIT),
        name="conv_sample",
    )(x, st, row(g), w1, row(b1), wdw, row(bdw), row(lng), row(lnb), w2, row(b2))


def _ffn_kernel(x_ref, g_ref, wg_ref, wu_ref, wd_ref, p_ref, gp_ref, wpg_ref, wpp_ref, out_ref, xn_ref, acc_ref):
    f = pl.program_id(1)

    @pl.when(f == 0)
    def _():
        xn_ref[...] = _rms(x_ref[...], g_ref[...]).astype(BF16)
        acc_ref[...] = jnp.zeros(acc_ref.shape, F32)

    xn = xn_ref[...]
    a = (_silu(_dot(xn, wg_ref[...])) * _dot(xn, wu_ref[...])).astype(BF16)
    acc_ref[...] += _dot(a, wd_ref[...])

    @pl.when(f == pl.num_programs(1) - 1)
    def _():
        out_ref[...] = _ple(x_ref[...] + acc_ref[...], p_ref[...], gp_ref[...], wpg_ref[...], wpp_ref[...])


def _ffn(x, g, wg, wu, wd, p, gp, wpg, wpp, *, tm, tf):
    n, d = x.shape
    dff = wg.shape[1]
    assert n % tm == 0 and dff % tf == 0
    row = lambda a: a.reshape(1, -1)
    full = lambda a: pl.BlockSpec(a.shape, lambda i, f: (0,) * a.ndim)
    return pl.pallas_call(
        _ffn_kernel,
        grid=(n // tm, dff // tf),
        in_specs=[pl.BlockSpec((tm, d), lambda i, f: (i, 0)), full(row(g)),
                  pl.BlockSpec((d, tf), lambda i, f: (0, f)), pl.BlockSpec((d, tf), lambda i, f: (0, f)),
                  pl.BlockSpec((tf, d), lambda i, f: (f, 0)),
                  pl.BlockSpec((tm, p.shape[1]), lambda i, f: (i, 0)), full(row(gp)), full(wpg), full(wpp)],
        out_specs=pl.BlockSpec((tm, d), lambda i, f: (i, 0)),
        out_shape=jax.ShapeDtypeStruct((n, d), F32),
        scratch_shapes=[pltpu.VMEM((tm, d), BF16), pltpu.VMEM((tm, d), F32)],
        compiler_params=_params("parallel", "arbitrary"),
        name="ffn_ple",
    )(x, row(g), wg, wu, wd, p, row(gp), wpg, wpp)


def _qkv_prompt_kernel(x_ref, g_ref, w_ref, cos_ref, sin_ref,
                       qt_ref, k_ref, v_ref, kb_ref, vt_ref, biast_ref, kmt_ref, *, tm, nh, nkv, nblk):
    t = pl.program_id(1)
    group = nh // nkv
    nq = nh * HEAD_DIM
    nk = nkv * HEAD_DIM

    @pl.when(t == 0)
    def _():
        kmt_ref[...] = jnp.zeros(kmt_ref.shape, F32)

    qkv = _dot(_rms(x_ref[0], g_ref[...]).astype(BF16), w_ref[...])
    cos = cos_ref[...]
    sin = sin_ref[...]
    lane = lax.broadcasted_iota(jnp.int32, (tm, HEAD_DIM), 1)
    scale = HEAD_DIM ** -0.5
    q_heads = [_rope_head(qkv[:, h * HEAD_DIM:(h + 1) * HEAD_DIM], cos, sin, lane) * scale for h in range(nh)]
    q = jnp.concatenate(q_heads, axis=1)
    k = jnp.concatenate([_rope_head(qkv[:, nq + h * HEAD_DIM:nq + (h + 1) * HEAD_DIM], cos, sin, lane)
                         for h in range(nkv)], axis=1)
    v = qkv[:, nq + nk:]
    for h in range(nh):
        qt_ref[0, h] = q_heads[h].T.astype(BF16)
    for kv in range(nkv):
        k_ref[0, pl.ds(kv, tm, stride=nkv), :] = k[:, kv * HEAD_DIM:(kv + 1) * HEAD_DIM]
        v_ref[0, pl.ds(kv, tm, stride=nkv), :] = v[:, kv * HEAD_DIM:(kv + 1) * HEAD_DIM]
        vt_ref[0, kv] = v[:, kv * HEAD_DIM:(kv + 1) * HEAD_DIM].T.astype(BF16)
    kb_ref[0] = k.astype(BF16)

    kmt = kmt_ref[...]
    kmt_row = lax.broadcasted_iota(jnp.int32, kmt.shape, 0)
    kmt_head = lax.broadcasted_iota(jnp.int32, kmt.shape, 1) // HEAD_DIM
    for i in range(tm // MOBA_BLOCK):
        km = jnp.mean(k[i * MOBA_BLOCK:(i + 1) * MOBA_BLOCK, :], axis=0, keepdims=True)
        km = jnp.concatenate([km[:, (h // group) * HEAD_DIM:(h // group + 1) * HEAD_DIM] for h in range(nh)], axis=1)
        n = t * (tm // MOBA_BLOCK) + i
        kmt = jnp.where(kmt_row == kmt_head * nblk + n, km, kmt)
    kmt_ref[...] = kmt

    width = nh * nblk
    col = lax.broadcasted_iota(jnp.int32, (tm, width), 1)
    n_idx = col % nblk
    own = (t * tm + lax.broadcasted_iota(jnp.int32, (tm, width), 0)) // MOBA_BLOCK
    past = n_idx < own
    s = jnp.where(past, _dot_t_split(q, kmt), -jnp.inf)
    rank = jnp.zeros((tm, width), F32)
    for dlt in range(1, nblk):
        wrap = n_idx + dlt >= nblk
        other = jnp.where(wrap, pltpu.roll(s, nblk - dlt, 1), pltpu.roll(s, width - dlt, 1))
        rank = rank + jnp.where(other > s, 1.0, 0.0) + jnp.where(wrap, jnp.where(other == s, 1.0, 0.0), 0.0)
    bias = jnp.where(past, jnp.where(rank < TOP_K_BLOCKS, 0.0, MASK_VALUE), MASK_VALUE)
    biast_ref[0] = bias.T.astype(BF16)


def _qkv_prompt(x, g, w, cos, sin, *, tm, nh, nkv):
    bsz, seq, d = x.shape
    nq, nk = nh * HEAD_DIM, nkv * HEAD_DIM
    nblk = seq // MOBA_BLOCK
    assert seq % tm == 0 and tm % MOBA_BLOCK == 0 and nh * nblk == LANES
    tok = lambda c: pl.BlockSpec((1, tm, c), lambda b, t: (b, t, 0))
    kv_rows = pl.BlockSpec((1, tm * nkv, HEAD_DIM), lambda b, t: (b, t, 0))
    full = lambda a: pl.BlockSpec(a.shape, lambda b, t: (0,) * a.ndim)
    g = g.reshape(1, -1)
    return pl.pallas_call(
        functools.partial(_qkv_prompt_kernel, tm=tm, nh=nh, nkv=nkv, nblk=nblk),
        grid=(bsz, seq // tm),
        in_specs=[tok(d), full(g), full(w),
                  pl.BlockSpec((tm, HEAD_DIM), lambda b, t: (t, 0)), pl.BlockSpec((tm, HEAD_DIM), lambda b, t: (t, 0))],
        out_specs=[pl.BlockSpec((1, nh, HEAD_DIM, tm), lambda b, t: (b, 0, 0, t)), kv_rows, kv_rows, tok(nk),
                   pl.BlockSpec((1, nkv, HEAD_DIM, tm), lambda b, t: (b, 0, 0, t)),
                   pl.BlockSpec((1, nh * nblk, tm), lambda b, t: (b, 0, t))],
        out_shape=[jax.ShapeDtypeStruct((bsz, nh, HEAD_DIM, seq), BF16),
                   jax.ShapeDtypeStruct((bsz, seq * nkv, HEAD_DIM), F32),
                   jax.ShapeDtypeStruct((bsz, seq * nkv, HEAD_DIM), F32),
                   jax.ShapeDtypeStruct((bsz, seq, nk), BF16),
                   jax.ShapeDtypeStruct((bsz, nkv, HEAD_DIM, seq), BF16),
                   jax.ShapeDtypeStruct((bsz, nh * nblk, seq), BF16)],
        scratch_shapes=[pltpu.VMEM((nh * nblk, nq), F32)],
        compiler_params=_params("parallel", "arbitrary"),
        name="qkv_prompt",
    )(x, g, w, cos, sin)


def _qkv_sample_kernel(x_ref, g_ref, w_ref, cos_ref, sin_ref, q_ref, k_ref, v_ref, *, nh, nkv):
    nb = x_ref.shape[0]
    nq = nh * HEAD_DIM
    nk = nkv * HEAD_DIM
    qkv = _dot(_rms(x_ref[...], g_ref[...]).astype(BF16), w_ref[...])
    cos = cos_ref[...]
    sin = sin_ref[...]
    lane = lax.broadcasted_iota(jnp.int32, (nb, HEAD_DIM), 1)
    scale = HEAD_DIM ** -0.5
    q_ref[...] = jnp.concatenate([_rope_head(qkv[:, h * HEAD_DIM:(h + 1) * HEAD_DIM], cos, sin, lane) * scale
                                  for h in range(nh)], axis=1)
    k_ref[...] = jnp.concatenate([_rope_head(qkv[:, nq + h * HEAD_DIM:nq + (h + 1) * HEAD_DIM], cos, sin, lane)
                                  for h in range(nkv)], axis=1)
    v_ref[...] = qkv[:, nq + nk:]


def _qkv_sample(x, g, w, cos, sin, *, nh, nkv):
    nb = x.shape[0]
    nq, nk = nh * HEAD_DIM, nkv * HEAD_DIM
    return pl.pallas_call(
        functools.partial(_qkv_sample_kernel, nh=nh, nkv=nkv),
        out_shape=[jax.ShapeDtypeStruct((nb, nq), F32), jax.ShapeDtypeStruct((nb, nk), F32),
                   jax.ShapeDtypeStruct((nb, nk), F32)],
        compiler_params=pltpu.CompilerParams(vmem_limit_bytes=VMEM_LIMIT),
        name="qkv_sample",
    )(x, g.reshape(1, -1), w, cos, sin)


def _attn_prompt_kernel(qt_ref, kb_ref, vt_ref, biast_ref, h_ref, wo_ref, out_ref, *, nh, nkv, nblk):
    o = pl.program_id(1)
    hk = pl.program_id(2)
    group = nh // nkv
    blk = MOBA_BLOCK
    bias = biast_ref[0]
    feat_head = lax.broadcasted_iota(jnp.int32, bias.shape, 0) // nblk
    q_t = jnp.concatenate([qt_ref[0, g] for g in range(group)], axis=1)
    q_aug = jnp.concatenate(
        [q_t, jnp.concatenate([jnp.where(feat_head == hk * group + g, bias, jnp.zeros_like(bias))
                               for g in range(group)], axis=1)], axis=0)
    cols = group * blk

    own0 = pl.multiple_of(o * blk, blk)
    s = _dot(kb_ref[0, pl.ds(own0, blk), :], q_t)
    kpos = lax.broadcasted_iota(jnp.int32, (blk, cols), 0)
    qpos = lax.broadcasted_iota(jnp.int32, (blk, cols), 1) % blk
    s = jnp.where(kpos <= qpos, s, MASK_VALUE)
    m = jnp.max(s, axis=0, keepdims=True)
    p = jnp.exp(s - m)
    l = jnp.sum(p, axis=0, keepdims=True)
    acc = _dot(vt_ref[0, 0, :, pl.ds(own0, blk)], p.astype(BF16))
    lane = lax.broadcasted_iota(jnp.int32, (blk, nh * nblk), 1)

    def past(j, carry):
        m, l, acc = carry
        j0 = pl.multiple_of(j * blk, blk)
        pick = jnp.where(lane % nblk == j, 1.0, 0.0).astype(BF16)
        s = _dot(jnp.concatenate([kb_ref[0, pl.ds(j0, blk), :], pick], axis=1), q_aug)
        m_new = jnp.maximum(m, jnp.max(s, axis=0, keepdims=True))
        alpha = jnp.exp(m - m_new)
        p = jnp.exp(s - m_new)
        l = alpha * l + jnp.sum(p, axis=0, keepdims=True)
        acc = alpha * acc + _dot(vt_ref[0, 0, :, pl.ds(j0, blk)], p.astype(BF16))
        return m_new, l, acc

    m, l, acc = lax.fori_loop(0, o, past, (m, l, acc))
    att = (acc / l).astype(BF16)
    proj = sum(lax.dot_general(att[:, g * blk:(g + 1) * blk], wo_ref[g * HEAD_DIM:(g + 1) * HEAD_DIM, :],
                               (((0,), (0,)), ((), ())), preferred_element_type=F32) for g in range(group))

    @pl.when(hk == 0)
    def _():
        out_ref[0] = h_ref[0] + proj

    @pl.when(hk != 0)
    def _():
        out_ref[0] += proj


def _attn_prompt(qt, kb, vt, biast, h, wo, *, nh, nkv):
    bsz, seq, d = h.shape
    nblk = seq // MOBA_BLOCK
    group = nh // nkv
    gw = group * HEAD_DIM
    return pl.pallas_call(
        functools.partial(_attn_prompt_kernel, nh=nh, nkv=nkv, nblk=nblk),
        grid=(bsz, nblk, nkv),
        in_specs=[pl.BlockSpec((1, group, HEAD_DIM, MOBA_BLOCK), lambda b, o, k: (b, k, 0, o)),
                  pl.BlockSpec((1, seq, HEAD_DIM), lambda b, o, k: (b, 0, k)),
                  pl.BlockSpec((1, 1, HEAD_DIM, seq), lambda b, o, k: (b, k, 0, 0)),
                  pl.BlockSpec((1, biast.shape[1], MOBA_BLOCK), lambda b, o, k: (b, 0, o)),
                  pl.BlockSpec((1, MOBA_BLOCK, d), lambda b, o, k: (b, o, 0)),
                  pl.BlockSpec((gw, d), lambda b, o, k: (k, 0))],
        out_specs=pl.BlockSpec((1, MOBA_BLOCK, d), lambda b, o, k: (b, o, 0)),
        out_shape=jax.ShapeDtypeStruct((bsz, seq, d), F32),
        compiler_params=_params("parallel", "parallel", "arbitrary"),
        name="attn_prompt",
    )(qt, kb, vt, biast, h, wo)


PAGES_PER_STEP = 32


def _sample_select_kernel(pt_ref, *refs, nh, nkv, ppb, nblk):
    pages = refs[:PAGES_PER_STEP]
    q_ref, idx_ref, km_ref = refs[PAGES_PER_STEP:]
    s = pl.program_id(1)
    group = nh // nkv
    bps = PAGES_PER_STEP // ppb
    psize = pages[0].shape[1] // nkv

    @pl.when(s == 0)
    def _():
        km_ref[...] = jnp.zeros(km_ref.shape, F32)

    km = km_ref[...]
    km_row = lax.broadcasted_iota(jnp.int32, km.shape, 0)
    for i in range(bps):
        tot = jnp.concatenate(
            [sum(jnp.sum(pages[i * ppb + j][0, pl.ds(kv, psize, stride=nkv), :], axis=0, keepdims=True)
                 for j in range(ppb)) for kv in range(nkv)], axis=1)
        km = jnp.where(km_row == s * bps + i, tot / (psize * ppb), km)
    km_ref[...] = km

    @pl.when(s == pl.num_programs(1) - 1)
    def _():
        km = km_ref[...]
        q = q_ref[0]
        sc = jnp.concatenate(
            [_dot_t(q[kv * group:(kv + 1) * group, :], km[:, kv * HEAD_DIM:(kv + 1) * HEAD_DIM],
                    precision=lax.Precision.HIGHEST) for kv in range(nkv)], axis=0)
        col = lax.broadcasted_iota(jnp.int32, sc.shape, 1)
        lane = lax.broadcasted_iota(jnp.int32, (nh, LANES), 1)
        out = jnp.zeros((nh, LANES), jnp.int32)
        for r in range(TOP_K_BLOCKS):
            best = jnp.max(sc, axis=-1, keepdims=True)
            pick = jnp.min(jnp.where(sc == best, col, nblk), axis=-1, keepdims=True)
            out = jnp.where(lane == r, pick, out)
            sc = jnp.where(col == pick, -jnp.inf, sc)
        idx_ref[0] = out


def _sample_select(cache_k, page_table, q, *, nh, nkv):
    npool, prow, hd = cache_k.shape
    nb, npages = page_table.shape
    ppb = MOBA_BLOCK // (prow // nkv)
    nblk = npages // ppb
    assert npages % PAGES_PER_STEP == 0 and PAGES_PER_STEP % ppb == 0 and nblk >= TOP_K_BLOCKS
    page_spec = lambda j: pl.BlockSpec((1, prow, hd), lambda b, s, pt: (pt[b * npages + s * PAGES_PER_STEP + j], 0, 0))
    grid_spec = pltpu.PrefetchScalarGridSpec(
        num_scalar_prefetch=1,
        grid=(nb, npages // PAGES_PER_STEP),
        in_specs=[page_spec(j) for j in range(PAGES_PER_STEP)]
        + [pl.BlockSpec((1, nh, HEAD_DIM), lambda b, s, pt: (b, 0, 0))],
        out_specs=pl.BlockSpec((1, nh, LANES), lambda b, s, pt: (b, 0, 0)),
        scratch_shapes=[pltpu.VMEM((nblk, nkv * hd), F32)],
    )
    return pl.pallas_call(
        functools.partial(_sample_select_kernel, nh=nh, nkv=nkv, ppb=ppb, nblk=nblk),
        grid_spec=grid_spec,
        out_shape=jax.ShapeDtypeStruct((nb, nh, LANES), jnp.int32),
        compiler_params=_params("parallel", "arbitrary"),
        name="sample_select",
    )(page_table.reshape(-1), *([cache_k] * PAGES_PER_STEP), q.reshape(nb, nh, HEAD_DIM))


def _sample_attn_kernel(pt_ref, idx_ref, q_ref, kn_ref, vn_ref, *refs, group, nkv, npg):
    kpages = refs[:group * npg]
    vpages = refs[group * npg:2 * group * npg]
    out_ref = refs[2 * group * npg]
    hkv = pl.program_id(1)
    prow = kpages[0].shape[1]
    mine = lax.broadcasted_iota(jnp.int32, (prow, 1), 0) % nkv == hkv
    outs = []
    for g in range(group):
        q = q_ref[0, g:g + 1, :]
        m = jnp.sum(q * kn_ref[0], axis=-1, keepdims=True)
        l = jnp.ones((1, 1), F32)
        acc = vn_ref[0]
        for i in range(npg):
            k = kpages[g * npg + i][0]
            sc = jnp.where(mine, jnp.sum(k * q, axis=-1, keepdims=True), MASK_VALUE)
            m_new = jnp.maximum(m, jnp.max(sc, axis=0, keepdims=True))
            alpha = jnp.exp(m - m_new)
            p = jnp.exp(sc - m_new)
            l = alpha * l + jnp.sum(p, axis=0, keepdims=True)
            acc = alpha * acc + jnp.sum(p * vpages[g * npg + i][0], axis=0, keepdims=True)
            m = m_new
        outs.append(acc / l)
    out_ref[0] = jnp.concatenate(outs, axis=0)


def _sample_attn(cache_k, cache_v, page_table, idx, q, kn, vn, *, nh, nkv):
    npool, prow, hd = cache_k.shape
    nb, npages = page_table.shape
    ppb = MOBA_BLOCK // (prow // nkv)
    group = nh // nkv
    nsel = TOP_K_BLOCKS
    npg = nsel * ppb

    def page_spec(g, i):
        def imap(b, kv, pt, ix):
            blk = ix[(b * nh + kv * group + g) * nsel + i // ppb]
            return (pt[b * npages + blk * ppb + i % ppb], 0, 0)
        return pl.BlockSpec((1, prow, hd), imap)

    qspec = pl.BlockSpec((1, group, HEAD_DIM), lambda b, kv, pt, ix: (b * nkv + kv, 0, 0))
    kvspec = pl.BlockSpec((1, 1, HEAD_DIM), lambda b, kv, pt, ix: (b * nkv + kv, 0, 0))
    pages = [page_spec(g, i) for g in range(group) for i in range(npg)]
    grid_spec = pltpu.PrefetchScalarGridSpec(
        num_scalar_prefetch=2,
        grid=(nb, nkv),
        in_specs=[qspec, kvspec, kvspec] + pages + pages,
        out_specs=qspec,
    )
    out = pl.pallas_call(
        functools.partial(_sample_attn_kernel, group=group, nkv=nkv, npg=npg),
        grid_spec=grid_spec,
        out_shape=jax.ShapeDtypeStruct((nb * nkv, group, HEAD_DIM), F32),
        compiler_params=_params("parallel", "parallel"),
        name="sample_attn",
    )(page_table.reshape(-1), idx.reshape(-1),
      q.reshape(nb * nkv, group, HEAD_DIM), kn.reshape(nb * nkv, 1, HEAD_DIM), vn.reshape(nb * nkv, 1, HEAD_DIM),
      *([cache_k] * (group * npg)), *([cache_v] * (group * npg)))
    return out.reshape(nb, nh * HEAD_DIM)


def _proj_residual_kernel(a_ref, h_ref, w_ref, out_ref):
    out_ref[...] = h_ref[...] + _dot(a_ref[...].astype(BF16), w_ref[...])


def _proj_residual(a, h, w):
    return pl.pallas_call(
        _proj_residual_kernel,
        out_shape=jax.ShapeDtypeStruct(h.shape, F32),
        compiler_params=pltpu.CompilerParams(vmem_limit_bytes=VMEM_LIMIT),
        name="proj_residual",
    )(a, h, w)


MXU_DEPTH = 256
MOE_ROW_GRAN = 64
MOE_VARIANTS = 8


def _router_kernel(x_ref, g_ref, wrt_ref, tri_ref, xn_ref, gate_ref, rank_ref, cnt_ref):
    xn = _rms(x_ref[...], g_ref[...])
    xn_ref[...] = xn.astype(BF16)
    logits = _dot_t_split(wrt_ref[...], xn)
    ne = logits.shape[0]
    e = jnp.exp(logits - jnp.max(logits, axis=0, keepdims=True))
    probs = e / jnp.sum(e, axis=0, keepdims=True)
    eid = lax.broadcasted_iota(jnp.int32, probs.shape, 0)
    rest = probs
    member = jnp.zeros(probs.shape, jnp.bool_)
    top_sum = jnp.zeros((1, probs.shape[1]), F32)
    for _ in range(TOP_K_EXPERTS):
        best = jnp.max(rest, axis=0, keepdims=True)
        pick = eid == jnp.min(jnp.where(rest == best, eid, ne), axis=0, keepdims=True)
        member = member | pick
        top_sum = top_sum + best
        rest = jnp.where(pick, -1.0, rest)
    gate_ref[...] = jnp.where(member, probs / top_sum, 0.0)
    mem = jnp.where(member, 1.0, 0.0)
    rank = _dot(mem.astype(BF16), tri_ref[...])
    rank_ref[...] = jnp.where(member, rank, -1.0)
    cnt_ref[0] = jnp.broadcast_to(jnp.sum(mem, axis=1, keepdims=True), cnt_ref.shape[1:])


def _router(x, g, wr, *, tm):
    n, d = x.shape
    ne = wr.shape[1]
    nt = n // tm
    assert n % tm == 0
    tri = (lax.broadcasted_iota(jnp.int32, (tm, tm), 0) < lax.broadcasted_iota(jnp.int32, (tm, tm), 1)).astype(BF16)
    full = lambda a: pl.BlockSpec(a.shape, lambda i: (0,) * a.ndim)
    g = g.reshape(1, -1)
    wrt = wr.T
    xn, gate, rank, cnt = pl.pallas_call(
        _router_kernel,
        grid=(nt,),
        in_specs=[pl.BlockSpec((tm, d), lambda i: (i, 0)), full(g), full(wrt), full(tri)],
        out_specs=[pl.BlockSpec((tm, d), lambda i: (i, 0)), pl.BlockSpec((ne, tm), lambda i: (0, i)),
                   pl.BlockSpec((ne, tm), lambda i: (0, i)), pl.BlockSpec((1, ne, LANES), lambda i: (i, 0, 0))],
        out_shape=[jax.ShapeDtypeStruct((n, d), BF16), jax.ShapeDtypeStruct((ne, n), F32),
                   jax.ShapeDtypeStruct((ne, n), F32), jax.ShapeDtypeStruct((nt, ne, LANES), F32)],
        compiler_params=_params("parallel"),
        name="router",
    )(x, g, wrt, tri)
    return xn, gate, rank, cnt[:, :, 0].astype(jnp.int32).reshape(-1)


def _moe_kernel(eid_ref, cnt_ref, xn_ref, gate_ref, rank_ref, wg_ref, wu_ref, wd_ref, hin_ref, out_ref, xc_ref, y_ref,
                *, tm, tf, gran, nvar, ne):
    e = eid_ref[0]
    i = pl.program_id(0)
    cnt = cnt_ref[i * ne + e]
    ngrp = (cnt + gran - 1) // gran
    rank = rank_ref[pl.ds(e, 1), :]
    kq = min(MXU_DEPTH, tm)

    def onehot(r0, rows):
        slot = lax.broadcasted_iota(jnp.int32, (rows, tm), 0) + r0
        return slot.astype(F32) == rank

    def process(r0, rows, base_ref):
        krows = -(-rows // kq) * kq
        sel = jnp.where(onehot(r0, rows), 1.0, 0.0).astype(BF16)
        xc_ref[0:rows, :] = _dot(sel, xn_ref[...]).astype(BF16)
        y_ref[0:krows, :] = jnp.zeros((krows, y_ref.shape[1]), F32)

        def ffn(fc, carry):
            xc = xc_ref[0:rows, :]
            f0 = pl.multiple_of(fc * tf, tf)
            a = (_silu(_dot(xc, wg_ref[0, :, pl.ds(f0, tf)])) * _dot(xc, wu_ref[0, :, pl.ds(f0, tf)])).astype(BF16)
            y_ref[0:rows, :] += _dot(a, wd_ref[0, pl.ds(f0, tf), :])
            return carry
        lax.fori_loop(0, wg_ref.shape[2] // tf, ffn, 0)
        w = jnp.where(onehot(r0, krows), gate_ref[pl.ds(e, 1), :], 0.0).astype(BF16)
        out_ref[...] = base_ref[...] + lax.dot_general(w, y_ref[0:krows, :].astype(BF16), (((0,), (0,)), ((), ())),
                                                       preferred_element_type=F32)

    @pl.when(ngrp == 0)
    def _():
        out_ref[...] = hin_ref[...]

    balanced = max(1, tm * TOP_K_EXPERTS // (ne * gran))
    sizes = range(max(1, balanced - 1), min(nvar, balanced + 2) + 1)
    for n in sizes:
        @pl.when(ngrp == n)
        def _(n=n):
            process(0, n * gran, hin_ref)

    common = functools.reduce(lambda a, n: a | (ngrp == n), sizes, ngrp == 0)

    @pl.when(jnp.logical_not(common))
    def _():
        full = nvar * gran
        out_ref[...] = hin_ref[...]

        def body(c, carry):
            process(c * full, full, out_ref)
            return carry
        lax.fori_loop(0, (cnt + full - 1) // full, body, 0)


def _moe_finish_kernel(h_ref, p_ref, gp_ref, wpg_ref, wpp_ref, gf_ref, out_ref):
    out_ref[...] = _rms(_ple(h_ref[...], p_ref[...], gp_ref[...], wpg_ref[...], wpp_ref[...]), gf_ref[...])


def _moe(h, g, wr, wg, wu, wd, p, gp, wpg, wpp, gf, *, tm, tf):
    n, d = h.shape
    ne, _, dff = wg.shape
    nt = n // tm
    gran = min(MOE_ROW_GRAN, tm)
    nvar = min(MOE_VARIANTS, tm // gran)
    full = nvar * gran
    assert n % tm == 0 and dff % tf == 0 and tm % full == 0 and full % min(MXU_DEPTH, tm) == 0
    xn, gate, rank, cnt = _router(h, g, wr, tm=tm)
    resident = lambda a: pl.BlockSpec((1,) + a.shape[1:], lambda i, eid, c: (eid[0], 0, 0),
                                      pipeline_mode=pl.Buffered(1))
    grid_spec = pltpu.PrefetchScalarGridSpec(
        num_scalar_prefetch=2,
        grid=(nt,),
        in_specs=[pl.BlockSpec((tm, d), lambda i, eid, c: (i, 0)),
                  pl.BlockSpec((ne, tm), lambda i, eid, c: (0, i)),
                  pl.BlockSpec((ne, tm), lambda i, eid, c: (0, i)),
                  resident(wg), resident(wu), resident(wd),
                  pl.BlockSpec((tm, d), lambda i, eid, c: (i, 0))],
        out_specs=pl.BlockSpec((tm, d), lambda i, eid, c: (i, 0)),
        scratch_shapes=[pltpu.VMEM((full, d), BF16), pltpu.VMEM((full, d), F32)],
    )
    one_expert = pl.pallas_call(
        functools.partial(_moe_kernel, tm=tm, tf=tf, gran=gran, nvar=nvar, ne=ne),
        grid_spec=grid_spec,
        out_shape=jax.ShapeDtypeStruct((n, d), F32),
        input_output_aliases={8: 0},
        compiler_params=_params("arbitrary"),
        name="moe_expert",
    )
    hmoe = h
    for e in range(ne):
        hmoe = one_expert(jnp.full((1,), e, jnp.int32), cnt, xn, gate, rank, wg, wu, wd, hmoe)

    row = lambda a: a.reshape(1, -1)
    whole = lambda a: pl.BlockSpec(a.shape, lambda i: (0,) * a.ndim)
    return pl.pallas_call(
        _moe_finish_kernel,
        grid=(nt,),
        in_specs=[pl.BlockSpec((tm, d), lambda i: (i, 0)), pl.BlockSpec((tm, p.shape[1]), lambda i: (i, 0)),
                  whole(row(gp)), whole(wpg), whole(wpp), whole(row(gf))],
        out_specs=pl.BlockSpec((tm, d), lambda i: (i, 0)),
        out_shape=jax.ShapeDtypeStruct((n, d), F32),
        compiler_params=_params("parallel"),
        name="moe_finish",
    )(hmoe, p, row(gp), wpg, wpp, row(gf))


def _rope_tables(pos):
    half = ROT_DIM // 2
    inv = jnp.power(ROPE_THETA, -jnp.arange(half, dtype=F32) / half)
    ang = pos.astype(F32)[:, None] * inv[None, :]
    cos, sin = jnp.cos(ang), jnp.sin(ang)
    rest = HEAD_DIM - ROT_DIM
    n = pos.shape[0]
    return (jnp.concatenate([cos, cos, jnp.ones((n, rest), F32)], axis=1),
            jnp.concatenate([-sin, sin, jnp.zeros((n, rest), F32)], axis=1))


def _tile(n, pref):
    return pref if n % pref == 0 else n


def kernel(x_prompt, x_sample, state_conv, cache_k, cache_v, page_table, p_prompt, p_sample, norm_mix, norm_ffn, norm_ple, ple_w_gate, ple_w_proj, conv_w_pw1, conv_b_pw1, conv_w_dw, conv_b_dw, conv_ln_g, conv_ln_b, conv_w_pw2, conv_b_pw2, ffn_w_gate, ffn_w_up, ffn_w_down, attn_w_qkv, attn_w_o, moe_w_router, moe_w_gate, moe_w_up, moe_w_down, norm_final):
    bsz, seq, d = x_prompt.shape
    nb, dec_seq, _ = x_sample.shape
    assert dec_seq == 1 and norm_mix.shape[0] == 2
    nh = d // HEAD_DIM
    nkv = (attn_w_qkv.shape[2] // HEAD_DIM - nh) // 2
    n_tok = bsz * seq
    dff = ffn_w_gate.shape[2]
    psize = cache_k.shape[2]
    past_len = page_table.shape[1] * psize
    bf = lambda w: w.astype(BF16)

    cw = (norm_mix[0], bf(conv_w_pw1[0]), conv_b_pw1[0], conv_w_dw[0], conv_b_dw[0], conv_ln_g[0], conv_ln_b[0],
          bf(conv_w_pw2[0]), conv_b_pw2[0])
    hp, conv_p = _conv_prompt(x_prompt, *cw, tm=_tile(seq, 512))
    hs, conv_s = _conv_sample(x_sample.reshape(nb, d), state_conv[0], *cw)
    fw = (norm_ffn[0], bf(ffn_w_gate[0]), bf(ffn_w_up[0]), bf(ffn_w_down[0]))
    pw0 = (norm_ple[0], bf(ple_w_gate[0]), bf(ple_w_proj[0]))
    tf = _tile(dff, 512)
    hp = _ffn(hp.reshape(n_tok, d), *fw, p_prompt[0].reshape(n_tok, -1), *pw0, tm=_tile(n_tok, 1024), tf=tf)
    hs = _ffn(hs, *fw, p_sample[0].reshape(nb, -1), *pw0, tm=nb, tf=tf)

    wqkv, wo = bf(attn_w_qkv[0]), bf(attn_w_o[0])
    cos_p, sin_p = _rope_tables(jnp.arange(seq, dtype=jnp.int32))
    qt, k_p, v_p, kb, vt, biast = _qkv_prompt(hp.reshape(bsz, seq, d), norm_mix[1], wqkv, cos_p, sin_p,
                                              tm=_tile(seq, 512), nh=nh, nkv=nkv)
    hp = _attn_prompt(qt, kb, vt, biast, hp.reshape(bsz, seq, d), wo, nh=nh, nkv=nkv).reshape(n_tok, d)

    cos_s, sin_s = _rope_tables(jnp.full((1,), past_len, jnp.int32))
    qs, k_s, v_s = _qkv_sample(hs, norm_mix[1], wqkv, cos_s, sin_s, nh=nh, nkv=nkv)
    ck = cache_k[0].reshape(cache_k.shape[1], psize * nkv, HEAD_DIM)
    cv = cache_v[0].reshape(cache_v.shape[1], psize * nkv, HEAD_DIM)
    idx = _sample_select(ck, page_table, qs, nh=nh, nkv=nkv)[:, :, :TOP_K_BLOCKS]
    att_s = _sample_attn(ck, cv, page_table, idx, qs, k_s, v_s, nh=nh, nkv=nkv)
    hs = _proj_residual(att_s, hs, wo)

    mw = (norm_ffn[1], moe_w_router[0], bf(moe_w_gate[0]), bf(moe_w_up[0]), bf(moe_w_down[0]))
    pw1 = (norm_ple[1], bf(ple_w_gate[1]), bf(ple_w_proj[1]), norm_final)
    yp = _moe(hp, *mw, p_prompt[1].reshape(n_tok, -1), *pw1, tm=_tile(n_tok, 1024), tf=tf)
    ys = _moe(hs, *mw, p_sample[1].reshape(nb, -1), *pw1, tm=nb, tf=tf)

    return (yp.reshape(bsz, seq, d), ys.reshape(nb, 1, d),
            conv_p[None], conv_s[None],
            k_p.reshape(1, bsz, seq, nkv, HEAD_DIM), v_p.reshape(1, bsz, seq, nkv, HEAD_DIM),
            k_s.reshape(1, nb, 1, nkv, HEAD_DIM), v_s.reshape(1, nb, 1, nkv, HEAD_DIM))
```

```python
import functools

import jax
import jax.numpy as jnp
from jax import lax
from jax.experimental import pallas as pl
from jax.experimental.pallas import tpu as pltpu

F32 = jnp.float32
BF16 = jnp.bfloat16

EPS = 1e-6
HEAD_DIM = 128
ROT_DIM = HEAD_DIM // 4
ROPE_THETA = 500000.0
MOBA_BLOCK = 256
TOP_K_BLOCKS = 3
TOP_K_EXPERTS = 2
LANES = 128
SUBLANES = 8
MASK_VALUE = -1e30
VMEM_LIMIT = 56 * 1024 * 1024


def _params(*sem):
    return pltpu.CompilerParams(dimension_semantics=sem, vmem_limit_bytes=VMEM_LIMIT)


def _rms(x, g):
    r = lax.rsqrt(jnp.mean(x * x, axis=-1, keepdims=True) + EPS)
    return x * r * g


def _dot(a, b):
    return jnp.dot(a, b, preferred_element_type=F32)


def _dot_t(a, b, precision=None):
    return lax.dot_general(a, b, (((1,), (1,)), ((), ())), preferred_element_type=F32, precision=precision)


def _dot_t_split(a, b):
    ah, bh = a.astype(BF16), b.astype(BF16)
    al, bl = (a - ah.astype(F32)).astype(BF16), (b - bh.astype(F32)).astype(BF16)
    return _dot_t(ah, bh) + _dot_t(ah, bl) + _dot_t(al, bh)


def _silu(x):
    return x * jax.nn.sigmoid(x)


def _ple(h, p, g, w_gate, w_proj):
    gate = jax.nn.sigmoid(_dot(_rms(h, g).astype(BF16), w_gate))
    return h + gate * _dot(p.astype(BF16), w_proj)


def _rope_head(x, cos, sin, lane):
    half = ROT_DIM // 2
    partner = jnp.where(lane < half, pltpu.roll(x, HEAD_DIM - half, 1), pltpu.roll(x, half, 1))
    return x * cos + partner * sin


CONV_HALO = 32
CONV_CHUNK = 16


def _conv_prompt_kernel(x_ref, g_ref, w1_ref, b1_ref, wrep_ref, bdw_ref, lng_ref, lnb_ref, w2_ref, b2_ref,
                        out_ref, state_ref, upad_ref, y_ref, *, tm, width, dc):
    t = pl.program_id(1)

    @pl.when(t == 0)
    def _():
        upad_ref[...] = jnp.zeros(upad_ref.shape, F32)

    x = x_ref[0]
    uu = _dot(_rms(x, g_ref[...]).astype(BF16), w1_ref[...]) + b1_ref[...]
    upad_ref[CONV_HALO:CONV_HALO + tm, :] = uu[:, :dc] * jax.nn.sigmoid(uu[:, dc:])

    first = CONV_HALO - (width - 1)
    span = CONV_CHUNK + SUBLANES

    def chunk(i, carry):
        base = pl.multiple_of(i * CONV_CHUNK, CONV_CHUNK)
        acc = jnp.broadcast_to(bdw_ref[...], (CONV_CHUNK, dc))
        for r in range(SUBLANES):
            part = None
            for k in range(width):
                a, kr = divmod(first + k, SUBLANES)
                if kr != r:
                    continue
                w = wrep_ref[k * SUBLANES:(k + 1) * SUBLANES, :]
                term = upad_ref[pl.ds(base + a * SUBLANES, span), :] * jnp.concatenate([w] * (span // SUBLANES), axis=0)
                part = term if part is None else part + term
            if part is not None:
                acc = acc + part[r:r + CONV_CHUNK, :]
        mu = jnp.mean(acc, axis=-1, keepdims=True)
        d = acc - mu
        var = jnp.mean(d * d, axis=-1, keepdims=True)
        z = d * lax.rsqrt(var + EPS) * lng_ref[...] + lnb_ref[...]
        y_ref[pl.ds(base, CONV_CHUNK), :] = _silu(z).astype(BF16)
        return carry

    lax.fori_loop(0, tm // CONV_CHUNK, chunk, 0)
    out_ref[0] = x + _dot(y_ref[...], w2_ref[...]) + b2_ref[...]
    state_ref[0] = upad_ref[tm + first:tm + CONV_HALO, :]
    upad_ref[0:CONV_HALO, :] = upad_ref[tm:tm + CONV_HALO, :]


def _conv_prompt(x, g, w1, b1, wdw, bdw, lng, lnb, w2, b2, *, tm):
    bsz, seq, d = x.shape
    width, dc = wdw.shape
    assert seq % tm == 0 and tm % CONV_CHUNK == 0 and width - 1 <= CONV_HALO
    row = lambda a: a.reshape(1, -1)
    full = lambda a: pl.BlockSpec(a.shape, lambda b, t: (0,) * a.ndim)
    wrep = jnp.repeat(wdw, SUBLANES, axis=0)
    args = (row(g), w1, row(b1), wrep, row(bdw), row(lng), row(lnb), w2, row(b2))
    return pl.pallas_call(
        functools.partial(_conv_prompt_kernel, tm=tm, width=width, dc=dc),
        grid=(bsz, seq // tm),
        in_specs=[pl.BlockSpec((1, tm, d), lambda b, t: (b, t, 0))] + [full(a) for a in args],
        out_specs=[pl.BlockSpec((1, tm, d), lambda b, t: (b, t, 0)),
                   pl.BlockSpec((1, width - 1, dc), lambda b, t: (b, 0, 0))],
        out_shape=[jax.ShapeDtypeStruct((bsz, seq, d), F32),
                   jax.ShapeDtypeStruct((bsz, width - 1, dc), F32)],
        scratch_shapes=[pltpu.VMEM((tm + CONV_HALO + SUBLANES, dc), F32), pltpu.VMEM((tm, dc), BF16)],
        compiler_params=_params("parallel", "arbitrary"),
        name="conv_prompt",
    )(x, *args)


def _conv_sample_kernel(x_ref, st_ref, g_ref, w1_ref, b1_ref, wdw_ref, bdw_ref, lng_ref, lnb_ref, w2_ref, b2_ref,
                        out_ref, state_ref, *, width, dc):
    nb = x_ref.shape[0]
    x = x_ref[...]
    uu = _dot(_rms(x, g_ref[...]).astype(BF16), w1_ref[...]) + b1_ref[...]
    u = uu[:, :dc] * jax.nn.sigmoid(uu[:, dc:])
    st = st_ref[...]
    acc = jnp.sum(st * wdw_ref[0:width - 1, :][None], axis=1) + u * wdw_ref[width - 1:width, :] + bdw_ref[...]
    mu = jnp.mean(acc, axis=-1, keepdims=True)
    d = acc - mu
    var = jnp.mean(d * d, axis=-1, keepdims=True)
    z = d * lax.rsqrt(var + EPS) * lng_ref[...] + lnb_ref[...]
    out_ref[...] = x + _dot(_silu(z).astype(BF16), w2_ref[...]) + b2_ref[...]
    state_ref[:, 0:width - 2, :] = st_ref[:, 1:width - 1, :]
    for b in range(nb):
        state_ref[b, width - 2:width - 1, :] = u[b:b + 1, :]


def _conv_sample(x, st, g, w1, b1, wdw, bdw, lng, lnb, w2, b2):
    nb, d = x.shape
    width, dc = wdw.shape
    row = lambda a: a.reshape(1, -1)
    return pl.pallas_call(
        functools.partial(_conv_sample_kernel, width=width, dc=dc),
        out_shape=[jax.ShapeDtypeStruct((nb, d), F32), jax.ShapeDtypeStruct((nb, width - 1, dc), F32)],
        compiler_params=pltpu.CompilerParams(vmem_limit_bytes=VMEM_LIMIT),
        name="conv_sample",
    )(x, st, row(g), w1, row(b1), wdw, row(bdw), row(lng), row(lnb), w2, row(b2))


def _ffn_kernel(x_ref, g_ref, wg_ref, wu_ref, wd_ref, p_ref, gp_ref, wpg_ref, wpp_ref, out_ref, xn_ref, acc_ref):
    f = pl.program_id(1)

    @pl.when(f == 0)
    def _():
        xn_ref[...] = _rms(x_ref[...], g_ref[...]).astype(BF16)
        acc_ref[...] = jnp.zeros(acc_ref.shape, F32)

    xn = xn_ref[...]
    a = (_silu(_dot(xn, wg_ref[...])) * _dot(xn, wu_ref[...])).astype(BF16)
    acc_ref[...] += _dot(a, wd_ref[...])

    @pl.when(f == pl.num_programs(1) - 1)
    def _():
        out_ref[...] = _ple(x_ref[...] + acc_ref[...], p_ref[...], gp_ref[...], wpg_ref[...], wpp_ref[...])


def _ffn(x, g, wg, wu, wd, p, gp, wpg, wpp, *, tm, tf):
    n, d = x.shape
    dff = wg.shape[1]
    assert n % tm == 0 and dff % tf == 0
    row = lambda a: a.reshape(1, -1)
    full = lambda a: pl.BlockSpec(a.shape, lambda i, f: (0,) * a.ndim)
    return pl.pallas_call(
        _ffn_kernel,
        grid=(n // tm, dff // tf),
        in_specs=[pl.BlockSpec((tm, d), lambda i, f: (i, 0)), full(row(g)),
                  pl.BlockSpec((d, tf), lambda i, f: (0, f)), pl.BlockSpec((d, tf), lambda i, f: (0, f)),
                  pl.BlockSpec((tf, d), lambda i, f: (f, 0)),
                  pl.BlockSpec((tm, p.shape[1]), lambda i, f: (i, 0)), full(row(gp)), full(wpg), full(wpp)],
        out_specs=pl.BlockSpec((tm, d), lambda i, f: (i, 0)),
        out_shape=jax.ShapeDtypeStruct((n, d), F32),
        scratch_shapes=[pltpu.VMEM((tm, d), BF16), pltpu.VMEM((tm, d), F32)],
        compiler_params=_params("parallel", "arbitrary"),
        name="ffn_ple",
    )(x, row(g), wg, wu, wd, p, row(gp), wpg, wpp)


def _qkv_prompt_kernel(x_ref, g_ref, w_ref, cos_ref, sin_ref,
                       qt_ref, k_ref, v_ref, kb_ref, vt_ref, biast_ref, kmt_ref, *, tm, nh, nkv, nblk):
    t = pl.program_id(1)
    group = nh // nkv
    nq = nh * HEAD_DIM
    nk = nkv * HEAD_DIM

    @pl.when(t == 0)
    def _():
        kmt_ref[...] = jnp.zeros(kmt_ref.shape, F32)

    qkv = _dot(_rms(x_ref[0], g_ref[...]).astype(BF16), w_ref[...])
    cos = cos_ref[...]
    sin = sin_ref[...]
    lane = lax.broadcasted_iota(jnp.int32, (tm, HEAD_DIM), 1)
    scale = HEAD_DIM ** -0.5
    q_heads = [_rope_head(qkv[:, h * HEAD_DIM:(h + 1) * HEAD_DIM], cos, sin, lane) * scale for h in range(nh)]
    q = jnp.concatenate(q_heads, axis=1)
    k = jnp.concatenate([_rope_head(qkv[:, nq + h * HEAD_DIM:nq + (h + 1) * HEAD_DIM], cos, sin, lane)
                         for h in range(nkv)], axis=1)
    v = qkv[:, nq + nk:]
    for h in range(nh):
        qt_ref[0, h] = q_heads[h].T.astype(BF16)
    for kv in range(nkv):
        k_ref[0, pl.ds(kv, tm, stride=nkv), :] = k[:, kv * HEAD_DIM:(kv + 1) * HEAD_DIM]
        v_ref[0, pl.ds(kv, tm, stride=nkv), :] = v[:, kv * HEAD_DIM:(kv + 1) * HEAD_DIM]
        vt_ref[0, kv] = v[:, kv * HEAD_DIM:(kv + 1) * HEAD_DIM].T.astype(BF16)
    kb_ref[0] = k.astype(BF16)

    kmt = kmt_ref[...]
    kmt_row = lax.broadcasted_iota(jnp.int32, kmt.shape, 0)
    kmt_head = lax.broadcasted_iota(jnp.int32, kmt.shape, 1) // HEAD_DIM
    for i in range(tm // MOBA_BLOCK):
        km = jnp.mean(k[i * MOBA_BLOCK:(i + 1) * MOBA_BLOCK, :], axis=0, keepdims=True)
        km = jnp.concatenate([km[:, (h // group) * HEAD_DIM:(h // group + 1) * HEAD_DIM] for h in range(nh)], axis=1)
        n = t * (tm // MOBA_BLOCK) + i
        kmt = jnp.where(kmt_row == kmt_head * nblk + n, km, kmt)
    kmt_ref[...] = kmt

    width = nh * nblk
    col = lax.broadcasted_iota(jnp.int32, (tm, width), 1)
    n_idx = col % nblk
    own = (t * tm + lax.broadcasted_iota(jnp.int32, (tm, width), 0)) // MOBA_BLOCK
    past = n_idx < own
    s = jnp.where(past, _dot_t_split(q, kmt), -jnp.inf)
    rank = jnp.zeros((tm, width), F32)
    for dlt in range(1, nblk):
        wrap = n_idx + dlt >= nblk
        other = jnp.where(wrap, pltpu.roll(s, nblk - dlt, 1), pltpu.roll(s, width - dlt, 1))
        rank = rank + jnp.where(other > s, 1.0, 0.0) + jnp.where(wrap, jnp.where(other == s, 1.0, 0.0), 0.0)
    bias = jnp.where(past, jnp.where(rank < TOP_K_BLOCKS, 0.0, MASK_VALUE), MASK_VALUE)
    biast_ref[0] = bias.T.astype(BF16)


def _qkv_prompt(x, g, w, cos, sin, *, tm, nh, nkv):
    bsz, seq, d = x.shape
    nq, nk = nh * HEAD_DIM, nkv * HEAD_DIM
    nblk = seq // MOBA_BLOCK
    assert seq % tm == 0 and tm % MOBA_BLOCK == 0 and nh * nblk == LANES
    tok = lambda c: pl.BlockSpec((1, tm, c), lambda b, t: (b, t, 0))
    kv_rows = pl.BlockSpec((1, tm * nkv, HEAD_DIM), lambda b, t: (b, t, 0))
    full = lambda a: pl.BlockSpec(a.shape, lambda b, t: (0,) * a.ndim)
    g = g.reshape(1, -1)
    return pl.pallas_call(
        functools.partial(_qkv_prompt_kernel, tm=tm, nh=nh, nkv=nkv, nblk=nblk),
        grid=(bsz, seq // tm),
        in_specs=[tok(d), full(g), full(w),
                  pl.BlockSpec((tm, HEAD_DIM), lambda b, t: (t, 0)), pl.BlockSpec((tm, HEAD_DIM), lambda b, t: (t, 0))],
        out_specs=[pl.BlockSpec((1, nh, HEAD_DIM, tm), lambda b, t: (b, 0, 0, t)), kv_rows, kv_rows, tok(nk),
                   pl.BlockSpec((1, nkv, HEAD_DIM, tm), lambda b, t: (b, 0, 0, t)),
                   pl.BlockSpec((1, nh * nblk, tm), lambda b, t: (b, 0, t))],
        out_shape=[jax.ShapeDtypeStruct((bsz, nh, HEAD_DIM, seq), BF16),
                   jax.ShapeDtypeStruct((bsz, seq * nkv, HEAD_DIM), F32),
                   jax.ShapeDtypeStruct((bsz, seq * nkv, HEAD_DIM), F32),
                   jax.ShapeDtypeStruct((bsz, seq, nk), BF16),
                   jax.ShapeDtypeStruct((bsz, nkv, HEAD_DIM, seq), BF16),
                   jax.ShapeDtypeStruct((bsz, nh * nblk, seq), BF16)],
        scratch_shapes=[pltpu.VMEM((nh * nblk, nq), F32)],
        compiler_params=_params("parallel", "arbitrary"),
        name="qkv_prompt",
    )(x, g, w, cos, sin)


def _qkv_sample_kernel(x_ref, g_ref, w_ref, cos_ref, sin_ref, q_ref, k_ref, v_ref, *, nh, nkv):
    nb = x_ref.shape[0]
    nq = nh * HEAD_DIM
    nk = nkv * HEAD_DIM
    qkv = _dot(_rms(x_ref[...], g_ref[...]).astype(BF16), w_ref[...])
    cos = cos_ref[...]
    sin = sin_ref[...]
    lane = lax.broadcasted_iota(jnp.int32, (nb, HEAD_DIM), 1)
    scale = HEAD_DIM ** -0.5
    q_ref[...] = jnp.concatenate([_rope_head(qkv[:, h * HEAD_DIM:(h + 1) * HEAD_DIM], cos, sin, lane) * scale
                                  for h in range(nh)], axis=1)
    k_ref[...] = jnp.concatenate([_rope_head(qkv[:, nq + h * HEAD_DIM:nq + (h + 1) * HEAD_DIM], cos, sin, lane)
                                  for h in range(nkv)], axis=1)
    v_ref[...] = qkv[:, nq + nk:]


def _qkv_sample(x, g, w, cos, sin, *, nh, nkv):
    nb = x.shape[0]
    nq, nk = nh * HEAD_DIM, nkv * HEAD_DIM
    return pl.pallas_call(
        functools.partial(_qkv_sample_kernel, nh=nh, nkv=nkv),
        out_shape=[jax.ShapeDtypeStruct((nb, nq), F32), jax.ShapeDtypeStruct((nb, nk), F32),
                   jax.ShapeDtypeStruct((nb, nk), F32)],
        compiler_params=pltpu.CompilerParams(vmem_limit_bytes=VMEM_LIMIT),
        name="qkv_sample",
    )(x, g.reshape(1, -1), w, cos, sin)


def _attn_prompt_kernel(qt_ref, kb_ref, vt_ref, biast_ref, h_ref, wo_ref, out_ref, *, nh, nkv, nblk):
    o = pl.program_id(1)
    hk = pl.program_id(2)
    group = nh // nkv
    blk = MOBA_BLOCK
    bias = biast_ref[0]
    feat_head = lax.broadcasted_iota(jnp.int32, bias.shape, 0) // nblk
    q_t = jnp.concatenate([qt_ref[0, g] for g in range(group)], axis=1)
    q_aug = jnp.concatenate(
        [q_t, jnp.concatenate([jnp.where(feat_head == hk * group + g, bias, jnp.zeros_like(bias))
                               for g in range(group)], axis=1)], axis=0)
    cols = group * blk

    own0 = pl.multiple_of(o * blk, blk)
    s = _dot(kb_ref[0, pl.ds(own0, blk), :], q_t)
    kpos = lax.broadcasted_iota(jnp.int32, (blk, cols), 0)
    qpos = lax.broadcasted_iota(jnp.int32, (blk, cols), 1) % blk
    s = jnp.where(kpos <= qpos, s, MASK_VALUE)
    m = jnp.max(s, axis=0, keepdims=True)
    p = jnp.exp(s - m)
    l = jnp.sum(p, axis=0, keepdims=True)
    acc = _dot(vt_ref[0, 0, :, pl.ds(own0, blk)], p.astype(BF16))
    lane = lax.broadcasted_iota(jnp.int32, (blk, nh * nblk), 1)

    def past(j, carry):
        m, l, acc = carry
        j0 = pl.multiple_of(j * blk, blk)
        pick = jnp.where(lane % nblk == j, 1.0, 0.0).astype(BF16)
        s = _dot(jnp.concatenate([kb_ref[0, pl.ds(j0, blk), :], pick], axis=1), q_aug)
        m_new = jnp.maximum(m, jnp.max(s, axis=0, keepdims=True))
        alpha = jnp.exp(m - m_new)
        p = jnp.exp(s - m_new)
        l = alpha * l + jnp.sum(p, axis=0, keepdims=True)
        acc = alpha * acc + _dot(vt_ref[0, 0, :, pl.ds(j0, blk)], p.astype(BF16))
        return m_new, l, acc

    m, l, acc = lax.fori_loop(0, o, past, (m, l, acc))
    att = (acc / l).astype(BF16)
    proj = sum(lax.dot_general(att[:, g * blk:(g + 1) * blk], wo_ref[g * HEAD_DIM:(g + 1) * HEAD_DIM, :],
                               (((0,), (0,)), ((), ())), preferred_element_type=F32) for g in range(group))

    @pl.when(hk == 0)
    def _():
        out_ref[0] = h_ref[0] + proj

    @pl.when(hk != 0)
    def _():
        out_ref[0] += proj


def _attn_prompt(qt, kb, vt, biast, h, wo, *, nh, nkv):
    bsz, seq, d = h.shape
    nblk = seq // MOBA_BLOCK
    group = nh // nkv
    gw = group * HEAD_DIM
    return pl.pallas_call(
        functools.partial(_attn_prompt_kernel, nh=nh, nkv=nkv, nblk=nblk),
        grid=(bsz, nblk, nkv),
        in_specs=[pl.BlockSpec((1, group, HEAD_DIM, MOBA_BLOCK), lambda b, o, k: (b, k, 0, o)),
                  pl.BlockSpec((1, seq, HEAD_DIM), lambda b, o, k: (b, 0, k)),
                  pl.BlockSpec((1, 1, HEAD_DIM, seq), lambda b, o, k: (b, k, 0, 0)),
                  pl.BlockSpec((1, biast.shape[1], MOBA_BLOCK), lambda b, o, k: (b, 0, o)),
                  pl.BlockSpec((1, MOBA_BLOCK, d), lambda b, o, k: (b, o, 0)),
                  pl.BlockSpec((gw, d), lambda b, o, k: (k, 0))],
        out_specs=pl.BlockSpec((1, MOBA_BLOCK, d), lambda b, o, k: (b, o, 0)),
        out_shape=jax.ShapeDtypeStruct((bsz, seq, d), F32),
        compiler_params=_params("parallel", "parallel", "arbitrary"),
        name="attn_prompt",
    )(qt, kb, vt, biast, h, wo)


PAGES_PER_STEP = 32


def _sample_select_kernel(pt_ref, *refs, nh, nkv, ppb, nblk):
    pages = refs[:PAGES_PER_STEP]
    q_ref, idx_ref, km_ref = refs[PAGES_PER_STEP:]
    s = pl.program_id(1)
    group = nh // nkv
    bps = PAGES_PER_STEP // ppb
    psize = pages[0].shape[1] // nkv

    @pl.when(s == 0)
    def _():
        km_ref[...] = jnp.zeros(km_ref.shape, F32)

    km = km_ref[...]
    km_row = lax.broadcasted_iota(jnp.int32, km.shape, 0)
    for i in range(bps):
        tot = jnp.concatenate(
            [sum(jnp.sum(pages[i * ppb + j][0, pl.ds(kv, psize, stride=nkv), :], axis=0, keepdims=True)
                 for j in range(ppb)) for kv in range(nkv)], axis=1)
        km = jnp.where(km_row == s * bps + i, tot / (psize * ppb), km)
    km_ref[...] = km

    @pl.when(s == pl.num_programs(1) - 1)
    def _():
        km = km_ref[...]
        q = q_ref[0]
        sc = jnp.concatenate(
            [_dot_t(q[kv * group:(kv + 1) * group, :], km[:, kv * HEAD_DIM:(kv + 1) * HEAD_DIM],
                    precision=lax.Precision.HIGHEST) for kv in range(nkv)], axis=0)
        col = lax.broadcasted_iota(jnp.int32, sc.shape, 1)
        lane = lax.broadcasted_iota(jnp.int32, (nh, LANES), 1)
        out = jnp.zeros((nh, LANES), jnp.int32)
        for r in range(TOP_K_BLOCKS):
            best = jnp.max(sc, axis=-1, keepdims=True)
            pick = jnp.min(jnp.where(sc == best, col, nblk), axis=-1, keepdims=True)
            out = jnp.where(lane == r, pick, out)
            sc = jnp.where(col == pick, -jnp.inf, sc)
        idx_ref[0] = out


def _sample_select(cache_k, page_table, q, *, nh, nkv):
    npool, prow, hd = cache_k.shape
    nb, npages = page_table.shape
    ppb = MOBA_BLOCK // (prow // nkv)
    nblk = npages // ppb
    assert npages % PAGES_PER_STEP == 0 and PAGES_PER_STEP % ppb == 0 and nblk >= TOP_K_BLOCKS
    page_spec = lambda j: pl.BlockSpec((1, prow, hd), lambda b, s, pt: (pt[b * npages + s * PAGES_PER_STEP + j], 0, 0))
    grid_spec = pltpu.PrefetchScalarGridSpec(
        num_scalar_prefetch=1,
        grid=(nb, npages // PAGES_PER_STEP),
        in_specs=[page_spec(j) for j in range(PAGES_PER_STEP)]
        + [pl.BlockSpec((1, nh, HEAD_DIM), lambda b, s, pt: (b, 0, 0))],
        out_specs=pl.BlockSpec((1, nh, LANES), lambda b, s, pt: (b, 0, 0)),
        scratch_shapes=[pltpu.VMEM((nblk, nkv * hd), F32)],
    )
    return pl.pallas_call(
        functools.partial(_sample_select_kernel, nh=nh, nkv=nkv, ppb=ppb, nblk=nblk),
        grid_spec=grid_spec,
        out_shape=jax.ShapeDtypeStruct((nb, nh, LANES), jnp.int32),
        compiler_params=_params("parallel", "arbitrary"),
        name="sample_select",
    )(page_table.reshape(-1), *([cache_k] * PAGES_PER_STEP), q.reshape(nb, nh, HEAD_DIM))


def _sample_attn_kernel(pt_ref, idx_ref, q_ref, kn_ref, vn_ref, *refs, group, nkv, npg):
    kpages = refs[:group * npg]
    vpages = refs[group * npg:2 * group * npg]
    out_ref = refs[2 * group * npg]
    hkv = pl.program_id(1)
    prow = kpages[0].shape[1]
    mine = lax.broadcasted_iota(jnp.int32, (prow, 1), 0) % nkv == hkv
    outs = []
    for g in range(group):
        q = q_ref[0, g:g + 1, :]
        m = jnp.sum(q * kn_ref[0], axis=-1, keepdims=True)
        l = jnp.ones((1, 1), F32)
        acc = vn_ref[0]
        for i in range(npg):
            k = kpages[g * npg + i][0]
            sc = jnp.where(mine, jnp.sum(k * q, axis=-1, keepdims=True), MASK_VALUE)
            m_new = jnp.maximum(m, jnp.max(sc, axis=0, keepdims=True))
            alpha = jnp.exp(m - m_new)
            p = jnp.exp(sc - m_new)
            l = alpha * l + jnp.sum(p, axis=0, keepdims=True)
            acc = alpha * acc + jnp.sum(p * vpages[g * npg + i][0], axis=0, keepdims=True)
            m = m_new
        outs.append(acc / l)
    out_ref[0] = jnp.concatenate(outs, axis=0)


def _sample_attn(cache_k, cache_v, page_table, idx, q, kn, vn, *, nh, nkv):
    npool, prow, hd = cache_k.shape
    nb, npages = page_table.shape
    ppb = MOBA_BLOCK // (prow // nkv)
    group = nh // nkv
    nsel = TOP_K_BLOCKS
    npg = nsel * ppb

    def page_spec(g, i):
        def imap(b, kv, pt, ix):
            blk = ix[(b * nh + kv * group + g) * nsel + i // ppb]
            return (pt[b * npages + blk * ppb + i % ppb], 0, 0)
        return pl.BlockSpec((1, prow, hd), imap)

    qspec = pl.BlockSpec((1, group, HEAD_DIM), lambda b, kv, pt, ix: (b * nkv + kv, 0, 0))
    kvspec = pl.BlockSpec((1, 1, HEAD_DIM), lambda b, kv, pt, ix: (b * nkv + kv, 0, 0))
    pages = [page_spec(g, i) for g in range(group) for i in range(npg)]
    grid_spec = pltpu.PrefetchScalarGridSpec(
        num_scalar_prefetch=2,
        grid=(nb, nkv),
        in_specs=[qspec, kvspec, kvspec] + pages + pages,
        out_specs=qspec,
    )
    out = pl.pallas_call(
        functools.partial(_sample_attn_kernel, group=group, nkv=nkv, npg=npg),
        grid_spec=grid_spec,
        out_shape=jax.ShapeDtypeStruct((nb * nkv, group, HEAD_DIM), F32),
        compiler_params=_params("parallel", "parallel"),
        name="sample_attn",
    )(page_table.reshape(-1), idx.reshape(-1),
      q.reshape(nb * nkv, group, HEAD_DIM), kn.reshape(nb * nkv, 1, HEAD_DIM), vn.reshape(nb * nkv, 1, HEAD_DIM),
      *([cache_k] * (group * npg)), *([cache_v] * (group * npg)))
    return out.reshape(nb, nh * HEAD_DIM)


def _proj_residual_kernel(a_ref, h_ref, w_ref, out_ref):
    out_ref[...] = h_ref[...] + _dot(a_ref[...].astype(BF16), w_ref[...])


def _proj_residual(a, h, w):
    return pl.pallas_call(
        _proj_residual_kernel,
        out_shape=jax.ShapeDtypeStruct(h.shape, F32),
        compiler_params=pltpu.CompilerParams(vmem_limit_bytes=VMEM_LIMIT),
        name="proj_residual",
    )(a, h, w)


MXU_DEPTH = 256
MOE_ROW_GRAN = 32
MOE_VARIANTS = 16
MOE_SPECIALISED = (-96, 160)


def _router_kernel(x_ref, g_ref, wrt_ref, tri_ref, xn_ref, gate_ref, rank_ref, cnt_ref):
    xn = _rms(x_ref[...], g_ref[...])
    xn_ref[...] = xn.astype(BF16)
    logits = _dot_t_split(wrt_ref[...], xn)
    ne = logits.shape[0]
    e = jnp.exp(logits - jnp.max(logits, axis=0, keepdims=True))
    probs = e / jnp.sum(e, axis=0, keepdims=True)
    eid = lax.broadcasted_iota(jnp.int32, probs.shape, 0)
    rest = probs
    member = jnp.zeros(probs.shape, jnp.bool_)
    top_sum = jnp.zeros((1, probs.shape[1]), F32)
    for _ in range(TOP_K_EXPERTS):
        best = jnp.max(rest, axis=0, keepdims=True)
        pick = eid == jnp.min(jnp.where(rest == best, eid, ne), axis=0, keepdims=True)
        member = member | pick
        top_sum = top_sum + best
        rest = jnp.where(pick, -1.0, rest)
    gate_ref[...] = jnp.where(member, probs / top_sum, 0.0)
    mem = jnp.where(member, 1.0, 0.0)
    rank = _dot(mem.astype(BF16), tri_ref[...])
    rank_ref[...] = jnp.where(member, rank, -1.0)
    cnt_ref[0] = jnp.broadcast_to(jnp.sum(mem, axis=1, keepdims=True), cnt_ref.shape[1:])


def _router(x, g, wr, *, tm):
    n, d = x.shape
    ne = wr.shape[1]
    nt = n // tm
    assert n % tm == 0
    tri = (lax.broadcasted_iota(jnp.int32, (tm, tm), 0) < lax.broadcasted_iota(jnp.int32, (tm, tm), 1)).astype(BF16)
    full = lambda a: pl.BlockSpec(a.shape, lambda i: (0,) * a.ndim)
    g = g.reshape(1, -1)
    wrt = wr.T
    xn, gate, rank, cnt = pl.pallas_call(
        _router_kernel,
        grid=(nt,),
        in_specs=[pl.BlockSpec((tm, d), lambda i: (i, 0)), full(g), full(wrt), full(tri)],
        out_specs=[pl.BlockSpec((tm, d), lambda i: (i, 0)), pl.BlockSpec((ne, tm), lambda i: (0, i)),
                   pl.BlockSpec((ne, tm), lambda i: (0, i)), pl.BlockSpec((1, ne, LANES), lambda i: (i, 0, 0))],
        out_shape=[jax.ShapeDtypeStruct((n, d), BF16), jax.ShapeDtypeStruct((ne, n), F32),
                   jax.ShapeDtypeStruct((ne, n), F32), jax.ShapeDtypeStruct((nt, ne, LANES), F32)],
        compiler_params=_params("parallel"),
        name="router",
    )(x, g, wrt, tri)
    return xn, gate, rank, cnt[:, :, 0].astype(jnp.int32).reshape(-1)


def _moe_kernel(eid_ref, cnt_ref, cnts_ref, xn_ref, gate_ref, rank_ref, wg_ref, wu_ref, wd_ref, hin_ref,
                xns_ref, gates_ref, ranks_ref, hins_ref, out_ref, outs_ref, xc_ref, y_ref, *, tm, tf, ne):
    e = eid_ref[0]
    i = pl.program_id(0)
    weights = (wg_ref, wu_ref, wd_ref)
    _expert_rows(e, cnt_ref[i * ne + e], xn_ref, gate_ref, rank_ref, hin_ref, out_ref, weights, xc_ref, y_ref,
                 tm=tm, tf=tf, ne=ne)

    @pl.when(i == pl.num_programs(0) - 1)
    def _():
        _expert_rows(e, cnts_ref[e], xns_ref, gates_ref, ranks_ref, hins_ref, outs_ref, weights, xc_ref, y_ref,
                     tm=xns_ref.shape[0], tf=tf, ne=ne)


def _moe_groups(tm):
    gran = min(MOE_ROW_GRAN, tm)
    return gran, min(MOE_VARIANTS, tm // gran)


def _expert_rows(e, cnt, xn_ref, gate_ref, rank_ref, hin_ref, out_ref, weights, xc_ref, y_ref, *, tm, tf, ne):
    wg_ref, wu_ref, wd_ref = weights
    gran, nvar = _moe_groups(tm)
    ngrp = (cnt + gran - 1) // gran
    rank = rank_ref[pl.ds(e, 1), :]
    kq = min(MXU_DEPTH, tm)

    def onehot(r0, rows):
        slot = lax.broadcasted_iota(jnp.int32, (rows, tm), 0) + r0
        return slot.astype(F32) == rank

    def process(r0, rows, base_ref):
        krows = -(-rows // kq) * kq
        sel = jnp.where(onehot(r0, rows), 1.0, 0.0).astype(BF16)
        xc_ref[0:rows, :] = _dot(sel, xn_ref[...]).astype(BF16)
        y_ref[0:krows, :] = jnp.zeros((krows, y_ref.shape[1]), F32)

        def ffn(fc, carry):
            xc = xc_ref[0:rows, :]
            f0 = pl.multiple_of(fc * tf, tf)
            a = (_silu(_dot(xc, wg_ref[0, :, pl.ds(f0, tf)])) * _dot(xc, wu_ref[0, :, pl.ds(f0, tf)])).astype(BF16)
            y_ref[0:rows, :] += _dot(a, wd_ref[0, pl.ds(f0, tf), :])
            return carry
        lax.fori_loop(0, wg_ref.shape[2] // tf, ffn, 0)
        w = jnp.where(onehot(r0, krows), gate_ref[pl.ds(e, 1), :], 0.0).astype(BF16)
        out_ref[...] = base_ref[...] + lax.dot_general(w, y_ref[0:krows, :].astype(BF16), (((0,), (0,)), ((), ())),
                                                       preferred_element_type=F32)

    @pl.when(ngrp == 0)
    def _():
        out_ref[...] = hin_ref[...]

    balanced = tm * TOP_K_EXPERTS // ne
    sizes = range(max(1, (balanced + MOE_SPECIALISED[0]) // gran), min(nvar, (balanced + MOE_SPECIALISED[1]) // gran) + 1)
    for n in sizes:
        @pl.when(ngrp == n)
        def _(n=n):
            process(0, n * gran, hin_ref)

    common = functools.reduce(lambda a, n: a | (ngrp == n), sizes, ngrp == 0)

    @pl.when(jnp.logical_not(common))
    def _():
        full = nvar * gran
        out_ref[...] = hin_ref[...]

        def body(c, carry):
            process(c * full, full, out_ref)
            return carry
        lax.fori_loop(0, (cnt + full - 1) // full, body, 0)


def _moe_finish_kernel(h_ref, p_ref, gp_ref, wpg_ref, wpp_ref, gf_ref, out_ref):
    out_ref[...] = _rms(_ple(h_ref[...], p_ref[...], gp_ref[...], wpg_ref[...], wpp_ref[...]), gf_ref[...])


def _moe(h, hs, g, wr, wg, wu, wd, p, ps, gp, wpg, wpp, gf, *, tm, tf):
    n, d = h.shape
    ns = hs.shape[0]
    ne, _, dff = wg.shape
    nt = n // tm
    for rows in (tm, ns):
        gran, nvar = _moe_groups(rows)
        assert rows % (nvar * gran) == 0 and (nvar * gran) % min(MXU_DEPTH, rows) == 0
    assert n % tm == 0 and dff % tf == 0 and ns <= tm
    xn, gate, rank, cnt = _router(h, g, wr, tm=tm)
    xns, gates, ranks, cnts = _router(hs, g, wr, tm=ns)
    resident = lambda a: pl.BlockSpec((1,) + a.shape[1:], lambda i, eid, c, cs: (eid[0], 0, 0),
                                      pipeline_mode=pl.Buffered(1))
    tile = lambda c: pl.BlockSpec((tm, c), lambda i, eid, c_, cs: (i, 0))
    lanes = pl.BlockSpec((ne, tm), lambda i, eid, c, cs: (0, i))
    whole = lambda a: pl.BlockSpec(a.shape, lambda i, eid, c, cs: (0,) * a.ndim)
    full = max(g * v for g, v in (_moe_groups(tm), _moe_groups(ns)))
    grid_spec = pltpu.PrefetchScalarGridSpec(
        num_scalar_prefetch=3,
        grid=(nt,),
        in_specs=[tile(d), lanes, lanes, resident(wg), resident(wu), resident(wd), tile(d),
                  whole(xns), whole(gates), whole(ranks), whole(hs)],
        out_specs=[tile(d), whole(hs)],
        scratch_shapes=[pltpu.VMEM((full, d), BF16), pltpu.VMEM((full, d), F32)],
    )
    one_expert = pl.pallas_call(
        functools.partial(_moe_kernel, tm=tm, tf=tf, ne=ne),
        grid_spec=grid_spec,
        out_shape=[jax.ShapeDtypeStruct((n, d), F32), jax.ShapeDtypeStruct((ns, d), F32)],
        input_output_aliases={9: 0, 13: 1},
        compiler_params=_params("arbitrary"),
        name="moe_expert",
    )
    for e in range(ne):
        h, hs = one_expert(jnp.full((1,), e, jnp.int32), cnt, cnts, xn, gate, rank, wg, wu, wd, h, xns, gates, ranks, hs)

    row = lambda a: a.reshape(1, -1)

    def finish(hm, pm, rows):
        whole1 = lambda a: pl.BlockSpec(a.shape, lambda i: (0,) * a.ndim)
        return pl.pallas_call(
            _moe_finish_kernel,
            grid=(hm.shape[0] // rows,),
            in_specs=[pl.BlockSpec((rows, d), lambda i: (i, 0)), pl.BlockSpec((rows, pm.shape[1]), lambda i: (i, 0)),
                      whole1(row(gp)), whole1(wpg), whole1(wpp), whole1(row(gf))],
            out_specs=pl.BlockSpec((rows, d), lambda i: (i, 0)),
            out_shape=jax.ShapeDtypeStruct(hm.shape, F32),
            compiler_params=_params("parallel"),
            name="moe_finish",
        )(hm, pm, row(gp), wpg, wpp, row(gf))
    return finish(h, p, tm), finish(hs, ps, ns)


def _rope_tables(pos):
    half = ROT_DIM // 2
    inv = jnp.power(ROPE_THETA, -jnp.arange(half, dtype=F32) / half)
    ang = pos.astype(F32)[:, None] * inv[None, :]
    cos, sin = jnp.cos(ang), jnp.sin(ang)
    rest = HEAD_DIM - ROT_DIM
    n = pos.shape[0]
    return (jnp.concatenate([cos, cos, jnp.ones((n, rest), F32)], axis=1),
            jnp.concatenate([-sin, sin, jnp.zeros((n, rest), F32)], axis=1))


def _tile(n, pref):
    return pref if n % pref == 0 else n


def kernel(x_prompt, x_sample, state_conv, cache_k, cache_v, page_table, p_prompt, p_sample, norm_mix, norm_ffn, norm_ple, ple_w_gate, ple_w_proj, conv_w_pw1, conv_b_pw1, conv_w_dw, conv_b_dw, conv_ln_g, conv_ln_b, conv_w_pw2, conv_b_pw2, ffn_w_gate, ffn_w_up, ffn_w_down, attn_w_qkv, attn_w_o, moe_w_router, moe_w_gate, moe_w_up, moe_w_down, norm_final):
    bsz, seq, d = x_prompt.shape
    nb, dec_seq, _ = x_sample.shape
    assert dec_seq == 1 and norm_mix.shape[0] == 2
    nh = d // HEAD_DIM
    nkv = (attn_w_qkv.shape[2] // HEAD_DIM - nh) // 2
    n_tok = bsz * seq
    dff = ffn_w_gate.shape[2]
    psize = cache_k.shape[2]
    past_len = page_table.shape[1] * psize
    bf = lambda w: w.astype(BF16)

    cw = (norm_mix[0], bf(conv_w_pw1[0]), conv_b_pw1[0], conv_w_dw[0], conv_b_dw[0], conv_ln_g[0], conv_ln_b[0],
          bf(conv_w_pw2[0]), conv_b_pw2[0])
    hp, conv_p = _conv_prompt(x_prompt, *cw, tm=_tile(seq, 512))
    hs, conv_s = _conv_sample(x_sample.reshape(nb, d), state_conv[0], *cw)
    fw = (norm_ffn[0], bf(ffn_w_gate[0]), bf(ffn_w_up[0]), bf(ffn_w_down[0]))
    pw0 = (norm_ple[0], bf(ple_w_gate[0]), bf(ple_w_proj[0]))
    tf = _tile(dff, 512)
    hp = _ffn(hp.reshape(n_tok, d), *fw, p_prompt[0].reshape(n_tok, -1), *pw0, tm=_tile(n_tok, 1024), tf=tf)
    hs = _ffn(hs, *fw, p_sample[0].reshape(nb, -1), *pw0, tm=nb, tf=tf)

    wqkv, wo = bf(attn_w_qkv[0]), bf(attn_w_o[0])
    cos_p, sin_p = _rope_tables(jnp.arange(seq, dtype=jnp.int32))
    qt, k_p, v_p, kb, vt, biast = _qkv_prompt(hp.reshape(bsz, seq, d), norm_mix[1], wqkv, cos_p, sin_p,
                                              tm=_tile(seq, 512), nh=nh, nkv=nkv)
    hp = _attn_prompt(qt, kb, vt, biast, hp.reshape(bsz, seq, d), wo, nh=nh, nkv=nkv).reshape(n_tok, d)

    cos_s, sin_s = _rope_tables(jnp.full((1,), past_len, jnp.int32))
    qs, k_s, v_s = _qkv_sample(hs, norm_mix[1], wqkv, cos_s, sin_s, nh=nh, nkv=nkv)
    ck = cache_k[0].reshape(cache_k.shape[1], psize * nkv, HEAD_DIM)
    cv = cache_v[0].reshape(cache_v.shape[1], psize * nkv, HEAD_DIM)
    idx = _sample_select(ck, page_table, qs, nh=nh, nkv=nkv)[:, :, :TOP_K_BLOCKS]
    att_s = _sample_attn(ck, cv, page_table, idx, qs, k_s, v_s, nh=nh, nkv=nkv)
    hs = _proj_residual(att_s, hs, wo)

    mw = (norm_ffn[1], moe_w_router[0], bf(moe_w_gate[0]), bf(moe_w_up[0]), bf(moe_w_down[0]))
    pw1 = (norm_ple[1], bf(ple_w_gate[1]), bf(ple_w_proj[1]), norm_final)
    yp, ys = _moe(hp, hs, *mw, p_prompt[1].reshape(n_tok, -1), p_sample[1].reshape(nb, -1), *pw1,
                  tm=_tile(n_tok, 1024), tf=tf)

    return (yp.reshape(bsz, seq, d), ys.reshape(nb, 1, d),
            conv_p[None], conv_s[None],
            k_p.reshape(1, bsz, seq, nkv, HEAD_DIM), v_p.reshape(1, bsz, seq, nkv, HEAD_DIM),
            k_s.reshape(1, nb, 1, nkv, HEAD_DIM), v_s.reshape(1, nb, 1, nkv, HEAD_DIM))
```

```python
import functools

import jax
import jax.numpy as jnp
from jax import lax
from jax.experimental import pallas as pl
from jax.experimental.pallas import tpu as pltpu

F32 = jnp.float32
BF16 = jnp.bfloat16

EPS = 1e-6
HEAD_DIM = 128
ROT_DIM = HEAD_DIM // 4
ROPE_THETA = 500000.0
MOBA_BLOCK = 256
TOP_K_BLOCKS = 3
TOP_K_EXPERTS = 2
LANES = 128
SUBLANES = 8
MASK_VALUE = -1e30
VMEM_LIMIT = 56 * 1024 * 1024


def _params(*sem):
    return pltpu.CompilerParams(dimension_semantics=sem, vmem_limit_bytes=VMEM_LIMIT)


def _rms(x, g):
    r = lax.rsqrt(jnp.mean(x * x, axis=-1, keepdims=True) + EPS)
    return x * r * g


def _dot(a, b):
    return jnp.dot(a, b, preferred_element_type=F32)


def _dot_t(a, b, precision=None):
    return lax.dot_general(a, b, (((1,), (1,)), ((), ())), preferred_element_type=F32, precision=precision)


def _dot_t_split(a, b):
    ah, bh = a.astype(BF16), b.astype(BF16)
    al, bl = (a - ah.astype(F32)).astype(BF16), (b - bh.astype(F32)).astype(BF16)
    return _dot_t(ah, bh) + _dot_t(ah, bl) + _dot_t(al, bh)


def _silu(x):
    return x * jax.nn.sigmoid(x)


def _ple(h, p, g, w_gate, w_proj):
    gate = jax.nn.sigmoid(_dot(_rms(h, g).astype(BF16), w_gate))
    return h + gate * _dot(p.astype(BF16), w_proj)


def _rope_head(x, cos, sin, lane):
    half = ROT_DIM // 2
    partner = jnp.where(lane < half, pltpu.roll(x, HEAD_DIM - half, 1), pltpu.roll(x, half, 1))
    return x * cos + partner * sin


CONV_HALO = 32
CONV_CHUNK = 64
CONV_STRIP = 128
CONV_NORM_ROWS = 16


def _conv_prompt_kernel(x_ref, g_ref, w1_ref, b1_ref, wrep_ref, bdw_ref, lng_ref, lnb_ref, w2_ref, b2_ref,
                        out_ref, state_ref, upad_ref, y_ref, cacc_ref, *, tm, width, dc):
    t = pl.program_id(1)

    @pl.when(t == 0)
    def _():
        upad_ref[...] = jnp.zeros(upad_ref.shape, F32)

    x = x_ref[0]
    uu = _dot(_rms(x, g_ref[...]).astype(BF16), w1_ref[...]) + b1_ref[...]
    upad_ref[CONV_HALO:CONV_HALO + tm, :] = uu[:, :dc] * jax.nn.sigmoid(uu[:, dc:])

    first = CONV_HALO - (width - 1)
    span = CONV_CHUNK + SUBLANES

    def chunk(i, carry):
        base = pl.multiple_of(i * CONV_CHUNK, CONV_CHUNK)
        for c0 in range(0, dc, CONV_STRIP):
            lanes = slice(c0, c0 + CONV_STRIP)
            acc = jnp.broadcast_to(bdw_ref[:, lanes], (CONV_CHUNK, CONV_STRIP))
            for r in range(SUBLANES):
                part = None
                for k in range(width):
                    a, kr = divmod(first + k, SUBLANES)
                    if kr != r:
                        continue
                    w = wrep_ref[k * SUBLANES:(k + 1) * SUBLANES, lanes]
                    term = upad_ref[pl.ds(base + a * SUBLANES, span), lanes] * jnp.concatenate(
                        [w] * (span // SUBLANES), axis=0)
                    part = term if part is None else part + term
                if part is not None:
                    acc = acc + part[r:r + CONV_CHUNK, :]
            cacc_ref[:, lanes] = acc
        for r0 in range(0, CONV_CHUNK, CONV_NORM_ROWS):
            acc = cacc_ref[r0:r0 + CONV_NORM_ROWS, :]
            mu = jnp.mean(acc, axis=-1, keepdims=True)
            d = acc - mu
            var = jnp.mean(d * d, axis=-1, keepdims=True)
            z = d * lax.rsqrt(var + EPS) * lng_ref[...] + lnb_ref[...]
            y_ref[pl.ds(base + r0, CONV_NORM_ROWS), :] = _silu(z).astype(BF16)
        return carry

    lax.fori_loop(0, tm // CONV_CHUNK, chunk, 0)
    out_ref[0] = x + _dot(y_ref[...], w2_ref[...]) + b2_ref[...]
    state_ref[0] = upad_ref[tm + first:tm + CONV_HALO, :]
    upad_ref[0:CONV_HALO, :] = upad_ref[tm:tm + CONV_HALO, :]


def _conv_prompt(x, g, w1, b1, wdw, bdw, lng, lnb, w2, b2, *, tm):
    bsz, seq, d = x.shape
    width, dc = wdw.shape
    assert seq % tm == 0 and tm % CONV_CHUNK == 0 and width - 1 <= CONV_HALO
    row = lambda a: a.reshape(1, -1)
    full = lambda a: pl.BlockSpec(a.shape, lambda b, t: (0,) * a.ndim)
    wrep = jnp.repeat(wdw, SUBLANES, axis=0)
    args = (row(g), w1, row(b1), wrep, row(bdw), row(lng), row(lnb), w2, row(b2))
    return pl.pallas_call(
        functools.partial(_conv_prompt_kernel, tm=tm, width=width, dc=dc),
        grid=(bsz, seq // tm),
        in_specs=[pl.BlockSpec((1, tm, d), lambda b, t: (b, t, 0))] + [full(a) for a in args],
        out_specs=[pl.BlockSpec((1, tm, d), lambda b, t: (b, t, 0)),
                   pl.BlockSpec((1, width - 1, dc), lambda b, t: (b, 0, 0))],
        out_shape=[jax.ShapeDtypeStruct((bsz, seq, d), F32),
                   jax.ShapeDtypeStruct((bsz, width - 1, dc), F32)],
        scratch_shapes=[pltpu.VMEM((tm + CONV_HALO + SUBLANES, dc), F32), pltpu.VMEM((tm, dc), BF16),
                        pltpu.VMEM((CONV_CHUNK, dc), F32)],
        compiler_params=_params("parallel", "arbitrary"),
        name="conv_prompt",
    )(x, *args)


def _conv_sample_kernel(x_ref, st_ref, g_ref, w1_ref, b1_ref, wdw_ref, bdw_ref, lng_ref, lnb_ref, w2_ref, b2_ref,
                        out_ref, state_ref, *, width, dc):
    nb = x_ref.shape[0]
    x = x_ref[...]
    uu = _dot(_rms(x, g_ref[...]).astype(BF16), w1_ref[...]) + b1_ref[...]
    u = uu[:, :dc] * jax.nn.sigmoid(uu[:, dc:])
    st = st_ref[...]
    acc = jnp.sum(st * wdw_ref[0:width - 1, :][None], axis=1) + u * wdw_ref[width - 1:width, :] + bdw_ref[...]
    mu = jnp.mean(acc, axis=-1, keepdims=True)
    d = acc - mu
    var = jnp.mean(d * d, axis=-1, keepdims=True)
    z = d * lax.rsqrt(var + EPS) * lng_ref[...] + lnb_ref[...]
    out_ref[...] = x + _dot(_silu(z).astype(BF16), w2_ref[...]) + b2_ref[...]
    state_ref[:, 0:width - 2, :] = st_ref[:, 1:width - 1, :]
    for b in range(nb):
        state_ref[b, width - 2:width - 1, :] = u[b:b + 1, :]


def _conv_sample(x, st, g, w1, b1, wdw, bdw, lng, lnb, w2, b2):
    nb, d = x.shape
    width, dc = wdw.shape
    row = lambda a: a.reshape(1, -1)
    return pl.pallas_call(
        functools.partial(_conv_sample_kernel, width=width, dc=dc),
        out_shape=[jax.ShapeDtypeStruct((nb, d), F32), jax.ShapeDtypeStruct((nb, width - 1, dc), F32)],
        compiler_params=pltpu.CompilerParams(vmem_limit_bytes=VMEM_LIMIT),
        name="conv_sample",
    )(x, st, row(g), w1, row(b1), wdw, row(bdw), row(lng), row(lnb), w2, row(b2))


def _ffn_kernel(x_ref, g_ref, wg_ref, wu_ref, wd_ref, p_ref, gp_ref, wpg_ref, wpp_ref, out_ref, xn_ref, acc_ref):
    f = pl.program_id(1)

    @pl.when(f == 0)
    def _():
        xn_ref[...] = _rms(x_ref[...], g_ref[...]).astype(BF16)
        acc_ref[...] = jnp.zeros(acc_ref.shape, F32)

    xn = xn_ref[...]
    a = (_silu(_dot(xn, wg_ref[...])) * _dot(xn, wu_ref[...])).astype(BF16)
    acc_ref[...] += _dot(a, wd_ref[...])

    @pl.when(f == pl.num_programs(1) - 1)
    def _():
        out_ref[...] = _ple(x_ref[...] + acc_ref[...], p_ref[...], gp_ref[...], wpg_ref[...], wpp_ref[...])


def _ffn(x, g, wg, wu, wd, p, gp, wpg, wpp, *, tm, tf):
    n, d = x.shape
    dff = wg.shape[1]
    assert n % tm == 0 and dff % tf == 0
    row = lambda a: a.reshape(1, -1)
    full = lambda a: pl.BlockSpec(a.shape, lambda i, f: (0,) * a.ndim)
    return pl.pallas_call(
        _ffn_kernel,
        grid=(n // tm, dff // tf),
        in_specs=[pl.BlockSpec((tm, d), lambda i, f: (i, 0)), full(row(g)),
                  pl.BlockSpec((d, tf), lambda i, f: (0, f)), pl.BlockSpec((d, tf), lambda i, f: (0, f)),
                  pl.BlockSpec((tf, d), lambda i, f: (f, 0)),
                  pl.BlockSpec((tm, p.shape[1]), lambda i, f: (i, 0)), full(row(gp)), full(wpg), full(wpp)],
        out_specs=pl.BlockSpec((tm, d), lambda i, f: (i, 0)),
        out_shape=jax.ShapeDtypeStruct((n, d), F32),
        scratch_shapes=[pltpu.VMEM((tm, d), BF16), pltpu.VMEM((tm, d), F32)],
        compiler_params=_params("parallel", "arbitrary"),
        name="ffn_ple",
    )(x, row(g), wg, wu, wd, p, row(gp), wpg, wpp)


def _qkv_prompt_kernel(x_ref, g_ref, w_ref, cos_ref, sin_ref,
                       qt_ref, k_ref, v_ref, kb_ref, vt_ref, biast_ref, kmt_ref, *, tm, nh, nkv, nblk):
    t = pl.program_id(1)
    group = nh // nkv
    nq = nh * HEAD_DIM
    nk = nkv * HEAD_DIM

    @pl.when(t == 0)
    def _():
        kmt_ref[...] = jnp.zeros(kmt_ref.shape, F32)

    qkv = _dot(_rms(x_ref[0], g_ref[...]).astype(BF16), w_ref[...])
    cos = cos_ref[...]
    sin = sin_ref[...]
    lane = lax.broadcasted_iota(jnp.int32, (tm, HEAD_DIM), 1)
    scale = HEAD_DIM ** -0.5
    q_heads = [_rope_head(qkv[:, h * HEAD_DIM:(h + 1) * HEAD_DIM], cos, sin, lane) * scale for h in range(nh)]
    q = jnp.concatenate(q_heads, axis=1)
    k = jnp.concatenate([_rope_head(qkv[:, nq + h * HEAD_DIM:nq + (h + 1) * HEAD_DIM], cos, sin, lane)
                         for h in range(nkv)], axis=1)
    v = qkv[:, nq + nk:]
    for h in range(nh):
        qt_ref[0, h] = q_heads[h].T.astype(BF16)
    for kv in range(nkv):
        k_ref[0, pl.ds(kv, tm, stride=nkv), :] = k[:, kv * HEAD_DIM:(kv + 1) * HEAD_DIM]
        v_ref[0, pl.ds(kv, tm, stride=nkv), :] = v[:, kv * HEAD_DIM:(kv + 1) * HEAD_DIM]
        vt_ref[0, kv] = v[:, kv * HEAD_DIM:(kv + 1) * HEAD_DIM].T.astype(BF16)
    kb_ref[0] = k.astype(BF16)

    kmt = kmt_ref[...]
    kmt_row = lax.broadcasted_iota(jnp.int32, kmt.shape, 0)
    kmt_head = lax.broadcasted_iota(jnp.int32, kmt.shape, 1) // HEAD_DIM
    for i in range(tm // MOBA_BLOCK):
        km = jnp.mean(k[i * MOBA_BLOCK:(i + 1) * MOBA_BLOCK, :], axis=0, keepdims=True)
        km = jnp.concatenate([km[:, (h // group) * HEAD_DIM:(h // group + 1) * HEAD_DIM] for h in range(nh)], axis=1)
        n = t * (tm // MOBA_BLOCK) + i
        kmt = jnp.where(kmt_row == kmt_head * nblk + n, km, kmt)
    kmt_ref[...] = kmt

    s = _dot_t_split(q, kmt).T.reshape(nh, nblk, tm)
    n_idx = lax.broadcasted_iota(jnp.int32, s.shape, 1)
    own = (t * tm + lax.broadcasted_iota(jnp.int32, s.shape, 2)) // MOBA_BLOCK
    past = n_idx < own
    s = jnp.where(past, s, -jnp.inf)
    rank = jnp.zeros(s.shape, F32)
    for dlt in range(1, nblk):
        wrap = n_idx + dlt >= nblk
        other = jnp.concatenate([s[:, dlt:, :], s[:, :dlt, :]], axis=1)
        rank = rank + jnp.where(other > s, 1.0, 0.0) + jnp.where(wrap, jnp.where(other == s, 1.0, 0.0), 0.0)
    bias = jnp.where(past, jnp.where(rank < TOP_K_BLOCKS, 0.0, MASK_VALUE), MASK_VALUE)
    biast_ref[0] = bias.reshape(nh * nblk, tm).astype(BF16)


def _qkv_prompt(x, g, w, cos, sin, *, tm, nh, nkv):
    bsz, seq, d = x.shape
    nq, nk = nh * HEAD_DIM, nkv * HEAD_DIM
    nblk = seq // MOBA_BLOCK
    assert seq % tm == 0 and tm % MOBA_BLOCK == 0 and nh * nblk == LANES
    tok = lambda c: pl.BlockSpec((1, tm, c), lambda b, t: (b, t, 0))
    kv_rows = pl.BlockSpec((1, tm * nkv, HEAD_DIM), lambda b, t: (b, t, 0))
    full = lambda a: pl.BlockSpec(a.shape, lambda b, t: (0,) * a.ndim)
    g = g.reshape(1, -1)
    return pl.pallas_call(
        functools.partial(_qkv_prompt_kernel, tm=tm, nh=nh, nkv=nkv, nblk=nblk),
        grid=(bsz, seq // tm),
        in_specs=[tok(d), full(g), full(w),
                  pl.BlockSpec((tm, HEAD_DIM), lambda b, t: (t, 0)), pl.BlockSpec((tm, HEAD_DIM), lambda b, t: (t, 0))],
        out_specs=[pl.BlockSpec((1, nh, HEAD_DIM, tm), lambda b, t: (b, 0, 0, t)), kv_rows, kv_rows, tok(nk),
                   pl.BlockSpec((1, nkv, HEAD_DIM, tm), lambda b, t: (b, 0, 0, t)),
                   pl.BlockSpec((1, nh * nblk, tm), lambda b, t: (b, 0, t))],
        out_shape=[jax.ShapeDtypeStruct((bsz, nh, HEAD_DIM, seq), BF16),
                   jax.ShapeDtypeStruct((bsz, seq * nkv, HEAD_DIM), F32),
                   jax.ShapeDtypeStruct((bsz, seq * nkv, HEAD_DIM), F32),
                   jax.ShapeDtypeStruct((bsz, seq, nk), BF16),
                   jax.ShapeDtypeStruct((bsz, nkv, HEAD_DIM, seq), BF16),
                   jax.ShapeDtypeStruct((bsz, nh * nblk, seq), BF16)],
        scratch_shapes=[pltpu.VMEM((nh * nblk, nq), F32)],
        compiler_params=_params("parallel", "arbitrary"),
        name="qkv_prompt",
    )(x, g, w, cos, sin)


def _qkv_sample_kernel(x_ref, g_ref, w_ref, cos_ref, sin_ref, q_ref, k_ref, v_ref, *, nh, nkv):
    nb = x_ref.shape[0]
    nq = nh * HEAD_DIM
    nk = nkv * HEAD_DIM
    qkv = _dot(_rms(x_ref[...], g_ref[...]).astype(BF16), w_ref[...])
    cos = cos_ref[...]
    sin = sin_ref[...]
    lane = lax.broadcasted_iota(jnp.int32, (nb, HEAD_DIM), 1)
    scale = HEAD_DIM ** -0.5
    q_ref[...] = jnp.concatenate([_rope_head(qkv[:, h * HEAD_DIM:(h + 1) * HEAD_DIM], cos, sin, lane) * scale
                                  for h in range(nh)], axis=1)
    k_ref[...] = jnp.concatenate([_rope_head(qkv[:, nq + h * HEAD_DIM:nq + (h + 1) * HEAD_DIM], cos, sin, lane)
                                  for h in range(nkv)], axis=1)
    v_ref[...] = qkv[:, nq + nk:]


def _qkv_sample(x, g, w, cos, sin, *, nh, nkv):
    nb = x.shape[0]
    nq, nk = nh * HEAD_DIM, nkv * HEAD_DIM
    return pl.pallas_call(
        functools.partial(_qkv_sample_kernel, nh=nh, nkv=nkv),
        out_shape=[jax.ShapeDtypeStruct((nb, nq), F32), jax.ShapeDtypeStruct((nb, nk), F32),
                   jax.ShapeDtypeStruct((nb, nk), F32)],
        compiler_params=pltpu.CompilerParams(vmem_limit_bytes=VMEM_LIMIT),
        name="qkv_sample",
    )(x, g.reshape(1, -1), w, cos, sin)


def _attn_prompt_kernel(qt_ref, kb_ref, vt_ref, biast_ref, h_ref, wo_ref, out_ref, *, nh, nkv, nblk):
    o = pl.program_id(1)
    hk = pl.program_id(2)
    group = nh // nkv
    blk = MOBA_BLOCK
    bias = biast_ref[0]
    feat_head = lax.broadcasted_iota(jnp.int32, bias.shape, 0) // nblk
    q_t = jnp.concatenate([qt_ref[0, g] for g in range(group)], axis=1)
    q_aug = jnp.concatenate(
        [q_t, jnp.concatenate([jnp.where(feat_head == hk * group + g, bias, jnp.zeros_like(bias))
                               for g in range(group)], axis=1)], axis=0)
    cols = group * blk

    own0 = pl.multiple_of(o * blk, blk)
    s = _dot(kb_ref[0, pl.ds(own0, blk), :], q_t)
    kpos = lax.broadcasted_iota(jnp.int32, (blk, cols), 0)
    qpos = lax.broadcasted_iota(jnp.int32, (blk, cols), 1) % blk
    s = jnp.where(kpos <= qpos, s, MASK_VALUE)
    m = jnp.max(s, axis=0, keepdims=True)
    p = jnp.exp(s - m)
    l = jnp.sum(p, axis=0, keepdims=True)
    acc = _dot(vt_ref[0, 0, :, pl.ds(own0, blk)], p.astype(BF16))
    lane = lax.broadcasted_iota(jnp.int32, (blk, nh * nblk), 1)

    def past(j, carry):
        m, l, acc = carry
        j0 = pl.multiple_of(j * blk, blk)
        pick = jnp.where(lane % nblk == j, 1.0, 0.0).astype(BF16)
        s = _dot(jnp.concatenate([kb_ref[0, pl.ds(j0, blk), :], pick], axis=1), q_aug)
        m_new = jnp.maximum(m, jnp.max(s, axis=0, keepdims=True))
        alpha = jnp.exp(m - m_new)
        p = jnp.exp(s - m_new)
        l = alpha * l + jnp.sum(p, axis=0, keepdims=True)
        acc = alpha * acc + _dot(vt_ref[0, 0, :, pl.ds(j0, blk)], p.astype(BF16))
        return m_new, l, acc

    m, l, acc = lax.fori_loop(0, o, past, (m, l, acc))
    att = (acc / l).astype(BF16)
    proj = sum(lax.dot_general(att[:, g * blk:(g + 1) * blk], wo_ref[g * HEAD_DIM:(g + 1) * HEAD_DIM, :],
                               (((0,), (0,)), ((), ())), preferred_element_type=F32) for g in range(group))

    @pl.when(hk == 0)
    def _():
        out_ref[0] = h_ref[0] + proj

    @pl.when(hk != 0)
    def _():
        out_ref[0] += proj


def _attn_prompt(qt, kb, vt, biast, h, wo, *, nh, nkv):
    bsz, seq, d = h.shape
    nblk = seq // MOBA_BLOCK
    group = nh // nkv
    gw = group * HEAD_DIM
    return pl.pallas_call(
        functools.partial(_attn_prompt_kernel, nh=nh, nkv=nkv, nblk=nblk),
        grid=(bsz, nblk, nkv),
        in_specs=[pl.BlockSpec((1, group, HEAD_DIM, MOBA_BLOCK), lambda b, o, k: (b, k, 0, o)),
                  pl.BlockSpec((1, seq, HEAD_DIM), lambda b, o, k: (b, 0, k)),
                  pl.BlockSpec((1, 1, HEAD_DIM, seq), lambda b, o, k: (b, k, 0, 0)),
                  pl.BlockSpec((1, biast.shape[1], MOBA_BLOCK), lambda b, o, k: (b, 0, o)),
                  pl.BlockSpec((1, MOBA_BLOCK, d), lambda b, o, k: (b, o, 0)),
                  pl.BlockSpec((gw, d), lambda b, o, k: (k, 0))],
        out_specs=pl.BlockSpec((1, MOBA_BLOCK, d), lambda b, o, k: (b, o, 0)),
        out_shape=jax.ShapeDtypeStruct((bsz, seq, d), F32),
        compiler_params=_params("parallel", "parallel", "arbitrary"),
        name="attn_prompt",
    )(qt, kb, vt, biast, h, wo)


PAGES_PER_STEP = 32


def _sample_select_kernel(pt_ref, *refs, nh, nkv, ppb, nblk):
    pages = refs[:PAGES_PER_STEP]
    q_ref, idx_ref, km_ref = refs[PAGES_PER_STEP:]
    s = pl.program_id(1)
    group = nh // nkv
    bps = PAGES_PER_STEP // ppb
    psize = pages[0].shape[1] // nkv

    @pl.when(s == 0)
    def _():
        km_ref[...] = jnp.zeros(km_ref.shape, F32)

    km = km_ref[...]
    km_row = lax.broadcasted_iota(jnp.int32, km.shape, 0)
    for i in range(bps):
        tot = jnp.concatenate(
            [sum(jnp.sum(pages[i * ppb + j][0, pl.ds(kv, psize, stride=nkv), :], axis=0, keepdims=True)
                 for j in range(ppb)) for kv in range(nkv)], axis=1)
        km = jnp.where(km_row == s * bps + i, tot / (psize * ppb), km)
    km_ref[...] = km

    @pl.when(s == pl.num_programs(1) - 1)
    def _():
        km = km_ref[...]
        q = q_ref[0]
        sc = jnp.concatenate(
            [_dot_t(q[kv * group:(kv + 1) * group, :], km[:, kv * HEAD_DIM:(kv + 1) * HEAD_DIM],
                    precision=lax.Precision.HIGHEST) for kv in range(nkv)], axis=0)
        col = lax.broadcasted_iota(jnp.int32, sc.shape, 1)
        lane = lax.broadcasted_iota(jnp.int32, (nh, LANES), 1)
        out = jnp.zeros((nh, LANES), jnp.int32)
        for r in range(TOP_K_BLOCKS):
            best = jnp.max(sc, axis=-1, keepdims=True)
            pick = jnp.min(jnp.where(sc == best, col, nblk), axis=-1, keepdims=True)
            out = jnp.where(lane == r, pick, out)
            sc = jnp.where(col == pick, -jnp.inf, sc)
        idx_ref[0] = out


def _sample_select(cache_k, page_table, q, *, nh, nkv):
    npool, prow, hd = cache_k.shape
    nb, npages = page_table.shape
    ppb = MOBA_BLOCK // (prow // nkv)
    nblk = npages // ppb
    assert npages % PAGES_PER_STEP == 0 and PAGES_PER_STEP % ppb == 0 and nblk >= TOP_K_BLOCKS
    page_spec = lambda j: pl.BlockSpec((1, prow, hd), lambda b, s, pt: (pt[b * npages + s * PAGES_PER_STEP + j], 0, 0))
    grid_spec = pltpu.PrefetchScalarGridSpec(
        num_scalar_prefetch=1,
        grid=(nb, npages // PAGES_PER_STEP),
        in_specs=[page_spec(j) for j in range(PAGES_PER_STEP)]
        + [pl.BlockSpec((1, nh, HEAD_DIM), lambda b, s, pt: (b, 0, 0))],
        out_specs=pl.BlockSpec((1, nh, LANES), lambda b, s, pt: (b, 0, 0)),
        scratch_shapes=[pltpu.VMEM((nblk, nkv * hd), F32)],
    )
    return pl.pallas_call(
        functools.partial(_sample_select_kernel, nh=nh, nkv=nkv, ppb=ppb, nblk=nblk),
        grid_spec=grid_spec,
        out_shape=jax.ShapeDtypeStruct((nb, nh, LANES), jnp.int32),
        compiler_params=_params("parallel", "arbitrary"),
        name="sample_select",
    )(page_table.reshape(-1), *([cache_k] * PAGES_PER_STEP), q.reshape(nb, nh, HEAD_DIM))


def _sample_attn_kernel(pt_ref, idx_ref, q_ref, kn_ref, vn_ref, *refs, group, nkv, npg):
    kpages = refs[:group * npg]
    vpages = refs[group * npg:2 * group * npg]
    out_ref = refs[2 * group * npg]
    hkv = pl.program_id(1)
    prow = kpages[0].shape[1]
    mine = lax.broadcasted_iota(jnp.int32, (prow, 1), 0) % nkv == hkv
    outs = []
    for g in range(group):
        q = q_ref[0, g:g + 1, :]
        m = jnp.sum(q * kn_ref[0], axis=-1, keepdims=True)
        l = jnp.ones((1, 1), F32)
        acc = vn_ref[0]
        for i in range(npg):
            k = kpages[g * npg + i][0]
            sc = jnp.where(mine, jnp.sum(k * q, axis=-1, keepdims=True), MASK_VALUE)
            m_new = jnp.maximum(m, jnp.max(sc, axis=0, keepdims=True))
            alpha = jnp.exp(m - m_new)
            p = jnp.exp(sc - m_new)
            l = alpha * l + jnp.sum(p, axis=0, keepdims=True)
            acc = alpha * acc + jnp.sum(p * vpages[g * npg + i][0], axis=0, keepdims=True)
            m = m_new
        outs.append(acc / l)
    out_ref[0] = jnp.concatenate(outs, axis=0)


def _sample_attn(cache_k, cache_v, page_table, idx, q, kn, vn, *, nh, nkv):
    npool, prow, hd = cache_k.shape
    nb, npages = page_table.shape
    ppb = MOBA_BLOCK // (prow // nkv)
    group = nh // nkv
    nsel = TOP_K_BLOCKS
    npg = nsel * ppb

    def page_spec(g, i):
        def imap(b, kv, pt, ix):
            blk = ix[(b * nh + kv * group + g) * nsel + i // ppb]
            return (pt[b * npages + blk * ppb + i % ppb], 0, 0)
        return pl.BlockSpec((1, prow, hd), imap)

    qspec = pl.BlockSpec((1, group, HEAD_DIM), lambda b, kv, pt, ix: (b * nkv + kv, 0, 0))
    kvspec = pl.BlockSpec((1, 1, HEAD_DIM), lambda b, kv, pt, ix: (b * nkv + kv, 0, 0))
    pages = [page_spec(g, i) for g in range(group) for i in range(npg)]
    grid_spec = pltpu.PrefetchScalarGridSpec(
        num_scalar_prefetch=2,
        grid=(nb, nkv),
        in_specs=[qspec, kvspec, kvspec] + pages + pages,
        out_specs=qspec,
    )
    out = pl.pallas_call(
        functools.partial(_sample_attn_kernel, group=group, nkv=nkv, npg=npg),
        grid_spec=grid_spec,
        out_shape=jax.ShapeDtypeStruct((nb * nkv, group, HEAD_DIM), F32),
        compiler_params=_params("parallel", "parallel"),
        name="sample_attn",
    )(page_table.reshape(-1), idx.reshape(-1),
      q.reshape(nb * nkv, group, HEAD_DIM), kn.reshape(nb * nkv, 1, HEAD_DIM), vn.reshape(nb * nkv, 1, HEAD_DIM),
      *([cache_k] * (group * npg)), *([cache_v] * (group * npg)))
    return out.reshape(nb, nh * HEAD_DIM)


def _proj_residual_kernel(a_ref, h_ref, w_ref, out_ref):
    out_ref[...] = h_ref[...] + _dot(a_ref[...].astype(BF16), w_ref[...])


def _proj_residual(a, h, w):
    return pl.pallas_call(
        _proj_residual_kernel,
        out_shape=jax.ShapeDtypeStruct(h.shape, F32),
        compiler_params=pltpu.CompilerParams(vmem_limit_bytes=VMEM_LIMIT),
        name="proj_residual",
    )(a, h, w)


MXU_DEPTH = 256
MOE_ROW_GRAN = 32
MOE_VARIANTS = 16
MOE_SPECIALISED = (-96, 160)


def _router_kernel(x_ref, g_ref, wrt_ref, tri_ref, xn_ref, gate_ref, rank_ref, cnt_ref):
    xn = _rms(x_ref[...], g_ref[...])
    xn_ref[...] = xn.astype(BF16)
    logits = _dot_t_split(wrt_ref[...], xn)
    ne = logits.shape[0]
    e = jnp.exp(logits - jnp.max(logits, axis=0, keepdims=True))
    probs = e / jnp.sum(e, axis=0, keepdims=True)
    eid = lax.broadcasted_iota(jnp.int32, probs.shape, 0)
    rest = probs
    member = jnp.zeros(probs.shape, jnp.bool_)
    top_sum = jnp.zeros((1, probs.shape[1]), F32)
    for _ in range(TOP_K_EXPERTS):
        best = jnp.max(rest, axis=0, keepdims=True)
        pick = eid == jnp.min(jnp.where(rest == best, eid, ne), axis=0, keepdims=True)
        member = member | pick
        top_sum = top_sum + best
        rest = jnp.where(pick, -1.0, rest)
    gate_ref[...] = jnp.where(member, probs / top_sum, 0.0)
    mem = jnp.where(member, 1.0, 0.0)
    rank = _dot(mem.astype(BF16), tri_ref[...])
    rank_ref[...] = jnp.where(member, rank, -1.0)
    cnt_ref[0] = jnp.broadcast_to(jnp.sum(mem, axis=1, keepdims=True), cnt_ref.shape[1:])


def _router(x, g, wr, *, tm):
    n, d = x.shape
    ne = wr.shape[1]
    nt = n // tm
    assert n % tm == 0
    tri = (lax.broadcasted_iota(jnp.int32, (tm, tm), 0) < lax.broadcasted_iota(jnp.int32, (tm, tm), 1)).astype(BF16)
    full = lambda a: pl.BlockSpec(a.shape, lambda i: (0,) * a.ndim)
    g = g.reshape(1, -1)
    wrt = wr.T
    xn, gate, rank, cnt = pl.pallas_call(
        _router_kernel,
        grid=(nt,),
        in_specs=[pl.BlockSpec((tm, d), lambda i: (i, 0)), full(g), full(wrt), full(tri)],
        out_specs=[pl.BlockSpec((tm, d), lambda i: (i, 0)), pl.BlockSpec((ne, tm), lambda i: (0, i)),
                   pl.BlockSpec((ne, tm), lambda i: (0, i)), pl.BlockSpec((1, ne, LANES), lambda i: (i, 0, 0))],
        out_shape=[jax.ShapeDtypeStruct((n, d), BF16), jax.ShapeDtypeStruct((ne, n), F32),
                   jax.ShapeDtypeStruct((ne, n), F32), jax.ShapeDtypeStruct((nt, ne, LANES), F32)],
        compiler_params=_params("parallel"),
        name="router",
    )(x, g, wrt, tri)
    return xn, gate, rank, cnt[:, :, 0].astype(jnp.int32).reshape(-1)


def _moe_kernel(eid_ref, cnt_ref, cnts_ref, xn_ref, gate_ref, rank_ref, wg_ref, wu_ref, wd_ref, hin_ref,
                xns_ref, gates_ref, ranks_ref, hins_ref, out_ref, outs_ref, xc_ref, y_ref, *, tm, tf, ne):
    e = eid_ref[0]
    i = pl.program_id(0)
    weights = (wg_ref, wu_ref, wd_ref)
    _expert_rows(e, cnt_ref[i * ne + e], xn_ref, gate_ref, rank_ref, hin_ref, out_ref, weights, xc_ref, y_ref,
                 tm=tm, tf=tf, ne=ne)

    @pl.when(i == pl.num_programs(0) - 1)
    def _():
        _expert_rows(e, cnts_ref[e], xns_ref, gates_ref, ranks_ref, hins_ref, outs_ref, weights, xc_ref, y_ref,
                     tm=xns_ref.shape[0], tf=tf, ne=ne)


def _moe_groups(tm):
    gran = min(MOE_ROW_GRAN, tm)
    return gran, min(MOE_VARIANTS, tm // gran)


def _expert_rows(e, cnt, xn_ref, gate_ref, rank_ref, hin_ref, out_ref, weights, xc_ref, y_ref, *, tm, tf, ne):
    wg_ref, wu_ref, wd_ref = weights
    gran, nvar = _moe_groups(tm)
    ngrp = (cnt + gran - 1) // gran
    rank = rank_ref[pl.ds(e, 1), :]
    kq = min(MXU_DEPTH, tm)

    def onehot(r0, rows):
        slot = lax.broadcasted_iota(jnp.int32, (rows, tm), 0) + r0
        return slot.astype(F32) == rank

    def process(r0, rows, base_ref):
        krows = -(-rows // kq) * kq
        sel = jnp.where(onehot(r0, rows), 1.0, 0.0).astype(BF16)
        xc_ref[0:rows, :] = _dot(sel, xn_ref[...]).astype(BF16)
        y_ref[0:krows, :] = jnp.zeros((krows, y_ref.shape[1]), F32)

        def ffn(fc, carry):
            xc = xc_ref[0:rows, :]
            f0 = pl.multiple_of(fc * tf, tf)
            a = (_silu(_dot(xc, wg_ref[0, :, pl.ds(f0, tf)])) * _dot(xc, wu_ref[0, :, pl.ds(f0, tf)])).astype(BF16)
            y_ref[0:rows, :] += _dot(a, wd_ref[0, pl.ds(f0, tf), :])
            return carry
        lax.fori_loop(0, wg_ref.shape[2] // tf, ffn, 0)
        w = jnp.where(onehot(r0, krows), gate_ref[pl.ds(e, 1), :], 0.0).astype(BF16)
        out_ref[...] = base_ref[...] + lax.dot_general(w, y_ref[0:krows, :].astype(BF16), (((0,), (0,)), ((), ())),
                                                       preferred_element_type=F32)

    @pl.when(ngrp == 0)
    def _():
        out_ref[...] = hin_ref[...]

    balanced = tm * TOP_K_EXPERTS // ne
    sizes = range(max(1, (balanced + MOE_SPECIALISED[0]) // gran), min(nvar, (balanced + MOE_SPECIALISED[1]) // gran) + 1)
    for n in sizes:
        @pl.when(ngrp == n)
        def _(n=n):
            process(0, n * gran, hin_ref)

    common = functools.reduce(lambda a, n: a | (ngrp == n), sizes, ngrp == 0)

    @pl.when(jnp.logical_not(common))
    def _():
        full = nvar * gran
        out_ref[...] = hin_ref[...]

        def body(c, carry):
            process(c * full, full, out_ref)
            return carry
        lax.fori_loop(0, (cnt + full - 1) // full, body, 0)


def _moe_finish_kernel(h_ref, p_ref, gp_ref, wpg_ref, wpp_ref, gf_ref, out_ref):
    out_ref[...] = _rms(_ple(h_ref[...], p_ref[...], gp_ref[...], wpg_ref[...], wpp_ref[...]), gf_ref[...])


def _moe(h, hs, g, wr, wg, wu, wd, p, ps, gp, wpg, wpp, gf, *, tm, tf):
    n, d = h.shape
    ns = hs.shape[0]
    ne, _, dff = wg.shape
    nt = n // tm
    for rows in (tm, ns):
        gran, nvar = _moe_groups(rows)
        assert rows % (nvar * gran) == 0 and (nvar * gran) % min(MXU_DEPTH, rows) == 0
    assert n % tm == 0 and dff % tf == 0 and ns <= tm
    xn, gate, rank, cnt = _router(h, g, wr, tm=tm)
    xns, gates, ranks, cnts = _router(hs, g, wr, tm=ns)
    resident = lambda a: pl.BlockSpec((1,) + a.shape[1:], lambda i, eid, c, cs: (eid[0], 0, 0),
                                      pipeline_mode=pl.Buffered(1))
    tile = lambda c: pl.BlockSpec((tm, c), lambda i, eid, c_, cs: (i, 0))
    lanes = pl.BlockSpec((ne, tm), lambda i, eid, c, cs: (0, i))
    whole = lambda a: pl.BlockSpec(a.shape, lambda i, eid, c, cs: (0,) * a.ndim)
    full = max(g * v for g, v in (_moe_groups(tm), _moe_groups(ns)))
    grid_spec = pltpu.PrefetchScalarGridSpec(
        num_scalar_prefetch=3,
        grid=(nt,),
        in_specs=[tile(d), lanes, lanes, resident(wg), resident(wu), resident(wd), tile(d),
                  whole(xns), whole(gates), whole(ranks), whole(hs)],
        out_specs=[tile(d), whole(hs)],
        scratch_shapes=[pltpu.VMEM((full, d), BF16), pltpu.VMEM((full, d), F32)],
    )
    one_expert = pl.pallas_call(
        functools.partial(_moe_kernel, tm=tm, tf=tf, ne=ne),
        grid_spec=grid_spec,
        out_shape=[jax.ShapeDtypeStruct((n, d), F32), jax.ShapeDtypeStruct((ns, d), F32)],
        input_output_aliases={9: 0, 13: 1},
        compiler_params=_params("arbitrary"),
        name="moe_expert",
    )
    for e in range(ne):
        h, hs = one_expert(jnp.full((1,), e, jnp.int32), cnt, cnts, xn, gate, rank, wg, wu, wd, h, xns, gates, ranks, hs)

    row = lambda a: a.reshape(1, -1)

    def finish(hm, pm, rows):
        whole1 = lambda a: pl.BlockSpec(a.shape, lambda i: (0,) * a.ndim)
        return pl.pallas_call(
            _moe_finish_kernel,
            grid=(hm.shape[0] // rows,),
            in_specs=[pl.BlockSpec((rows, d), lambda i: (i, 0)), pl.BlockSpec((rows, pm.shape[1]), lambda i: (i, 0)),
                      whole1(row(gp)), whole1(wpg), whole1(wpp), whole1(row(gf))],
            out_specs=pl.BlockSpec((rows, d), lambda i: (i, 0)),
            out_shape=jax.ShapeDtypeStruct(hm.shape, F32),
            compiler_params=_params("parallel"),
            name="moe_finish",
        )(hm, pm, row(gp), wpg, wpp, row(gf))
    return finish(h, p, tm), finish(hs, ps, ns)


def _rope_tables(pos):
    half = ROT_DIM // 2
    inv = jnp.power(ROPE_THETA, -jnp.arange(half, dtype=F32) / half)
    ang = pos.astype(F32)[:, None] * inv[None, :]
    cos, sin = jnp.cos(ang), jnp.sin(ang)
    rest = HEAD_DIM - ROT_DIM
    n = pos.shape[0]
    return (jnp.concatenate([cos, cos, jnp.ones((n, rest), F32)], axis=1),
            jnp.concatenate([-sin, sin, jnp.zeros((n, rest), F32)], axis=1))


def _tile(n, pref):
    return pref if n % pref == 0 else n


def kernel(x_prompt, x_sample, state_conv, cache_k, cache_v, page_table, p_prompt, p_sample, norm_mix, norm_ffn, norm_ple, ple_w_gate, ple_w_proj, conv_w_pw1, conv_b_pw1, conv_w_dw, conv_b_dw, conv_ln_g, conv_ln_b, conv_w_pw2, conv_b_pw2, ffn_w_gate, ffn_w_up, ffn_w_down, attn_w_qkv, attn_w_o, moe_w_router, moe_w_gate, moe_w_up, moe_w_down, norm_final):
    bsz, seq, d = x_prompt.shape
    nb, dec_seq, _ = x_sample.shape
    assert dec_seq == 1 and norm_mix.shape[0] == 2
    nh = d // HEAD_DIM
    nkv = (attn_w_qkv.shape[2] // HEAD_DIM - nh) // 2
    n_tok = bsz * seq
    dff = ffn_w_gate.shape[2]
    psize = cache_k.shape[2]
    past_len = page_table.shape[1] * psize
    bf = lambda w: w.astype(BF16)

    cw = (norm_mix[0], bf(conv_w_pw1[0]), conv_b_pw1[0], conv_w_dw[0], conv_b_dw[0], conv_ln_g[0], conv_ln_b[0],
          bf(conv_w_pw2[0]), conv_b_pw2[0])
    hp, conv_p = _conv_prompt(x_prompt, *cw, tm=_tile(seq, 512))
    hs, conv_s = _conv_sample(x_sample.reshape(nb, d), state_conv[0], *cw)
    fw = (norm_ffn[0], bf(ffn_w_gate[0]), bf(ffn_w_up[0]), bf(ffn_w_down[0]))
    pw0 = (norm_ple[0], bf(ple_w_gate[0]), bf(ple_w_proj[0]))
    tf = _tile(dff, 512)
    hp = _ffn(hp.reshape(n_tok, d), *fw, p_prompt[0].reshape(n_tok, -1), *pw0, tm=_tile(n_tok, 1024), tf=tf)
    hs = _ffn(hs, *fw, p_sample[0].reshape(nb, -1), *pw0, tm=nb, tf=tf)

    wqkv, wo = bf(attn_w_qkv[0]), bf(attn_w_o[0])
    cos_p, sin_p = _rope_tables(jnp.arange(seq, dtype=jnp.int32))
    qt, k_p, v_p, kb, vt, biast = _qkv_prompt(hp.reshape(bsz, seq, d), norm_mix[1], wqkv, cos_p, sin_p,
                                              tm=_tile(seq, 512), nh=nh, nkv=nkv)
    hp = _attn_prompt(qt, kb, vt, biast, hp.reshape(bsz, seq, d), wo, nh=nh, nkv=nkv).reshape(n_tok, d)

    cos_s, sin_s = _rope_tables(jnp.full((1,), past_len, jnp.int32))
    qs, k_s, v_s = _qkv_sample(hs, norm_mix[1], wqkv, cos_s, sin_s, nh=nh, nkv=nkv)
    ck = cache_k[0].reshape(cache_k.shape[1], psize * nkv, HEAD_DIM)
    cv = cache_v[0].reshape(cache_v.shape[1], psize * nkv, HEAD_DIM)
    idx = _sample_select(ck, page_table, qs, nh=nh, nkv=nkv)[:, :, :TOP_K_BLOCKS]
    att_s = _sample_attn(ck, cv, page_table, idx, qs, k_s, v_s, nh=nh, nkv=nkv)
    hs = _proj_residual(att_s, hs, wo)

    mw = (norm_ffn[1], moe_w_router[0], bf(moe_w_gate[0]), bf(moe_w_up[0]), bf(moe_w_down[0]))
    pw1 = (norm_ple[1], bf(ple_w_gate[1]), bf(ple_w_proj[1]), norm_final)
    yp, ys = _moe(hp, hs, *mw, p_prompt[1].reshape(n_tok, -1), p_sample[1].reshape(nb, -1), *pw1,
                  tm=_tile(n_tok, 1024), tf=tf)

    return (yp.reshape(bsz, seq, d), ys.reshape(nb, 1, d),
            conv_p[None], conv_s[None],
            k_p.reshape(1, bsz, seq, nkv, HEAD_DIM), v_p.reshape(1, bsz, seq, nkv, HEAD_DIM),
            k_s.reshape(1, nb, 1, nkv, HEAD_DIM), v_s.reshape(1, nb, 1, nkv, HEAD_DIM))
```

```python
import functools

import jax
import jax.numpy as jnp
from jax import lax
from jax.experimental import pallas as pl
from jax.experimental.pallas import tpu as pltpu

F32 = jnp.float32
BF16 = jnp.bfloat16

EPS = 1e-6
HEAD_DIM = 128
ROT_DIM = HEAD_DIM // 4
ROPE_THETA = 500000.0
MOBA_BLOCK = 256
TOP_K_BLOCKS = 3
TOP_K_EXPERTS = 2
LANES = 128
SUBLANES = 8
MASK_VALUE = -1e30
VMEM_LIMIT = 56 * 1024 * 1024
EPILOGUE_ROWS = 256


def _params(*sem):
    return pltpu.CompilerParams(dimension_semantics=sem, vmem_limit_bytes=VMEM_LIMIT)


def _rms(x, g):
    r = lax.rsqrt(jnp.mean(x * x, axis=-1, keepdims=True) + EPS)
    return x * r * g


def _dot(a, b):
    return jnp.dot(a, b, preferred_element_type=F32)


def _dot_t(a, b, precision=None):
    return lax.dot_general(a, b, (((1,), (1,)), ((), ())), preferred_element_type=F32, precision=precision)


def _dot_t_split(a, b):
    ah, bh = a.astype(BF16), b.astype(BF16)
    al, bl = (a - ah.astype(F32)).astype(BF16), (b - bh.astype(F32)).astype(BF16)
    return _dot_t(ah, bh) + _dot_t(ah, bl) + _dot_t(al, bh)


def _silu(x):
    return x * jax.nn.sigmoid(x)


def _ple(h, p, g, w_gate, w_proj):
    gate = jax.nn.sigmoid(_dot(_rms(h, g).astype(BF16), w_gate))
    return h + gate * _dot(p.astype(BF16), w_proj)


def _rope_head(x, cos, sin, lane):
    half = ROT_DIM // 2
    partner = jnp.where(lane < half, pltpu.roll(x, HEAD_DIM - half, 1), pltpu.roll(x, half, 1))
    return x * cos + partner * sin


CONV_HALO = 32
CONV_CHUNK = 64
CONV_STRIP = 128
CONV_NORM_ROWS = 16


def _conv_prompt_kernel(x_ref, g_ref, w1_ref, b1_ref, wrep_ref, bdw_ref, lng_ref, lnb_ref, w2_ref, b2_ref,
                        out_ref, state_ref, upad_ref, y_ref, cacc_ref, *, tm, width, dc):
    t = pl.program_id(1)

    @pl.when(t == 0)
    def _():
        upad_ref[...] = jnp.zeros(upad_ref.shape, F32)

    x = x_ref[0]
    uu = _dot(_rms(x, g_ref[...]).astype(BF16), w1_ref[...]) + b1_ref[...]
    upad_ref[CONV_HALO:CONV_HALO + tm, :] = uu[:, :dc] * jax.nn.sigmoid(uu[:, dc:])

    first = CONV_HALO - (width - 1)
    span = CONV_CHUNK + SUBLANES

    def chunk(i, carry):
        base = pl.multiple_of(i * CONV_CHUNK, CONV_CHUNK)
        for c0 in range(0, dc, CONV_STRIP):
            lanes = slice(c0, c0 + CONV_STRIP)
            acc = jnp.broadcast_to(bdw_ref[:, lanes], (CONV_CHUNK, CONV_STRIP))
            for r in range(SUBLANES):
                part = None
                for k in range(width):
                    a, kr = divmod(first + k, SUBLANES)
                    if kr != r:
                        continue
                    w = wrep_ref[k * SUBLANES:(k + 1) * SUBLANES, lanes]
                    term = upad_ref[pl.ds(base + a * SUBLANES, span), lanes] * jnp.concatenate(
                        [w] * (span // SUBLANES), axis=0)
                    part = term if part is None else part + term
                if part is not None:
                    acc = acc + part[r:r + CONV_CHUNK, :]
            cacc_ref[:, lanes] = acc
        for r0 in range(0, CONV_CHUNK, CONV_NORM_ROWS):
            acc = cacc_ref[r0:r0 + CONV_NORM_ROWS, :]
            mu = jnp.mean(acc, axis=-1, keepdims=True)
            d = acc - mu
            var = jnp.mean(d * d, axis=-1, keepdims=True)
            z = d * lax.rsqrt(var + EPS) * lng_ref[...] + lnb_ref[...]
            y_ref[pl.ds(base + r0, CONV_NORM_ROWS), :] = _silu(z).astype(BF16)
        return carry

    lax.fori_loop(0, tm // CONV_CHUNK, chunk, 0)
    out_ref[0] = x + _dot(y_ref[...], w2_ref[...]) + b2_ref[...]
    state_ref[0] = upad_ref[tm + first:tm + CONV_HALO, :]
    upad_ref[0:CONV_HALO, :] = upad_ref[tm:tm + CONV_HALO, :]


def _conv_prompt(x, g, w1, b1, wdw, bdw, lng, lnb, w2, b2, *, tm):
    bsz, seq, d = x.shape
    width, dc = wdw.shape
    assert seq % tm == 0 and tm % CONV_CHUNK == 0 and width - 1 <= CONV_HALO
    row = lambda a: a.reshape(1, -1)
    full = lambda a: pl.BlockSpec(a.shape, lambda b, t: (0,) * a.ndim)
    wrep = jnp.repeat(wdw, SUBLANES, axis=0)
    args = (row(g), w1, row(b1), wrep, row(bdw), row(lng), row(lnb), w2, row(b2))
    return pl.pallas_call(
        functools.partial(_conv_prompt_kernel, tm=tm, width=width, dc=dc),
        grid=(bsz, seq // tm),
        in_specs=[pl.BlockSpec((1, tm, d), lambda b, t: (b, t, 0))] + [full(a) for a in args],
        out_specs=[pl.BlockSpec((1, tm, d), lambda b, t: (b, t, 0)),
                   pl.BlockSpec((1, width - 1, dc), lambda b, t: (b, 0, 0))],
        out_shape=[jax.ShapeDtypeStruct((bsz, seq, d), F32),
                   jax.ShapeDtypeStruct((bsz, width - 1, dc), F32)],
        scratch_shapes=[pltpu.VMEM((tm + CONV_HALO + SUBLANES, dc), F32), pltpu.VMEM((tm, dc), BF16),
                        pltpu.VMEM((CONV_CHUNK, dc), F32)],
        compiler_params=_params("parallel", "arbitrary"),
        name="conv_prompt",
    )(x, *args)


def _conv_sample_kernel(x_ref, st_ref, g_ref, w1_ref, b1_ref, wdw_ref, bdw_ref, lng_ref, lnb_ref, w2_ref, b2_ref,
                        out_ref, state_ref, *, width, dc):
    nb = x_ref.shape[0]
    x = x_ref[...]
    uu = _dot(_rms(x, g_ref[...]).astype(BF16), w1_ref[...]) + b1_ref[...]
    u = uu[:, :dc] * jax.nn.sigmoid(uu[:, dc:])
    st = st_ref[...]
    acc = jnp.sum(st * wdw_ref[0:width - 1, :][None], axis=1) + u * wdw_ref[width - 1:width, :] + bdw_ref[...]
    mu = jnp.mean(acc, axis=-1, keepdims=True)
    d = acc - mu
    var = jnp.mean(d * d, axis=-1, keepdims=True)
    z = d * lax.rsqrt(var + EPS) * lng_ref[...] + lnb_ref[...]
    out_ref[...] = x + _dot(_silu(z).astype(BF16), w2_ref[...]) + b2_ref[...]
    state_ref[:, 0:width - 2, :] = st_ref[:, 1:width - 1, :]
    for b in range(nb):
        state_ref[b, width - 2:width - 1, :] = u[b:b + 1, :]


def _conv_sample(x, st, g, w1, b1, wdw, bdw, lng, lnb, w2, b2):
    nb, d = x.shape
    width, dc = wdw.shape
    row = lambda a: a.reshape(1, -1)
    return pl.pallas_call(
        functools.partial(_conv_sample_kernel, width=width, dc=dc),
        out_shape=[jax.ShapeDtypeStruct((nb, d), F32), jax.ShapeDtypeStruct((nb, width - 1, dc), F32)],
        compiler_params=pltpu.CompilerParams(vmem_limit_bytes=VMEM_LIMIT),
        name="conv_sample",
    )(x, st, row(g), w1, row(b1), wdw, row(bdw), row(lng), row(lnb), w2, row(b2))


def _ffn_kernel(x_ref, g_ref, wg_ref, wu_ref, wd_ref, p_ref, gp_ref, wpg_ref, wpp_ref, out_ref, xn_ref, acc_ref):
    f = pl.program_id(1)

    @pl.when(f == 0)
    def _():
        xn_ref[...] = _rms(x_ref[...], g_ref[...]).astype(BF16)
        acc_ref[...] = jnp.zeros(acc_ref.shape, F32)

    xn = xn_ref[...]
    a = (_silu(_dot(xn, wg_ref[...])) * _dot(xn, wu_ref[...])).astype(BF16)
    acc_ref[...] += _dot(a, wd_ref[...])

    @pl.when(f == pl.num_programs(1) - 1)
    def _():
        tm = x_ref.shape[0]
        step = min(EPILOGUE_ROWS, tm)
        for r in range(0, tm, step):
            rows = slice(r, r + step)
            out_ref[rows, :] = _ple(x_ref[rows, :] + acc_ref[rows, :], p_ref[rows, :], gp_ref[...], wpg_ref[...],
                                    wpp_ref[...])


def _ffn(x, g, wg, wu, wd, p, gp, wpg, wpp, *, tm, tf):
    n, d = x.shape
    dff = wg.shape[1]
    assert n % tm == 0 and dff % tf == 0
    row = lambda a: a.reshape(1, -1)
    full = lambda a: pl.BlockSpec(a.shape, lambda i, f: (0,) * a.ndim)
    return pl.pallas_call(
        _ffn_kernel,
        grid=(n // tm, dff // tf),
        in_specs=[pl.BlockSpec((tm, d), lambda i, f: (i, 0)), full(row(g)),
                  pl.BlockSpec((d, tf), lambda i, f: (0, f)), pl.BlockSpec((d, tf), lambda i, f: (0, f)),
                  pl.BlockSpec((tf, d), lambda i, f: (f, 0)),
                  pl.BlockSpec((tm, p.shape[1]), lambda i, f: (i, 0)), full(row(gp)), full(wpg), full(wpp)],
        out_specs=pl.BlockSpec((tm, d), lambda i, f: (i, 0)),
        out_shape=jax.ShapeDtypeStruct((n, d), F32),
        scratch_shapes=[pltpu.VMEM((tm, d), BF16), pltpu.VMEM((tm, d), F32)],
        compiler_params=_params("parallel", "arbitrary"),
        name="ffn_ple",
    )(x, row(g), wg, wu, wd, p, row(gp), wpg, wpp)


def _qkv_prompt_kernel(x_ref, g_ref, w_ref, cos_ref, sin_ref,
                       qt_ref, k_ref, v_ref, kb_ref, vt_ref, biast_ref, kmt_ref, *, tm, nh, nkv, nblk):
    t = pl.program_id(1)
    group = nh // nkv
    nq = nh * HEAD_DIM
    nk = nkv * HEAD_DIM

    @pl.when(t == 0)
    def _():
        kmt_ref[...] = jnp.zeros(kmt_ref.shape, F32)

    qkv = _dot(_rms(x_ref[0], g_ref[...]).astype(BF16), w_ref[...])
    cos = cos_ref[...]
    sin = sin_ref[...]
    lane = lax.broadcasted_iota(jnp.int32, (tm, HEAD_DIM), 1)
    scale = HEAD_DIM ** -0.5
    q_heads = [_rope_head(qkv[:, h * HEAD_DIM:(h + 1) * HEAD_DIM], cos, sin, lane) * scale for h in range(nh)]
    q = jnp.concatenate(q_heads, axis=1)
    k = jnp.concatenate([_rope_head(qkv[:, nq + h * HEAD_DIM:nq + (h + 1) * HEAD_DIM], cos, sin, lane)
                         for h in range(nkv)], axis=1)
    v = qkv[:, nq + nk:]
    for h in range(nh):
        qt_ref[0, h] = q_heads[h].T.astype(BF16)
    for kv in range(nkv):
        k_ref[0, pl.ds(kv, tm, stride=nkv), :] = k[:, kv * HEAD_DIM:(kv + 1) * HEAD_DIM]
        v_ref[0, pl.ds(kv, tm, stride=nkv), :] = v[:, kv * HEAD_DIM:(kv + 1) * HEAD_DIM]
        vt_ref[0, kv] = v[:, kv * HEAD_DIM:(kv + 1) * HEAD_DIM].T.astype(BF16)
    kb_ref[0] = k.astype(BF16)

    kmt = kmt_ref[...]
    kmt_row = lax.broadcasted_iota(jnp.int32, kmt.shape, 0)
    kmt_head = lax.broadcasted_iota(jnp.int32, kmt.shape, 1) // HEAD_DIM
    for i in range(tm // MOBA_BLOCK):
        km = jnp.mean(k[i * MOBA_BLOCK:(i + 1) * MOBA_BLOCK, :], axis=0, keepdims=True)
        km = jnp.concatenate([km[:, (h // group) * HEAD_DIM:(h // group + 1) * HEAD_DIM] for h in range(nh)], axis=1)
        n = t * (tm // MOBA_BLOCK) + i
        kmt = jnp.where(kmt_row == kmt_head * nblk + n, km, kmt)
    kmt_ref[...] = kmt

    s = _dot_t_split(q, kmt).T.reshape(nh, nblk, tm)
    n_idx = lax.broadcasted_iota(jnp.int32, s.shape, 1)
    own = (t * tm + lax.broadcasted_iota(jnp.int32, s.shape, 2)) // MOBA_BLOCK
    past = n_idx < own
    s = jnp.where(past, s, -jnp.inf)
    rank = jnp.zeros(s.shape, F32)
    for dlt in range(1, nblk):
        wrap = n_idx + dlt >= nblk
        other = jnp.concatenate([s[:, dlt:, :], s[:, :dlt, :]], axis=1)
        rank = rank + jnp.where(other > s, 1.0, 0.0) + jnp.where(wrap, jnp.where(other == s, 1.0, 0.0), 0.0)
    bias = jnp.where(past, jnp.where(rank < TOP_K_BLOCKS, 0.0, MASK_VALUE), MASK_VALUE)
    biast_ref[0] = bias.reshape(nh * nblk, tm).astype(BF16)


def _qkv_prompt(x, g, w, cos, sin, *, tm, nh, nkv):
    bsz, seq, d = x.shape
    nq, nk = nh * HEAD_DIM, nkv * HEAD_DIM
    nblk = seq // MOBA_BLOCK
    assert seq % tm == 0 and tm % MOBA_BLOCK == 0 and nh * nblk == LANES
    tok = lambda c: pl.BlockSpec((1, tm, c), lambda b, t: (b, t, 0))
    kv_rows = pl.BlockSpec((1, tm * nkv, HEAD_DIM), lambda b, t: (b, t, 0))
    full = lambda a: pl.BlockSpec(a.shape, lambda b, t: (0,) * a.ndim)
    g = g.reshape(1, -1)
    return pl.pallas_call(
        functools.partial(_qkv_prompt_kernel, tm=tm, nh=nh, nkv=nkv, nblk=nblk),
        grid=(bsz, seq // tm),
        in_specs=[tok(d), full(g), full(w),
                  pl.BlockSpec((tm, HEAD_DIM), lambda b, t: (t, 0)), pl.BlockSpec((tm, HEAD_DIM), lambda b, t: (t, 0))],
        out_specs=[pl.BlockSpec((1, nh, HEAD_DIM, tm), lambda b, t: (b, 0, 0, t)), kv_rows, kv_rows, tok(nk),
                   pl.BlockSpec((1, nkv, HEAD_DIM, tm), lambda b, t: (b, 0, 0, t)),
                   pl.BlockSpec((1, nh * nblk, tm), lambda b, t: (b, 0, t))],
        out_shape=[jax.ShapeDtypeStruct((bsz, nh, HEAD_DIM, seq), BF16),
                   jax.ShapeDtypeStruct((bsz, seq * nkv, HEAD_DIM), F32),
                   jax.ShapeDtypeStruct((bsz, seq * nkv, HEAD_DIM), F32),
                   jax.ShapeDtypeStruct((bsz, seq, nk), BF16),
                   jax.ShapeDtypeStruct((bsz, nkv, HEAD_DIM, seq), BF16),
                   jax.ShapeDtypeStruct((bsz, nh * nblk, seq), BF16)],
        scratch_shapes=[pltpu.VMEM((nh * nblk, nq), F32)],
        compiler_params=_params("parallel", "arbitrary"),
        name="qkv_prompt",
    )(x, g, w, cos, sin)


def _qkv_sample_kernel(x_ref, g_ref, w_ref, cos_ref, sin_ref, q_ref, k_ref, v_ref, *, nh, nkv):
    nb = x_ref.shape[0]
    nq = nh * HEAD_DIM
    nk = nkv * HEAD_DIM
    qkv = _dot(_rms(x_ref[...], g_ref[...]).astype(BF16), w_ref[...])
    cos = cos_ref[...]
    sin = sin_ref[...]
    lane = lax.broadcasted_iota(jnp.int32, (nb, HEAD_DIM), 1)
    scale = HEAD_DIM ** -0.5
    q_ref[...] = jnp.concatenate([_rope_head(qkv[:, h * HEAD_DIM:(h + 1) * HEAD_DIM], cos, sin, lane) * scale
                                  for h in range(nh)], axis=1)
    k_ref[...] = jnp.concatenate([_rope_head(qkv[:, nq + h * HEAD_DIM:nq + (h + 1) * HEAD_DIM], cos, sin, lane)
                                  for h in range(nkv)], axis=1)
    v_ref[...] = qkv[:, nq + nk:]


def _qkv_sample(x, g, w, cos, sin, *, nh, nkv):
    nb = x.shape[0]
    nq, nk = nh * HEAD_DIM, nkv * HEAD_DIM
    return pl.pallas_call(
        functools.partial(_qkv_sample_kernel, nh=nh, nkv=nkv),
        out_shape=[jax.ShapeDtypeStruct((nb, nq), F32), jax.ShapeDtypeStruct((nb, nk), F32),
                   jax.ShapeDtypeStruct((nb, nk), F32)],
        compiler_params=pltpu.CompilerParams(vmem_limit_bytes=VMEM_LIMIT),
        name="qkv_sample",
    )(x, g.reshape(1, -1), w, cos, sin)


def _attn_prompt_kernel(qt_ref, kb_ref, vt_ref, biast_ref, h_ref, wo_ref, out_ref, *, nh, nkv, nblk):
    o = pl.program_id(1)
    hk = pl.program_id(2)
    group = nh // nkv
    blk = MOBA_BLOCK
    bias = biast_ref[0]
    feat_head = lax.broadcasted_iota(jnp.int32, bias.shape, 0) // nblk
    q_t = jnp.concatenate([qt_ref[0, g] for g in range(group)], axis=1)
    q_aug = jnp.concatenate(
        [q_t, jnp.concatenate([jnp.where(feat_head == hk * group + g, bias, jnp.zeros_like(bias))
                               for g in range(group)], axis=1)], axis=0)
    cols = group * blk

    own0 = pl.multiple_of(o * blk, blk)
    s = _dot(kb_ref[0, pl.ds(own0, blk), :], q_t)
    kpos = lax.broadcasted_iota(jnp.int32, (blk, cols), 0)
    qpos = lax.broadcasted_iota(jnp.int32, (blk, cols), 1) % blk
    s = jnp.where(kpos <= qpos, s, MASK_VALUE)
    m = jnp.max(s, axis=0, keepdims=True)
    p = jnp.exp(s - m)
    l = jnp.sum(p, axis=0, keepdims=True)
    acc = _dot(vt_ref[0, 0, :, pl.ds(own0, blk)], p.astype(BF16))
    lane = lax.broadcasted_iota(jnp.int32, (blk, nh * nblk), 1)

    def past(j, carry):
        m, l, acc = carry
        j0 = pl.multiple_of(j * blk, blk)
        pick = jnp.where(lane % nblk == j, 1.0, 0.0).astype(BF16)
        s = _dot(jnp.concatenate([kb_ref[0, pl.ds(j0, blk), :], pick], axis=1), q_aug)
        m_new = jnp.maximum(m, jnp.max(s, axis=0, keepdims=True))
        alpha = jnp.exp(m - m_new)
        p = jnp.exp(s - m_new)
        l = alpha * l + jnp.sum(p, axis=0, keepdims=True)
        acc = alpha * acc + _dot(vt_ref[0, 0, :, pl.ds(j0, blk)], p.astype(BF16))
        return m_new, l, acc

    m, l, acc = lax.fori_loop(0, o, past, (m, l, acc))
    att = acc / l
    att = jnp.concatenate([att[:, g * blk:(g + 1) * blk].T for g in range(group)], axis=1).astype(BF16)
    proj = _dot(att, wo_ref[...])

    @pl.when(hk == 0)
    def _():
        out_ref[0] = h_ref[0] + proj

    @pl.when(hk != 0)
    def _():
        out_ref[0] += proj


def _attn_prompt(qt, kb, vt, biast, h, wo, *, nh, nkv):
    bsz, seq, d = h.shape
    nblk = seq // MOBA_BLOCK
    group = nh // nkv
    gw = group * HEAD_DIM
    return pl.pallas_call(
        functools.partial(_attn_prompt_kernel, nh=nh, nkv=nkv, nblk=nblk),
        grid=(bsz, nblk, nkv),
        in_specs=[pl.BlockSpec((1, group, HEAD_DIM, MOBA_BLOCK), lambda b, o, k: (b, k, 0, o)),
                  pl.BlockSpec((1, seq, HEAD_DIM), lambda b, o, k: (b, 0, k)),
                  pl.BlockSpec((1, 1, HEAD_DIM, seq), lambda b, o, k: (b, k, 0, 0)),
                  pl.BlockSpec((1, biast.shape[1], MOBA_BLOCK), lambda b, o, k: (b, 0, o)),
                  pl.BlockSpec((1, MOBA_BLOCK, d), lambda b, o, k: (b, o, 0)),
                  pl.BlockSpec((gw, d), lambda b, o, k: (k, 0))],
        out_specs=pl.BlockSpec((1, MOBA_BLOCK, d), lambda b, o, k: (b, o, 0)),
        out_shape=jax.ShapeDtypeStruct((bsz, seq, d), F32),
        compiler_params=_params("parallel", "parallel", "arbitrary"),
        name="attn_prompt",
    )(qt, kb, vt, biast, h, wo)


PAGES_PER_STEP = 64


def _sample_select_kernel(pt_ref, *refs, nh, nkv, ppb, nblk):
    pages = refs[:PAGES_PER_STEP]
    q_ref, idx_ref, km_ref = refs[PAGES_PER_STEP:]
    s = pl.program_id(1)
    group = nh // nkv
    bps = PAGES_PER_STEP // ppb
    psize = pages[0].shape[1] // nkv

    @pl.when(s == 0)
    def _():
        km_ref[...] = jnp.zeros(km_ref.shape, F32)

    km = km_ref[...]
    km_row = lax.broadcasted_iota(jnp.int32, km.shape, 0)
    for i in range(bps):
        tot = jnp.concatenate(
            [sum(jnp.sum(pages[i * ppb + j][0, pl.ds(kv, psize, stride=nkv), :], axis=0, keepdims=True)
                 for j in range(ppb)) for kv in range(nkv)], axis=1)
        km = jnp.where(km_row == s * bps + i, tot / (psize * ppb), km)
    km_ref[...] = km

    @pl.when(s == pl.num_programs(1) - 1)
    def _():
        km = km_ref[...]
        q = q_ref[0]
        sc = jnp.concatenate(
            [_dot_t(q[kv * group:(kv + 1) * group, :], km[:, kv * HEAD_DIM:(kv + 1) * HEAD_DIM],
                    precision=lax.Precision.HIGHEST) for kv in range(nkv)], axis=0)
        col = lax.broadcasted_iota(jnp.int32, sc.shape, 1)
        lane = lax.broadcasted_iota(jnp.int32, (nh, LANES), 1)
        out = jnp.zeros((nh, LANES), jnp.int32)
        for r in range(TOP_K_BLOCKS):
            best = jnp.max(sc, axis=-1, keepdims=True)
            pick = jnp.min(jnp.where(sc == best, col, nblk), axis=-1, keepdims=True)
            out = jnp.where(lane == r, pick, out)
            sc = jnp.where(col == pick, -jnp.inf, sc)
        idx_ref[0] = out


def _sample_select(cache_k, page_table, q, *, nh, nkv):
    npool, prow, hd = cache_k.shape
    nb, npages = page_table.shape
    ppb = MOBA_BLOCK // (prow // nkv)
    nblk = npages // ppb
    assert npages % PAGES_PER_STEP == 0 and PAGES_PER_STEP % ppb == 0 and nblk >= TOP_K_BLOCKS
    page_spec = lambda j: pl.BlockSpec((1, prow, hd), lambda b, s, pt: (pt[b * npages + s * PAGES_PER_STEP + j], 0, 0))
    grid_spec = pltpu.PrefetchScalarGridSpec(
        num_scalar_prefetch=1,
        grid=(nb, npages // PAGES_PER_STEP),
        in_specs=[page_spec(j) for j in range(PAGES_PER_STEP)]
        + [pl.BlockSpec((1, nh, HEAD_DIM), lambda b, s, pt: (b, 0, 0))],
        out_specs=pl.BlockSpec((1, nh, LANES), lambda b, s, pt: (b, 0, 0)),
        scratch_shapes=[pltpu.VMEM((nblk, nkv * hd), F32)],
    )
    return pl.pallas_call(
        functools.partial(_sample_select_kernel, nh=nh, nkv=nkv, ppb=ppb, nblk=nblk),
        grid_spec=grid_spec,
        out_shape=jax.ShapeDtypeStruct((nb, nh, LANES), jnp.int32),
        compiler_params=_params("parallel", "arbitrary"),
        name="sample_select",
    )(page_table.reshape(-1), *([cache_k] * PAGES_PER_STEP), q.reshape(nb, nh, HEAD_DIM))


def _sample_attn_kernel(pt_ref, idx_ref, q_ref, kn_ref, vn_ref, *refs, group, nkv, npg):
    kpages = refs[:group * npg]
    vpages = refs[group * npg:2 * group * npg]
    out_ref = refs[2 * group * npg]
    hkv = pl.program_id(1)
    prow = kpages[0].shape[1]
    mine = lax.broadcasted_iota(jnp.int32, (prow, 1), 0) % nkv == hkv
    outs = []
    for g in range(group):
        q = q_ref[0, g:g + 1, :]
        m = jnp.sum(q * kn_ref[0], axis=-1, keepdims=True)
        l = jnp.ones((1, 1), F32)
        acc = vn_ref[0]
        for i in range(npg):
            k = kpages[g * npg + i][0]
            sc = jnp.where(mine, jnp.sum(k * q, axis=-1, keepdims=True), MASK_VALUE)
            m_new = jnp.maximum(m, jnp.max(sc, axis=0, keepdims=True))
            alpha = jnp.exp(m - m_new)
            p = jnp.exp(sc - m_new)
            l = alpha * l + jnp.sum(p, axis=0, keepdims=True)
            acc = alpha * acc + jnp.sum(p * vpages[g * npg + i][0], axis=0, keepdims=True)
            m = m_new
        outs.append(acc / l)
    out_ref[0] = jnp.concatenate(outs, axis=0)


def _sample_attn(cache_k, cache_v, page_table, idx, q, kn, vn, *, nh, nkv):
    npool, prow, hd = cache_k.shape
    nb, npages = page_table.shape
    ppb = MOBA_BLOCK // (prow // nkv)
    group = nh // nkv
    nsel = TOP_K_BLOCKS
    npg = nsel * ppb

    def page_spec(g, i):
        def imap(b, kv, pt, ix):
            blk = ix[(b * nh + kv * group + g) * nsel + i // ppb]
            return (pt[b * npages + blk * ppb + i % ppb], 0, 0)
        return pl.BlockSpec((1, prow, hd), imap)

    qspec = pl.BlockSpec((1, group, HEAD_DIM), lambda b, kv, pt, ix: (b * nkv + kv, 0, 0))
    kvspec = pl.BlockSpec((1, 1, HEAD_DIM), lambda b, kv, pt, ix: (b * nkv + kv, 0, 0))
    pages = [page_spec(g, i) for g in range(group) for i in range(npg)]
    grid_spec = pltpu.PrefetchScalarGridSpec(
        num_scalar_prefetch=2,
        grid=(nb, nkv),
        in_specs=[qspec, kvspec, kvspec] + pages + pages,
        out_specs=qspec,
    )
    out = pl.pallas_call(
        functools.partial(_sample_attn_kernel, group=group, nkv=nkv, npg=npg),
        grid_spec=grid_spec,
        out_shape=jax.ShapeDtypeStruct((nb * nkv, group, HEAD_DIM), F32),
        compiler_params=_params("parallel", "parallel"),
        name="sample_attn",
    )(page_table.reshape(-1), idx.reshape(-1),
      q.reshape(nb * nkv, group, HEAD_DIM), kn.reshape(nb * nkv, 1, HEAD_DIM), vn.reshape(nb * nkv, 1, HEAD_DIM),
      *([cache_k] * (group * npg)), *([cache_v] * (group * npg)))
    return out.reshape(nb, nh * HEAD_DIM)


def _proj_residual_kernel(a_ref, h_ref, w_ref, out_ref):
    out_ref[...] = h_ref[...] + _dot(a_ref[...].astype(BF16), w_ref[...])


def _proj_residual(a, h, w):
    return pl.pallas_call(
        _proj_residual_kernel,
        out_shape=jax.ShapeDtypeStruct(h.shape, F32),
        compiler_params=pltpu.CompilerParams(vmem_limit_bytes=VMEM_LIMIT),
        name="proj_residual",
    )(a, h, w)


MXU_DEPTH = 256
MOE_ROW_GRAN = 32
MOE_VARIANTS = 16
MOE_SPECIALISED = (-96, 160)


def _router_kernel(x_ref, g_ref, wrt_ref, tri_ref, xn_ref, gate_ref, rank_ref, cnt_ref):
    xn = _rms(x_ref[...], g_ref[...])
    xn_ref[...] = xn.astype(BF16)
    logits = _dot_t_split(wrt_ref[...], xn)
    ne = logits.shape[0]
    e = jnp.exp(logits - jnp.max(logits, axis=0, keepdims=True))
    probs = e / jnp.sum(e, axis=0, keepdims=True)
    eid = lax.broadcasted_iota(jnp.int32, probs.shape, 0)
    rest = probs
    member = jnp.zeros(probs.shape, jnp.bool_)
    top_sum = jnp.zeros((1, probs.shape[1]), F32)
    for _ in range(TOP_K_EXPERTS):
        best = jnp.max(rest, axis=0, keepdims=True)
        pick = eid == jnp.min(jnp.where(rest == best, eid, ne), axis=0, keepdims=True)
        member = member | pick
        top_sum = top_sum + best
        rest = jnp.where(pick, -1.0, rest)
    gate_ref[...] = jnp.where(member, probs / top_sum, 0.0)
    mem = jnp.where(member, 1.0, 0.0)
    rank = _dot(mem.astype(BF16), tri_ref[...])
    rank_ref[...] = jnp.where(member, rank, -1.0)
    cnt_ref[0] = jnp.broadcast_to(jnp.sum(mem, axis=1, keepdims=True), cnt_ref.shape[1:])


def _router(x, g, wr, *, tm):
    n, d = x.shape
    ne = wr.shape[1]
    nt = n // tm
    assert n % tm == 0
    tri = (lax.broadcasted_iota(jnp.int32, (tm, tm), 0) < lax.broadcasted_iota(jnp.int32, (tm, tm), 1)).astype(BF16)
    full = lambda a: pl.BlockSpec(a.shape, lambda i: (0,) * a.ndim)
    g = g.reshape(1, -1)
    wrt = wr.T
    xn, gate, rank, cnt = pl.pallas_call(
        _router_kernel,
        grid=(nt,),
        in_specs=[pl.BlockSpec((tm, d), lambda i: (i, 0)), full(g), full(wrt), full(tri)],
        out_specs=[pl.BlockSpec((tm, d), lambda i: (i, 0)), pl.BlockSpec((ne, tm), lambda i: (0, i)),
                   pl.BlockSpec((ne, tm), lambda i: (0, i)), pl.BlockSpec((1, ne, LANES), lambda i: (i, 0, 0))],
        out_shape=[jax.ShapeDtypeStruct((n, d), BF16), jax.ShapeDtypeStruct((ne, n), F32),
                   jax.ShapeDtypeStruct((ne, n), F32), jax.ShapeDtypeStruct((nt, ne, LANES), F32)],
        compiler_params=_params("parallel"),
        name="router",
    )(x, g, wrt, tri)
    return xn, gate, rank, cnt[:, :, 0].astype(jnp.int32).reshape(-1)


def _moe_kernel(eid_ref, cnt_ref, cnts_ref, xn_ref, gate_ref, rank_ref, wg_ref, wu_ref, wd_ref, hin_ref,
                xns_ref, gates_ref, ranks_ref, hins_ref, out_ref, outs_ref, xc_ref, y_ref, *, tm, tf, ne):
    e = eid_ref[0]
    i = pl.program_id(0)
    weights = (wg_ref, wu_ref, wd_ref)
    _expert_rows(e, cnt_ref[i * ne + e], xn_ref, gate_ref, rank_ref, hin_ref, out_ref, weights, xc_ref, y_ref,
                 tm=tm, tf=tf, ne=ne)

    @pl.when(i == pl.num_programs(0) - 1)
    def _():
        _expert_rows(e, cnts_ref[e], xns_ref, gates_ref, ranks_ref, hins_ref, outs_ref, weights, xc_ref, y_ref,
                     tm=xns_ref.shape[0], tf=tf, ne=ne)


def _moe_groups(tm):
    gran = min(MOE_ROW_GRAN, tm)
    return gran, min(MOE_VARIANTS, tm // gran)


def _expert_rows(e, cnt, xn_ref, gate_ref, rank_ref, hin_ref, out_ref, weights, xc_ref, y_ref, *, tm, tf, ne):
    wg_ref, wu_ref, wd_ref = weights
    gran, nvar = _moe_groups(tm)
    ngrp = (cnt + gran - 1) // gran
    rank = rank_ref[pl.ds(e, 1), :]
    kq = min(MXU_DEPTH, tm)

    def onehot(r0, rows):
        slot = lax.broadcasted_iota(jnp.int32, (rows, tm), 0) + r0
        return slot.astype(F32) == rank

    def process(r0, rows, base_ref):
        krows = -(-rows // kq) * kq
        sel = jnp.where(onehot(r0, rows), 1.0, 0.0).astype(BF16)
        xc_ref[0:rows, :] = _dot(sel, xn_ref[...]).astype(BF16)
        y_ref[0:krows, :] = jnp.zeros((krows, y_ref.shape[1]), F32)

        def ffn(fc, carry):
            xc = xc_ref[0:rows, :]
            f0 = pl.multiple_of(fc * tf, tf)
            a = (_silu(_dot(xc, wg_ref[0, :, pl.ds(f0, tf)])) * _dot(xc, wu_ref[0, :, pl.ds(f0, tf)])).astype(BF16)
            y_ref[0:rows, :] += _dot(a, wd_ref[0, pl.ds(f0, tf), :])
            return carry
        lax.fori_loop(0, wg_ref.shape[2] // tf, ffn, 0)
        w = jnp.where(onehot(r0, krows), gate_ref[pl.ds(e, 1), :], 0.0).astype(BF16)
        out_ref[...] = base_ref[...] + lax.dot_general(w, y_ref[0:krows, :].astype(BF16), (((0,), (0,)), ((), ())),
                                                       preferred_element_type=F32)

    @pl.when(ngrp == 0)
    def _():
        out_ref[...] = hin_ref[...]

    balanced = tm * TOP_K_EXPERTS // ne
    sizes = range(max(1, (balanced + MOE_SPECIALISED[0]) // gran), min(nvar, (balanced + MOE_SPECIALISED[1]) // gran) + 1)
    for n in sizes:
        @pl.when(ngrp == n)
        def _(n=n):
            process(0, n * gran, hin_ref)

    common = functools.reduce(lambda a, n: a | (ngrp == n), sizes, ngrp == 0)

    @pl.when(jnp.logical_not(common))
    def _():
        full = nvar * gran
        out_ref[...] = hin_ref[...]

        def body(c, carry):
            process(c * full, full, out_ref)
            return carry
        lax.fori_loop(0, (cnt + full - 1) // full, body, 0)


def _moe_finish_kernel(h_ref, p_ref, gp_ref, wpg_ref, wpp_ref, gf_ref, out_ref):
    tm = h_ref.shape[0]
    step = min(EPILOGUE_ROWS, tm)
    for r in range(0, tm, step):
        rows = slice(r, r + step)
        out_ref[rows, :] = _rms(_ple(h_ref[rows, :], p_ref[rows, :], gp_ref[...], wpg_ref[...], wpp_ref[...]),
                                gf_ref[...])


def _moe(h, hs, g, wr, wg, wu, wd, p, ps, gp, wpg, wpp, gf, *, tm, tf):
    n, d = h.shape
    ns = hs.shape[0]
    ne, _, dff = wg.shape
    nt = n // tm
    for rows in (tm, ns):
        gran, nvar = _moe_groups(rows)
        assert rows % (nvar * gran) == 0 and (nvar * gran) % min(MXU_DEPTH, rows) == 0
    assert n % tm == 0 and dff % tf == 0 and ns <= tm
    xn, gate, rank, cnt = _router(h, g, wr, tm=tm)
    xns, gates, ranks, cnts = _router(hs, g, wr, tm=ns)
    resident = lambda a: pl.BlockSpec((1,) + a.shape[1:], lambda i, eid, c, cs: (eid[0], 0, 0),
                                      pipeline_mode=pl.Buffered(1))
    tile = lambda c: pl.BlockSpec((tm, c), lambda i, eid, c_, cs: (i, 0))
    lanes = pl.BlockSpec((ne, tm), lambda i, eid, c, cs: (0, i))
    whole = lambda a: pl.BlockSpec(a.shape, lambda i, eid, c, cs: (0,) * a.ndim)
    full = max(g * v for g, v in (_moe_groups(tm), _moe_groups(ns)))
    grid_spec = pltpu.PrefetchScalarGridSpec(
        num_scalar_prefetch=3,
        grid=(nt,),
        in_specs=[tile(d), lanes, lanes, resident(wg), resident(wu), resident(wd), tile(d),
                  whole(xns), whole(gates), whole(ranks), whole(hs)],
        out_specs=[tile(d), whole(hs)],
        scratch_shapes=[pltpu.VMEM((full, d), BF16), pltpu.VMEM((full, d), F32)],
    )
    one_expert = pl.pallas_call(
        functools.partial(_moe_kernel, tm=tm, tf=tf, ne=ne),
        grid_spec=grid_spec,
        out_shape=[jax.ShapeDtypeStruct((n, d), F32), jax.ShapeDtypeStruct((ns, d), F32)],
        input_output_aliases={9: 0, 13: 1},
        compiler_params=_params("arbitrary"),
        name="moe_expert",
    )
    for e in range(ne):
        h, hs = one_expert(jnp.full((1,), e, jnp.int32), cnt, cnts, xn, gate, rank, wg, wu, wd, h, xns, gates, ranks, hs)

    row = lambda a: a.reshape(1, -1)

    def finish(hm, pm, rows):
        whole1 = lambda a: pl.BlockSpec(a.shape, lambda i: (0,) * a.ndim)
        return pl.pallas_call(
            _moe_finish_kernel,
            grid=(hm.shape[0] // rows,),
            in_specs=[pl.BlockSpec((rows, d), lambda i: (i, 0)), pl.BlockSpec((rows, pm.shape[1]), lambda i: (i, 0)),
                      whole1(row(gp)), whole1(wpg), whole1(wpp), whole1(row(gf))],
            out_specs=pl.BlockSpec((rows, d), lambda i: (i, 0)),
            out_shape=jax.ShapeDtypeStruct(hm.shape, F32),
            compiler_params=_params("parallel"),
            name="moe_finish",
        )(hm, pm, row(gp), wpg, wpp, row(gf))
    return finish(h, p, tm), finish(hs, ps, ns)


def _rope_tables(pos):
    half = ROT_DIM // 2
    inv = jnp.power(ROPE_THETA, -jnp.arange(half, dtype=F32) / half)
    ang = pos.astype(F32)[:, None] * inv[None, :]
    cos, sin = jnp.cos(ang), jnp.sin(ang)
    rest = HEAD_DIM - ROT_DIM
    n = pos.shape[0]
    return (jnp.concatenate([cos, cos, jnp.ones((n, rest), F32)], axis=1),
            jnp.concatenate([-sin, sin, jnp.zeros((n, rest), F32)], axis=1))


def _tile(n, pref):
    return pref if n % pref == 0 else n


def kernel(x_prompt, x_sample, state_conv, cache_k, cache_v, page_table, p_prompt, p_sample, norm_mix, norm_ffn, norm_ple, ple_w_gate, ple_w_proj, conv_w_pw1, conv_b_pw1, conv_w_dw, conv_b_dw, conv_ln_g, conv_ln_b, conv_w_pw2, conv_b_pw2, ffn_w_gate, ffn_w_up, ffn_w_down, attn_w_qkv, attn_w_o, moe_w_router, moe_w_gate, moe_w_up, moe_w_down, norm_final):
    bsz, seq, d = x_prompt.shape
    nb, dec_seq, _ = x_sample.shape
    assert dec_seq == 1 and norm_mix.shape[0] == 2
    nh = d // HEAD_DIM
    nkv = (attn_w_qkv.shape[2] // HEAD_DIM - nh) // 2
    n_tok = bsz * seq
    dff = ffn_w_gate.shape[2]
    psize = cache_k.shape[2]
    past_len = page_table.shape[1] * psize
    bf = lambda w: w.astype(BF16)

    cw = (norm_mix[0], bf(conv_w_pw1[0]), conv_b_pw1[0], conv_w_dw[0], conv_b_dw[0], conv_ln_g[0], conv_ln_b[0],
          bf(conv_w_pw2[0]), conv_b_pw2[0])
    hp, conv_p = _conv_prompt(x_prompt, *cw, tm=_tile(seq, 512))
    hs, conv_s = _conv_sample(x_sample.reshape(nb, d), state_conv[0], *cw)
    fw = (norm_ffn[0], bf(ffn_w_gate[0]), bf(ffn_w_up[0]), bf(ffn_w_down[0]))
    pw0 = (norm_ple[0], bf(ple_w_gate[0]), bf(ple_w_proj[0]))
    tf = _tile(dff, 512)
    hp = _ffn(hp.reshape(n_tok, d), *fw, p_prompt[0].reshape(n_tok, -1), *pw0, tm=_tile(n_tok, 1024), tf=tf)
    hs = _ffn(hs, *fw, p_sample[0].reshape(nb, -1), *pw0, tm=nb, tf=tf)

    wqkv, wo = bf(attn_w_qkv[0]), bf(attn_w_o[0])
    cos_p, sin_p = _rope_tables(jnp.arange(seq, dtype=jnp.int32))
    qt, k_p, v_p, kb, vt, biast = _qkv_prompt(hp.reshape(bsz, seq, d), norm_mix[1], wqkv, cos_p, sin_p,
                                              tm=_tile(seq, 512), nh=nh, nkv=nkv)
    hp = _attn_prompt(qt, kb, vt, biast, hp.reshape(bsz, seq, d), wo, nh=nh, nkv=nkv).reshape(n_tok, d)

    cos_s, sin_s = _rope_tables(jnp.full((1,), past_len, jnp.int32))
    qs, k_s, v_s = _qkv_sample(hs, norm_mix[1], wqkv, cos_s, sin_s, nh=nh, nkv=nkv)
    ck = cache_k[0].reshape(cache_k.shape[1], psize * nkv, HEAD_DIM)
    cv = cache_v[0].reshape(cache_v.shape[1], psize * nkv, HEAD_DIM)
    idx = _sample_select(ck, page_table, qs, nh=nh, nkv=nkv)[:, :, :TOP_K_BLOCKS]
    att_s = _sample_attn(ck, cv, page_table, idx, qs, k_s, v_s, nh=nh, nkv=nkv)
    hs = _proj_residual(att_s, hs, wo)

    mw = (norm_ffn[1], moe_w_router[0], bf(moe_w_gate[0]), bf(moe_w_up[0]), bf(moe_w_down[0]))
    pw1 = (norm_ple[1], bf(ple_w_gate[1]), bf(ple_w_proj[1]), norm_final)
    yp, ys = _moe(hp, hs, *mw, p_prompt[1].reshape(n_tok, -1), p_sample[1].reshape(nb, -1), *pw1,
                  tm=_tile(n_tok, 1024), tf=tf)

    return (yp.reshape(bsz, seq, d), ys.reshape(nb, 1, d),
            conv_p[None], conv_s[None],
            k_p.reshape(1, bsz, seq, nkv, HEAD_DIM), v_p.reshape(1, bsz, seq, nkv, HEAD_DIM),
            k_s.reshape(1, nb, 1, nkv, HEAD_DIM), v_s.reshape(1, nb, 1, nkv, HEAD_DIM))
```

```python
import functools

import jax
import jax.numpy as jnp
from jax import lax
from jax.experimental import pallas as pl
from jax.experimental.pallas import tpu as pltpu

F32 = jnp.float32
BF16 = jnp.bfloat16

EPS = 1e-6
HEAD_DIM = 128
ROT_DIM = HEAD_DIM // 4
ROPE_THETA = 500000.0
MOBA_BLOCK = 256
TOP_K_BLOCKS = 3
TOP_K_EXPERTS = 2
LANES = 128
SUBLANES = 8
MASK_VALUE = -1e30
VMEM_LIMIT = 56 * 1024 * 1024
EPILOGUE_ROWS = 256


def _params(*sem):
    return pltpu.CompilerParams(dimension_semantics=sem, vmem_limit_bytes=VMEM_LIMIT)


def _rms(x, g):
    r = lax.rsqrt(jnp.mean(x * x, axis=-1, keepdims=True) + EPS)
    return x * r * g


def _dot(a, b):
    return jnp.dot(a, b, preferred_element_type=F32)


def _dot_t(a, b, precision=None):
    return lax.dot_general(a, b, (((1,), (1,)), ((), ())), preferred_element_type=F32, precision=precision)


def _dot_t_split(a, b):
    ah, bh = a.astype(BF16), b.astype(BF16)
    al, bl = (a - ah.astype(F32)).astype(BF16), (b - bh.astype(F32)).astype(BF16)
    return _dot_t(ah, bh) + _dot_t(ah, bl) + _dot_t(al, bh)


def _silu(x):
    return x * jax.nn.sigmoid(x)


def _ple(h, p, g, w_gate, w_proj):
    gate = jax.nn.sigmoid(_dot(_rms(h, g).astype(BF16), w_gate))
    return h + gate * _dot(p.astype(BF16), w_proj)


def _rope_head(x, cos, sin, lane):
    half = ROT_DIM // 2
    partner = jnp.where(lane < half, pltpu.roll(x, HEAD_DIM - half, 1), pltpu.roll(x, half, 1))
    return x * cos + partner * sin


CONV_HALO = 32
CONV_CHUNK = 64
CONV_STRIP = 128
CONV_NORM_ROWS = 16


def _conv_prompt_kernel(x_ref, g_ref, w1_ref, b1_ref, wrep_ref, bdw_ref, lng_ref, lnb_ref, w2_ref, b2_ref,
                        out_ref, state_ref, upad_ref, y_ref, cacc_ref, *, tm, width, dc):
    t = pl.program_id(1)

    @pl.when(t == 0)
    def _():
        upad_ref[...] = jnp.zeros(upad_ref.shape, F32)

    x = x_ref[0]
    uu = _dot(_rms(x, g_ref[...]).astype(BF16), w1_ref[...]) + b1_ref[...]
    upad_ref[CONV_HALO:CONV_HALO + tm, :] = uu[:, :dc] * jax.nn.sigmoid(uu[:, dc:])

    first = CONV_HALO - (width - 1)
    span = CONV_CHUNK + SUBLANES

    def chunk(i, carry):
        base = pl.multiple_of(i * CONV_CHUNK, CONV_CHUNK)
        for c0 in range(0, dc, CONV_STRIP):
            lanes = slice(c0, c0 + CONV_STRIP)
            acc = jnp.broadcast_to(bdw_ref[:, lanes], (CONV_CHUNK, CONV_STRIP))
            for r in range(SUBLANES):
                part = None
                for k in range(width):
                    a, kr = divmod(first + k, SUBLANES)
                    if kr != r:
                        continue
                    w = wrep_ref[k * SUBLANES:(k + 1) * SUBLANES, lanes]
                    term = upad_ref[pl.ds(base + a * SUBLANES, span), lanes] * jnp.concatenate(
                        [w] * (span // SUBLANES), axis=0)
                    part = term if part is None else part + term
                if part is not None:
                    acc = acc + part[r:r + CONV_CHUNK, :]
            cacc_ref[:, lanes] = acc
        for r0 in range(0, CONV_CHUNK, CONV_NORM_ROWS):
            acc = cacc_ref[r0:r0 + CONV_NORM_ROWS, :]
            mu = jnp.mean(acc, axis=-1, keepdims=True)
            d = acc - mu
            var = jnp.mean(d * d, axis=-1, keepdims=True)
            z = d * lax.rsqrt(var + EPS) * lng_ref[...] + lnb_ref[...]
            y_ref[pl.ds(base + r0, CONV_NORM_ROWS), :] = _silu(z).astype(BF16)
        return carry

    lax.fori_loop(0, tm // CONV_CHUNK, chunk, 0)
    out_ref[0] = x + _dot(y_ref[...], w2_ref[...]) + b2_ref[...]
    state_ref[0] = upad_ref[tm + first:tm + CONV_HALO, :]
    upad_ref[0:CONV_HALO, :] = upad_ref[tm:tm + CONV_HALO, :]


def _conv_prompt(x, g, w1, b1, wdw, bdw, lng, lnb, w2, b2, *, tm):
    bsz, seq, d = x.shape
    width, dc = wdw.shape
    assert seq % tm == 0 and tm % CONV_CHUNK == 0 and width - 1 <= CONV_HALO
    row = lambda a: a.reshape(1, -1)
    full = lambda a: pl.BlockSpec(a.shape, lambda b, t: (0,) * a.ndim)
    wrep = jnp.repeat(wdw, SUBLANES, axis=0)
    args = (row(g), w1, row(b1), wrep, row(bdw), row(lng), row(lnb), w2, row(b2))
    return pl.pallas_call(
        functools.partial(_conv_prompt_kernel, tm=tm, width=width, dc=dc),
        grid=(bsz, seq // tm),
        in_specs=[pl.BlockSpec((1, tm, d), lambda b, t: (b, t, 0))] + [full(a) for a in args],
        out_specs=[pl.BlockSpec((1, tm, d), lambda b, t: (b, t, 0)),
                   pl.BlockSpec((1, width - 1, dc), lambda b, t: (b, 0, 0))],
        out_shape=[jax.ShapeDtypeStruct((bsz, seq, d), F32),
                   jax.ShapeDtypeStruct((bsz, width - 1, dc), F32)],
        scratch_shapes=[pltpu.VMEM((tm + CONV_HALO + SUBLANES, dc), F32), pltpu.VMEM((tm, dc), BF16),
                        pltpu.VMEM((CONV_CHUNK, dc), F32)],
        compiler_params=_params("parallel", "arbitrary"),
        name="conv_prompt",
    )(x, *args)


def _conv_sample_kernel(x_ref, st_ref, g_ref, w1_ref, b1_ref, wdw_ref, bdw_ref, lng_ref, lnb_ref, w2_ref, b2_ref,
                        out_ref, state_ref, *, width, dc):
    nb = x_ref.shape[0]
    x = x_ref[...]
    uu = _dot(_rms(x, g_ref[...]).astype(BF16), w1_ref[...]) + b1_ref[...]
    u = uu[:, :dc] * jax.nn.sigmoid(uu[:, dc:])
    st = st_ref[...]
    acc = jnp.sum(st * wdw_ref[0:width - 1, :][None], axis=1) + u * wdw_ref[width - 1:width, :] + bdw_ref[...]
    mu = jnp.mean(acc, axis=-1, keepdims=True)
    d = acc - mu
    var = jnp.mean(d * d, axis=-1, keepdims=True)
    z = d * lax.rsqrt(var + EPS) * lng_ref[...] + lnb_ref[...]
    out_ref[...] = x + _dot(_silu(z).astype(BF16), w2_ref[...]) + b2_ref[...]
    state_ref[:, 0:width - 2, :] = st_ref[:, 1:width - 1, :]
    for b in range(nb):
        state_ref[b, width - 2:width - 1, :] = u[b:b + 1, :]


def _conv_sample(x, st, g, w1, b1, wdw, bdw, lng, lnb, w2, b2):
    nb, d = x.shape
    width, dc = wdw.shape
    row = lambda a: a.reshape(1, -1)
    return pl.pallas_call(
        functools.partial(_conv_sample_kernel, width=width, dc=dc),
        out_shape=[jax.ShapeDtypeStruct((nb, d), F32), jax.ShapeDtypeStruct((nb, width - 1, dc), F32)],
        compiler_params=pltpu.CompilerParams(vmem_limit_bytes=VMEM_LIMIT),
        name="conv_sample",
    )(x, st, row(g), w1, row(b1), wdw, row(bdw), row(lng), row(lnb), w2, row(b2))


def _ffn_kernel(x_ref, g_ref, wg_ref, wu_ref, wd_ref, p_ref, gp_ref, wpg_ref, wpp_ref, out_ref, xn_ref, acc_ref):
    f = pl.program_id(1)

    @pl.when(f == 0)
    def _():
        xn_ref[...] = _rms(x_ref[...], g_ref[...]).astype(BF16)
        acc_ref[...] = jnp.zeros(acc_ref.shape, F32)

    xn = xn_ref[...]
    a = (_silu(_dot(xn, wg_ref[...])) * _dot(xn, wu_ref[...])).astype(BF16)
    acc_ref[...] += _dot(a, wd_ref[...])

    @pl.when(f == pl.num_programs(1) - 1)
    def _():
        tm = x_ref.shape[0]
        step = min(EPILOGUE_ROWS, tm)
        for r in range(0, tm, step):
            rows = slice(r, r + step)
            out_ref[rows, :] = _ple(x_ref[rows, :] + acc_ref[rows, :], p_ref[rows, :], gp_ref[...], wpg_ref[...],
                                    wpp_ref[...])


def _ffn(x, g, wg, wu, wd, p, gp, wpg, wpp, *, tm, tf):
    n, d = x.shape
    dff = wg.shape[1]
    assert n % tm == 0 and dff % tf == 0
    row = lambda a: a.reshape(1, -1)
    full = lambda a: pl.BlockSpec(a.shape, lambda i, f: (0,) * a.ndim)
    return pl.pallas_call(
        _ffn_kernel,
        grid=(n // tm, dff // tf),
        in_specs=[pl.BlockSpec((tm, d), lambda i, f: (i, 0)), full(row(g)),
                  pl.BlockSpec((d, tf), lambda i, f: (0, f)), pl.BlockSpec((d, tf), lambda i, f: (0, f)),
                  pl.BlockSpec((tf, d), lambda i, f: (f, 0)),
                  pl.BlockSpec((tm, p.shape[1]), lambda i, f: (i, 0)), full(row(gp)), full(wpg), full(wpp)],
        out_specs=pl.BlockSpec((tm, d), lambda i, f: (i, 0)),
        out_shape=jax.ShapeDtypeStruct((n, d), F32),
        scratch_shapes=[pltpu.VMEM((tm, d), BF16), pltpu.VMEM((tm, d), F32)],
        compiler_params=_params("parallel", "arbitrary"),
        name="ffn_ple",
    )(x, row(g), wg, wu, wd, p, row(gp), wpg, wpp)


def _qkv_prompt_kernel(x_ref, g_ref, w_ref, cos_ref, sin_ref,
                       qt_ref, k_ref, v_ref, kb_ref, vt_ref, biast_ref, kmt_ref, *, tm, nh, nkv, nblk):
    t = pl.program_id(1)
    group = nh // nkv
    nq = nh * HEAD_DIM
    nk = nkv * HEAD_DIM

    @pl.when(t == 0)
    def _():
        kmt_ref[...] = jnp.zeros(kmt_ref.shape, F32)

    qkv = _dot(_rms(x_ref[0], g_ref[...]).astype(BF16), w_ref[...])
    cos = cos_ref[...]
    sin = sin_ref[...]
    lane = lax.broadcasted_iota(jnp.int32, (tm, HEAD_DIM), 1)
    scale = HEAD_DIM ** -0.5
    q_heads = [_rope_head(qkv[:, h * HEAD_DIM:(h + 1) * HEAD_DIM], cos, sin, lane) * scale for h in range(nh)]
    q = jnp.concatenate(q_heads, axis=1)
    k = jnp.concatenate([_rope_head(qkv[:, nq + h * HEAD_DIM:nq + (h + 1) * HEAD_DIM], cos, sin, lane)
                         for h in range(nkv)], axis=1)
    v = qkv[:, nq + nk:]
    for h in range(nh):
        qt_ref[0, h] = q_heads[h].T.astype(BF16)
    for kv in range(nkv):
        k_ref[0, pl.ds(kv, tm, stride=nkv), :] = k[:, kv * HEAD_DIM:(kv + 1) * HEAD_DIM]
        v_ref[0, pl.ds(kv, tm, stride=nkv), :] = v[:, kv * HEAD_DIM:(kv + 1) * HEAD_DIM]
        vt_ref[0, kv] = v[:, kv * HEAD_DIM:(kv + 1) * HEAD_DIM].T.astype(BF16)
    kb_ref[0] = k.astype(BF16)

    kmt = kmt_ref[...]
    kmt_row = lax.broadcasted_iota(jnp.int32, kmt.shape, 0)
    kmt_head = lax.broadcasted_iota(jnp.int32, kmt.shape, 1) // HEAD_DIM
    for i in range(tm // MOBA_BLOCK):
        km = jnp.mean(k[i * MOBA_BLOCK:(i + 1) * MOBA_BLOCK, :], axis=0, keepdims=True)
        km = jnp.concatenate([km[:, (h // group) * HEAD_DIM:(h // group + 1) * HEAD_DIM] for h in range(nh)], axis=1)
        n = t * (tm // MOBA_BLOCK) + i
        kmt = jnp.where(kmt_row == kmt_head * nblk + n, km, kmt)
    kmt_ref[...] = kmt

    s = _dot_t_split(q, kmt).T.reshape(nh, nblk, tm)
    n_idx = lax.broadcasted_iota(jnp.int32, s.shape, 1)
    own = (t * tm + lax.broadcasted_iota(jnp.int32, s.shape, 2)) // MOBA_BLOCK
    past = n_idx < own
    s = jnp.where(past, s, -jnp.inf)
    rank = jnp.zeros(s.shape, F32)
    for dlt in range(1, nblk):
        wrap = n_idx + dlt >= nblk
        other = jnp.concatenate([s[:, dlt:, :], s[:, :dlt, :]], axis=1)
        rank = rank + jnp.where(other > s, 1.0, 0.0) + jnp.where(wrap, jnp.where(other == s, 1.0, 0.0), 0.0)
    bias = jnp.where(past, jnp.where(rank < TOP_K_BLOCKS, 0.0, MASK_VALUE), MASK_VALUE)
    biast_ref[0] = bias.reshape(nh * nblk, tm).astype(BF16)


def _qkv_prompt(x, g, w, cos, sin, *, tm, nh, nkv):
    bsz, seq, d = x.shape
    nq, nk = nh * HEAD_DIM, nkv * HEAD_DIM
    nblk = seq // MOBA_BLOCK
    assert seq % tm == 0 and tm % MOBA_BLOCK == 0 and nh * nblk == LANES
    tok = lambda c: pl.BlockSpec((1, tm, c), lambda b, t: (b, t, 0))
    kv_rows = pl.BlockSpec((1, tm * nkv, HEAD_DIM), lambda b, t: (b, t, 0))
    full = lambda a: pl.BlockSpec(a.shape, lambda b, t: (0,) * a.ndim)
    g = g.reshape(1, -1)
    return pl.pallas_call(
        functools.partial(_qkv_prompt_kernel, tm=tm, nh=nh, nkv=nkv, nblk=nblk),
        grid=(bsz, seq // tm),
        in_specs=[tok(d), full(g), full(w),
                  pl.BlockSpec((tm, HEAD_DIM), lambda b, t: (t, 0)), pl.BlockSpec((tm, HEAD_DIM), lambda b, t: (t, 0))],
        out_specs=[pl.BlockSpec((1, nh, HEAD_DIM, tm), lambda b, t: (b, 0, 0, t)), kv_rows, kv_rows, tok(nk),
                   pl.BlockSpec((1, nkv, HEAD_DIM, tm), lambda b, t: (b, 0, 0, t)),
                   pl.BlockSpec((1, nh * nblk, tm), lambda b, t: (b, 0, t))],
        out_shape=[jax.ShapeDtypeStruct((bsz, nh, HEAD_DIM, seq), BF16),
                   jax.ShapeDtypeStruct((bsz, seq * nkv, HEAD_DIM), F32),
                   jax.ShapeDtypeStruct((bsz, seq * nkv, HEAD_DIM), F32),
                   jax.ShapeDtypeStruct((bsz, seq, nk), BF16),
                   jax.ShapeDtypeStruct((bsz, nkv, HEAD_DIM, seq), BF16),
                   jax.ShapeDtypeStruct((bsz, nh * nblk, seq), BF16)],
        scratch_shapes=[pltpu.VMEM((nh * nblk, nq), F32)],
        compiler_params=_params("parallel", "arbitrary"),
        name="qkv_prompt",
    )(x, g, w, cos, sin)


def _qkv_sample_kernel(x_ref, g_ref, w_ref, cos_ref, sin_ref, q_ref, k_ref, v_ref, *, nh, nkv):
    nb = x_ref.shape[0]
    nq = nh * HEAD_DIM
    nk = nkv * HEAD_DIM
    qkv = _dot(_rms(x_ref[...], g_ref[...]).astype(BF16), w_ref[...])
    cos = cos_ref[...]
    sin = sin_ref[...]
    lane = lax.broadcasted_iota(jnp.int32, (nb, HEAD_DIM), 1)
    scale = HEAD_DIM ** -0.5
    q_ref[...] = jnp.concatenate([_rope_head(qkv[:, h * HEAD_DIM:(h + 1) * HEAD_DIM], cos, sin, lane) * scale
                                  for h in range(nh)], axis=1)
    k_ref[...] = jnp.concatenate([_rope_head(qkv[:, nq + h * HEAD_DIM:nq + (h + 1) * HEAD_DIM], cos, sin, lane)
                                  for h in range(nkv)], axis=1)
    v_ref[...] = qkv[:, nq + nk:]


def _qkv_sample(x, g, w, cos, sin, *, nh, nkv):
    nb = x.shape[0]
    nq, nk = nh * HEAD_DIM, nkv * HEAD_DIM
    return pl.pallas_call(
        functools.partial(_qkv_sample_kernel, nh=nh, nkv=nkv),
        out_shape=[jax.ShapeDtypeStruct((nb, nq), F32), jax.ShapeDtypeStruct((nb, nk), F32),
                   jax.ShapeDtypeStruct((nb, nk), F32)],
        compiler_params=pltpu.CompilerParams(vmem_limit_bytes=VMEM_LIMIT),
        name="qkv_sample",
    )(x, g.reshape(1, -1), w, cos, sin)


ATTN_BLOCKS_PER_PASS = (4, 2, 1)


def _attn_prompt_kernel(qt_ref, kb_ref, vt_ref, biast_ref, h_ref, wo_ref, out_ref, *, nh, nkv, nblk):
    o = pl.program_id(1)
    hk = pl.program_id(2)
    group = nh // nkv
    blk = MOBA_BLOCK
    bias = biast_ref[0]
    feat_head = lax.broadcasted_iota(jnp.int32, bias.shape, 0) // nblk
    q_t = jnp.concatenate([qt_ref[0, g] for g in range(group)], axis=1)
    q_aug = jnp.concatenate(
        [q_t, jnp.concatenate([jnp.where(feat_head == hk * group + g, bias, jnp.zeros_like(bias))
                               for g in range(group)], axis=1)], axis=0)
    cols = group * blk

    own0 = pl.multiple_of(o * blk, blk)
    s = _dot(kb_ref[0, pl.ds(own0, blk), :], q_t)
    kpos = lax.broadcasted_iota(jnp.int32, (blk, cols), 0)
    qpos = lax.broadcasted_iota(jnp.int32, (blk, cols), 1) % blk
    s = jnp.where(kpos <= qpos, s, MASK_VALUE)
    m = jnp.max(s, axis=0, keepdims=True)
    p = jnp.exp(s - m)
    l = jnp.sum(p, axis=0, keepdims=True)
    acc = _dot(vt_ref[0, 0, :, pl.ds(own0, blk)], p.astype(BF16))
    def past(first, nb_, carry):
        m, l, acc = carry
        keys = nb_ * blk
        j0 = pl.multiple_of(first * blk, keys)
        lane = lax.broadcasted_iota(jnp.int32, (keys, nh * nblk), 1)
        key_blk = first + lax.broadcasted_iota(jnp.int32, (keys, nh * nblk), 0) // blk
        pick = jnp.where(lane % nblk == key_blk, 1.0, 0.0).astype(BF16)
        s = _dot(jnp.concatenate([kb_ref[0, pl.ds(j0, keys), :], pick], axis=1), q_aug)
        m_new = jnp.maximum(m, jnp.max(s, axis=0, keepdims=True))
        alpha = jnp.exp(m - m_new)
        p = jnp.exp(s - m_new)
        l = alpha * l + jnp.sum(p, axis=0, keepdims=True)
        acc = alpha * acc + _dot(vt_ref[0, 0, :, pl.ds(j0, keys)], p.astype(BF16))
        return m_new, l, acc

    state, done = (m, l, acc), 0
    for nb_ in ATTN_BLOCKS_PER_PASS:
        cnt = (o - done) // nb_
        state = lax.fori_loop(0, cnt, lambda i, c, nb_=nb_, done=done: past(done + i * nb_, nb_, c), state)
        done = done + cnt * nb_
    m, l, acc = state
    att = acc / l
    att = jnp.concatenate([att[:, g * blk:(g + 1) * blk].T for g in range(group)], axis=1).astype(BF16)
    proj = _dot(att, wo_ref[...])

    @pl.when(hk == 0)
    def _():
        out_ref[0] = h_ref[0] + proj

    @pl.when(hk != 0)
    def _():
        out_ref[0] += proj


def _attn_prompt(qt, kb, vt, biast, h, wo, *, nh, nkv):
    bsz, seq, d = h.shape
    nblk = seq // MOBA_BLOCK
    group = nh // nkv
    gw = group * HEAD_DIM
    return pl.pallas_call(
        functools.partial(_attn_prompt_kernel, nh=nh, nkv=nkv, nblk=nblk),
        grid=(bsz, nblk, nkv),
        in_specs=[pl.BlockSpec((1, group, HEAD_DIM, MOBA_BLOCK), lambda b, o, k: (b, k, 0, o)),
                  pl.BlockSpec((1, seq, HEAD_DIM), lambda b, o, k: (b, 0, k)),
                  pl.BlockSpec((1, 1, HEAD_DIM, seq), lambda b, o, k: (b, k, 0, 0)),
                  pl.BlockSpec((1, biast.shape[1], MOBA_BLOCK), lambda b, o, k: (b, 0, o)),
                  pl.BlockSpec((1, MOBA_BLOCK, d), lambda b, o, k: (b, o, 0)),
                  pl.BlockSpec((gw, d), lambda b, o, k: (k, 0))],
        out_specs=pl.BlockSpec((1, MOBA_BLOCK, d), lambda b, o, k: (b, o, 0)),
        out_shape=jax.ShapeDtypeStruct((bsz, seq, d), F32),
        compiler_params=_params("parallel", "parallel", "arbitrary"),
        name="attn_prompt",
    )(qt, kb, vt, biast, h, wo)


PAGES_PER_STEP = 64


def _sample_select_kernel(pt_ref, *refs, nh, nkv, ppb, nblk):
    pages = refs[:PAGES_PER_STEP]
    q_ref, idx_ref, km_ref = refs[PAGES_PER_STEP:]
    s = pl.program_id(1)
    group = nh // nkv
    bps = PAGES_PER_STEP // ppb
    psize = pages[0].shape[1] // nkv

    @pl.when(s == 0)
    def _():
        km_ref[...] = jnp.zeros(km_ref.shape, F32)

    km = km_ref[...]
    km_row = lax.broadcasted_iota(jnp.int32, km.shape, 0)
    for i in range(bps):
        tot = jnp.concatenate(
            [sum(jnp.sum(pages[i * ppb + j][0, pl.ds(kv, psize, stride=nkv), :], axis=0, keepdims=True)
                 for j in range(ppb)) for kv in range(nkv)], axis=1)
        km = jnp.where(km_row == s * bps + i, tot / (psize * ppb), km)
    km_ref[...] = km

    @pl.when(s == pl.num_programs(1) - 1)
    def _():
        km = km_ref[...]
        q = q_ref[0]
        sc = jnp.concatenate(
            [_dot_t(q[kv * group:(kv + 1) * group, :], km[:, kv * HEAD_DIM:(kv + 1) * HEAD_DIM],
                    precision=lax.Precision.HIGHEST) for kv in range(nkv)], axis=0)
        col = lax.broadcasted_iota(jnp.int32, sc.shape, 1)
        lane = lax.broadcasted_iota(jnp.int32, (nh, LANES), 1)
        out = jnp.zeros((nh, LANES), jnp.int32)
        for r in range(TOP_K_BLOCKS):
            best = jnp.max(sc, axis=-1, keepdims=True)
            pick = jnp.min(jnp.where(sc == best, col, nblk), axis=-1, keepdims=True)
            out = jnp.where(lane == r, pick, out)
            sc = jnp.where(col == pick, -jnp.inf, sc)
        idx_ref[0] = out


def _sample_select(cache_k, page_table, q, *, nh, nkv):
    npool, prow, hd = cache_k.shape
    nb, npages = page_table.shape
    ppb = MOBA_BLOCK // (prow // nkv)
    nblk = npages // ppb
    assert npages % PAGES_PER_STEP == 0 and PAGES_PER_STEP % ppb == 0 and nblk >= TOP_K_BLOCKS
    page_spec = lambda j: pl.BlockSpec((1, prow, hd), lambda b, s, pt: (pt[b * npages + s * PAGES_PER_STEP + j], 0, 0))
    grid_spec = pltpu.PrefetchScalarGridSpec(
        num_scalar_prefetch=1,
        grid=(nb, npages // PAGES_PER_STEP),
        in_specs=[page_spec(j) for j in range(PAGES_PER_STEP)]
        + [pl.BlockSpec((1, nh, HEAD_DIM), lambda b, s, pt: (b, 0, 0))],
        out_specs=pl.BlockSpec((1, nh, LANES), lambda b, s, pt: (b, 0, 0)),
        scratch_shapes=[pltpu.VMEM((nblk, nkv * hd), F32)],
    )
    return pl.pallas_call(
        functools.partial(_sample_select_kernel, nh=nh, nkv=nkv, ppb=ppb, nblk=nblk),
        grid_spec=grid_spec,
        out_shape=jax.ShapeDtypeStruct((nb, nh, LANES), jnp.int32),
        compiler_params=_params("parallel", "arbitrary"),
        name="sample_select",
    )(page_table.reshape(-1), *([cache_k] * PAGES_PER_STEP), q.reshape(nb, nh, HEAD_DIM))


def _sample_attn_kernel(pt_ref, idx_ref, q_ref, kn_ref, vn_ref, *refs, group, nkv, npg):
    kpages = refs[:group * npg]
    vpages = refs[group * npg:2 * group * npg]
    out_ref = refs[2 * group * npg]
    hkv = pl.program_id(1)
    prow = kpages[0].shape[1]
    mine = lax.broadcasted_iota(jnp.int32, (prow, 1), 0) % nkv == hkv
    outs = []
    for g in range(group):
        q = q_ref[0, g:g + 1, :]
        m = jnp.sum(q * kn_ref[0], axis=-1, keepdims=True)
        l = jnp.ones((1, 1), F32)
        acc = vn_ref[0]
        for i in range(npg):
            k = kpages[g * npg + i][0]
            sc = jnp.where(mine, jnp.sum(k * q, axis=-1, keepdims=True), MASK_VALUE)
            m_new = jnp.maximum(m, jnp.max(sc, axis=0, keepdims=True))
            alpha = jnp.exp(m - m_new)
            p = jnp.exp(sc - m_new)
            l = alpha * l + jnp.sum(p, axis=0, keepdims=True)
            acc = alpha * acc + jnp.sum(p * vpages[g * npg + i][0], axis=0, keepdims=True)
            m = m_new
        outs.append(acc / l)
    out_ref[0] = jnp.concatenate(outs, axis=0)


def _sample_attn(cache_k, cache_v, page_table, idx, q, kn, vn, *, nh, nkv):
    npool, prow, hd = cache_k.shape
    nb, npages = page_table.shape
    ppb = MOBA_BLOCK // (prow // nkv)
    group = nh // nkv
    nsel = TOP_K_BLOCKS
    npg = nsel * ppb

    def page_spec(g, i):
        def imap(b, kv, pt, ix):
            blk = ix[(b * nh + kv * group + g) * nsel + i // ppb]
            return (pt[b * npages + blk * ppb + i % ppb], 0, 0)
        return pl.BlockSpec((1, prow, hd), imap)

    qspec = pl.BlockSpec((1, group, HEAD_DIM), lambda b, kv, pt, ix: (b * nkv + kv, 0, 0))
    kvspec = pl.BlockSpec((1, 1, HEAD_DIM), lambda b, kv, pt, ix: (b * nkv + kv, 0, 0))
    pages = [page_spec(g, i) for g in range(group) for i in range(npg)]
    grid_spec = pltpu.PrefetchScalarGridSpec(
        num_scalar_prefetch=2,
        grid=(nb, nkv),
        in_specs=[qspec, kvspec, kvspec] + pages + pages,
        out_specs=qspec,
    )
    out = pl.pallas_call(
        functools.partial(_sample_attn_kernel, group=group, nkv=nkv, npg=npg),
        grid_spec=grid_spec,
        out_shape=jax.ShapeDtypeStruct((nb * nkv, group, HEAD_DIM), F32),
        compiler_params=_params("parallel", "parallel"),
        name="sample_attn",
    )(page_table.reshape(-1), idx.reshape(-1),
      q.reshape(nb * nkv, group, HEAD_DIM), kn.reshape(nb * nkv, 1, HEAD_DIM), vn.reshape(nb * nkv, 1, HEAD_DIM),
      *([cache_k] * (group * npg)), *([cache_v] * (group * npg)))
    return out.reshape(nb, nh * HEAD_DIM)


def _proj_residual_kernel(a_ref, h_ref, w_ref, out_ref):
    out_ref[...] = h_ref[...] + _dot(a_ref[...].astype(BF16), w_ref[...])


def _proj_residual(a, h, w):
    return pl.pallas_call(
        _proj_residual_kernel,
        out_shape=jax.ShapeDtypeStruct(h.shape, F32),
        compiler_params=pltpu.CompilerParams(vmem_limit_bytes=VMEM_LIMIT),
        name="proj_residual",
    )(a, h, w)


MXU_DEPTH = 256
MOE_ROW_GRAN = 32
MOE_VARIANTS = 16
MOE_SPECIALISED = (-96, 160)


def _router_kernel(x_ref, g_ref, wrt_ref, tri_ref, xn_ref, gate_ref, rank_ref, cnt_ref):
    xn = _rms(x_ref[...], g_ref[...])
    xn_ref[...] = xn.astype(BF16)
    logits = _dot_t_split(wrt_ref[...], xn)
    ne = logits.shape[0]
    e = jnp.exp(logits - jnp.max(logits, axis=0, keepdims=True))
    probs = e / jnp.sum(e, axis=0, keepdims=True)
    eid = lax.broadcasted_iota(jnp.int32, probs.shape, 0)
    rest = probs
    member = jnp.zeros(probs.shape, jnp.bool_)
    top_sum = jnp.zeros((1, probs.shape[1]), F32)
    for _ in range(TOP_K_EXPERTS):
        best = jnp.max(rest, axis=0, keepdims=True)
        pick = eid == jnp.min(jnp.where(rest == best, eid, ne), axis=0, keepdims=True)
        member = member | pick
        top_sum = top_sum + best
        rest = jnp.where(pick, -1.0, rest)
    gate_ref[...] = jnp.where(member, probs / top_sum, 0.0)
    mem = jnp.where(member, 1.0, 0.0)
    rank = _dot(mem.astype(BF16), tri_ref[...])
    rank_ref[...] = jnp.where(member, rank, -1.0)
    cnt_ref[0] = jnp.broadcast_to(jnp.sum(mem, axis=1, keepdims=True), cnt_ref.shape[1:])


def _router(x, g, wr, *, tm):
    n, d = x.shape
    ne = wr.shape[1]
    nt = n // tm
    assert n % tm == 0
    tri = (lax.broadcasted_iota(jnp.int32, (tm, tm), 0) < lax.broadcasted_iota(jnp.int32, (tm, tm), 1)).astype(BF16)
    full = lambda a: pl.BlockSpec(a.shape, lambda i: (0,) * a.ndim)
    g = g.reshape(1, -1)
    wrt = wr.T
    xn, gate, rank, cnt = pl.pallas_call(
        _router_kernel,
        grid=(nt,),
        in_specs=[pl.BlockSpec((tm, d), lambda i: (i, 0)), full(g), full(wrt), full(tri)],
        out_specs=[pl.BlockSpec((tm, d), lambda i: (i, 0)), pl.BlockSpec((ne, tm), lambda i: (0, i)),
                   pl.BlockSpec((ne, tm), lambda i: (0, i)), pl.BlockSpec((1, ne, LANES), lambda i: (i, 0, 0))],
        out_shape=[jax.ShapeDtypeStruct((n, d), BF16), jax.ShapeDtypeStruct((ne, n), F32),
                   jax.ShapeDtypeStruct((ne, n), F32), jax.ShapeDtypeStruct((nt, ne, LANES), F32)],
        compiler_params=_params("parallel"),
        name="router",
    )(x, g, wrt, tri)
    return xn, gate, rank, cnt[:, :, 0].astype(jnp.int32).reshape(-1)


def _moe_kernel(eid_ref, cnt_ref, cnts_ref, xn_ref, gate_ref, rank_ref, wg_ref, wu_ref, wd_ref, hin_ref,
                xns_ref, gates_ref, ranks_ref, hins_ref, out_ref, outs_ref, xc_ref, y_ref, *, tm, tf, ne):
    e = eid_ref[0]
    i = pl.program_id(0)
    weights = (wg_ref, wu_ref, wd_ref)
    _expert_rows(e, cnt_ref[i * ne + e], xn_ref, gate_ref, rank_ref, hin_ref, out_ref, weights, xc_ref, y_ref,
                 tm=tm, tf=tf, ne=ne)

    @pl.when(i == pl.num_programs(0) - 1)
    def _():
        _expert_rows(e, cnts_ref[e], xns_ref, gates_ref, ranks_ref, hins_ref, outs_ref, weights, xc_ref, y_ref,
                     tm=xns_ref.shape[0], tf=tf, ne=ne)


def _moe_groups(tm):
    gran = min(MOE_ROW_GRAN, tm)
    return gran, min(MOE_VARIANTS, tm // gran)


def _expert_rows(e, cnt, xn_ref, gate_ref, rank_ref, hin_ref, out_ref, weights, xc_ref, y_ref, *, tm, tf, ne):
    wg_ref, wu_ref, wd_ref = weights
    gran, nvar = _moe_groups(tm)
    ngrp = (cnt + gran - 1) // gran
    rank = rank_ref[pl.ds(e, 1), :]
    kq = min(MXU_DEPTH, tm)

    def onehot(r0, rows):
        slot = lax.broadcasted_iota(jnp.int32, (rows, tm), 0) + r0
        return slot.astype(F32) == rank

    def process(r0, rows, base_ref):
        krows = -(-rows // kq) * kq
        sel = jnp.where(onehot(r0, rows), 1.0, 0.0).astype(BF16)
        xc_ref[0:rows, :] = _dot(sel, xn_ref[...]).astype(BF16)
        y_ref[0:krows, :] = jnp.zeros((krows, y_ref.shape[1]), F32)

        def ffn(fc, carry):
            xc = xc_ref[0:rows, :]
            f0 = pl.multiple_of(fc * tf, tf)
            a = (_silu(_dot(xc, wg_ref[0, :, pl.ds(f0, tf)])) * _dot(xc, wu_ref[0, :, pl.ds(f0, tf)])).astype(BF16)
            y_ref[0:rows, :] += _dot(a, wd_ref[0, pl.ds(f0, tf), :])
            return carry
        lax.fori_loop(0, wg_ref.shape[2] // tf, ffn, 0)
        w = jnp.where(onehot(r0, krows), gate_ref[pl.ds(e, 1), :], 0.0).astype(BF16)
        out_ref[...] = base_ref[...] + lax.dot_general(w, y_ref[0:krows, :].astype(BF16), (((0,), (0,)), ((), ())),
                                                       preferred_element_type=F32)

    @pl.when(ngrp == 0)
    def _():
        out_ref[...] = hin_ref[...]

    balanced = tm * TOP_K_EXPERTS // ne
    sizes = range(max(1, (balanced + MOE_SPECIALISED[0]) // gran), min(nvar, (balanced + MOE_SPECIALISED[1]) // gran) + 1)
    for n in sizes:
        @pl.when(ngrp == n)
        def _(n=n):
            process(0, n * gran, hin_ref)

    common = functools.reduce(lambda a, n: a | (ngrp == n), sizes, ngrp == 0)

    @pl.when(jnp.logical_not(common))
    def _():
        full = nvar * gran
        out_ref[...] = hin_ref[...]

        def body(c, carry):
            process(c * full, full, out_ref)
            return carry
        lax.fori_loop(0, (cnt + full - 1) // full, body, 0)


def _moe_finish_kernel(h_ref, p_ref, gp_ref, wpg_ref, wpp_ref, gf_ref, out_ref):
    tm = h_ref.shape[0]
    step = min(EPILOGUE_ROWS, tm)
    for r in range(0, tm, step):
        rows = slice(r, r + step)
        out_ref[rows, :] = _rms(_ple(h_ref[rows, :], p_ref[rows, :], gp_ref[...], wpg_ref[...], wpp_ref[...]),
                                gf_ref[...])


def _moe(h, hs, g, wr, wg, wu, wd, p, ps, gp, wpg, wpp, gf, *, tm, tf):
    n, d = h.shape
    ns = hs.shape[0]
    ne, _, dff = wg.shape
    nt = n // tm
    for rows in (tm, ns):
        gran, nvar = _moe_groups(rows)
        assert rows % (nvar * gran) == 0 and (nvar * gran) % min(MXU_DEPTH, rows) == 0
    assert n % tm == 0 and dff % tf == 0 and ns <= tm
    xn, gate, rank, cnt = _router(h, g, wr, tm=tm)
    xns, gates, ranks, cnts = _router(hs, g, wr, tm=ns)
    resident = lambda a: pl.BlockSpec((1,) + a.shape[1:], lambda i, eid, c, cs: (eid[0], 0, 0),
                                      pipeline_mode=pl.Buffered(1))
    tile = lambda c: pl.BlockSpec((tm, c), lambda i, eid, c_, cs: (i, 0))
    lanes = pl.BlockSpec((ne, tm), lambda i, eid, c, cs: (0, i))
    whole = lambda a: pl.BlockSpec(a.shape, lambda i, eid, c, cs: (0,) * a.ndim)
    full = max(g * v for g, v in (_moe_groups(tm), _moe_groups(ns)))
    grid_spec = pltpu.PrefetchScalarGridSpec(
        num_scalar_prefetch=3,
        grid=(nt,),
        in_specs=[tile(d), lanes, lanes, resident(wg), resident(wu), resident(wd), tile(d),
                  whole(xns), whole(gates), whole(ranks), whole(hs)],
        out_specs=[tile(d), whole(hs)],
        scratch_shapes=[pltpu.VMEM((full, d), BF16), pltpu.VMEM((full, d), F32)],
    )
    one_expert = pl.pallas_call(
        functools.partial(_moe_kernel, tm=tm, tf=tf, ne=ne),
        grid_spec=grid_spec,
        out_shape=[jax.ShapeDtypeStruct((n, d), F32), jax.ShapeDtypeStruct((ns, d), F32)],
        input_output_aliases={9: 0, 13: 1},
        compiler_params=_params("arbitrary"),
        name="moe_expert",
    )
    for e in range(ne):
        h, hs = one_expert(jnp.full((1,), e, jnp.int32), cnt, cnts, xn, gate, rank, wg, wu, wd, h, xns, gates, ranks, hs)

    row = lambda a: a.reshape(1, -1)

    def finish(hm, pm, rows):
        whole1 = lambda a: pl.BlockSpec(a.shape, lambda i: (0,) * a.ndim)
        return pl.pallas_call(
            _moe_finish_kernel,
            grid=(hm.shape[0] // rows,),
            in_specs=[pl.BlockSpec((rows, d), lambda i: (i, 0)), pl.BlockSpec((rows, pm.shape[1]), lambda i: (i, 0)),
                      whole1(row(gp)), whole1(wpg), whole1(wpp), whole1(row(gf))],
            out_specs=pl.BlockSpec((rows, d), lambda i: (i, 0)),
            out_shape=jax.ShapeDtypeStruct(hm.shape, F32),
            compiler_params=_params("parallel"),
            name="moe_finish",
        )(hm, pm, row(gp), wpg, wpp, row(gf))
    return finish(h, p, tm), finish(hs, ps, ns)


def _rope_tables(pos):
    half = ROT_DIM // 2
    inv = jnp.power(ROPE_THETA, -jnp.arange(half, dtype=F32) / half)
    ang = pos.astype(F32)[:, None] * inv[None, :]
    cos, sin = jnp.cos(ang), jnp.sin(ang)
    rest = HEAD_DIM - ROT_DIM
    n = pos.shape[0]
    return (jnp.concatenate([cos, cos, jnp.ones((n, rest), F32)], axis=1),
            jnp.concatenate([-sin, sin, jnp.zeros((n, rest), F32)], axis=1))


def _tile(n, pref):
    return pref if n % pref == 0 else n


def kernel(x_prompt, x_sample, state_conv, cache_k, cache_v, page_table, p_prompt, p_sample, norm_mix, norm_ffn, norm_ple, ple_w_gate, ple_w_proj, conv_w_pw1, conv_b_pw1, conv_w_dw, conv_b_dw, conv_ln_g, conv_ln_b, conv_w_pw2, conv_b_pw2, ffn_w_gate, ffn_w_up, ffn_w_down, attn_w_qkv, attn_w_o, moe_w_router, moe_w_gate, moe_w_up, moe_w_down, norm_final):
    bsz, seq, d = x_prompt.shape
    nb, dec_seq, _ = x_sample.shape
    assert dec_seq == 1 and norm_mix.shape[0] == 2
    nh = d // HEAD_DIM
    nkv = (attn_w_qkv.shape[2] // HEAD_DIM - nh) // 2
    n_tok = bsz * seq
    dff = ffn_w_gate.shape[2]
    psize = cache_k.shape[2]
    past_len = page_table.shape[1] * psize
    bf = lambda w: w.astype(BF16)

    cw = (norm_mix[0], bf(conv_w_pw1[0]), conv_b_pw1[0], conv_w_dw[0], conv_b_dw[0], conv_ln_g[0], conv_ln_b[0],
          bf(conv_w_pw2[0]), conv_b_pw2[0])
    hp, conv_p = _conv_prompt(x_prompt, *cw, tm=_tile(seq, 1024))
    hs, conv_s = _conv_sample(x_sample.reshape(nb, d), state_conv[0], *cw)
    fw = (norm_ffn[0], bf(ffn_w_gate[0]), bf(ffn_w_up[0]), bf(ffn_w_down[0]))
    pw0 = (norm_ple[0], bf(ple_w_gate[0]), bf(ple_w_proj[0]))
    tf = _tile(dff, 512)
    hp = _ffn(hp.reshape(n_tok, d), *fw, p_prompt[0].reshape(n_tok, -1), *pw0, tm=_tile(n_tok, 1024), tf=tf)
    hs = _ffn(hs, *fw, p_sample[0].reshape(nb, -1), *pw0, tm=nb, tf=tf)

    wqkv, wo = bf(attn_w_qkv[0]), bf(attn_w_o[0])
    cos_p, sin_p = _rope_tables(jnp.arange(seq, dtype=jnp.int32))
    qt, k_p, v_p, kb, vt, biast = _qkv_prompt(hp.reshape(bsz, seq, d), norm_mix[1], wqkv, cos_p, sin_p,
                                              tm=_tile(seq, 1024), nh=nh, nkv=nkv)
    hp = _attn_prompt(qt, kb, vt, biast, hp.reshape(bsz, seq, d), wo, nh=nh, nkv=nkv).reshape(n_tok, d)

    cos_s, sin_s = _rope_tables(jnp.full((1,), past_len, jnp.int32))
    qs, k_s, v_s = _qkv_sample(hs, norm_mix[1], wqkv, cos_s, sin_s, nh=nh, nkv=nkv)
    ck = cache_k[0].reshape(cache_k.shape[1], psize * nkv, HEAD_DIM)
    cv = cache_v[0].reshape(cache_v.shape[1], psize * nkv, HEAD_DIM)
    idx = _sample_select(ck, page_table, qs, nh=nh, nkv=nkv)[:, :, :TOP_K_BLOCKS]
    att_s = _sample_attn(ck, cv, page_table, idx, qs, k_s, v_s, nh=nh, nkv=nkv)
    hs = _proj_residual(att_s, hs, wo)

    mw = (norm_ffn[1], moe_w_router[0], bf(moe_w_gate[0]), bf(moe_w_up[0]), bf(moe_w_down[0]))
    pw1 = (norm_ple[1], bf(ple_w_gate[1]), bf(ple_w_proj[1]), norm_final)
    yp, ys = _moe(hp, hs, *mw, p_prompt[1].reshape(n_tok, -1), p_sample[1].reshape(nb, -1), *pw1,
                  tm=_tile(n_tok, 1024), tf=tf)

    return (yp.reshape(bsz, seq, d), ys.reshape(nb, 1, d),
            conv_p[None], conv_s[None],
            k_p.reshape(1, bsz, seq, nkv, HEAD_DIM), v_p.reshape(1, bsz, seq, nkv, HEAD_DIM),
            k_s.reshape(1, nb, 1, nkv, HEAD_DIM), v_s.reshape(1, nb, 1, nkv, HEAD_DIM))
```

```python
import functools

import jax
import jax.numpy as jnp
from jax import lax
from jax.experimental import pallas as pl
from jax.experimental.pallas import tpu as pltpu

F32 = jnp.float32
BF16 = jnp.bfloat16

EPS = 1e-6
HEAD_DIM = 128
ROT_DIM = HEAD_DIM // 4
ROPE_THETA = 500000.0
MOBA_BLOCK = 256
TOP_K_BLOCKS = 3
TOP_K_EXPERTS = 2
LANES = 128
SUBLANES = 8
MASK_VALUE = -1e30
VMEM_LIMIT = 56 * 1024 * 1024
EPILOGUE_ROWS = 256


def _params(*sem):
    return pltpu.CompilerParams(dimension_semantics=sem, vmem_limit_bytes=VMEM_LIMIT)


def _rms(x, g):
    r = lax.rsqrt(jnp.mean(x * x, axis=-1, keepdims=True) + EPS)
    return x * r * g


def _dot(a, b):
    return jnp.dot(a, b, preferred_element_type=F32)


def _dot_t(a, b, precision=None):
    return lax.dot_general(a, b, (((1,), (1,)), ((), ())), preferred_element_type=F32, precision=precision)


def _dot_t_split(a, b):
    ah, bh = a.astype(BF16), b.astype(BF16)
    al, bl = (a - ah.astype(F32)).astype(BF16), (b - bh.astype(F32)).astype(BF16)
    return _dot_t(ah, bh) + _dot_t(ah, bl) + _dot_t(al, bh)


def _silu(x):
    return x * jax.nn.sigmoid(x)


def _ple(h, p, g, w_gate, w_proj):
    gate = jax.nn.sigmoid(_dot(_rms(h, g).astype(BF16), w_gate))
    return h + gate * _dot(p.astype(BF16), w_proj)


def _rope_head(x, cos, sin, lane):
    half = ROT_DIM // 2
    partner = jnp.where(lane < half, pltpu.roll(x, HEAD_DIM - half, 1), pltpu.roll(x, half, 1))
    return x * cos + partner * sin


CONV_HALO = 32
CONV_CHUNK = 64
CONV_STRIP = 128
CONV_NORM_ROWS = 16


def _conv_prompt_kernel(x_ref, g_ref, w1_ref, b1_ref, wrep_ref, bdw_ref, lng_ref, lnb_ref, w2_ref, b2_ref,
                        out_ref, state_ref, upad_ref, y_ref, cacc_ref, *, tm, width, dc):
    t = pl.program_id(1)

    @pl.when(t == 0)
    def _():
        upad_ref[...] = jnp.zeros(upad_ref.shape, F32)

    x = x_ref[0]
    uu = _dot(_rms(x, g_ref[...]).astype(BF16), w1_ref[...]) + b1_ref[...]
    upad_ref[CONV_HALO:CONV_HALO + tm, :] = uu[:, :dc] * jax.nn.sigmoid(uu[:, dc:])

    first = CONV_HALO - (width - 1)
    span = CONV_CHUNK + SUBLANES

    def chunk(i, carry):
        base = pl.multiple_of(i * CONV_CHUNK, CONV_CHUNK)
        for c0 in range(0, dc, CONV_STRIP):
            lanes = slice(c0, c0 + CONV_STRIP)
            acc = jnp.broadcast_to(bdw_ref[:, lanes], (CONV_CHUNK, CONV_STRIP))
            for r in range(SUBLANES):
                part = None
                for k in range(width):
                    a, kr = divmod(first + k, SUBLANES)
                    if kr != r:
                        continue
                    w = wrep_ref[k * SUBLANES:(k + 1) * SUBLANES, lanes]
                    term = upad_ref[pl.ds(base + a * SUBLANES, span), lanes] * jnp.concatenate(
                        [w] * (span // SUBLANES), axis=0)
                    part = term if part is None else part + term
                if part is not None:
                    acc = acc + part[r:r + CONV_CHUNK, :]
            cacc_ref[:, lanes] = acc
        for r0 in range(0, CONV_CHUNK, CONV_NORM_ROWS):
            acc = cacc_ref[r0:r0 + CONV_NORM_ROWS, :]
            mu = jnp.mean(acc, axis=-1, keepdims=True)
            d = acc - mu
            var = jnp.mean(d * d, axis=-1, keepdims=True)
            z = d * lax.rsqrt(var + EPS) * lng_ref[...] + lnb_ref[...]
            y_ref[pl.ds(base + r0, CONV_NORM_ROWS), :] = _silu(z).astype(BF16)
        return carry

    lax.fori_loop(0, tm // CONV_CHUNK, chunk, 0)
    out_ref[0] = x + _dot(y_ref[...], w2_ref[...]) + b2_ref[...]
    state_ref[0] = upad_ref[tm + first:tm + CONV_HALO, :]
    upad_ref[0:CONV_HALO, :] = upad_ref[tm:tm + CONV_HALO, :]


def _conv_prompt(x, g, w1, b1, wdw, bdw, lng, lnb, w2, b2, *, tm):
    bsz, seq, d = x.shape
    width, dc = wdw.shape
    assert seq % tm == 0 and tm % CONV_CHUNK == 0 and width - 1 <= CONV_HALO
    row = lambda a: a.reshape(1, -1)
    full = lambda a: pl.BlockSpec(a.shape, lambda b, t: (0,) * a.ndim)
    wrep = jnp.repeat(wdw, SUBLANES, axis=0)
    args = (row(g), w1, row(b1), wrep, row(bdw), row(lng), row(lnb), w2, row(b2))
    return pl.pallas_call(
        functools.partial(_conv_prompt_kernel, tm=tm, width=width, dc=dc),
        grid=(bsz, seq // tm),
        in_specs=[pl.BlockSpec((1, tm, d), lambda b, t: (b, t, 0))] + [full(a) for a in args],
        out_specs=[pl.BlockSpec((1, tm, d), lambda b, t: (b, t, 0)),
                   pl.BlockSpec((1, width - 1, dc), lambda b, t: (b, 0, 0))],
        out_shape=[jax.ShapeDtypeStruct((bsz, seq, d), F32),
                   jax.ShapeDtypeStruct((bsz, width - 1, dc), F32)],
        scratch_shapes=[pltpu.VMEM((tm + CONV_HALO + SUBLANES, dc), F32), pltpu.VMEM((tm, dc), BF16),
                        pltpu.VMEM((CONV_CHUNK, dc), F32)],
        compiler_params=_params("parallel", "arbitrary"),
        name="conv_prompt",
    )(x, *args)


def _conv_sample_kernel(x_ref, st_ref, g_ref, w1_ref, b1_ref, wdw_ref, bdw_ref, lng_ref, lnb_ref, w2_ref, b2_ref,
                        out_ref, state_ref, *, width, dc):
    nb = x_ref.shape[0]
    x = x_ref[...]
    uu = _dot(_rms(x, g_ref[...]).astype(BF16), w1_ref[...]) + b1_ref[...]
    u = uu[:, :dc] * jax.nn.sigmoid(uu[:, dc:])
    st = st_ref[...]
    acc = jnp.sum(st * wdw_ref[0:width - 1, :][None], axis=1) + u * wdw_ref[width - 1:width, :] + bdw_ref[...]
    mu = jnp.mean(acc, axis=-1, keepdims=True)
    d = acc - mu
    var = jnp.mean(d * d, axis=-1, keepdims=True)
    z = d * lax.rsqrt(var + EPS) * lng_ref[...] + lnb_ref[...]
    out_ref[...] = x + _dot(_silu(z).astype(BF16), w2_ref[...]) + b2_ref[...]
    state_ref[:, 0:width - 2, :] = st_ref[:, 1:width - 1, :]
    for b in range(nb):
        state_ref[b, width - 2:width - 1, :] = u[b:b + 1, :]


def _conv_sample(x, st, g, w1, b1, wdw, bdw, lng, lnb, w2, b2):
    nb, d = x.shape
    width, dc = wdw.shape
    row = lambda a: a.reshape(1, -1)
    return pl.pallas_call(
        functools.partial(_conv_sample_kernel, width=width, dc=dc),
        out_shape=[jax.ShapeDtypeStruct((nb, d), F32), jax.ShapeDtypeStruct((nb, width - 1, dc), F32)],
        compiler_params=pltpu.CompilerParams(vmem_limit_bytes=VMEM_LIMIT),
        name="conv_sample",
    )(x, st, row(g), w1, row(b1), wdw, row(bdw), row(lng), row(lnb), w2, row(b2))


def _ffn_kernel(x_ref, g_ref, wg_ref, wu_ref, wd_ref, p_ref, gp_ref, wpg_ref, wpp_ref, out_ref, xn_ref, acc_ref,
                *, tf):
    xn_ref[...] = _rms(x_ref[...], g_ref[...]).astype(BF16)
    acc_ref[...] = jnp.zeros(acc_ref.shape, F32)

    def ffn(fc, carry):
        f0 = pl.multiple_of(fc * tf, tf)
        xn = xn_ref[...]
        a = (_silu(_dot(xn, wg_ref[:, pl.ds(f0, tf)])) * _dot(xn, wu_ref[:, pl.ds(f0, tf)])).astype(BF16)
        acc_ref[...] += _dot(a, wd_ref[pl.ds(f0, tf), :])
        return carry
    lax.fori_loop(0, wg_ref.shape[1] // tf, ffn, 0)

    tm = x_ref.shape[0]
    step = min(EPILOGUE_ROWS, tm)
    for r in range(0, tm, step):
        rows = slice(r, r + step)
        out_ref[rows, :] = _ple(x_ref[rows, :] + acc_ref[rows, :], p_ref[rows, :], gp_ref[...], wpg_ref[...],
                                wpp_ref[...])


def _ffn(x, g, wg, wu, wd, p, gp, wpg, wpp, *, tm, tf):
    n, d = x.shape
    dff = wg.shape[1]
    assert n % tm == 0 and dff % tf == 0
    row = lambda a: a.reshape(1, -1)
    full = lambda a: pl.BlockSpec(a.shape, lambda i: (0,) * a.ndim)
    resident = lambda a: pl.BlockSpec(a.shape, lambda i: (0,) * a.ndim, pipeline_mode=pl.Buffered(1))
    return pl.pallas_call(
        functools.partial(_ffn_kernel, tf=tf),
        grid=(n // tm,),
        in_specs=[pl.BlockSpec((tm, d), lambda i: (i, 0)), full(row(g)), resident(wg), resident(wu), resident(wd),
                  pl.BlockSpec((tm, p.shape[1]), lambda i: (i, 0)), full(row(gp)), full(wpg), full(wpp)],
        out_specs=pl.BlockSpec((tm, d), lambda i: (i, 0)),
        out_shape=jax.ShapeDtypeStruct((n, d), F32),
        scratch_shapes=[pltpu.VMEM((tm, d), BF16), pltpu.VMEM((tm, d), F32)],
        compiler_params=_params("parallel"),
        name="ffn_ple",
    )(x, row(g), wg, wu, wd, p, row(gp), wpg, wpp)


def _qkv_prompt_kernel(x_ref, g_ref, w_ref, cos_ref, sin_ref,
                       qt_ref, k_ref, v_ref, kb_ref, vt_ref, biast_ref, kmt_ref, *, tm, nh, nkv, nblk):
    t = pl.program_id(1)
    group = nh // nkv
    nq = nh * HEAD_DIM
    nk = nkv * HEAD_DIM

    @pl.when(t == 0)
    def _():
        kmt_ref[...] = jnp.zeros(kmt_ref.shape, F32)

    qkv = _dot(_rms(x_ref[0], g_ref[...]).astype(BF16), w_ref[...])
    cos = cos_ref[...]
    sin = sin_ref[...]
    lane = lax.broadcasted_iota(jnp.int32, (tm, HEAD_DIM), 1)
    scale = HEAD_DIM ** -0.5
    q_heads = [_rope_head(qkv[:, h * HEAD_DIM:(h + 1) * HEAD_DIM], cos, sin, lane) * scale for h in range(nh)]
    q = jnp.concatenate(q_heads, axis=1)
    k = jnp.concatenate([_rope_head(qkv[:, nq + h * HEAD_DIM:nq + (h + 1) * HEAD_DIM], cos, sin, lane)
                         for h in range(nkv)], axis=1)
    v = qkv[:, nq + nk:]
    for h in range(nh):
        qt_ref[0, h] = q_heads[h].T.astype(BF16)
    for kv in range(nkv):
        k_ref[0, pl.ds(kv, tm, stride=nkv), :] = k[:, kv * HEAD_DIM:(kv + 1) * HEAD_DIM]
        v_ref[0, pl.ds(kv, tm, stride=nkv), :] = v[:, kv * HEAD_DIM:(kv + 1) * HEAD_DIM]
        vt_ref[0, kv] = v[:, kv * HEAD_DIM:(kv + 1) * HEAD_DIM].T.astype(BF16)
    kb_ref[0] = k.astype(BF16)

    kmt = kmt_ref[...]
    kmt_row = lax.broadcasted_iota(jnp.int32, kmt.shape, 0)
    kmt_head = lax.broadcasted_iota(jnp.int32, kmt.shape, 1) // HEAD_DIM
    for i in range(tm // MOBA_BLOCK):
        km = jnp.mean(k[i * MOBA_BLOCK:(i + 1) * MOBA_BLOCK, :], axis=0, keepdims=True)
        km = jnp.concatenate([km[:, (h // group) * HEAD_DIM:(h // group + 1) * HEAD_DIM] for h in range(nh)], axis=1)
        n = t * (tm // MOBA_BLOCK) + i
        kmt = jnp.where(kmt_row == kmt_head * nblk + n, km, kmt)
    kmt_ref[...] = kmt

    s = _dot_t_split(q, kmt).T.reshape(nh, nblk, tm)
    n_idx = lax.broadcasted_iota(jnp.int32, s.shape, 1)
    own = (t * tm + lax.broadcasted_iota(jnp.int32, s.shape, 2)) // MOBA_BLOCK
    past = n_idx < own
    s = jnp.where(past, s, -jnp.inf)
    rank = jnp.zeros(s.shape, F32)
    for dlt in range(1, nblk):
        wrap = n_idx + dlt >= nblk
        other = jnp.concatenate([s[:, dlt:, :], s[:, :dlt, :]], axis=1)
        rank = rank + jnp.where(other > s, 1.0, 0.0) + jnp.where(wrap, jnp.where(other == s, 1.0, 0.0), 0.0)
    bias = jnp.where(past, jnp.where(rank < TOP_K_BLOCKS, 0.0, MASK_VALUE), MASK_VALUE)
    biast_ref[0] = bias.reshape(nh * nblk, tm).astype(BF16)


def _qkv_prompt(x, g, w, cos, sin, *, tm, nh, nkv):
    bsz, seq, d = x.shape
    nq, nk = nh * HEAD_DIM, nkv * HEAD_DIM
    nblk = seq // MOBA_BLOCK
    assert seq % tm == 0 and tm % MOBA_BLOCK == 0 and nh * nblk == LANES
    tok = lambda c: pl.BlockSpec((1, tm, c), lambda b, t: (b, t, 0))
    kv_rows = pl.BlockSpec((1, tm * nkv, HEAD_DIM), lambda b, t: (b, t, 0))
    full = lambda a: pl.BlockSpec(a.shape, lambda b, t: (0,) * a.ndim)
    g = g.reshape(1, -1)
    return pl.pallas_call(
        functools.partial(_qkv_prompt_kernel, tm=tm, nh=nh, nkv=nkv, nblk=nblk),
        grid=(bsz, seq // tm),
        in_specs=[tok(d), full(g), full(w),
                  pl.BlockSpec((tm, HEAD_DIM), lambda b, t: (t, 0)), pl.BlockSpec((tm, HEAD_DIM), lambda b, t: (t, 0))],
        out_specs=[pl.BlockSpec((1, nh, HEAD_DIM, tm), lambda b, t: (b, 0, 0, t)), kv_rows, kv_rows, tok(nk),
                   pl.BlockSpec((1, nkv, HEAD_DIM, tm), lambda b, t: (b, 0, 0, t)),
                   pl.BlockSpec((1, nh * nblk, tm), lambda b, t: (b, 0, t))],
        out_shape=[jax.ShapeDtypeStruct((bsz, nh, HEAD_DIM, seq), BF16),
                   jax.ShapeDtypeStruct((bsz, seq * nkv, HEAD_DIM), F32),
                   jax.ShapeDtypeStruct((bsz, seq * nkv, HEAD_DIM), F32),
                   jax.ShapeDtypeStruct((bsz, seq, nk), BF16),
                   jax.ShapeDtypeStruct((bsz, nkv, HEAD_DIM, seq), BF16),
                   jax.ShapeDtypeStruct((bsz, nh * nblk, seq), BF16)],
        scratch_shapes=[pltpu.VMEM((nh * nblk, nq), F32)],
        compiler_params=_params("parallel", "arbitrary"),
        name="qkv_prompt",
    )(x, g, w, cos, sin)


def _qkv_sample_kernel(x_ref, g_ref, w_ref, cos_ref, sin_ref, q_ref, k_ref, v_ref, *, nh, nkv):
    nb = x_ref.shape[0]
    nq = nh * HEAD_DIM
    nk = nkv * HEAD_DIM
    qkv = _dot(_rms(x_ref[...], g_ref[...]).astype(BF16), w_ref[...])
    cos = cos_ref[...]
    sin = sin_ref[...]
    lane = lax.broadcasted_iota(jnp.int32, (nb, HEAD_DIM), 1)
    scale = HEAD_DIM ** -0.5
    q_ref[...] = jnp.concatenate([_rope_head(qkv[:, h * HEAD_DIM:(h + 1) * HEAD_DIM], cos, sin, lane) * scale
                                  for h in range(nh)], axis=1)
    k_ref[...] = jnp.concatenate([_rope_head(qkv[:, nq + h * HEAD_DIM:nq + (h + 1) * HEAD_DIM], cos, sin, lane)
                                  for h in range(nkv)], axis=1)
    v_ref[...] = qkv[:, nq + nk:]


def _qkv_sample(x, g, w, cos, sin, *, nh, nkv):
    nb = x.shape[0]
    nq, nk = nh * HEAD_DIM, nkv * HEAD_DIM
    return pl.pallas_call(
        functools.partial(_qkv_sample_kernel, nh=nh, nkv=nkv),
        out_shape=[jax.ShapeDtypeStruct((nb, nq), F32), jax.ShapeDtypeStruct((nb, nk), F32),
                   jax.ShapeDtypeStruct((nb, nk), F32)],
        compiler_params=pltpu.CompilerParams(vmem_limit_bytes=VMEM_LIMIT),
        name="qkv_sample",
    )(x, g.reshape(1, -1), w, cos, sin)


ATTN_BLOCKS_PER_PASS = (4, 2, 1)


def _attn_prompt_kernel(qt_ref, kb_ref, vt_ref, biast_ref, h_ref, wo_ref, out_ref, *, nh, nkv, nblk):
    o = pl.program_id(1)
    hk = pl.program_id(2)
    group = nh // nkv
    blk = MOBA_BLOCK
    bias = biast_ref[0]
    feat_head = lax.broadcasted_iota(jnp.int32, bias.shape, 0) // nblk
    q_t = jnp.concatenate([qt_ref[0, g] for g in range(group)], axis=1)
    q_aug = jnp.concatenate(
        [q_t, jnp.concatenate([jnp.where(feat_head == hk * group + g, bias, jnp.zeros_like(bias))
                               for g in range(group)], axis=1)], axis=0)
    cols = group * blk

    own0 = pl.multiple_of(o * blk, blk)
    s = _dot(kb_ref[0, pl.ds(own0, blk), :], q_t)
    kpos = lax.broadcasted_iota(jnp.int32, (blk, cols), 0)
    qpos = lax.broadcasted_iota(jnp.int32, (blk, cols), 1) % blk
    s = jnp.where(kpos <= qpos, s, MASK_VALUE)
    m = jnp.max(s, axis=0, keepdims=True)
    p = jnp.exp(s - m)
    l = jnp.sum(p, axis=0, keepdims=True)
    acc = _dot(vt_ref[0, 0, :, pl.ds(own0, blk)], p.astype(BF16))
    def past(first, nb_, carry):
        m, l, acc = carry
        keys = nb_ * blk
        j0 = pl.multiple_of(first * blk, keys)
        lane = lax.broadcasted_iota(jnp.int32, (keys, nh * nblk), 1)
        key_blk = first + lax.broadcasted_iota(jnp.int32, (keys, nh * nblk), 0) // blk
        pick = jnp.where(lane % nblk == key_blk, 1.0, 0.0).astype(BF16)
        s = _dot(jnp.concatenate([kb_ref[0, pl.ds(j0, keys), :], pick], axis=1), q_aug)
        m_new = jnp.maximum(m, jnp.max(s, axis=0, keepdims=True))
        alpha = jnp.exp(m - m_new)
        p = jnp.exp(s - m_new)
        l = alpha * l + jnp.sum(p, axis=0, keepdims=True)
        acc = alpha * acc + _dot(vt_ref[0, 0, :, pl.ds(j0, keys)], p.astype(BF16))
        return m_new, l, acc

    state, done = (m, l, acc), 0
    for nb_ in ATTN_BLOCKS_PER_PASS:
        cnt = (o - done) // nb_
        state = lax.fori_loop(0, cnt, lambda i, c, nb_=nb_, done=done: past(done + i * nb_, nb_, c), state)
        done = done + cnt * nb_
    m, l, acc = state
    att = acc / l
    att = jnp.concatenate([att[:, g * blk:(g + 1) * blk].T for g in range(group)], axis=1).astype(BF16)
    proj = _dot(att, wo_ref[...])

    @pl.when(hk == 0)
    def _():
        out_ref[0] = h_ref[0] + proj

    @pl.when(hk != 0)
    def _():
        out_ref[0] += proj


def _attn_prompt(qt, kb, vt, biast, h, wo, *, nh, nkv):
    bsz, seq, d = h.shape
    nblk = seq // MOBA_BLOCK
    group = nh // nkv
    gw = group * HEAD_DIM
    return pl.pallas_call(
        functools.partial(_attn_prompt_kernel, nh=nh, nkv=nkv, nblk=nblk),
        grid=(bsz, nblk, nkv),
        in_specs=[pl.BlockSpec((1, group, HEAD_DIM, MOBA_BLOCK), lambda b, o, k: (b, k, 0, o)),
                  pl.BlockSpec((1, seq, HEAD_DIM), lambda b, o, k: (b, 0, k)),
                  pl.BlockSpec((1, 1, HEAD_DIM, seq), lambda b, o, k: (b, k, 0, 0)),
                  pl.BlockSpec((1, biast.shape[1], MOBA_BLOCK), lambda b, o, k: (b, 0, o)),
                  pl.BlockSpec((1, MOBA_BLOCK, d), lambda b, o, k: (b, o, 0)),
                  pl.BlockSpec((gw, d), lambda b, o, k: (k, 0))],
        out_specs=pl.BlockSpec((1, MOBA_BLOCK, d), lambda b, o, k: (b, o, 0)),
        out_shape=jax.ShapeDtypeStruct((bsz, seq, d), F32),
        compiler_params=_params("parallel", "parallel", "arbitrary"),
        name="attn_prompt",
    )(qt, kb, vt, biast, h, wo)


PAGES_PER_STEP = 64


def _sample_select_kernel(pt_ref, *refs, nh, nkv, ppb, nblk):
    pages = refs[:PAGES_PER_STEP]
    q_ref, idx_ref, km_ref = refs[PAGES_PER_STEP:]
    s = pl.program_id(1)
    group = nh // nkv
    bps = PAGES_PER_STEP // ppb
    psize = pages[0].shape[1] // nkv

    @pl.when(s == 0)
    def _():
        km_ref[...] = jnp.zeros(km_ref.shape, F32)

    km = km_ref[...]
    km_row = lax.broadcasted_iota(jnp.int32, km.shape, 0)
    for i in range(bps):
        tot = jnp.concatenate(
            [sum(jnp.sum(pages[i * ppb + j][0, pl.ds(kv, psize, stride=nkv), :], axis=0, keepdims=True)
                 for j in range(ppb)) for kv in range(nkv)], axis=1)
        km = jnp.where(km_row == s * bps + i, tot / (psize * ppb), km)
    km_ref[...] = km

    @pl.when(s == pl.num_programs(1) - 1)
    def _():
        km = km_ref[...]
        q = q_ref[0]
        sc = jnp.concatenate(
            [_dot_t(q[kv * group:(kv + 1) * group, :], km[:, kv * HEAD_DIM:(kv + 1) * HEAD_DIM],
                    precision=lax.Precision.HIGHEST) for kv in range(nkv)], axis=0)
        col = lax.broadcasted_iota(jnp.int32, sc.shape, 1)
        lane = lax.broadcasted_iota(jnp.int32, (nh, LANES), 1)
        out = jnp.zeros((nh, LANES), jnp.int32)
        for r in range(TOP_K_BLOCKS):
            best = jnp.max(sc, axis=-1, keepdims=True)
            pick = jnp.min(jnp.where(sc == best, col, nblk), axis=-1, keepdims=True)
            out = jnp.where(lane == r, pick, out)
            sc = jnp.where(col == pick, -jnp.inf, sc)
        idx_ref[0] = out


def _sample_select(cache_k, page_table, q, *, nh, nkv):
    npool, prow, hd = cache_k.shape
    nb, npages = page_table.shape
    ppb = MOBA_BLOCK // (prow // nkv)
    nblk = npages // ppb
    assert npages % PAGES_PER_STEP == 0 and PAGES_PER_STEP % ppb == 0 and nblk >= TOP_K_BLOCKS
    page_spec = lambda j: pl.BlockSpec((1, prow, hd), lambda b, s, pt: (pt[b * npages + s * PAGES_PER_STEP + j], 0, 0))
    grid_spec = pltpu.PrefetchScalarGridSpec(
        num_scalar_prefetch=1,
        grid=(nb, npages // PAGES_PER_STEP),
        in_specs=[page_spec(j) for j in range(PAGES_PER_STEP)]
        + [pl.BlockSpec((1, nh, HEAD_DIM), lambda b, s, pt: (b, 0, 0))],
        out_specs=pl.BlockSpec((1, nh, LANES), lambda b, s, pt: (b, 0, 0)),
        scratch_shapes=[pltpu.VMEM((nblk, nkv * hd), F32)],
    )
    return pl.pallas_call(
        functools.partial(_sample_select_kernel, nh=nh, nkv=nkv, ppb=ppb, nblk=nblk),
        grid_spec=grid_spec,
        out_shape=jax.ShapeDtypeStruct((nb, nh, LANES), jnp.int32),
        compiler_params=_params("parallel", "arbitrary"),
        name="sample_select",
    )(page_table.reshape(-1), *([cache_k] * PAGES_PER_STEP), q.reshape(nb, nh, HEAD_DIM))


def _sample_attn_kernel(pt_ref, idx_ref, q_ref, kn_ref, vn_ref, *refs, group, nkv, npg):
    kpages = refs[:group * npg]
    vpages = refs[group * npg:2 * group * npg]
    out_ref = refs[2 * group * npg]
    hkv = pl.program_id(1)
    prow = kpages[0].shape[1]
    mine = lax.broadcasted_iota(jnp.int32, (prow, 1), 0) % nkv == hkv
    outs = []
    for g in range(group):
        q = q_ref[0, g:g + 1, :]
        m = jnp.sum(q * kn_ref[0], axis=-1, keepdims=True)
        l = jnp.ones((1, 1), F32)
        acc = vn_ref[0]
        for i in range(npg):
            k = kpages[g * npg + i][0]
            sc = jnp.where(mine, jnp.sum(k * q, axis=-1, keepdims=True), MASK_VALUE)
            m_new = jnp.maximum(m, jnp.max(sc, axis=0, keepdims=True))
            alpha = jnp.exp(m - m_new)
            p = jnp.exp(sc - m_new)
            l = alpha * l + jnp.sum(p, axis=0, keepdims=True)
            acc = alpha * acc + jnp.sum(p * vpages[g * npg + i][0], axis=0, keepdims=True)
            m = m_new
        outs.append(acc / l)
    out_ref[0] = jnp.concatenate(outs, axis=0)


def _sample_attn(cache_k, cache_v, page_table, idx, q, kn, vn, *, nh, nkv):
    npool, prow, hd = cache_k.shape
    nb, npages = page_table.shape
    ppb = MOBA_BLOCK // (prow // nkv)
    group = nh // nkv
    nsel = TOP_K_BLOCKS
    npg = nsel * ppb

    def page_spec(g, i):
        def imap(b, kv, pt, ix):
            blk = ix[(b * nh + kv * group + g) * nsel + i // ppb]
            return (pt[b * npages + blk * ppb + i % ppb], 0, 0)
        return pl.BlockSpec((1, prow, hd), imap)

    qspec = pl.BlockSpec((1, group, HEAD_DIM), lambda b, kv, pt, ix: (b * nkv + kv, 0, 0))
    kvspec = pl.BlockSpec((1, 1, HEAD_DIM), lambda b, kv, pt, ix: (b * nkv + kv, 0, 0))
    pages = [page_spec(g, i) for g in range(group) for i in range(npg)]
    grid_spec = pltpu.PrefetchScalarGridSpec(
        num_scalar_prefetch=2,
        grid=(nb, nkv),
        in_specs=[qspec, kvspec, kvspec] + pages + pages,
        out_specs=qspec,
    )
    out = pl.pallas_call(
        functools.partial(_sample_attn_kernel, group=group, nkv=nkv, npg=npg),
        grid_spec=grid_spec,
        out_shape=jax.ShapeDtypeStruct((nb * nkv, group, HEAD_DIM), F32),
        compiler_params=_params("parallel", "parallel"),
        name="sample_attn",
    )(page_table.reshape(-1), idx.reshape(-1),
      q.reshape(nb * nkv, group, HEAD_DIM), kn.reshape(nb * nkv, 1, HEAD_DIM), vn.reshape(nb * nkv, 1, HEAD_DIM),
      *([cache_k] * (group * npg)), *([cache_v] * (group * npg)))
    return out.reshape(nb, nh * HEAD_DIM)


def _proj_residual_kernel(a_ref, h_ref, w_ref, out_ref):
    out_ref[...] = h_ref[...] + _dot(a_ref[...].astype(BF16), w_ref[...])


def _proj_residual(a, h, w):
    return pl.pallas_call(
        _proj_residual_kernel,
        out_shape=jax.ShapeDtypeStruct(h.shape, F32),
        compiler_params=pltpu.CompilerParams(vmem_limit_bytes=VMEM_LIMIT),
        name="proj_residual",
    )(a, h, w)


MXU_DEPTH = 256
MOE_ROW_GRAN = 32
MOE_VARIANTS = 16
MOE_SPECIALISED = (-96, 160)


def _router_kernel(x_ref, g_ref, wrt_ref, tri_ref, xn_ref, gate_ref, rank_ref, cnt_ref):
    xn = _rms(x_ref[...], g_ref[...])
    xn_ref[...] = xn.astype(BF16)
    logits = _dot_t_split(wrt_ref[...], xn)
    ne = logits.shape[0]
    e = jnp.exp(logits - jnp.max(logits, axis=0, keepdims=True))
    probs = e / jnp.sum(e, axis=0, keepdims=True)
    eid = lax.broadcasted_iota(jnp.int32, probs.shape, 0)
    rest = probs
    member = jnp.zeros(probs.shape, jnp.bool_)
    top_sum = jnp.zeros((1, probs.shape[1]), F32)
    for _ in range(TOP_K_EXPERTS):
        best = jnp.max(rest, axis=0, keepdims=True)
        pick = eid == jnp.min(jnp.where(rest == best, eid, ne), axis=0, keepdims=True)
        member = member | pick
        top_sum = top_sum + best
        rest = jnp.where(pick, -1.0, rest)
    gate_ref[...] = jnp.where(member, probs / top_sum, 0.0)
    mem = jnp.where(member, 1.0, 0.0)
    rank = _dot(mem.astype(BF16), tri_ref[...])
    rank_ref[...] = jnp.where(member, rank, -1.0)
    cnt_ref[0] = jnp.broadcast_to(jnp.sum(mem, axis=1, keepdims=True), cnt_ref.shape[1:])


def _router(x, g, wr, *, tm):
    n, d = x.shape
    ne = wr.shape[1]
    nt = n // tm
    assert n % tm == 0
    tri = (lax.broadcasted_iota(jnp.int32, (tm, tm), 0) < lax.broadcasted_iota(jnp.int32, (tm, tm), 1)).astype(BF16)
    full = lambda a: pl.BlockSpec(a.shape, lambda i: (0,) * a.ndim)
    g = g.reshape(1, -1)
    wrt = wr.T
    xn, gate, rank, cnt = pl.pallas_call(
        _router_kernel,
        grid=(nt,),
        in_specs=[pl.BlockSpec((tm, d), lambda i: (i, 0)), full(g), full(wrt), full(tri)],
        out_specs=[pl.BlockSpec((tm, d), lambda i: (i, 0)), pl.BlockSpec((ne, tm), lambda i: (0, i)),
                   pl.BlockSpec((ne, tm), lambda i: (0, i)), pl.BlockSpec((1, ne, LANES), lambda i: (i, 0, 0))],
        out_shape=[jax.ShapeDtypeStruct((n, d), BF16), jax.ShapeDtypeStruct((ne, n), F32),
                   jax.ShapeDtypeStruct((ne, n), F32), jax.ShapeDtypeStruct((nt, ne, LANES), F32)],
        compiler_params=_params("parallel"),
        name="router",
    )(x, g, wrt, tri)
    return xn, gate, rank, cnt[:, :, 0].astype(jnp.int32).reshape(-1)


def _moe_kernel(eid_ref, cnt_ref, cnts_ref, xn_ref, gate_ref, rank_ref, wg_ref, wu_ref, wd_ref, hin_ref,
                xns_ref, gates_ref, ranks_ref, hins_ref, out_ref, outs_ref, xc_ref, y_ref, *, tm, tf, ne):
    e = eid_ref[0]
    i = pl.program_id(0)
    weights = (wg_ref, wu_ref, wd_ref)
    _expert_rows(e, cnt_ref[i * ne + e], xn_ref, gate_ref, rank_ref, hin_ref, out_ref, weights, xc_ref, y_ref,
                 tm=tm, tf=tf, ne=ne)

    @pl.when(i == pl.num_programs(0) - 1)
    def _():
        _expert_rows(e, cnts_ref[e], xns_ref, gates_ref, ranks_ref, hins_ref, outs_ref, weights, xc_ref, y_ref,
                     tm=xns_ref.shape[0], tf=tf, ne=ne)


def _moe_groups(tm):
    gran = min(MOE_ROW_GRAN, tm)
    return gran, min(MOE_VARIANTS, tm // gran)


def _expert_rows(e, cnt, xn_ref, gate_ref, rank_ref, hin_ref, out_ref, weights, xc_ref, y_ref, *, tm, tf, ne):
    wg_ref, wu_ref, wd_ref = weights
    gran, nvar = _moe_groups(tm)
    ngrp = (cnt + gran - 1) // gran
    rank = rank_ref[pl.ds(e, 1), :]
    kq = min(MXU_DEPTH, tm)

    def onehot(r0, rows):
        slot = lax.broadcasted_iota(jnp.int32, (rows, tm), 0) + r0
        return slot.astype(F32) == rank

    def process(r0, rows, base_ref):
        krows = -(-rows // kq) * kq
        sel = jnp.where(onehot(r0, rows), 1.0, 0.0).astype(BF16)
        xc_ref[0:rows, :] = _dot(sel, xn_ref[...]).astype(BF16)
        y_ref[0:krows, :] = jnp.zeros((krows, y_ref.shape[1]), F32)

        def ffn(fc, carry):
            xc = xc_ref[0:rows, :]
            f0 = pl.multiple_of(fc * tf, tf)
            a = (_silu(_dot(xc, wg_ref[0, :, pl.ds(f0, tf)])) * _dot(xc, wu_ref[0, :, pl.ds(f0, tf)])).astype(BF16)
            y_ref[0:rows, :] += _dot(a, wd_ref[0, pl.ds(f0, tf), :])
            return carry
        lax.fori_loop(0, wg_ref.shape[2] // tf, ffn, 0)
        w = jnp.where(onehot(r0, krows), gate_ref[pl.ds(e, 1), :], 0.0).astype(BF16)
        out_ref[...] = base_ref[...] + lax.dot_general(w, y_ref[0:krows, :].astype(BF16), (((0,), (0,)), ((), ())),
                                                       preferred_element_type=F32)

    @pl.when(ngrp == 0)
    def _():
        out_ref[...] = hin_ref[...]

    balanced = tm * TOP_K_EXPERTS // ne
    sizes = range(max(1, (balanced + MOE_SPECIALISED[0]) // gran), min(nvar, (balanced + MOE_SPECIALISED[1]) // gran) + 1)
    for n in sizes:
        @pl.when(ngrp == n)
        def _(n=n):
            process(0, n * gran, hin_ref)

    common = functools.reduce(lambda a, n: a | (ngrp == n), sizes, ngrp == 0)

    @pl.when(jnp.logical_not(common))
    def _():
        full = nvar * gran
        out_ref[...] = hin_ref[...]

        def body(c, carry):
            process(c * full, full, out_ref)
            return carry
        lax.fori_loop(0, (cnt + full - 1) // full, body, 0)


def _moe_finish_kernel(h_ref, p_ref, gp_ref, wpg_ref, wpp_ref, gf_ref, out_ref):
    tm = h_ref.shape[0]
    step = min(EPILOGUE_ROWS, tm)
    for r in range(0, tm, step):
        rows = slice(r, r + step)
        out_ref[rows, :] = _rms(_ple(h_ref[rows, :], p_ref[rows, :], gp_ref[...], wpg_ref[...], wpp_ref[...]),
                                gf_ref[...])


def _moe(h, hs, g, wr, wg, wu, wd, p, ps, gp, wpg, wpp, gf, *, tm, tf):
    n, d = h.shape
    ns = hs.shape[0]
    ne, _, dff = wg.shape
    nt = n // tm
    for rows in (tm, ns):
        gran, nvar = _moe_groups(rows)
        assert rows % (nvar * gran) == 0 and (nvar * gran) % min(MXU_DEPTH, rows) == 0
    assert n % tm == 0 and dff % tf == 0 and ns <= tm
    xn, gate, rank, cnt = _router(h, g, wr, tm=tm)
    xns, gates, ranks, cnts = _router(hs, g, wr, tm=ns)
    resident = lambda a: pl.BlockSpec((1,) + a.shape[1:], lambda i, eid, c, cs: (eid[0], 0, 0),
                                      pipeline_mode=pl.Buffered(1))
    tile = lambda c: pl.BlockSpec((tm, c), lambda i, eid, c_, cs: (i, 0))
    lanes = pl.BlockSpec((ne, tm), lambda i, eid, c, cs: (0, i))
    whole = lambda a: pl.BlockSpec(a.shape, lambda i, eid, c, cs: (0,) * a.ndim)
    full = max(g * v for g, v in (_moe_groups(tm), _moe_groups(ns)))
    grid_spec = pltpu.PrefetchScalarGridSpec(
        num_scalar_prefetch=3,
        grid=(nt,),
        in_specs=[tile(d), lanes, lanes, resident(wg), resident(wu), resident(wd), tile(d),
                  whole(xns), whole(gates), whole(ranks), whole(hs)],
        out_specs=[tile(d), whole(hs)],
        scratch_shapes=[pltpu.VMEM((full, d), BF16), pltpu.VMEM((full, d), F32)],
    )
    one_expert = pl.pallas_call(
        functools.partial(_moe_kernel, tm=tm, tf=tf, ne=ne),
        grid_spec=grid_spec,
        out_shape=[jax.ShapeDtypeStruct((n, d), F32), jax.ShapeDtypeStruct((ns, d), F32)],
        input_output_aliases={9: 0, 13: 1},
        compiler_params=_params("arbitrary"),
        name="moe_expert",
    )
    for e in range(ne):
        h, hs = one_expert(jnp.full((1,), e, jnp.int32), cnt, cnts, xn, gate, rank, wg, wu, wd, h, xns, gates, ranks, hs)

    row = lambda a: a.reshape(1, -1)

    def finish(hm, pm, rows):
        whole1 = lambda a: pl.BlockSpec(a.shape, lambda i: (0,) * a.ndim)
        return pl.pallas_call(
            _moe_finish_kernel,
            grid=(hm.shape[0] // rows,),
            in_specs=[pl.BlockSpec((rows, d), lambda i: (i, 0)), pl.BlockSpec((rows, pm.shape[1]), lambda i: (i, 0)),
                      whole1(row(gp)), whole1(wpg), whole1(wpp), whole1(row(gf))],
            out_specs=pl.BlockSpec((rows, d), lambda i: (i, 0)),
            out_shape=jax.ShapeDtypeStruct(hm.shape, F32),
            compiler_params=_params("parallel"),
            name="moe_finish",
        )(hm, pm, row(gp), wpg, wpp, row(gf))
    return finish(h, p, tm), finish(hs, ps, ns)


def _rope_tables(pos):
    half = ROT_DIM // 2
    inv = jnp.power(ROPE_THETA, -jnp.arange(half, dtype=F32) / half)
    ang = pos.astype(F32)[:, None] * inv[None, :]
    cos, sin = jnp.cos(ang), jnp.sin(ang)
    rest = HEAD_DIM - ROT_DIM
    n = pos.shape[0]
    return (jnp.concatenate([cos, cos, jnp.ones((n, rest), F32)], axis=1),
            jnp.concatenate([-sin, sin, jnp.zeros((n, rest), F32)], axis=1))


def _tile(n, pref):
    return pref if n % pref == 0 else n


def kernel(x_prompt, x_sample, state_conv, cache_k, cache_v, page_table, p_prompt, p_sample, norm_mix, norm_ffn, norm_ple, ple_w_gate, ple_w_proj, conv_w_pw1, conv_b_pw1, conv_w_dw, conv_b_dw, conv_ln_g, conv_ln_b, conv_w_pw2, conv_b_pw2, ffn_w_gate, ffn_w_up, ffn_w_down, attn_w_qkv, attn_w_o, moe_w_router, moe_w_gate, moe_w_up, moe_w_down, norm_final):
    bsz, seq, d = x_prompt.shape
    nb, dec_seq, _ = x_sample.shape
    assert dec_seq == 1 and norm_mix.shape[0] == 2
    nh = d // HEAD_DIM
    nkv = (attn_w_qkv.shape[2] // HEAD_DIM - nh) // 2
    n_tok = bsz * seq
    dff = ffn_w_gate.shape[2]
    psize = cache_k.shape[2]
    past_len = page_table.shape[1] * psize
    bf = lambda w: w.astype(BF16)

    cw = (norm_mix[0], bf(conv_w_pw1[0]), conv_b_pw1[0], conv_w_dw[0], conv_b_dw[0], conv_ln_g[0], conv_ln_b[0],
          bf(conv_w_pw2[0]), conv_b_pw2[0])
    hp, conv_p = _conv_prompt(x_prompt, *cw, tm=_tile(seq, 1024))
    hs, conv_s = _conv_sample(x_sample.reshape(nb, d), state_conv[0], *cw)
    fw = (norm_ffn[0], bf(ffn_w_gate[0]), bf(ffn_w_up[0]), bf(ffn_w_down[0]))
    pw0 = (norm_ple[0], bf(ple_w_gate[0]), bf(ple_w_proj[0]))
    tf = _tile(dff, 512)
    hp = _ffn(hp.reshape(n_tok, d), *fw, p_prompt[0].reshape(n_tok, -1), *pw0, tm=_tile(n_tok, 1024), tf=tf)
    hs = _ffn(hs, *fw, p_sample[0].reshape(nb, -1), *pw0, tm=nb, tf=tf)

    wqkv, wo = bf(attn_w_qkv[0]), bf(attn_w_o[0])
    cos_p, sin_p = _rope_tables(jnp.arange(seq, dtype=jnp.int32))
    qt, k_p, v_p, kb, vt, biast = _qkv_prompt(hp.reshape(bsz, seq, d), norm_mix[1], wqkv, cos_p, sin_p,
                                              tm=_tile(seq, 1024), nh=nh, nkv=nkv)
    hp = _attn_prompt(qt, kb, vt, biast, hp.reshape(bsz, seq, d), wo, nh=nh, nkv=nkv).reshape(n_tok, d)

    cos_s, sin_s = _rope_tables(jnp.full((1,), past_len, jnp.int32))
    qs, k_s, v_s = _qkv_sample(hs, norm_mix[1], wqkv, cos_s, sin_s, nh=nh, nkv=nkv)
    ck = cache_k[0].reshape(cache_k.shape[1], psize * nkv, HEAD_DIM)
    cv = cache_v[0].reshape(cache_v.shape[1], psize * nkv, HEAD_DIM)
    idx = _sample_select(ck, page_table, qs, nh=nh, nkv=nkv)[:, :, :TOP_K_BLOCKS]
    att_s = _sample_attn(ck, cv, page_table, idx, qs, k_s, v_s, nh=nh, nkv=nkv)
    hs = _proj_residual(att_s, hs, wo)

    mw = (norm_ffn[1], moe_w_router[0], bf(moe_w_gate[0]), bf(moe_w_up[0]), bf(moe_w_down[0]))
    pw1 = (norm_ple[1], bf(ple_w_gate[1]), bf(ple_w_proj[1]), norm_final)
    yp, ys = _moe(hp, hs, *mw, p_prompt[1].reshape(n_tok, -1), p_sample[1].reshape(nb, -1), *pw1,
                  tm=_tile(n_tok, 1024), tf=tf)

    return (yp.reshape(bsz, seq, d), ys.reshape(nb, 1, d),
            conv_p[None], conv_s[None],
            k_p.reshape(1, bsz, seq, nkv, HEAD_DIM), v_p.reshape(1, bsz, seq, nkv, HEAD_DIM),
            k_s.reshape(1, nb, 1, nkv, HEAD_DIM), v_s.reshape(1, nb, 1, nkv, HEAD_DIM))
```

```python
import functools

import jax
import jax.numpy as jnp
from jax import lax
from jax.experimental import pallas as pl
from jax.experimental.pallas import tpu as pltpu

F32 = jnp.float32
BF16 = jnp.bfloat16

EPS = 1e-6
HEAD_DIM = 128
ROT_DIM = HEAD_DIM // 4
ROPE_THETA = 500000.0
MOBA_BLOCK = 256
TOP_K_BLOCKS = 3
TOP_K_EXPERTS = 2
LANES = 128
SUBLANES = 8
MASK_VALUE = -1e30
VMEM_LIMIT = 56 * 1024 * 1024
EPILOGUE_ROWS = 256


def _params(*sem):
    return pltpu.CompilerParams(dimension_semantics=sem, vmem_limit_bytes=VMEM_LIMIT)


def _rms(x, g):
    r = lax.rsqrt(jnp.mean(x * x, axis=-1, keepdims=True) + EPS)
    return x * r * g


def _dot(a, b):
    return jnp.dot(a, b, preferred_element_type=F32)


def _dot_t(a, b, precision=None):
    return lax.dot_general(a, b, (((1,), (1,)), ((), ())), preferred_element_type=F32, precision=precision)


def _dot_t_split(a, b):
    ah, bh = a.astype(BF16), b.astype(BF16)
    al, bl = (a - ah.astype(F32)).astype(BF16), (b - bh.astype(F32)).astype(BF16)
    return _dot_t(ah, bh) + _dot_t(ah, bl) + _dot_t(al, bh)


def _silu(x):
    return x * jax.nn.sigmoid(x)


def _ple(h, p, g, w_gate, w_proj):
    gate = jax.nn.sigmoid(_dot(_rms(h, g).astype(BF16), w_gate))
    return h + gate * _dot(p.astype(BF16), w_proj)


def _rope_head(x, cos, sin, lane):
    half = ROT_DIM // 2
    partner = jnp.where(lane < half, pltpu.roll(x, HEAD_DIM - half, 1), pltpu.roll(x, half, 1))
    return x * cos + partner * sin


CONV_HALO = 32
CONV_CHUNK = 64
CONV_STRIP = 128
CONV_NORM_ROWS = 16


def _conv_prompt_kernel(x_ref, g_ref, w1_ref, b1_ref, wrep_ref, bdw_ref, lng_ref, lnb_ref, w2_ref, b2_ref,
                        out_ref, state_ref, upad_ref, y_ref, cacc_ref, *, tm, width, dc):
    t = pl.program_id(1)

    @pl.when(t == 0)
    def _():
        upad_ref[...] = jnp.zeros(upad_ref.shape, F32)

    x = x_ref[0]
    uu = _dot(_rms(x, g_ref[...]).astype(BF16), w1_ref[...]) + b1_ref[...]
    upad_ref[CONV_HALO:CONV_HALO + tm, :] = uu[:, :dc] * jax.nn.sigmoid(uu[:, dc:])

    first = CONV_HALO - (width - 1)
    span = CONV_CHUNK + SUBLANES

    def chunk(i, carry):
        base = pl.multiple_of(i * CONV_CHUNK, CONV_CHUNK)
        for c0 in range(0, dc, CONV_STRIP):
            lanes = slice(c0, c0 + CONV_STRIP)
            acc = jnp.broadcast_to(bdw_ref[:, lanes], (CONV_CHUNK, CONV_STRIP))
            for r in range(SUBLANES):
                part = None
                for k in range(width):
                    a, kr = divmod(first + k, SUBLANES)
                    if kr != r:
                        continue
                    w = wrep_ref[k * SUBLANES:(k + 1) * SUBLANES, lanes]
                    term = upad_ref[pl.ds(base + a * SUBLANES, span), lanes] * jnp.concatenate(
                        [w] * (span // SUBLANES), axis=0)
                    part = term if part is None else part + term
                if part is not None:
                    acc = acc + part[r:r + CONV_CHUNK, :]
            cacc_ref[:, lanes] = acc
        for r0 in range(0, CONV_CHUNK, CONV_NORM_ROWS):
            acc = cacc_ref[r0:r0 + CONV_NORM_ROWS, :]
            mu = jnp.mean(acc, axis=-1, keepdims=True)
            d = acc - mu
            var = jnp.mean(d * d, axis=-1, keepdims=True)
            z = d * lax.rsqrt(var + EPS) * lng_ref[...] + lnb_ref[...]
            y_ref[pl.ds(base + r0, CONV_NORM_ROWS), :] = _silu(z).astype(BF16)
        return carry

    lax.fori_loop(0, tm // CONV_CHUNK, chunk, 0)
    out_ref[0] = x + _dot(y_ref[...], w2_ref[...]) + b2_ref[...]
    state_ref[0] = upad_ref[tm + first:tm + CONV_HALO, :]
    upad_ref[0:CONV_HALO, :] = upad_ref[tm:tm + CONV_HALO, :]


def _conv_prompt(x, g, w1, b1, wdw, bdw, lng, lnb, w2, b2, *, tm):
    bsz, seq, d = x.shape
    width, dc = wdw.shape
    assert seq % tm == 0 and tm % CONV_CHUNK == 0 and width - 1 <= CONV_HALO
    row = lambda a: a.reshape(1, -1)
    full = lambda a: pl.BlockSpec(a.shape, lambda b, t: (0,) * a.ndim)
    wrep = jnp.repeat(wdw, SUBLANES, axis=0)
    args = (row(g), w1, row(b1), wrep, row(bdw), row(lng), row(lnb), w2, row(b2))
    return pl.pallas_call(
        functools.partial(_conv_prompt_kernel, tm=tm, width=width, dc=dc),
        grid=(bsz, seq // tm),
        in_specs=[pl.BlockSpec((1, tm, d), lambda b, t: (b, t, 0))] + [full(a) for a in args],
        out_specs=[pl.BlockSpec((1, tm, d), lambda b, t: (b, t, 0)),
                   pl.BlockSpec((1, width - 1, dc), lambda b, t: (b, 0, 0))],
        out_shape=[jax.ShapeDtypeStruct((bsz, seq, d), F32),
                   jax.ShapeDtypeStruct((bsz, width - 1, dc), F32)],
        scratch_shapes=[pltpu.VMEM((tm + CONV_HALO + SUBLANES, dc), F32), pltpu.VMEM((tm, dc), BF16),
                        pltpu.VMEM((CONV_CHUNK, dc), F32)],
        compiler_params=_params("parallel", "arbitrary"),
        name="conv_prompt",
    )(x, *args)


def _conv_sample_kernel(x_ref, st_ref, g_ref, w1_ref, b1_ref, wdw_ref, bdw_ref, lng_ref, lnb_ref, w2_ref, b2_ref,
                        out_ref, state_ref, *, width, dc):
    nb = x_ref.shape[0]
    x = x_ref[...]
    uu = _dot(_rms(x, g_ref[...]).astype(BF16), w1_ref[...]) + b1_ref[...]
    u = uu[:, :dc] * jax.nn.sigmoid(uu[:, dc:])
    st = st_ref[...]
    acc = jnp.sum(st * wdw_ref[0:width - 1, :][None], axis=1) + u * wdw_ref[width - 1:width, :] + bdw_ref[...]
    mu = jnp.mean(acc, axis=-1, keepdims=True)
    d = acc - mu
    var = jnp.mean(d * d, axis=-1, keepdims=True)
    z = d * lax.rsqrt(var + EPS) * lng_ref[...] + lnb_ref[...]
    out_ref[...] = x + _dot(_silu(z).astype(BF16), w2_ref[...]) + b2_ref[...]
    state_ref[:, 0:width - 2, :] = st_ref[:, 1:width - 1, :]
    for b in range(nb):
        state_ref[b, width - 2:width - 1, :] = u[b:b + 1, :]


def _conv_sample(x, st, g, w1, b1, wdw, bdw, lng, lnb, w2, b2):
    nb, d = x.shape
    width, dc = wdw.shape
    row = lambda a: a.reshape(1, -1)
    return pl.pallas_call(
        functools.partial(_conv_sample_kernel, width=width, dc=dc),
        out_shape=[jax.ShapeDtypeStruct((nb, d), F32), jax.ShapeDtypeStruct((nb, width - 1, dc), F32)],
        compiler_params=pltpu.CompilerParams(vmem_limit_bytes=VMEM_LIMIT),
        name="conv_sample",
    )(x, st, row(g), w1, row(b1), wdw, row(bdw), row(lng), row(lnb), w2, row(b2))


def _ffn_kernel(x_ref, g_ref, wg_ref, wu_ref, wd_ref, p_ref, gp_ref, wpg_ref, wpp_ref, out_ref, xn_ref, acc_ref,
                *, tf):
    xn_ref[...] = _rms(x_ref[...], g_ref[...]).astype(BF16)
    acc_ref[...] = jnp.zeros(acc_ref.shape, F32)

    def ffn(fc, carry):
        f0 = pl.multiple_of(fc * tf, tf)
        xn = xn_ref[...]
        a = (_silu(_dot(xn, wg_ref[:, pl.ds(f0, tf)])) * _dot(xn, wu_ref[:, pl.ds(f0, tf)])).astype(BF16)
        acc_ref[...] += _dot(a, wd_ref[pl.ds(f0, tf), :])
        return carry
    lax.fori_loop(0, wg_ref.shape[1] // tf, ffn, 0)

    tm = x_ref.shape[0]
    step = min(EPILOGUE_ROWS, tm)
    for r in range(0, tm, step):
        rows = slice(r, r + step)
        out_ref[rows, :] = _ple(x_ref[rows, :] + acc_ref[rows, :], p_ref[rows, :], gp_ref[...], wpg_ref[...],
                                wpp_ref[...])


def _layer_rows(p, layer, tm):
    return pl.BlockSpec((None, tm, p.shape[2]), lambda i: (layer, i, 0))


def _ffn(x, g, wg, wu, wd, p, gp, wpg, wpp, *, layer, tm, tf):
    n, d = x.shape
    dff = wg.shape[1]
    assert n % tm == 0 and dff % tf == 0
    row = lambda a: a.reshape(1, -1)
    full = lambda a: pl.BlockSpec(a.shape, lambda i: (0,) * a.ndim)
    resident = lambda a: pl.BlockSpec(a.shape, lambda i: (0,) * a.ndim, pipeline_mode=pl.Buffered(1))
    return pl.pallas_call(
        functools.partial(_ffn_kernel, tf=tf),
        grid=(n // tm,),
        in_specs=[pl.BlockSpec((tm, d), lambda i: (i, 0)), full(row(g)), resident(wg), resident(wu), resident(wd),
                  _layer_rows(p, layer, tm), full(row(gp)), full(wpg), full(wpp)],
        out_specs=pl.BlockSpec((tm, d), lambda i: (i, 0)),
        out_shape=jax.ShapeDtypeStruct((n, d), F32),
        scratch_shapes=[pltpu.VMEM((tm, d), BF16), pltpu.VMEM((tm, d), F32)],
        compiler_params=_params("parallel"),
        name="ffn_ple",
    )(x, row(g), wg, wu, wd, p, row(gp), wpg, wpp)


def _qkv_prompt_kernel(x_ref, g_ref, w_ref, cos_ref, sin_ref,
                       qt_ref, k_ref, v_ref, kb_ref, vt_ref, biast_ref, kmt_ref, *, tm, nh, nkv, nblk):
    t = pl.program_id(1)
    group = nh // nkv
    nq = nh * HEAD_DIM
    nk = nkv * HEAD_DIM

    @pl.when(t == 0)
    def _():
        kmt_ref[...] = jnp.zeros(kmt_ref.shape, F32)

    qkv = _dot(_rms(x_ref[0], g_ref[...]).astype(BF16), w_ref[...])
    cos = cos_ref[...]
    sin = sin_ref[...]
    lane = lax.broadcasted_iota(jnp.int32, (tm, HEAD_DIM), 1)
    scale = HEAD_DIM ** -0.5
    q_heads = [_rope_head(qkv[:, h * HEAD_DIM:(h + 1) * HEAD_DIM], cos, sin, lane) * scale for h in range(nh)]
    q = jnp.concatenate(q_heads, axis=1)
    k = jnp.concatenate([_rope_head(qkv[:, nq + h * HEAD_DIM:nq + (h + 1) * HEAD_DIM], cos, sin, lane)
                         for h in range(nkv)], axis=1)
    v = qkv[:, nq + nk:]
    for h in range(nh):
        qt_ref[0, h] = q_heads[h].T.astype(BF16)
    for kv in range(nkv):
        k_ref[0, pl.ds(kv, tm, stride=nkv), :] = k[:, kv * HEAD_DIM:(kv + 1) * HEAD_DIM]
        v_ref[0, pl.ds(kv, tm, stride=nkv), :] = v[:, kv * HEAD_DIM:(kv + 1) * HEAD_DIM]
        vt_ref[0, kv] = v[:, kv * HEAD_DIM:(kv + 1) * HEAD_DIM].T.astype(BF16)
    kb_ref[0] = k.astype(BF16)

    kmt = kmt_ref[...]
    kmt_row = lax.broadcasted_iota(jnp.int32, kmt.shape, 0)
    kmt_head = lax.broadcasted_iota(jnp.int32, kmt.shape, 1) // HEAD_DIM
    for i in range(tm // MOBA_BLOCK):
        km = jnp.mean(k[i * MOBA_BLOCK:(i + 1) * MOBA_BLOCK, :], axis=0, keepdims=True)
        km = jnp.concatenate([km[:, (h // group) * HEAD_DIM:(h // group + 1) * HEAD_DIM] for h in range(nh)], axis=1)
        n = t * (tm // MOBA_BLOCK) + i
        kmt = jnp.where(kmt_row == kmt_head * nblk + n, km, kmt)
    kmt_ref[...] = kmt

    s = _dot_t_split(q, kmt).T.reshape(nh, nblk, tm)
    n_idx = lax.broadcasted_iota(jnp.int32, s.shape, 1)
    own = (t * tm + lax.broadcasted_iota(jnp.int32, s.shape, 2)) // MOBA_BLOCK
    past = n_idx < own
    s = jnp.where(past, s, -jnp.inf)
    rank = jnp.zeros(s.shape, F32)
    for dlt in range(1, nblk):
        wrap = n_idx + dlt >= nblk
        other = jnp.concatenate([s[:, dlt:, :], s[:, :dlt, :]], axis=1)
        rank = rank + jnp.where(other > s, 1.0, 0.0) + jnp.where(wrap, jnp.where(other == s, 1.0, 0.0), 0.0)
    bias = jnp.where(past, jnp.where(rank < TOP_K_BLOCKS, 0.0, MASK_VALUE), MASK_VALUE)
    biast_ref[0] = bias.reshape(nh * nblk, tm).astype(BF16)


def _qkv_prompt(x, g, w, cos, sin, *, tm, nh, nkv):
    bsz, seq, d = x.shape
    nq, nk = nh * HEAD_DIM, nkv * HEAD_DIM
    nblk = seq // MOBA_BLOCK
    assert seq % tm == 0 and tm % MOBA_BLOCK == 0 and nh * nblk == LANES
    tok = lambda c: pl.BlockSpec((1, tm, c), lambda b, t: (b, t, 0))
    kv_rows = pl.BlockSpec((1, tm * nkv, HEAD_DIM), lambda b, t: (b, t, 0))
    full = lambda a: pl.BlockSpec(a.shape, lambda b, t: (0,) * a.ndim)
    g = g.reshape(1, -1)
    return pl.pallas_call(
        functools.partial(_qkv_prompt_kernel, tm=tm, nh=nh, nkv=nkv, nblk=nblk),
        grid=(bsz, seq // tm),
        in_specs=[tok(d), full(g), full(w),
                  pl.BlockSpec((tm, HEAD_DIM), lambda b, t: (t, 0)), pl.BlockSpec((tm, HEAD_DIM), lambda b, t: (t, 0))],
        out_specs=[pl.BlockSpec((1, nh, HEAD_DIM, tm), lambda b, t: (b, 0, 0, t)), kv_rows, kv_rows, tok(nk),
                   pl.BlockSpec((1, nkv, HEAD_DIM, tm), lambda b, t: (b, 0, 0, t)),
                   pl.BlockSpec((1, nh * nblk, tm), lambda b, t: (b, 0, t))],
        out_shape=[jax.ShapeDtypeStruct((bsz, nh, HEAD_DIM, seq), BF16),
                   jax.ShapeDtypeStruct((bsz, seq * nkv, HEAD_DIM), F32),
                   jax.ShapeDtypeStruct((bsz, seq * nkv, HEAD_DIM), F32),
                   jax.ShapeDtypeStruct((bsz, seq, nk), BF16),
                   jax.ShapeDtypeStruct((bsz, nkv, HEAD_DIM, seq), BF16),
                   jax.ShapeDtypeStruct((bsz, nh * nblk, seq), BF16)],
        scratch_shapes=[pltpu.VMEM((nh * nblk, nq), F32)],
        compiler_params=_params("parallel", "arbitrary"),
        name="qkv_prompt",
    )(x, g, w, cos, sin)


def _qkv_sample_kernel(x_ref, g_ref, w_ref, cos_ref, sin_ref, q_ref, k_ref, v_ref, *, nh, nkv):
    nb = x_ref.shape[0]
    nq = nh * HEAD_DIM
    nk = nkv * HEAD_DIM
    qkv = _dot(_rms(x_ref[...], g_ref[...]).astype(BF16), w_ref[...])
    cos = cos_ref[...]
    sin = sin_ref[...]
    lane = lax.broadcasted_iota(jnp.int32, (nb, HEAD_DIM), 1)
    scale = HEAD_DIM ** -0.5
    q_ref[...] = jnp.concatenate([_rope_head(qkv[:, h * HEAD_DIM:(h + 1) * HEAD_DIM], cos, sin, lane) * scale
                                  for h in range(nh)], axis=1)
    k_ref[...] = jnp.concatenate([_rope_head(qkv[:, nq + h * HEAD_DIM:nq + (h + 1) * HEAD_DIM], cos, sin, lane)
                                  for h in range(nkv)], axis=1)
    v_ref[...] = qkv[:, nq + nk:]


def _qkv_sample(x, g, w, cos, sin, *, nh, nkv):
    nb = x.shape[0]
    nq, nk = nh * HEAD_DIM, nkv * HEAD_DIM
    return pl.pallas_call(
        functools.partial(_qkv_sample_kernel, nh=nh, nkv=nkv),
        out_shape=[jax.ShapeDtypeStruct((nb, nq), F32), jax.ShapeDtypeStruct((nb, nk), F32),
                   jax.ShapeDtypeStruct((nb, nk), F32)],
        compiler_params=pltpu.CompilerParams(vmem_limit_bytes=VMEM_LIMIT),
        name="qkv_sample",
    )(x, g.reshape(1, -1), w, cos, sin)


ATTN_BLOCKS_PER_PASS = (4, 2, 1)


def _attn_prompt_kernel(qt_ref, kb_ref, vt_ref, biast_ref, h_ref, wo_ref, out_ref, *, nh, nkv, nblk):
    o = pl.program_id(1)
    hk = pl.program_id(2)
    group = nh // nkv
    blk = MOBA_BLOCK
    bias = biast_ref[0]
    feat_head = lax.broadcasted_iota(jnp.int32, bias.shape, 0) // nblk
    q_t = jnp.concatenate([qt_ref[0, g] for g in range(group)], axis=1)
    q_aug = jnp.concatenate(
        [q_t, jnp.concatenate([jnp.where(feat_head == hk * group + g, bias, jnp.zeros_like(bias))
                               for g in range(group)], axis=1)], axis=0)
    cols = group * blk

    own0 = pl.multiple_of(o * blk, blk)
    s = _dot(kb_ref[0, pl.ds(own0, blk), :], q_t)
    kpos = lax.broadcasted_iota(jnp.int32, (blk, cols), 0)
    qpos = lax.broadcasted_iota(jnp.int32, (blk, cols), 1) % blk
    s = jnp.where(kpos <= qpos, s, MASK_VALUE)
    m = jnp.max(s, axis=0, keepdims=True)
    p = jnp.exp(s - m)
    l = jnp.sum(p, axis=0, keepdims=True)
    acc = _dot(vt_ref[0, 0, :, pl.ds(own0, blk)], p.astype(BF16))
    def past(first, nb_, carry):
        m, l, acc = carry
        keys = nb_ * blk
        j0 = pl.multiple_of(first * blk, keys)
        lane = lax.broadcasted_iota(jnp.int32, (keys, nh * nblk), 1)
        key_blk = first + lax.broadcasted_iota(jnp.int32, (keys, nh * nblk), 0) // blk
        pick = jnp.where(lane % nblk == key_blk, 1.0, 0.0).astype(BF16)
        s = _dot(jnp.concatenate([kb_ref[0, pl.ds(j0, keys), :], pick], axis=1), q_aug)
        m_new = jnp.maximum(m, jnp.max(s, axis=0, keepdims=True))
        alpha = jnp.exp(m - m_new)
        p = jnp.exp(s - m_new)
        l = alpha * l + jnp.sum(p, axis=0, keepdims=True)
        acc = alpha * acc + _dot(vt_ref[0, 0, :, pl.ds(j0, keys)], p.astype(BF16))
        return m_new, l, acc

    state, done = (m, l, acc), 0
    for nb_ in ATTN_BLOCKS_PER_PASS:
        cnt = (o - done) // nb_
        state = lax.fori_loop(0, cnt, lambda i, c, nb_=nb_, done=done: past(done + i * nb_, nb_, c), state)
        done = done + cnt * nb_
    m, l, acc = state
    att = acc / l
    att = jnp.concatenate([att[:, g * blk:(g + 1) * blk].T for g in range(group)], axis=1).astype(BF16)
    proj = _dot(att, wo_ref[...])

    @pl.when(hk == 0)
    def _():
        out_ref[0] = h_ref[0] + proj

    @pl.when(hk != 0)
    def _():
        out_ref[0] += proj


def _attn_prompt(qt, kb, vt, biast, h, wo, *, nh, nkv):
    bsz, seq, d = h.shape
    nblk = seq // MOBA_BLOCK
    group = nh // nkv
    gw = group * HEAD_DIM
    return pl.pallas_call(
        functools.partial(_attn_prompt_kernel, nh=nh, nkv=nkv, nblk=nblk),
        grid=(bsz, nblk, nkv),
        in_specs=[pl.BlockSpec((1, group, HEAD_DIM, MOBA_BLOCK), lambda b, o, k: (b, k, 0, o)),
                  pl.BlockSpec((1, seq, HEAD_DIM), lambda b, o, k: (b, 0, k)),
                  pl.BlockSpec((1, 1, HEAD_DIM, seq), lambda b, o, k: (b, k, 0, 0)),
                  pl.BlockSpec((1, biast.shape[1], MOBA_BLOCK), lambda b, o, k: (b, 0, o)),
                  pl.BlockSpec((1, MOBA_BLOCK, d), lambda b, o, k: (b, o, 0)),
                  pl.BlockSpec((gw, d), lambda b, o, k: (k, 0))],
        out_specs=pl.BlockSpec((1, MOBA_BLOCK, d), lambda b, o, k: (b, o, 0)),
        out_shape=jax.ShapeDtypeStruct((bsz, seq, d), F32),
        compiler_params=_params("parallel", "parallel", "arbitrary"),
        name="attn_prompt",
    )(qt, kb, vt, biast, h, wo)


PAGES_PER_STEP = 64


def _sample_select_kernel(pt_ref, *refs, nh, nkv, ppb, nblk):
    pages = refs[:PAGES_PER_STEP]
    q_ref, idx_ref, km_ref = refs[PAGES_PER_STEP:]
    s = pl.program_id(1)
    group = nh // nkv
    bps = PAGES_PER_STEP // ppb
    psize = pages[0].shape[1] // nkv

    @pl.when(s == 0)
    def _():
        km_ref[...] = jnp.zeros(km_ref.shape, F32)

    km = km_ref[...]
    km_row = lax.broadcasted_iota(jnp.int32, km.shape, 0)
    for i in range(bps):
        tot = jnp.concatenate(
            [sum(jnp.sum(pages[i * ppb + j][0, pl.ds(kv, psize, stride=nkv), :], axis=0, keepdims=True)
                 for j in range(ppb)) for kv in range(nkv)], axis=1)
        km = jnp.where(km_row == s * bps + i, tot / (psize * ppb), km)
    km_ref[...] = km

    @pl.when(s == pl.num_programs(1) - 1)
    def _():
        km = km_ref[...]
        q = q_ref[0]
        sc = jnp.concatenate(
            [_dot_t(q[kv * group:(kv + 1) * group, :], km[:, kv * HEAD_DIM:(kv + 1) * HEAD_DIM],
                    precision=lax.Precision.HIGHEST) for kv in range(nkv)], axis=0)
        col = lax.broadcasted_iota(jnp.int32, sc.shape, 1)
        lane = lax.broadcasted_iota(jnp.int32, (nh, LANES), 1)
        out = jnp.zeros((nh, LANES), jnp.int32)
        for r in range(TOP_K_BLOCKS):
            best = jnp.max(sc, axis=-1, keepdims=True)
            pick = jnp.min(jnp.where(sc == best, col, nblk), axis=-1, keepdims=True)
            out = jnp.where(lane == r, pick, out)
            sc = jnp.where(col == pick, -jnp.inf, sc)
        idx_ref[0] = out


def _sample_select(cache_k, page_table, q, *, nh, nkv):
    npool, prow, hd = cache_k.shape
    nb, npages = page_table.shape
    ppb = MOBA_BLOCK // (prow // nkv)
    nblk = npages // ppb
    assert npages % PAGES_PER_STEP == 0 and PAGES_PER_STEP % ppb == 0 and nblk >= TOP_K_BLOCKS
    page_spec = lambda j: pl.BlockSpec((1, prow, hd), lambda b, s, pt: (pt[b * npages + s * PAGES_PER_STEP + j], 0, 0))
    grid_spec = pltpu.PrefetchScalarGridSpec(
        num_scalar_prefetch=1,
        grid=(nb, npages // PAGES_PER_STEP),
        in_specs=[page_spec(j) for j in range(PAGES_PER_STEP)]
        + [pl.BlockSpec((1, nh, HEAD_DIM), lambda b, s, pt: (b, 0, 0))],
        out_specs=pl.BlockSpec((1, nh, LANES), lambda b, s, pt: (b, 0, 0)),
        scratch_shapes=[pltpu.VMEM((nblk, nkv * hd), F32)],
    )
    return pl.pallas_call(
        functools.partial(_sample_select_kernel, nh=nh, nkv=nkv, ppb=ppb, nblk=nblk),
        grid_spec=grid_spec,
        out_shape=jax.ShapeDtypeStruct((nb, nh, LANES), jnp.int32),
        compiler_params=_params("parallel", "arbitrary"),
        name="sample_select",
    )(page_table.reshape(-1), *([cache_k] * PAGES_PER_STEP), q.reshape(nb, nh, HEAD_DIM))


def _sample_attn_kernel(pt_ref, idx_ref, q_ref, kn_ref, vn_ref, *refs, group, nkv, npg):
    kpages = refs[:group * npg]
    vpages = refs[group * npg:2 * group * npg]
    out_ref = refs[2 * group * npg]
    hkv = pl.program_id(1)
    prow = kpages[0].shape[1]
    mine = lax.broadcasted_iota(jnp.int32, (prow, 1), 0) % nkv == hkv
    outs = []
    for g in range(group):
        q = q_ref[0, g:g + 1, :]
        m = jnp.sum(q * kn_ref[0], axis=-1, keepdims=True)
        l = jnp.ones((1, 1), F32)
        acc = vn_ref[0]
        for i in range(npg):
            k = kpages[g * npg + i][0]
            sc = jnp.where(mine, jnp.sum(k * q, axis=-1, keepdims=True), MASK_VALUE)
            m_new = jnp.maximum(m, jnp.max(sc, axis=0, keepdims=True))
            alpha = jnp.exp(m - m_new)
            p = jnp.exp(sc - m_new)
            l = alpha * l + jnp.sum(p, axis=0, keepdims=True)
            acc = alpha * acc + jnp.sum(p * vpages[g * npg + i][0], axis=0, keepdims=True)
            m = m_new
        outs.append(acc / l)
    out_ref[0] = jnp.concatenate(outs, axis=0)


def _sample_attn(cache_k, cache_v, page_table, idx, q, kn, vn, *, nh, nkv):
    npool, prow, hd = cache_k.shape
    nb, npages = page_table.shape
    ppb = MOBA_BLOCK // (prow // nkv)
    group = nh // nkv
    nsel = TOP_K_BLOCKS
    npg = nsel * ppb

    def page_spec(g, i):
        def imap(b, kv, pt, ix):
            blk = ix[(b * nh + kv * group + g) * nsel + i // ppb]
            return (pt[b * npages + blk * ppb + i % ppb], 0, 0)
        return pl.BlockSpec((1, prow, hd), imap)

    qspec = pl.BlockSpec((1, group, HEAD_DIM), lambda b, kv, pt, ix: (b * nkv + kv, 0, 0))
    kvspec = pl.BlockSpec((1, 1, HEAD_DIM), lambda b, kv, pt, ix: (b * nkv + kv, 0, 0))
    pages = [page_spec(g, i) for g in range(group) for i in range(npg)]
    grid_spec = pltpu.PrefetchScalarGridSpec(
        num_scalar_prefetch=2,
        grid=(nb, nkv),
        in_specs=[qspec, kvspec, kvspec] + pages + pages,
        out_specs=qspec,
    )
    out = pl.pallas_call(
        functools.partial(_sample_attn_kernel, group=group, nkv=nkv, npg=npg),
        grid_spec=grid_spec,
        out_shape=jax.ShapeDtypeStruct((nb * nkv, group, HEAD_DIM), F32),
        compiler_params=_params("parallel", "parallel"),
        name="sample_attn",
    )(page_table.reshape(-1), idx.reshape(-1),
      q.reshape(nb * nkv, group, HEAD_DIM), kn.reshape(nb * nkv, 1, HEAD_DIM), vn.reshape(nb * nkv, 1, HEAD_DIM),
      *([cache_k] * (group * npg)), *([cache_v] * (group * npg)))
    return out.reshape(nb, nh * HEAD_DIM)


def _proj_residual_kernel(a_ref, h_ref, w_ref, out_ref):
    out_ref[...] = h_ref[...] + _dot(a_ref[...].astype(BF16), w_ref[...])


def _proj_residual(a, h, w):
    return pl.pallas_call(
        _proj_residual_kernel,
        out_shape=jax.ShapeDtypeStruct(h.shape, F32),
        compiler_params=pltpu.CompilerParams(vmem_limit_bytes=VMEM_LIMIT),
        name="proj_residual",
    )(a, h, w)


MXU_DEPTH = 256
MOE_ROW_GRAN = 32
MOE_VARIANTS = 16
MOE_SPECIALISED = (-96, 160)


def _router_kernel(x_ref, g_ref, wrt_ref, tri_ref, xn_ref, gate_ref, rank_ref, cnt_ref):
    xn = _rms(x_ref[...], g_ref[...])
    xn_ref[...] = xn.astype(BF16)
    logits = _dot_t_split(wrt_ref[...], xn)
    ne = logits.shape[0]
    e = jnp.exp(logits - jnp.max(logits, axis=0, keepdims=True))
    probs = e / jnp.sum(e, axis=0, keepdims=True)
    eid = lax.broadcasted_iota(jnp.int32, probs.shape, 0)
    rest = probs
    member = jnp.zeros(probs.shape, jnp.bool_)
    top_sum = jnp.zeros((1, probs.shape[1]), F32)
    for _ in range(TOP_K_EXPERTS):
        best = jnp.max(rest, axis=0, keepdims=True)
        pick = eid == jnp.min(jnp.where(rest == best, eid, ne), axis=0, keepdims=True)
        member = member | pick
        top_sum = top_sum + best
        rest = jnp.where(pick, -1.0, rest)
    gate_ref[...] = jnp.where(member, probs / top_sum, 0.0)
    mem = jnp.where(member, 1.0, 0.0)
    rank = _dot(mem.astype(BF16), tri_ref[...])
    rank_ref[...] = jnp.where(member, rank, -1.0)
    cnt_ref[0] = jnp.broadcast_to(jnp.sum(mem, axis=1, keepdims=True), cnt_ref.shape[1:])


def _router(x, g, wr, *, tm):
    n, d = x.shape
    ne = wr.shape[1]
    nt = n // tm
    assert n % tm == 0
    tri = (lax.broadcasted_iota(jnp.int32, (tm, tm), 0) < lax.broadcasted_iota(jnp.int32, (tm, tm), 1)).astype(BF16)
    full = lambda a: pl.BlockSpec(a.shape, lambda i: (0,) * a.ndim)
    g = g.reshape(1, -1)
    wrt = wr.T
    xn, gate, rank, cnt = pl.pallas_call(
        _router_kernel,
        grid=(nt,),
        in_specs=[pl.BlockSpec((tm, d), lambda i: (i, 0)), full(g), full(wrt), full(tri)],
        out_specs=[pl.BlockSpec((tm, d), lambda i: (i, 0)), pl.BlockSpec((ne, tm), lambda i: (0, i)),
                   pl.BlockSpec((ne, tm), lambda i: (0, i)), pl.BlockSpec((1, ne, LANES), lambda i: (i, 0, 0))],
        out_shape=[jax.ShapeDtypeStruct((n, d), BF16), jax.ShapeDtypeStruct((ne, n), F32),
                   jax.ShapeDtypeStruct((ne, n), F32), jax.ShapeDtypeStruct((nt, ne, LANES), F32)],
        compiler_params=_params("parallel"),
        name="router",
    )(x, g, wrt, tri)
    return xn, gate, rank, cnt[:, :, 0].astype(jnp.int32).reshape(-1)


def _moe_kernel(eid_ref, cnt_ref, cnts_ref, xn_ref, gate_ref, rank_ref, wg_ref, wu_ref, wd_ref, hin_ref,
                xns_ref, gates_ref, ranks_ref, hins_ref, out_ref, outs_ref, xc_ref, y_ref, *, tm, tf, ne):
    e = eid_ref[0]
    i = pl.program_id(0)
    weights = (wg_ref, wu_ref, wd_ref)
    _expert_rows(e, cnt_ref[i * ne + e], xn_ref, gate_ref, rank_ref, hin_ref, out_ref, weights, xc_ref, y_ref,
                 tm=tm, tf=tf, ne=ne)

    @pl.when(i == pl.num_programs(0) - 1)
    def _():
        _expert_rows(e, cnts_ref[e], xns_ref, gates_ref, ranks_ref, hins_ref, outs_ref, weights, xc_ref, y_ref,
                     tm=xns_ref.shape[0], tf=tf, ne=ne)


def _moe_groups(tm):
    gran = min(MOE_ROW_GRAN, tm)
    return gran, min(MOE_VARIANTS, tm // gran)


def _expert_rows(e, cnt, xn_ref, gate_ref, rank_ref, hin_ref, out_ref, weights, xc_ref, y_ref, *, tm, tf, ne):
    wg_ref, wu_ref, wd_ref = weights
    gran, nvar = _moe_groups(tm)
    ngrp = (cnt + gran - 1) // gran
    rank = rank_ref[pl.ds(e, 1), :]
    kq = min(MXU_DEPTH, tm)

    def onehot(r0, rows):
        slot = lax.broadcasted_iota(jnp.int32, (rows, tm), 0) + r0
        return slot.astype(F32) == rank

    def process(r0, rows, base_ref):
        krows = -(-rows // kq) * kq
        sel = jnp.where(onehot(r0, rows), 1.0, 0.0).astype(BF16)
        xc_ref[0:rows, :] = _dot(sel, xn_ref[...]).astype(BF16)
        y_ref[0:krows, :] = jnp.zeros((krows, y_ref.shape[1]), F32)

        def ffn(fc, carry):
            xc = xc_ref[0:rows, :]
            f0 = pl.multiple_of(fc * tf, tf)
            a = (_silu(_dot(xc, wg_ref[0, :, pl.ds(f0, tf)])) * _dot(xc, wu_ref[0, :, pl.ds(f0, tf)])).astype(BF16)
            y_ref[0:rows, :] += _dot(a, wd_ref[0, pl.ds(f0, tf), :])
            return carry
        lax.fori_loop(0, wg_ref.shape[2] // tf, ffn, 0)
        w = jnp.where(onehot(r0, krows), gate_ref[pl.ds(e, 1), :], 0.0).astype(BF16)
        out_ref[...] = base_ref[...] + lax.dot_general(w, y_ref[0:krows, :].astype(BF16), (((0,), (0,)), ((), ())),
                                                       preferred_element_type=F32)

    @pl.when(ngrp == 0)
    def _():
        out_ref[...] = hin_ref[...]

    balanced = tm * TOP_K_EXPERTS // ne
    sizes = range(max(1, (balanced + MOE_SPECIALISED[0]) // gran), min(nvar, (balanced + MOE_SPECIALISED[1]) // gran) + 1)
    for n in sizes:
        @pl.when(ngrp == n)
        def _(n=n):
            process(0, n * gran, hin_ref)

    common = functools.reduce(lambda a, n: a | (ngrp == n), sizes, ngrp == 0)

    @pl.when(jnp.logical_not(common))
    def _():
        full = nvar * gran
        out_ref[...] = hin_ref[...]

        def body(c, carry):
            process(c * full, full, out_ref)
            return carry
        lax.fori_loop(0, (cnt + full - 1) // full, body, 0)


def _moe_finish_kernel(h_ref, p_ref, gp_ref, wpg_ref, wpp_ref, gf_ref, out_ref):
    tm = h_ref.shape[0]
    step = min(EPILOGUE_ROWS, tm)
    for r in range(0, tm, step):
        rows = slice(r, r + step)
        out_ref[rows, :] = _rms(_ple(h_ref[rows, :], p_ref[rows, :], gp_ref[...], wpg_ref[...], wpp_ref[...]),
                                gf_ref[...])


def _moe(h, hs, g, wr, wg, wu, wd, p, ps, gp, wpg, wpp, gf, *, layer, tm, tf):
    n, d = h.shape
    ns = hs.shape[0]
    ne, _, dff = wg.shape
    nt = n // tm
    for rows in (tm, ns):
        gran, nvar = _moe_groups(rows)
        assert rows % (nvar * gran) == 0 and (nvar * gran) % min(MXU_DEPTH, rows) == 0
    assert n % tm == 0 and dff % tf == 0 and ns <= tm
    xn, gate, rank, cnt = _router(h, g, wr, tm=tm)
    xns, gates, ranks, cnts = _router(hs, g, wr, tm=ns)
    resident = lambda a: pl.BlockSpec((1,) + a.shape[1:], lambda i, eid, c, cs: (eid[0], 0, 0),
                                      pipeline_mode=pl.Buffered(1))
    tile = lambda c: pl.BlockSpec((tm, c), lambda i, eid, c_, cs: (i, 0))
    lanes = pl.BlockSpec((ne, tm), lambda i, eid, c, cs: (0, i))
    whole = lambda a: pl.BlockSpec(a.shape, lambda i, eid, c, cs: (0,) * a.ndim)
    full = max(g * v for g, v in (_moe_groups(tm), _moe_groups(ns)))
    grid_spec = pltpu.PrefetchScalarGridSpec(
        num_scalar_prefetch=3,
        grid=(nt,),
        in_specs=[tile(d), lanes, lanes, resident(wg), resident(wu), resident(wd), tile(d),
                  whole(xns), whole(gates), whole(ranks), whole(hs)],
        out_specs=[tile(d), whole(hs)],
        scratch_shapes=[pltpu.VMEM((full, d), BF16), pltpu.VMEM((full, d), F32)],
    )
    one_expert = pl.pallas_call(
        functools.partial(_moe_kernel, tm=tm, tf=tf, ne=ne),
        grid_spec=grid_spec,
        out_shape=[jax.ShapeDtypeStruct((n, d), F32), jax.ShapeDtypeStruct((ns, d), F32)],
        input_output_aliases={9: 0, 13: 1},
        compiler_params=_params("arbitrary"),
        name="moe_expert",
    )
    for e in range(ne):
        h, hs = one_expert(jnp.full((1,), e, jnp.int32), cnt, cnts, xn, gate, rank, wg, wu, wd, h, xns, gates, ranks, hs)

    row = lambda a: a.reshape(1, -1)

    def finish(hm, pm, rows):
        whole1 = lambda a: pl.BlockSpec(a.shape, lambda i: (0,) * a.ndim)
        return pl.pallas_call(
            _moe_finish_kernel,
            grid=(hm.shape[0] // rows,),
            in_specs=[pl.BlockSpec((rows, d), lambda i: (i, 0)), _layer_rows(pm, layer, rows),
                      whole1(row(gp)), whole1(wpg), whole1(wpp), whole1(row(gf))],
            out_specs=pl.BlockSpec((rows, d), lambda i: (i, 0)),
            out_shape=jax.ShapeDtypeStruct(hm.shape, F32),
            compiler_params=_params("parallel"),
            name="moe_finish",
        )(hm, pm, row(gp), wpg, wpp, row(gf))
    return finish(h, p, tm), finish(hs, ps, ns)


def _rope_tables(pos):
    half = ROT_DIM // 2
    inv = jnp.power(ROPE_THETA, -jnp.arange(half, dtype=F32) / half)
    ang = pos.astype(F32)[:, None] * inv[None, :]
    cos, sin = jnp.cos(ang), jnp.sin(ang)
    rest = HEAD_DIM - ROT_DIM
    n = pos.shape[0]
    return (jnp.concatenate([cos, cos, jnp.ones((n, rest), F32)], axis=1),
            jnp.concatenate([-sin, sin, jnp.zeros((n, rest), F32)], axis=1))


def _tile(n, pref):
    return pref if n % pref == 0 else n


def kernel(x_prompt, x_sample, state_conv, cache_k, cache_v, page_table, p_prompt, p_sample, norm_mix, norm_ffn, norm_ple, ple_w_gate, ple_w_proj, conv_w_pw1, conv_b_pw1, conv_w_dw, conv_b_dw, conv_ln_g, conv_ln_b, conv_w_pw2, conv_b_pw2, ffn_w_gate, ffn_w_up, ffn_w_down, attn_w_qkv, attn_w_o, moe_w_router, moe_w_gate, moe_w_up, moe_w_down, norm_final):
    bsz, seq, d = x_prompt.shape
    nb, dec_seq, _ = x_sample.shape
    assert dec_seq == 1 and norm_mix.shape[0] == 2
    nh = d // HEAD_DIM
    nkv = (attn_w_qkv.shape[2] // HEAD_DIM - nh) // 2
    n_tok = bsz * seq
    dff = ffn_w_gate.shape[2]
    psize = cache_k.shape[2]
    past_len = page_table.shape[1] * psize
    bf = lambda w: w.astype(BF16)

    cw = (norm_mix[0], bf(conv_w_pw1[0]), conv_b_pw1[0], conv_w_dw[0], conv_b_dw[0], conv_ln_g[0], conv_ln_b[0],
          bf(conv_w_pw2[0]), conv_b_pw2[0])
    hp, conv_p = _conv_prompt(x_prompt, *cw, tm=_tile(seq, 1024))
    hs, conv_s = _conv_sample(x_sample.reshape(nb, d), state_conv[0], *cw)
    fw = (norm_ffn[0], bf(ffn_w_gate[0]), bf(ffn_w_up[0]), bf(ffn_w_down[0]))
    pw0 = (norm_ple[0], bf(ple_w_gate[0]), bf(ple_w_proj[0]))
    tf = _tile(dff, 512)
    pp = p_prompt.reshape(p_prompt.shape[0], n_tok, -1)
    ps = p_sample.reshape(p_sample.shape[0], nb, -1)
    hp = _ffn(hp.reshape(n_tok, d), *fw, pp, *pw0, layer=0, tm=_tile(n_tok, 1024), tf=tf)
    hs = _ffn(hs, *fw, ps, *pw0, layer=0, tm=nb, tf=tf)

    wqkv, wo = bf(attn_w_qkv[0]), bf(attn_w_o[0])
    cos_p, sin_p = _rope_tables(jnp.arange(seq, dtype=jnp.int32))
    qt, k_p, v_p, kb, vt, biast = _qkv_prompt(hp.reshape(bsz, seq, d), norm_mix[1], wqkv, cos_p, sin_p,
                                              tm=_tile(seq, 1024), nh=nh, nkv=nkv)
    hp = _attn_prompt(qt, kb, vt, biast, hp.reshape(bsz, seq, d), wo, nh=nh, nkv=nkv).reshape(n_tok, d)

    cos_s, sin_s = _rope_tables(jnp.full((1,), past_len, jnp.int32))
    qs, k_s, v_s = _qkv_sample(hs, norm_mix[1], wqkv, cos_s, sin_s, nh=nh, nkv=nkv)
    ck = cache_k[0].reshape(cache_k.shape[1], psize * nkv, HEAD_DIM)
    cv = cache_v[0].reshape(cache_v.shape[1], psize * nkv, HEAD_DIM)
    idx = _sample_select(ck, page_table, qs, nh=nh, nkv=nkv)[:, :, :TOP_K_BLOCKS]
    att_s = _sample_attn(ck, cv, page_table, idx, qs, k_s, v_s, nh=nh, nkv=nkv)
    hs = _proj_residual(att_s, hs, wo)

    mw = (norm_ffn[1], moe_w_router[0], bf(moe_w_gate[0]), bf(moe_w_up[0]), bf(moe_w_down[0]))
    pw1 = (norm_ple[1], bf(ple_w_gate[1]), bf(ple_w_proj[1]), norm_final)
    yp, ys = _moe(hp, hs, *mw, pp, ps, *pw1, layer=1, tm=_tile(n_tok, 1024), tf=tf)

    return (yp.reshape(bsz, seq, d), ys.reshape(nb, 1, d),
            conv_p[None], conv_s[None],
            k_p.reshape(1, bsz, seq, nkv, HEAD_DIM), v_p.reshape(1, bsz, seq, nkv, HEAD_DIM),
            k_s.reshape(1, nb, 1, nkv, HEAD_DIM), v_s.reshape(1, nb, 1, nkv, HEAD_DIM))
```

```python
import functools

import jax
import jax.numpy as jnp
from jax import lax
from jax.experimental import pallas as pl
from jax.experimental.pallas import tpu as pltpu

F32 = jnp.float32
BF16 = jnp.bfloat16

EPS = 1e-6
HEAD_DIM = 128
ROT_DIM = HEAD_DIM // 4
ROPE_THETA = 500000.0
MOBA_BLOCK = 256
TOP_K_BLOCKS = 3
TOP_K_EXPERTS = 2
LANES = 128
SUBLANES = 8
MASK_VALUE = -1e30
VMEM_LIMIT = 56 * 1024 * 1024
EPILOGUE_ROWS = 256


def _params(*sem):
    return pltpu.CompilerParams(dimension_semantics=sem, vmem_limit_bytes=VMEM_LIMIT)


def _rms(x, g):
    r = lax.rsqrt(jnp.mean(x * x, axis=-1, keepdims=True) + EPS)
    return x * r * g


def _dot(a, b):
    return jnp.dot(a, b, preferred_element_type=F32)


def _dot_t(a, b, precision=None):
    return lax.dot_general(a, b, (((1,), (1,)), ((), ())), preferred_element_type=F32, precision=precision)


def _dot_t_split(a, b):
    ah, bh = a.astype(BF16), b.astype(BF16)
    al, bl = (a - ah.astype(F32)).astype(BF16), (b - bh.astype(F32)).astype(BF16)
    return _dot_t(ah, bh) + _dot_t(ah, bl) + _dot_t(al, bh)


def _silu(x):
    return x * jax.nn.sigmoid(x)


def _ple(h, p, g, w_gate, w_proj):
    gate = jax.nn.sigmoid(_dot(_rms(h, g).astype(BF16), w_gate))
    return h + gate * _dot(p.astype(BF16), w_proj)


def _rope_head(x, cos, sin, lane):
    half = ROT_DIM // 2
    partner = jnp.where(lane < half, pltpu.roll(x, HEAD_DIM - half, 1), pltpu.roll(x, half, 1))
    return x * cos + partner * sin


CONV_HALO = 32
CONV_CHUNK = 128
CONV_STRIP = 128
CONV_NORM_ROWS = 16


def _conv_prompt_kernel(x_ref, g_ref, w1_ref, b1_ref, wrep_ref, bdw_ref, lng_ref, lnb_ref, w2_ref, b2_ref,
                        out_ref, state_ref, upad_ref, y_ref, cacc_ref, *, tm, width, dc):
    t = pl.program_id(1)

    @pl.when(t == 0)
    def _():
        upad_ref[...] = jnp.zeros(upad_ref.shape, F32)

    x = x_ref[0]
    uu = _dot(_rms(x, g_ref[...]).astype(BF16), w1_ref[...]) + b1_ref[...]
    upad_ref[CONV_HALO:CONV_HALO + tm, :] = uu[:, :dc] * jax.nn.sigmoid(uu[:, dc:])

    first = CONV_HALO - (width - 1)
    span = CONV_CHUNK + SUBLANES

    def chunk(i, carry):
        base = pl.multiple_of(i * CONV_CHUNK, CONV_CHUNK)
        for c0 in range(0, dc, CONV_STRIP):
            lanes = slice(c0, c0 + CONV_STRIP)
            acc = jnp.broadcast_to(bdw_ref[:, lanes], (CONV_CHUNK, CONV_STRIP))
            for r in range(SUBLANES):
                part = None
                for k in range(width):
                    a, kr = divmod(first + k, SUBLANES)
                    if kr != r:
                        continue
                    w = wrep_ref[k * SUBLANES:(k + 1) * SUBLANES, lanes]
                    term = upad_ref[pl.ds(base + a * SUBLANES, span), lanes] * jnp.concatenate(
                        [w] * (span // SUBLANES), axis=0)
                    part = term if part is None else part + term
                if part is not None:
                    acc = acc + part[r:r + CONV_CHUNK, :]
            cacc_ref[:, lanes] = acc
        for r0 in range(0, CONV_CHUNK, CONV_NORM_ROWS):
            acc = cacc_ref[r0:r0 + CONV_NORM_ROWS, :]
            mu = jnp.mean(acc, axis=-1, keepdims=True)
            d = acc - mu
            var = jnp.mean(d * d, axis=-1, keepdims=True)
            z = d * lax.rsqrt(var + EPS) * lng_ref[...] + lnb_ref[...]
            y_ref[pl.ds(base + r0, CONV_NORM_ROWS), :] = _silu(z).astype(BF16)
        return carry

    lax.fori_loop(0, tm // CONV_CHUNK, chunk, 0)
    out_ref[0] = x + _dot(y_ref[...], w2_ref[...]) + b2_ref[...]
    state_ref[0] = upad_ref[tm + first:tm + CONV_HALO, :]
    upad_ref[0:CONV_HALO, :] = upad_ref[tm:tm + CONV_HALO, :]


def _conv_prompt(x, g, w1, b1, wdw, bdw, lng, lnb, w2, b2, *, tm):
    bsz, seq, d = x.shape
    width, dc = wdw.shape
    assert seq % tm == 0 and tm % CONV_CHUNK == 0 and width - 1 <= CONV_HALO
    row = lambda a: a.reshape(1, -1)
    full = lambda a: pl.BlockSpec(a.shape, lambda b, t: (0,) * a.ndim)
    wrep = jnp.repeat(wdw, SUBLANES, axis=0)
    args = (row(g), w1, row(b1), wrep, row(bdw), row(lng), row(lnb), w2, row(b2))
    return pl.pallas_call(
        functools.partial(_conv_prompt_kernel, tm=tm, width=width, dc=dc),
        grid=(bsz, seq // tm),
        in_specs=[pl.BlockSpec((1, tm, d), lambda b, t: (b, t, 0))] + [full(a) for a in args],
        out_specs=[pl.BlockSpec((1, tm, d), lambda b, t: (b, t, 0)),
                   pl.BlockSpec((1, width - 1, dc), lambda b, t: (b, 0, 0))],
        out_shape=[jax.ShapeDtypeStruct((bsz, seq, d), F32),
                   jax.ShapeDtypeStruct((bsz, width - 1, dc), F32)],
        scratch_shapes=[pltpu.VMEM((tm + CONV_HALO + SUBLANES, dc), F32), pltpu.VMEM((tm, dc), BF16),
                        pltpu.VMEM((CONV_CHUNK, dc), F32)],
        compiler_params=_params("parallel", "arbitrary"),
        name="conv_prompt",
    )(x, *args)


def _conv_sample_kernel(x_ref, st_ref, g_ref, w1_ref, b1_ref, wdw_ref, bdw_ref, lng_ref, lnb_ref, w2_ref, b2_ref,
                        out_ref, state_ref, *, width, dc):
    nb = x_ref.shape[0]
    x = x_ref[...]
    uu = _dot(_rms(x, g_ref[...]).astype(BF16), w1_ref[...]) + b1_ref[...]
    u = uu[:, :dc] * jax.nn.sigmoid(uu[:, dc:])
    st = st_ref[...]
    acc = jnp.sum(st * wdw_ref[0:width - 1, :][None], axis=1) + u * wdw_ref[width - 1:width, :] + bdw_ref[...]
    mu = jnp.mean(acc, axis=-1, keepdims=True)
    d = acc - mu
    var = jnp.mean(d * d, axis=-1, keepdims=True)
    z = d * lax.rsqrt(var + EPS) * lng_ref[...] + lnb_ref[...]
    out_ref[...] = x + _dot(_silu(z).astype(BF16), w2_ref[...]) + b2_ref[...]
    state_ref[:, 0:width - 2, :] = st_ref[:, 1:width - 1, :]
    for b in range(nb):
        state_ref[b, width - 2:width - 1, :] = u[b:b + 1, :]


def _conv_sample(x, st, g, w1, b1, wdw, bdw, lng, lnb, w2, b2):
    nb, d = x.shape
    width, dc = wdw.shape
    row = lambda a: a.reshape(1, -1)
    return pl.pallas_call(
        functools.partial(_conv_sample_kernel, width=width, dc=dc),
        out_shape=[jax.ShapeDtypeStruct((nb, d), F32), jax.ShapeDtypeStruct((nb, width - 1, dc), F32)],
        compiler_params=pltpu.CompilerParams(vmem_limit_bytes=VMEM_LIMIT),
        name="conv_sample",
    )(x, st, row(g), w1, row(b1), wdw, row(bdw), row(lng), row(lnb), w2, row(b2))


def _ffn_kernel(x_ref, g_ref, wg_ref, wu_ref, wd_ref, p_ref, gp_ref, wpg_ref, wpp_ref, out_ref, xn_ref, acc_ref,
                *, tf):
    xn_ref[...] = _rms(x_ref[...], g_ref[...]).astype(BF16)
    acc_ref[...] = jnp.zeros(acc_ref.shape, F32)

    def ffn(fc, carry):
        f0 = pl.multiple_of(fc * tf, tf)
        xn = xn_ref[...]
        a = (_silu(_dot(xn, wg_ref[:, pl.ds(f0, tf)])) * _dot(xn, wu_ref[:, pl.ds(f0, tf)])).astype(BF16)
        acc_ref[...] += _dot(a, wd_ref[pl.ds(f0, tf), :])
        return carry
    lax.fori_loop(0, wg_ref.shape[1] // tf, ffn, 0)

    tm = x_ref.shape[0]
    step = min(EPILOGUE_ROWS, tm)
    for r in range(0, tm, step):
        rows = slice(r, r + step)
        out_ref[rows, :] = _ple(x_ref[rows, :] + acc_ref[rows, :], p_ref[rows, :], gp_ref[...], wpg_ref[...],
                                wpp_ref[...])


def _layer_rows(p, layer, tm):
    return pl.BlockSpec((None, tm, p.shape[2]), lambda i: (layer, i, 0))


def _ffn(x, g, wg, wu, wd, p, gp, wpg, wpp, *, layer, tm, tf):
    n, d = x.shape
    dff = wg.shape[1]
    assert n % tm == 0 and dff % tf == 0
    row = lambda a: a.reshape(1, -1)
    full = lambda a: pl.BlockSpec(a.shape, lambda i: (0,) * a.ndim)
    resident = lambda a: pl.BlockSpec(a.shape, lambda i: (0,) * a.ndim, pipeline_mode=pl.Buffered(1))
    return pl.pallas_call(
        functools.partial(_ffn_kernel, tf=tf),
        grid=(n // tm,),
        in_specs=[pl.BlockSpec((tm, d), lambda i: (i, 0)), full(row(g)), resident(wg), resident(wu), resident(wd),
                  _layer_rows(p, layer, tm), full(row(gp)), full(wpg), full(wpp)],
        out_specs=pl.BlockSpec((tm, d), lambda i: (i, 0)),
        out_shape=jax.ShapeDtypeStruct((n, d), F32),
        scratch_shapes=[pltpu.VMEM((tm, d), BF16), pltpu.VMEM((tm, d), F32)],
        compiler_params=_params("parallel"),
        name="ffn_ple",
    )(x, row(g), wg, wu, wd, p, row(gp), wpg, wpp)


def _qkv_prompt_kernel(x_ref, g_ref, w_ref, cos_ref, sin_ref,
                       qt_ref, k_ref, v_ref, kb_ref, vt_ref, biast_ref, kmt_ref, *, tm, nh, nkv, nblk):
    t = pl.program_id(1)
    group = nh // nkv
    nq = nh * HEAD_DIM
    nk = nkv * HEAD_DIM

    @pl.when(t == 0)
    def _():
        kmt_ref[...] = jnp.zeros(kmt_ref.shape, F32)

    qkv = _dot(_rms(x_ref[0], g_ref[...]).astype(BF16), w_ref[...])
    cos = cos_ref[...]
    sin = sin_ref[...]
    lane = lax.broadcasted_iota(jnp.int32, (tm, HEAD_DIM), 1)
    scale = HEAD_DIM ** -0.5
    q_heads = [_rope_head(qkv[:, h * HEAD_DIM:(h + 1) * HEAD_DIM], cos, sin, lane) * scale for h in range(nh)]
    q = jnp.concatenate(q_heads, axis=1)
    k = jnp.concatenate([_rope_head(qkv[:, nq + h * HEAD_DIM:nq + (h + 1) * HEAD_DIM], cos, sin, lane)
                         for h in range(nkv)], axis=1)
    v = qkv[:, nq + nk:]
    for h in range(nh):
        qt_ref[0, h] = q_heads[h].T.astype(BF16)
    for kv in range(nkv):
        k_ref[0, pl.ds(kv, tm, stride=nkv), :] = k[:, kv * HEAD_DIM:(kv + 1) * HEAD_DIM]
        v_ref[0, pl.ds(kv, tm, stride=nkv), :] = v[:, kv * HEAD_DIM:(kv + 1) * HEAD_DIM]
        vt_ref[0, kv] = v[:, kv * HEAD_DIM:(kv + 1) * HEAD_DIM].T.astype(BF16)
    kb_ref[0] = k.astype(BF16)

    kmt = kmt_ref[...]
    kmt_row = lax.broadcasted_iota(jnp.int32, kmt.shape, 0)
    kmt_head = lax.broadcasted_iota(jnp.int32, kmt.shape, 1) // HEAD_DIM
    for i in range(tm // MOBA_BLOCK):
        km = jnp.mean(k[i * MOBA_BLOCK:(i + 1) * MOBA_BLOCK, :], axis=0, keepdims=True)
        km = jnp.concatenate([km[:, (h // group) * HEAD_DIM:(h // group + 1) * HEAD_DIM] for h in range(nh)], axis=1)
        n = t * (tm // MOBA_BLOCK) + i
        kmt = jnp.where(kmt_row == kmt_head * nblk + n, km, kmt)
    kmt_ref[...] = kmt

    s = _dot_t_split(q, kmt).T.reshape(nh, nblk, tm)
    n_idx = lax.broadcasted_iota(jnp.int32, s.shape, 1)
    own = (t * tm + lax.broadcasted_iota(jnp.int32, s.shape, 2)) // MOBA_BLOCK
    past = n_idx < own
    s = jnp.where(past, s, -jnp.inf)
    rank = jnp.zeros(s.shape, F32)
    for dlt in range(1, nblk):
        wrap = n_idx + dlt >= nblk
        other = jnp.concatenate([s[:, dlt:, :], s[:, :dlt, :]], axis=1)
        rank = rank + jnp.where(other > s, 1.0, 0.0) + jnp.where(wrap, jnp.where(other == s, 1.0, 0.0), 0.0)
    bias = jnp.where(past, jnp.where(rank < TOP_K_BLOCKS, 0.0, MASK_VALUE), MASK_VALUE)
    biast_ref[0] = bias.reshape(nh * nblk, tm).astype(BF16)


def _qkv_prompt(x, g, w, cos, sin, *, tm, nh, nkv):
    bsz, seq, d = x.shape
    nq, nk = nh * HEAD_DIM, nkv * HEAD_DIM
    nblk = seq // MOBA_BLOCK
    assert seq % tm == 0 and tm % MOBA_BLOCK == 0 and nh * nblk == LANES
    tok = lambda c: pl.BlockSpec((1, tm, c), lambda b, t: (b, t, 0))
    kv_rows = pl.BlockSpec((1, tm * nkv, HEAD_DIM), lambda b, t: (b, t, 0))
    full = lambda a: pl.BlockSpec(a.shape, lambda b, t: (0,) * a.ndim)
    g = g.reshape(1, -1)
    return pl.pallas_call(
        functools.partial(_qkv_prompt_kernel, tm=tm, nh=nh, nkv=nkv, nblk=nblk),
        grid=(bsz, seq // tm),
        in_specs=[tok(d), full(g), full(w),
                  pl.BlockSpec((tm, HEAD_DIM), lambda b, t: (t, 0)), pl.BlockSpec((tm, HEAD_DIM), lambda b, t: (t, 0))],
        out_specs=[pl.BlockSpec((1, nh, HEAD_DIM, tm), lambda b, t: (b, 0, 0, t)), kv_rows, kv_rows, tok(nk),
                   pl.BlockSpec((1, nkv, HEAD_DIM, tm), lambda b, t: (b, 0, 0, t)),
                   pl.BlockSpec((1, nh * nblk, tm), lambda b, t: (b, 0, t))],
        out_shape=[jax.ShapeDtypeStruct((bsz, nh, HEAD_DIM, seq), BF16),
                   jax.ShapeDtypeStruct((bsz, seq * nkv, HEAD_DIM), F32),
                   jax.ShapeDtypeStruct((bsz, seq * nkv, HEAD_DIM), F32),
                   jax.ShapeDtypeStruct((bsz, seq, nk), BF16),
                   jax.ShapeDtypeStruct((bsz, nkv, HEAD_DIM, seq), BF16),
                   jax.ShapeDtypeStruct((bsz, nh * nblk, seq), BF16)],
        scratch_shapes=[pltpu.VMEM((nh * nblk, nq), F32)],
        compiler_params=_params("parallel", "arbitrary"),
        name="qkv_prompt",
    )(x, g, w, cos, sin)


def _qkv_sample_kernel(x_ref, g_ref, w_ref, cos_ref, sin_ref, q_ref, k_ref, v_ref, *, nh, nkv):
    nb = x_ref.shape[0]
    nq = nh * HEAD_DIM
    nk = nkv * HEAD_DIM
    qkv = _dot(_rms(x_ref[...], g_ref[...]).astype(BF16), w_ref[...])
    cos = cos_ref[...]
    sin = sin_ref[...]
    lane = lax.broadcasted_iota(jnp.int32, (nb, HEAD_DIM), 1)
    scale = HEAD_DIM ** -0.5
    q_ref[...] = jnp.concatenate([_rope_head(qkv[:, h * HEAD_DIM:(h + 1) * HEAD_DIM], cos, sin, lane) * scale
                                  for h in range(nh)], axis=1)
    k_ref[...] = jnp.concatenate([_rope_head(qkv[:, nq + h * HEAD_DIM:nq + (h + 1) * HEAD_DIM], cos, sin, lane)
                                  for h in range(nkv)], axis=1)
    v_ref[...] = qkv[:, nq + nk:]


def _qkv_sample(x, g, w, cos, sin, *, nh, nkv):
    nb = x.shape[0]
    nq, nk = nh * HEAD_DIM, nkv * HEAD_DIM
    return pl.pallas_call(
        functools.partial(_qkv_sample_kernel, nh=nh, nkv=nkv),
        out_shape=[jax.ShapeDtypeStruct((nb, nq), F32), jax.ShapeDtypeStruct((nb, nk), F32),
                   jax.ShapeDtypeStruct((nb, nk), F32)],
        compiler_params=pltpu.CompilerParams(vmem_limit_bytes=VMEM_LIMIT),
        name="qkv_sample",
    )(x, g.reshape(1, -1), w, cos, sin)


ATTN_BLOCKS_PER_PASS = (4, 2, 1)


def _attn_prompt_kernel(qt_ref, kb_ref, vt_ref, biast_ref, h_ref, wo_ref, out_ref, *, nh, nkv, nblk):
    o = pl.program_id(1)
    hk = pl.program_id(2)
    group = nh // nkv
    blk = MOBA_BLOCK
    bias = biast_ref[0]
    feat_head = lax.broadcasted_iota(jnp.int32, bias.shape, 0) // nblk
    q_t = jnp.concatenate([qt_ref[0, g] for g in range(group)], axis=1)
    q_aug = jnp.concatenate(
        [q_t, jnp.concatenate([jnp.where(feat_head == hk * group + g, bias, jnp.zeros_like(bias))
                               for g in range(group)], axis=1)], axis=0)
    cols = group * blk

    own0 = pl.multiple_of(o * blk, blk)
    s = _dot(kb_ref[0, pl.ds(own0, blk), :], q_t)
    kpos = lax.broadcasted_iota(jnp.int32, (blk, cols), 0)
    qpos = lax.broadcasted_iota(jnp.int32, (blk, cols), 1) % blk
    s = jnp.where(kpos <= qpos, s, MASK_VALUE)
    m = jnp.max(s, axis=0, keepdims=True)
    p = jnp.exp(s - m)
    l = jnp.sum(p, axis=0, keepdims=True)
    acc = _dot(vt_ref[0, 0, :, pl.ds(own0, blk)], p.astype(BF16))
    def past(first, nb_, carry):
        m, l, acc = carry
        keys = nb_ * blk
        j0 = pl.multiple_of(first * blk, keys)
        lane = lax.broadcasted_iota(jnp.int32, (keys, nh * nblk), 1)
        key_blk = first + lax.broadcasted_iota(jnp.int32, (keys, nh * nblk), 0) // blk
        pick = jnp.where(lane % nblk == key_blk, 1.0, 0.0).astype(BF16)
        s = _dot(jnp.concatenate([kb_ref[0, pl.ds(j0, keys), :], pick], axis=1), q_aug)
        m_new = jnp.maximum(m, jnp.max(s, axis=0, keepdims=True))
        alpha = jnp.exp(m - m_new)
        p = jnp.exp(s - m_new)
        l = alpha * l + jnp.sum(p, axis=0, keepdims=True)
        acc = alpha * acc + _dot(vt_ref[0, 0, :, pl.ds(j0, keys)], p.astype(BF16))
        return m_new, l, acc

    state, done = (m, l, acc), 0
    for nb_ in ATTN_BLOCKS_PER_PASS:
        cnt = (o - done) // nb_
        state = lax.fori_loop(0, cnt, lambda i, c, nb_=nb_, done=done: past(done + i * nb_, nb_, c), state)
        done = done + cnt * nb_
    m, l, acc = state
    att = acc / l
    att = jnp.concatenate([att[:, g * blk:(g + 1) * blk].T for g in range(group)], axis=1).astype(BF16)
    proj = _dot(att, wo_ref[...])

    @pl.when(hk == 0)
    def _():
        out_ref[0] = h_ref[0] + proj

    @pl.when(hk != 0)
    def _():
        out_ref[0] += proj


def _attn_prompt(qt, kb, vt, biast, h, wo, *, nh, nkv):
    bsz, seq, d = h.shape
    nblk = seq // MOBA_BLOCK
    group = nh // nkv
    gw = group * HEAD_DIM
    return pl.pallas_call(
        functools.partial(_attn_prompt_kernel, nh=nh, nkv=nkv, nblk=nblk),
        grid=(bsz, nblk, nkv),
        in_specs=[pl.BlockSpec((1, group, HEAD_DIM, MOBA_BLOCK), lambda b, o, k: (b, k, 0, o)),
                  pl.BlockSpec((1, seq, HEAD_DIM), lambda b, o, k: (b, 0, k)),
                  pl.BlockSpec((1, 1, HEAD_DIM, seq), lambda b, o, k: (b, k, 0, 0)),
                  pl.BlockSpec((1, biast.shape[1], MOBA_BLOCK), lambda b, o, k: (b, 0, o)),
                  pl.BlockSpec((1, MOBA_BLOCK, d), lambda b, o, k: (b, o, 0)),
                  pl.BlockSpec((gw, d), lambda b, o, k: (k, 0))],
        out_specs=pl.BlockSpec((1, MOBA_BLOCK, d), lambda b, o, k: (b, o, 0)),
        out_shape=jax.ShapeDtypeStruct((bsz, seq, d), F32),
        compiler_params=_params("parallel", "parallel", "arbitrary"),
        name="attn_prompt",
    )(qt, kb, vt, biast, h, wo)


PAGES_PER_STEP = 64


def _sample_select_kernel(pt_ref, *refs, nh, nkv, ppb, nblk):
    pages = refs[:PAGES_PER_STEP]
    q_ref, idx_ref, km_ref = refs[PAGES_PER_STEP:]
    s = pl.program_id(1)
    group = nh // nkv
    bps = PAGES_PER_STEP // ppb
    psize = pages[0].shape[1] // nkv

    @pl.when(s == 0)
    def _():
        km_ref[...] = jnp.zeros(km_ref.shape, F32)

    km = km_ref[...]
    km_row = lax.broadcasted_iota(jnp.int32, km.shape, 0)
    for i in range(bps):
        tot = jnp.concatenate(
            [sum(jnp.sum(pages[i * ppb + j][0, pl.ds(kv, psize, stride=nkv), :], axis=0, keepdims=True)
                 for j in range(ppb)) for kv in range(nkv)], axis=1)
        km = jnp.where(km_row == s * bps + i, tot / (psize * ppb), km)
    km_ref[...] = km

    @pl.when(s == pl.num_programs(1) - 1)
    def _():
        km = km_ref[...]
        q = q_ref[0]
        sc = jnp.concatenate(
            [_dot_t(q[kv * group:(kv + 1) * group, :], km[:, kv * HEAD_DIM:(kv + 1) * HEAD_DIM],
                    precision=lax.Precision.HIGHEST) for kv in range(nkv)], axis=0)
        col = lax.broadcasted_iota(jnp.int32, sc.shape, 1)
        lane = lax.broadcasted_iota(jnp.int32, (nh, LANES), 1)
        out = jnp.zeros((nh, LANES), jnp.int32)
        for r in range(TOP_K_BLOCKS):
            best = jnp.max(sc, axis=-1, keepdims=True)
            pick = jnp.min(jnp.where(sc == best, col, nblk), axis=-1, keepdims=True)
            out = jnp.where(lane == r, pick, out)
            sc = jnp.where(col == pick, -jnp.inf, sc)
        idx_ref[0] = out


def _sample_select(cache_k, page_table, q, *, nh, nkv):
    npool, prow, hd = cache_k.shape
    nb, npages = page_table.shape
    ppb = MOBA_BLOCK // (prow // nkv)
    nblk = npages // ppb
    assert npages % PAGES_PER_STEP == 0 and PAGES_PER_STEP % ppb == 0 and nblk >= TOP_K_BLOCKS
    page_spec = lambda j: pl.BlockSpec((1, prow, hd), lambda b, s, pt: (pt[b * npages + s * PAGES_PER_STEP + j], 0, 0))
    grid_spec = pltpu.PrefetchScalarGridSpec(
        num_scalar_prefetch=1,
        grid=(nb, npages // PAGES_PER_STEP),
        in_specs=[page_spec(j) for j in range(PAGES_PER_STEP)]
        + [pl.BlockSpec((1, nh, HEAD_DIM), lambda b, s, pt: (b, 0, 0))],
        out_specs=pl.BlockSpec((1, nh, LANES), lambda b, s, pt: (b, 0, 0)),
        scratch_shapes=[pltpu.VMEM((nblk, nkv * hd), F32)],
    )
    return pl.pallas_call(
        functools.partial(_sample_select_kernel, nh=nh, nkv=nkv, ppb=ppb, nblk=nblk),
        grid_spec=grid_spec,
        out_shape=jax.ShapeDtypeStruct((nb, nh, LANES), jnp.int32),
        compiler_params=_params("parallel", "arbitrary"),
        name="sample_select",
    )(page_table.reshape(-1), *([cache_k] * PAGES_PER_STEP), q.reshape(nb, nh, HEAD_DIM))


def _sample_attn_kernel(pt_ref, idx_ref, q_ref, kn_ref, vn_ref, *refs, group, nkv, npg):
    kpages = refs[:group * npg]
    vpages = refs[group * npg:2 * group * npg]
    out_ref = refs[2 * group * npg]
    hkv = pl.program_id(1)
    prow = kpages[0].shape[1]
    mine = lax.broadcasted_iota(jnp.int32, (prow, 1), 0) % nkv == hkv
    outs = []
    for g in range(group):
        q = q_ref[0, g:g + 1, :]
        m = jnp.sum(q * kn_ref[0], axis=-1, keepdims=True)
        l = jnp.ones((1, 1), F32)
        acc = vn_ref[0]
        for i in range(npg):
            k = kpages[g * npg + i][0]
            sc = jnp.where(mine, jnp.sum(k * q, axis=-1, keepdims=True), MASK_VALUE)
            m_new = jnp.maximum(m, jnp.max(sc, axis=0, keepdims=True))
            alpha = jnp.exp(m - m_new)
            p = jnp.exp(sc - m_new)
            l = alpha * l + jnp.sum(p, axis=0, keepdims=True)
            acc = alpha * acc + jnp.sum(p * vpages[g * npg + i][0], axis=0, keepdims=True)
            m = m_new
        outs.append(acc / l)
    out_ref[0] = jnp.concatenate(outs, axis=0)


def _sample_attn(cache_k, cache_v, page_table, idx, q, kn, vn, *, nh, nkv):
    npool, prow, hd = cache_k.shape
    nb, npages = page_table.shape
    ppb = MOBA_BLOCK // (prow // nkv)
    group = nh // nkv
    nsel = TOP_K_BLOCKS
    npg = nsel * ppb

    def page_spec(g, i):
        def imap(b, kv, pt, ix):
            blk = ix[(b * nh + kv * group + g) * nsel + i // ppb]
            return (pt[b * npages + blk * ppb + i % ppb], 0, 0)
        return pl.BlockSpec((1, prow, hd), imap)

    qspec = pl.BlockSpec((1, group, HEAD_DIM), lambda b, kv, pt, ix: (b * nkv + kv, 0, 0))
    kvspec = pl.BlockSpec((1, 1, HEAD_DIM), lambda b, kv, pt, ix: (b * nkv + kv, 0, 0))
    pages = [page_spec(g, i) for g in range(group) for i in range(npg)]
    grid_spec = pltpu.PrefetchScalarGridSpec(
        num_scalar_prefetch=2,
        grid=(nb, nkv),
        in_specs=[qspec, kvspec, kvspec] + pages + pages,
        out_specs=qspec,
    )
    out = pl.pallas_call(
        functools.partial(_sample_attn_kernel, group=group, nkv=nkv, npg=npg),
        grid_spec=grid_spec,
        out_shape=jax.ShapeDtypeStruct((nb * nkv, group, HEAD_DIM), F32),
        compiler_params=_params("parallel", "parallel"),
        name="sample_attn",
    )(page_table.reshape(-1), idx.reshape(-1),
      q.reshape(nb * nkv, group, HEAD_DIM), kn.reshape(nb * nkv, 1, HEAD_DIM), vn.reshape(nb * nkv, 1, HEAD_DIM),
      *([cache_k] * (group * npg)), *([cache_v] * (group * npg)))
    return out.reshape(nb, nh * HEAD_DIM)


def _proj_residual_kernel(a_ref, h_ref, w_ref, out_ref):
    out_ref[...] = h_ref[...] + _dot(a_ref[...].astype(BF16), w_ref[...])


def _proj_residual(a, h, w):
    return pl.pallas_call(
        _proj_residual_kernel,
        out_shape=jax.ShapeDtypeStruct(h.shape, F32),
        compiler_params=pltpu.CompilerParams(vmem_limit_bytes=VMEM_LIMIT),
        name="proj_residual",
    )(a, h, w)


MXU_DEPTH = 256
MOE_ROW_GRAN = 32
MOE_VARIANTS = 16
MOE_SPECIALISED = (-96, 160)


def _router_kernel(x_ref, g_ref, wrt_ref, tri_ref, xn_ref, gate_ref, rank_ref, cnt_ref):
    xn = _rms(x_ref[...], g_ref[...])
    xn_ref[...] = xn.astype(BF16)
    logits = _dot_t_split(wrt_ref[...], xn)
    ne = logits.shape[0]
    e = jnp.exp(logits - jnp.max(logits, axis=0, keepdims=True))
    probs = e / jnp.sum(e, axis=0, keepdims=True)
    eid = lax.broadcasted_iota(jnp.int32, probs.shape, 0)
    rest = probs
    member = jnp.zeros(probs.shape, jnp.bool_)
    top_sum = jnp.zeros((1, probs.shape[1]), F32)
    for _ in range(TOP_K_EXPERTS):
        best = jnp.max(rest, axis=0, keepdims=True)
        pick = eid == jnp.min(jnp.where(rest == best, eid, ne), axis=0, keepdims=True)
        member = member | pick
        top_sum = top_sum + best
        rest = jnp.where(pick, -1.0, rest)
    gate_ref[...] = jnp.where(member, probs / top_sum, 0.0)
    mem = jnp.where(member, 1.0, 0.0)
    rank = _dot(mem.astype(BF16), tri_ref[...])
    rank_ref[...] = jnp.where(member, rank, -1.0)
    cnt_ref[0] = jnp.broadcast_to(jnp.sum(mem, axis=1, keepdims=True), cnt_ref.shape[1:])


def _router(x, g, wr, *, tm):
    n, d = x.shape
    ne = wr.shape[1]
    nt = n // tm
    assert n % tm == 0
    tri = (lax.broadcasted_iota(jnp.int32, (tm, tm), 0) < lax.broadcasted_iota(jnp.int32, (tm, tm), 1)).astype(BF16)
    full = lambda a: pl.BlockSpec(a.shape, lambda i: (0,) * a.ndim)
    g = g.reshape(1, -1)
    wrt = wr.T
    xn, gate, rank, cnt = pl.pallas_call(
        _router_kernel,
        grid=(nt,),
        in_specs=[pl.BlockSpec((tm, d), lambda i: (i, 0)), full(g), full(wrt), full(tri)],
        out_specs=[pl.BlockSpec((tm, d), lambda i: (i, 0)), pl.BlockSpec((ne, tm), lambda i: (0, i)),
                   pl.BlockSpec((ne, tm), lambda i: (0, i)), pl.BlockSpec((1, ne, LANES), lambda i: (i, 0, 0))],
        out_shape=[jax.ShapeDtypeStruct((n, d), BF16), jax.ShapeDtypeStruct((ne, n), F32),
                   jax.ShapeDtypeStruct((ne, n), F32), jax.ShapeDtypeStruct((nt, ne, LANES), F32)],
        compiler_params=_params("parallel"),
        name="router",
    )(x, g, wrt, tri)
    return xn, gate, rank, cnt[:, :, 0].astype(jnp.int32).reshape(-1)


def _moe_kernel(eid_ref, cnt_ref, cnts_ref, xn_ref, gate_ref, rank_ref, wg_ref, wu_ref, wd_ref, hin_ref,
                xns_ref, gates_ref, ranks_ref, hins_ref, out_ref, outs_ref, xc_ref, y_ref, *, tm, tf, ne):
    e = eid_ref[0]
    i = pl.program_id(0)
    weights = (wg_ref, wu_ref, wd_ref)
    _expert_rows(e, cnt_ref[i * ne + e], xn_ref, gate_ref, rank_ref, hin_ref, out_ref, weights, xc_ref, y_ref,
                 tm=tm, tf=tf, ne=ne)

    @pl.when(i == pl.num_programs(0) - 1)
    def _():
        _expert_rows(e, cnts_ref[e], xns_ref, gates_ref, ranks_ref, hins_ref, outs_ref, weights, xc_ref, y_ref,
                     tm=xns_ref.shape[0], tf=tf, ne=ne)


def _moe_groups(tm):
    gran = min(MOE_ROW_GRAN, tm)
    return gran, min(MOE_VARIANTS, tm // gran)


def _expert_rows(e, cnt, xn_ref, gate_ref, rank_ref, hin_ref, out_ref, weights, xc_ref, y_ref, *, tm, tf, ne):
    wg_ref, wu_ref, wd_ref = weights
    gran, nvar = _moe_groups(tm)
    ngrp = (cnt + gran - 1) // gran
    rank = rank_ref[pl.ds(e, 1), :]
    kq = min(MXU_DEPTH, tm)

    def onehot(r0, rows):
        slot = lax.broadcasted_iota(jnp.int32, (rows, tm), 0) + r0
        return slot.astype(F32) == rank

    def process(r0, rows, base_ref):
        krows = -(-rows // kq) * kq
        sel = jnp.where(onehot(r0, rows), 1.0, 0.0).astype(BF16)
        xc_ref[0:rows, :] = _dot(sel, xn_ref[...]).astype(BF16)
        y_ref[0:krows, :] = jnp.zeros((krows, y_ref.shape[1]), F32)

        def ffn(fc, carry):
            xc = xc_ref[0:rows, :]
            f0 = pl.multiple_of(fc * tf, tf)
            a = (_silu(_dot(xc, wg_ref[0, :, pl.ds(f0, tf)])) * _dot(xc, wu_ref[0, :, pl.ds(f0, tf)])).astype(BF16)
            y_ref[0:rows, :] += _dot(a, wd_ref[0, pl.ds(f0, tf), :])
            return carry
        lax.fori_loop(0, wg_ref.shape[2] // tf, ffn, 0)
        w = jnp.where(onehot(r0, krows), gate_ref[pl.ds(e, 1), :], 0.0).astype(BF16)
        out_ref[...] = base_ref[...] + lax.dot_general(w, y_ref[0:krows, :].astype(BF16), (((0,), (0,)), ((), ())),
                                                       preferred_element_type=F32)

    @pl.when(ngrp == 0)
    def _():
        out_ref[...] = hin_ref[...]

    balanced = tm * TOP_K_EXPERTS // ne
    sizes = range(max(1, (balanced + MOE_SPECIALISED[0]) // gran), min(nvar, (balanced + MOE_SPECIALISED[1]) // gran) + 1)
    for n in sizes:
        @pl.when(ngrp == n)
        def _(n=n):
            process(0, n * gran, hin_ref)

    common = functools.reduce(lambda a, n: a | (ngrp == n), sizes, ngrp == 0)

    @pl.when(jnp.logical_not(common))
    def _():
        full = nvar * gran
        out_ref[...] = hin_ref[...]

        def body(c, carry):
            process(c * full, full, out_ref)
            return carry
        lax.fori_loop(0, (cnt + full - 1) // full, body, 0)


def _moe_finish_kernel(h_ref, p_ref, gp_ref, wpg_ref, wpp_ref, gf_ref, out_ref):
    tm = h_ref.shape[0]
    step = min(EPILOGUE_ROWS, tm)
    for r in range(0, tm, step):
        rows = slice(r, r + step)
        out_ref[rows, :] = _rms(_ple(h_ref[rows, :], p_ref[rows, :], gp_ref[...], wpg_ref[...], wpp_ref[...]),
                                gf_ref[...])


def _moe(h, hs, g, wr, wg, wu, wd, p, ps, gp, wpg, wpp, gf, *, layer, tm, tf):
    n, d = h.shape
    ns = hs.shape[0]
    ne, _, dff = wg.shape
    nt = n // tm
    for rows in (tm, ns):
        gran, nvar = _moe_groups(rows)
        assert rows % (nvar * gran) == 0 and (nvar * gran) % min(MXU_DEPTH, rows) == 0
    assert n % tm == 0 and dff % tf == 0 and ns <= tm
    xn, gate, rank, cnt = _router(h, g, wr, tm=tm)
    xns, gates, ranks, cnts = _router(hs, g, wr, tm=ns)
    resident = lambda a: pl.BlockSpec((1,) + a.shape[1:], lambda i, eid, c, cs: (eid[0], 0, 0),
                                      pipeline_mode=pl.Buffered(1))
    tile = lambda c: pl.BlockSpec((tm, c), lambda i, eid, c_, cs: (i, 0))
    lanes = pl.BlockSpec((ne, tm), lambda i, eid, c, cs: (0, i))
    whole = lambda a: pl.BlockSpec(a.shape, lambda i, eid, c, cs: (0,) * a.ndim)
    full = max(g * v for g, v in (_moe_groups(tm), _moe_groups(ns)))
    grid_spec = pltpu.PrefetchScalarGridSpec(
        num_scalar_prefetch=3,
        grid=(nt,),
        in_specs=[tile(d), lanes, lanes, resident(wg), resident(wu), resident(wd), tile(d),
                  whole(xns), whole(gates), whole(ranks), whole(hs)],
        out_specs=[tile(d), whole(hs)],
        scratch_shapes=[pltpu.VMEM((full, d), BF16), pltpu.VMEM((full, d), F32)],
    )
    one_expert = pl.pallas_call(
        functools.partial(_moe_kernel, tm=tm, tf=tf, ne=ne),
        grid_spec=grid_spec,
        out_shape=[jax.ShapeDtypeStruct((n, d), F32), jax.ShapeDtypeStruct((ns, d), F32)],
        input_output_aliases={9: 0, 13: 1},
        compiler_params=_params("arbitrary"),
        name="moe_expert",
    )
    for e in range(ne):
        h, hs = one_expert(jnp.full((1,), e, jnp.int32), cnt, cnts, xn, gate, rank, wg, wu, wd, h, xns, gates, ranks, hs)

    row = lambda a: a.reshape(1, -1)

    def finish(hm, pm, rows):
        whole1 = lambda a: pl.BlockSpec(a.shape, lambda i: (0,) * a.ndim)
        return pl.pallas_call(
            _moe_finish_kernel,
            grid=(hm.shape[0] // rows,),
            in_specs=[pl.BlockSpec((rows, d), lambda i: (i, 0)), _layer_rows(pm, layer, rows),
                      whole1(row(gp)), whole1(wpg), whole1(wpp), whole1(row(gf))],
            out_specs=pl.BlockSpec((rows, d), lambda i: (i, 0)),
            out_shape=jax.ShapeDtypeStruct(hm.shape, F32),
            compiler_params=_params("parallel"),
            name="moe_finish",
        )(hm, pm, row(gp), wpg, wpp, row(gf))
    return finish(h, p, tm), finish(hs, ps, ns)


def _rope_tables(pos):
    half = ROT_DIM // 2
    inv = jnp.power(ROPE_THETA, -jnp.arange(half, dtype=F32) / half)
    ang = pos.astype(F32)[:, None] * inv[None, :]
    cos, sin = jnp.cos(ang), jnp.sin(ang)
    rest = HEAD_DIM - ROT_DIM
    n = pos.shape[0]
    return (jnp.concatenate([cos, cos, jnp.ones((n, rest), F32)], axis=1),
            jnp.concatenate([-sin, sin, jnp.zeros((n, rest), F32)], axis=1))


def _tile(n, pref):
    return pref if n % pref == 0 else n


def kernel(x_prompt, x_sample, state_conv, cache_k, cache_v, page_table, p_prompt, p_sample, norm_mix, norm_ffn, norm_ple, ple_w_gate, ple_w_proj, conv_w_pw1, conv_b_pw1, conv_w_dw, conv_b_dw, conv_ln_g, conv_ln_b, conv_w_pw2, conv_b_pw2, ffn_w_gate, ffn_w_up, ffn_w_down, attn_w_qkv, attn_w_o, moe_w_router, moe_w_gate, moe_w_up, moe_w_down, norm_final):
    bsz, seq, d = x_prompt.shape
    nb, dec_seq, _ = x_sample.shape
    assert dec_seq == 1 and norm_mix.shape[0] == 2
    nh = d // HEAD_DIM
    nkv = (attn_w_qkv.shape[2] // HEAD_DIM - nh) // 2
    n_tok = bsz * seq
    dff = ffn_w_gate.shape[2]
    psize = cache_k.shape[2]
    past_len = page_table.shape[1] * psize
    bf = lambda w: w.astype(BF16)

    cw = (norm_mix[0], bf(conv_w_pw1[0]), conv_b_pw1[0], conv_w_dw[0], conv_b_dw[0], conv_ln_g[0], conv_ln_b[0],
          bf(conv_w_pw2[0]), conv_b_pw2[0])
    hp, conv_p = _conv_prompt(x_prompt, *cw, tm=_tile(seq, 1024))
    hs, conv_s = _conv_sample(x_sample.reshape(nb, d), state_conv[0], *cw)
    fw = (norm_ffn[0], bf(ffn_w_gate[0]), bf(ffn_w_up[0]), bf(ffn_w_down[0]))
    pw0 = (norm_ple[0], bf(ple_w_gate[0]), bf(ple_w_proj[0]))
    tf = _tile(dff, 512)
    pp = p_prompt.reshape(p_prompt.shape[0], n_tok, -1)
    ps = p_sample.reshape(p_sample.shape[0], nb, -1)
    hp = _ffn(hp.reshape(n_tok, d), *fw, pp, *pw0, layer=0, tm=_tile(n_tok, 1024), tf=tf)
    hs = _ffn(hs, *fw, ps, *pw0, layer=0, tm=nb, tf=tf)

    wqkv, wo = bf(attn_w_qkv[0]), bf(attn_w_o[0])
    cos_p, sin_p = _rope_tables(jnp.arange(seq, dtype=jnp.int32))
    qt, k_p, v_p, kb, vt, biast = _qkv_prompt(hp.reshape(bsz, seq, d), norm_mix[1], wqkv, cos_p, sin_p,
                                              tm=_tile(seq, 1024), nh=nh, nkv=nkv)
    hp = _attn_prompt(qt, kb, vt, biast, hp.reshape(bsz, seq, d), wo, nh=nh, nkv=nkv).reshape(n_tok, d)

    cos_s, sin_s = _rope_tables(jnp.full((1,), past_len, jnp.int32))
    qs, k_s, v_s = _qkv_sample(hs, norm_mix[1], wqkv, cos_s, sin_s, nh=nh, nkv=nkv)
    ck = cache_k[0].reshape(cache_k.shape[1], psize * nkv, HEAD_DIM)
    cv = cache_v[0].reshape(cache_v.shape[1], psize * nkv, HEAD_DIM)
    idx = _sample_select(ck, page_table, qs, nh=nh, nkv=nkv)[:, :, :TOP_K_BLOCKS]
    att_s = _sample_attn(ck, cv, page_table, idx, qs, k_s, v_s, nh=nh, nkv=nkv)
    hs = _proj_residual(att_s, hs, wo)

    mw = (norm_ffn[1], moe_w_router[0], bf(moe_w_gate[0]), bf(moe_w_up[0]), bf(moe_w_down[0]))
    pw1 = (norm_ple[1], bf(ple_w_gate[1]), bf(ple_w_proj[1]), norm_final)
    yp, ys = _moe(hp, hs, *mw, pp, ps, *pw1, layer=1, tm=_tile(n_tok, 1024), tf=tf)

    return (yp.reshape(bsz, seq, d), ys.reshape(nb, 1, d),
            conv_p[None], conv_s[None],
            k_p.reshape(1, bsz, seq, nkv, HEAD_DIM), v_p.reshape(1, bsz, seq, nkv, HEAD_DIM),
            k_s.reshape(1, nb, 1, nkv, HEAD_DIM), v_s.reshape(1, nb, 1, nkv, HEAD_DIM))
```
